```python
import jax
import jax.numpy as jnp
from jax import lax
import numpy as np

D_MODEL = 1024
BATCH = 8
SEQ = 2048
DEPTH = 4

M_HEADS = 4
M_HEAD_DIM = 64
M_WIDTH = M_HEADS * M_HEAD_DIM
M_CONV = 4
M_CHUNK = 64
A_HEADS = 8
A_NOPE = 64
A_ROPE = 32
A_VDIM = 64
A_QRANK = 256
A_KVRANK = 128
A_WIDTH = A_HEADS * A_VDIM
A_QBLOCK = 128
ROPE_THETA = 10000.0
P_WINDOWS = (2, 4, 8, 16)
P_GROUPS = 4
P_GROUP_DIM = 64
P_WIDTH = P_GROUPS * P_GROUP_DIM
MIX_WIDTH = M_WIDTH + A_WIDTH + P_WIDTH
IN_WIDTH = 4 * M_WIDTH + 2 * M_HEADS + A_QRANK + A_KVRANK + A_ROPE + P_WIDTH
N_EXPERTS = 32
TOP_K = 4
D_FF = 1024
SWIGLU_LIMIT = 7.0
SWIGLU_ALPHA = 1.702
MOE_BLOCK = 128
EPS = 1e-6

kernel_name = "hybrid_mlstm_mla_pool_moe"


def rms_norm(x, g):
    xf = x.astype(jnp.float32)
    y = xf * lax.rsqrt(jnp.mean(xf * xf, axis=-1, keepdims=True) + EPS)
    return (y * g.astype(jnp.float32)).astype(x.dtype)


def rope_tables(seq):
    inv = ROPE_THETA ** (-jnp.arange(0, A_ROPE, 2, dtype=jnp.float32) / A_ROPE)
    ang = jnp.arange(seq, dtype=jnp.float32)[:, None] * inv[None, :]
    return jnp.cos(ang), jnp.sin(ang)


def apply_rope(x, cos, sin):
    x1, x2 = jnp.split(x, 2, axis=-1)
    c = cos.astype(x.dtype)
    s = sin.astype(x.dtype)
    return jnp.concatenate([x1 * c - x2 * s, x1 * s + x2 * c], axis=-1)


def causal_dwconv(x, w, b):
    width = w.shape[0]
    seq = x.shape[1]
    xp = jnp.pad(x, ((0, 0), (width - 1, 0), (0, 0)))
    y = b
    for j in range(width):
        y = y + xp[:, j:j + seq] * w[j]
    return y


def mlstm_chunkwise(q, k, v, i_pre, f_pre):
    bsz, nh, seq, d = q.shape
    L = M_CHUNK
    nc = seq // L
    k = k * (d ** -0.5)
    logf = jax.nn.log_sigmoid(f_pre)

    def to_chunks(t):
        return jnp.moveaxis(t.reshape(bsz, nh, nc, L, *t.shape[3:]), 2, 0)

    xs = (to_chunks(q), to_chunks(k), to_chunks(v), to_chunks(i_pre), to_chunks(logf))
    causal = jnp.tril(jnp.ones((L, L), dtype=bool))

    def step(carry, inp):
        C, n, m = carry
        qj, kj, vj, ij, fj = inp
        g = jnp.cumsum(fj, axis=-1)
        D = g[..., :, None] - g[..., None, :] + ij[..., None, :]
        D = jnp.where(causal, D, -jnp.inf)
        inter = g + m[..., None]
        m_row = jnp.maximum(inter, jnp.max(D, axis=-1))
        s = jnp.einsum('bhld,bhmd->bhlm', qj, kj) * jnp.exp(D - m_row[..., None])
        w_inter = jnp.exp(inter - m_row)
        num = (w_inter[..., None] * jnp.einsum('bhvk,bhlk->bhlv', C, qj)
               + jnp.einsum('bhlm,bhmv->bhlv', s, vj))
        den = w_inter * jnp.einsum('bhk,bhlk->bhl', n, qj) + jnp.sum(s, axis=-1)
        h = num / jnp.maximum(jnp.abs(den), jnp.exp(-m_row))[..., None]
        g_end = g[..., -1]
        a = g_end[..., None] - g + ij
        m_new = jnp.maximum(g_end + m, jnp.max(a, axis=-1))
        wa = jnp.exp(a - m_new[..., None])
        decay = jnp.exp(g_end + m - m_new)
        C_new = decay[..., None, None] * C + jnp.einsum('bhl,bhlv,bhlk->bhvk', wa, vj, kj)
        n_new = decay[..., None] * n + jnp.einsum('bhl,bhlk->bhk', wa, kj)
        return (C_new, n_new, m_new), h

    init = (jnp.zeros((bsz, nh, d, d), jnp.float32),
            jnp.zeros((bsz, nh, d), jnp.float32),
            jnp.zeros((bsz, nh), jnp.float32))
    _, hs = lax.scan(step, init, xs)
    return jnp.moveaxis(hs, 0, 2).reshape(bsz, nh, seq, d)


def mla_attention(c_q, c_kv, k_rope, q_norm_g, kv_norm_g, w_uq, w_ukv, cos, sin):
    bsz, seq, _ = c_q.shape
    q = (rms_norm(c_q, q_norm_g) @ w_uq).reshape(bsz, seq, A_HEADS, A_NOPE + A_ROPE)
    q = q.transpose(0, 2, 1, 3)
    q_nope = q[..., :A_NOPE]
    q_rot = apply_rope(q[..., A_NOPE:], cos, sin)
    kv = (rms_norm(c_kv, kv_norm_g) @ w_ukv).reshape(bsz, seq, A_HEADS, A_NOPE + A_VDIM)
    kv = kv.transpose(0, 2, 1, 3)
    k_nope = kv[..., :A_NOPE]
    v = kv[..., A_NOPE:]
    k_rot = apply_rope(k_rope, cos, sin)
    scale = (A_NOPE + A_ROPE) ** -0.5
    nqb = seq // A_QBLOCK

    def to_blocks(t):
        return jnp.moveaxis(t.reshape(bsz, A_HEADS, nqb, A_QBLOCK, t.shape[-1]), 2, 0)

    kpos = jnp.arange(seq)

    def block(args):
        qn, qr, start = args
        s = (jnp.einsum('bhqd,bhkd->bhqk', qn, k_nope)
             + jnp.einsum('bhqr,bkr->bhqk', qr, k_rot)).astype(jnp.float32) * scale
        qpos = start + jnp.arange(A_QBLOCK)
        s = jnp.where(kpos[None, :] <= qpos[:, None], s, -jnp.inf)
        p = jax.nn.softmax(s, axis=-1).astype(v.dtype)
        return jnp.einsum('bhqk,bhkv->bhqv', p, v)

    out = lax.map(block, (to_blocks(q_nope), to_blocks(q_rot), jnp.arange(nqb) * A_QBLOCK))
    out = jnp.moveaxis(out, 0, 2).reshape(bsz, A_HEADS, seq, A_VDIM)
    return out.transpose(0, 2, 1, 3).reshape(bsz, seq, A_WIDTH)


def multiscale_pool(u, w_pool, pool_scale):
    bsz, seq, _ = u.shape
    uf = u.astype(jnp.float32)
    cs = jnp.concatenate([jnp.zeros((bsz, 1, P_WIDTH), jnp.float32), jnp.cumsum(uf, axis=1)], axis=1)
    t = jnp.arange(seq)
    outs = []
    for gi, win in enumerate(P_WINDOWS):
        sl = slice(gi * P_GROUP_DIM, (gi + 1) * P_GROUP_DIM)
        c = cs[..., sl]
        lo = jnp.maximum(t + 1 - win, 0)
        cnt = (t + 1 - lo).astype(jnp.float32)
        pooled = (c[:, t + 1] - c[:, lo]) / cnt[None, :, None] - uf[..., sl]
        outs.append(pooled.astype(u.dtype) @ w_pool[gi])
    return jnp.concatenate(outs, axis=-1) * pool_scale


def hybrid_mixer(h, w_in, conv_w, conv_b, gate_b, mlstm_norm_g, q_norm_g, kv_norm_g,
                 w_uq, w_ukv, w_pool, pool_scale, w_out, cos, sin):
    bsz, seq, _ = h.shape
    sizes = (M_WIDTH, M_WIDTH, M_WIDTH, M_WIDTH, 2 * M_HEADS, A_QRANK, A_KVRANK, A_ROPE, P_WIDTH)
    points = [int(p) for p in np.cumsum(sizes)[:-1]]
    q_m, k_m, v_m, o_m, gates, c_q, c_kv, k_r, u_p = jnp.split(h @ w_in, points, axis=-1)

    qk = jax.nn.silu(causal_dwconv(jnp.concatenate([q_m, k_m], axis=-1), conv_w, conv_b))
    q_m, k_m = jnp.split(qk, 2, axis=-1)

    def heads(t):
        return t.reshape(bsz, seq, M_HEADS, M_HEAD_DIM).transpose(0, 2, 1, 3).astype(jnp.float32)

    g = (gates + gate_b).astype(jnp.float32).transpose(0, 2, 1)
    h_til = mlstm_chunkwise(heads(q_m), heads(k_m), heads(v_m), g[:, :M_HEADS], g[:, M_HEADS:])
    h_til = rms_norm(h_til.transpose(0, 2, 1, 3), mlstm_norm_g.reshape(M_HEADS, M_HEAD_DIM))
    y_m = (h_til.reshape(bsz, seq, M_WIDTH) * jax.nn.sigmoid(o_m.astype(jnp.float32))).astype(h.dtype)

    y_a = mla_attention(c_q, c_kv, k_r, q_norm_g, kv_norm_g, w_uq, w_ukv, cos, sin).astype(h.dtype)

    y_p = multiscale_pool(u_p, w_pool, pool_scale).astype(h.dtype)

    return jnp.concatenate([y_m, y_a, y_p], axis=-1) @ w_out


def moe_ffn(h, w_router, b_router, w_gate_up, b_gate_up, w_down, b_down):
    bsz, seq, d = h.shape
    T = bsz * seq
    M = T * TOP_K
    xt = h.reshape(T, d)
    logits = (xt @ w_router + b_router).astype(jnp.float32)
    top_val, top_idx = lax.top_k(logits, TOP_K)
    top_w = jax.nn.softmax(top_val, axis=-1)

    flat_e = top_idx.reshape(M)
    flat_tok = jnp.arange(M, dtype=jnp.int32) // TOP_K
    order = jnp.argsort(flat_e)
    e_sorted = flat_e[order]
    counts = jnp.bincount(flat_e, length=N_EXPERTS)
    padded = (counts + MOE_BLOCK - 1) // MOE_BLOCK * MOE_BLOCK
    pad_end = jnp.cumsum(padded)
    pad_start = pad_end - padded
    raw_start = jnp.cumsum(counts) - counts
    dest = pad_start[e_sorted] + jnp.arange(M, dtype=jnp.int32) - raw_start[e_sorted]

    P = M + N_EXPERTS * MOE_BLOCK
    nb = P // MOE_BLOCK
    tok_sorted = flat_tok[order]
    row_tok = jnp.zeros((P,), jnp.int32).at[dest].set(tok_sorted)
    row_valid = jnp.zeros((P,), bool).at[dest].set(True)
    x_rows = jnp.where(row_valid[:, None], xt[row_tok], 0)
    block_e = jnp.minimum(jnp.searchsorted(pad_end, jnp.arange(nb) * MOE_BLOCK, side='right'),
                          N_EXPERTS - 1)

    def expert_block(args):
        xb, e = args
        gu = xb @ w_gate_up[e] + b_gate_up[e]
        gate = jnp.minimum(gu[:, :D_FF], SWIGLU_LIMIT)
        up = jnp.clip(gu[:, D_FF:], -SWIGLU_LIMIT, SWIGLU_LIMIT)
        act = (up + 1.0) * gate * jax.nn.sigmoid(SWIGLU_ALPHA * gate)
        return act @ w_down[e] + b_down[e]

    y_rows = lax.map(expert_block, (x_rows.reshape(nb, MOE_BLOCK, d), block_e)).reshape(P, d)
    w_sorted = top_w.reshape(M)[order].astype(y_rows.dtype)
    y_assign = y_rows[dest] * w_sorted[:, None]
    out = jax.ops.segment_sum(y_assign, tok_sorted, num_segments=T)
    return out.reshape(bsz, seq, d)


def setup_inputs(seed: int = 0) -> dict:
    key = jax.random.key(seed)
    ks = jax.random.split(key, 24)
    f32 = jnp.float32
    nrm = lambda k, shape: jax.random.normal(k, shape, f32)
    res_scale = (2 * DEPTH) ** -0.5
    gate_b = jnp.concatenate([
        0.1 * nrm(ks[5], (DEPTH, M_HEADS)),
        jnp.broadcast_to(jnp.linspace(3.0, 6.0, M_HEADS, dtype=f32), (DEPTH, M_HEADS))
        + 0.1 * nrm(ks[6], (DEPTH, M_HEADS))], axis=-1)
    return {
        "x": nrm(ks[0], (BATCH, SEQ, D_MODEL)),
        "norm1_g": 1.0 + 0.05 * nrm(ks[1], (DEPTH, D_MODEL)),
        "w_in": nrm(ks[2], (DEPTH, D_MODEL, IN_WIDTH)) * D_MODEL ** -0.5,
        "conv_w": nrm(ks[3], (DEPTH, M_CONV, 2 * M_WIDTH)) * M_CONV ** -0.5,
        "conv_b": 0.01 * nrm(ks[4], (DEPTH, 2 * M_WIDTH)),
        "gate_b": gate_b,
        "mlstm_norm_g": 1.0 + 0.05 * nrm(ks[7], (DEPTH, M_WIDTH)),
        "q_norm_g": 1.0 + 0.05 * nrm(ks[8], (DEPTH, A_QRANK)),
        "kv_norm_g": 1.0 + 0.05 * nrm(ks[9], (DEPTH, A_KVRANK)),
        "w_uq": nrm(ks[10], (DEPTH, A_QRANK, A_HEADS * (A_NOPE + A_ROPE))) * A_QRANK ** -0.5,
        "w_ukv": nrm(ks[11], (DEPTH, A_KVRANK, A_HEADS * (A_NOPE + A_VDIM))) * A_KVRANK ** -0.5,
        "w_pool": nrm(ks[12], (DEPTH, P_GROUPS, P_GROUP_DIM, P_GROUP_DIM)) * P_GROUP_DIM ** -0.5,
        "pool_scale": 1.0 + 0.1 * nrm(ks[13], (DEPTH, P_WIDTH)),
        "w_out": nrm(ks[14], (DEPTH, MIX_WIDTH, D_MODEL)) * MIX_WIDTH ** -0.5 * res_scale,
        "norm2_g": 1.0 + 0.05 * nrm(ks[15], (DEPTH, D_MODEL)),
        "w_router": nrm(ks[16], (DEPTH, D_MODEL, N_EXPERTS)) * D_MODEL ** -0.5,
        "b_router": 0.01 * nrm(ks[17], (DEPTH, N_EXPERTS)),
        "w_gate_up": nrm(ks[18], (DEPTH, N_EXPERTS, D_MODEL, 2 * D_FF)) * D_MODEL ** -0.5,
        "b_gate_up": 0.01 * nrm(ks[19], (DEPTH, N_EXPERTS, 2 * D_FF)),
        "w_down": nrm(ks[20], (DEPTH, N_EXPERTS, D_FF, D_MODEL)) * D_FF ** -0.5 * res_scale,
        "b_down": 0.01 * nrm(ks[21], (DEPTH, N_EXPERTS, D_MODEL)),
        "final_norm_g": 1.0 + 0.05 * nrm(ks[22], (D_MODEL,)),
    }


def reference(x, norm1_g, w_in, conv_w, conv_b, gate_b, mlstm_norm_g, q_norm_g, kv_norm_g,
              w_uq, w_ukv, w_pool, pool_scale, w_out, norm2_g, w_router, b_router,
              w_gate_up, b_gate_up, w_down, b_down, final_norm_g):
    cos, sin = rope_tables(x.shape[1])
    h = x
    for l in range(DEPTH):
        mix = hybrid_mixer(rms_norm(h, norm1_g[l]), w_in[l], conv_w[l], conv_b[l], gate_b[l],
                           mlstm_norm_g[l], q_norm_g[l], kv_norm_g[l], w_uq[l], w_ukv[l],
                           w_pool[l], pool_scale[l], w_out[l], cos, sin)
        h = h + mix.astype(h.dtype)
        ffn = moe_ffn(rms_norm(h, norm2_g[l]), w_router[l], b_router[l], w_gate_up[l],
                      b_gate_up[l], w_down[l], b_down[l])
        h = h + ffn.astype(h.dtype)
    return rms_norm(h, final_norm_g)
```

```python
import functools

import jax
import jax.numpy as jnp
import numpy as np
from jax import lax
from jax.experimental import pallas as pl
from jax.experimental.pallas import tpu as pltpu

F32 = jnp.float32
BF16 = jnp.bfloat16
HIGHEST = lax.Precision.HIGHEST

D_MODEL = 1024
M_HEADS = 4
M_HEAD_DIM = 64
M_WIDTH = 256
M_CONV = 4
M_CHUNK = 64
A_HEADS = 8
A_NOPE = 64
A_ROPE = 32
A_VDIM = 64
A_QRANK = 256
A_KVRANK = 128
A_WIDTH = 512
ROPE_THETA = 10000.0
P_WINDOWS = (2, 4, 8, 16)
P_WIDTH = 256
N_EXPERTS = 32
TOP_K = 4
D_FF = 1024
SWIGLU_LIMIT = 7.0
SWIGLU_ALPHA = 1.702
EPS = 1e-6

LANES = 128
SUBLANES = 8
ROW_CHUNKS = D_MODEL // LANES

PROJ_QKVO = 0
PROJ_CQ = 1024
PROJ_UP = 1280
PROJ_CKV = 1536
PROJ_SMALL = 1664
PROJ_WIDTH = 1792
SMALL_KR = 64
SMALL_GATE = 96

MOE_ROWS = 256
VMEM_LIMIT = 56 * 1024 * 1024


def _cparams(*sem):
    return pltpu.CompilerParams(dimension_semantics=sem, vmem_limit_bytes=VMEM_LIMIT)


def _sigmoid(x):
    return 1.0 / (1.0 + jnp.exp(-x))


def _log_sigmoid(x):
    return jnp.minimum(x, 0.0) - jnp.log(1.0 + jnp.exp(-jnp.abs(x)))


def _dot(a, b, **kw):
    return jnp.dot(a, b, preferred_element_type=F32, **kw)


def _dot_nt(a, b, **kw):
    return lax.dot_general(a, b, (((1,), (1,)), ((), ())), preferred_element_type=F32, **kw)


def _dot_tn(a, b, **kw):
    return lax.dot_general(a, b, (((0,), (0,)), ((), ())), preferred_element_type=F32, **kw)


def _inproj_kernel(h_ref, g_ref, w_ref, o_ref):
    x = h_ref[...]
    ms = jnp.mean(x * x, axis=-1, keepdims=True)
    xn = x * lax.rsqrt(ms + EPS) * g_ref[...]
    o_ref[...] = _dot(xn.astype(BF16), w_ref[...])


def _inproj(h2, g, w, tm=512):
    T = h2.shape[0]
    return pl.pallas_call(
        _inproj_kernel,
        grid=(T // tm,),
        in_specs=[pl.BlockSpec((tm, D_MODEL), lambda i: (i, 0)),
                  pl.BlockSpec((1, D_MODEL), lambda i: (0, 0)),
                  pl.BlockSpec((D_MODEL, PROJ_WIDTH), lambda i: (0, 0))],
        out_specs=pl.BlockSpec((tm, PROJ_WIDTH), lambda i: (i, 0)),
        out_shape=jax.ShapeDtypeStruct((T, PROJ_WIDTH), F32),
        compiler_params=_cparams("parallel"),
        name="inproj",
    )(h2, g, w)


def _mlstm_kernel(proj_ref, small_ref, gt_ref, cw_ref, cb_ref, gbc_ref, gbr_ref, ng_ref, o_ref,
                  xpad_ref, ct_ref, n_ref, m_ref):
    S = proj_ref.shape[0]
    L = M_CHUNK
    nc = S // L
    W = M_WIDTH
    xpad_ref[0:SUBLANES, :] = jnp.zeros((SUBLANES, 2 * W), F32)
    xpad_ref[SUBLANES:, :] = proj_ref[:, 0:2 * W]
    ct_ref[...] = jnp.zeros_like(ct_ref)
    n_ref[...] = jnp.zeros_like(n_ref)
    m_ref[...] = jnp.zeros_like(m_ref)

    lane_head = lax.broadcasted_iota(jnp.int32, (1, W), 1) // M_HEAD_DIM
    masks = [(lane_head == h).astype(F32) for h in range(M_HEADS)]
    rh = lax.broadcasted_iota(jnp.int32, (W, W), 0) // M_HEAD_DIM
    chd = lax.broadcasted_iota(jnp.int32, (W, W), 1) // M_HEAD_DIM
    same_head = (rh == chd).astype(F32)
    ri = lax.broadcasted_iota(jnp.int32, (L, L), 0)
    ci = lax.broadcasted_iota(jnp.int32, (L, L), 1)
    causal = ri >= ci
    tril = causal.astype(F32)
    triu = (ri <= ci).astype(F32)
    cw = cw_ref[...]
    cb = cb_ref[...]
    gbc = gbc_ref[...]
    gbr = gbr_ref[...]
    ng = ng_ref[...]

    def body(c, carry):
        r0 = pl.multiple_of(c * L, L)
        win = xpad_ref[pl.ds(r0, L + SUBLANES), :]
        acc = jnp.zeros((L, 2 * W), F32) + cb
        for j in range(M_CONV):
            s = M_CONV - 1 - j
            xs = win if s == 0 else pltpu.roll(win, s, axis=0)
            acc = acc + xs[SUBLANES:, :] * cw[j:j + 1, :]
        qk = acc * _sigmoid(acc)
        q = qk[:, 0:W]
        k = qk[:, W:2 * W] * (M_HEAD_DIM ** -0.5)
        v = proj_ref[pl.ds(r0, L), 2 * W:3 * W]
        og = proj_ref[pl.ds(r0, L), 3 * W:4 * W]
        sm = small_ref[pl.ds(r0, L), :] + gbc
        gt = gt_ref[0, c] + gbr
        i_col_all = sm[:, SMALL_GATE:SMALL_GATE + M_HEADS]
        logf_col = _log_sigmoid(sm[:, SMALL_GATE + M_HEADS:SMALL_GATE + 2 * M_HEADS])
        g_col_all = _dot(tril, logf_col, precision=HIGHEST)
        i_row_all = gt[0:M_HEADS, :]
        logf_row = _log_sigmoid(gt[M_HEADS:2 * M_HEADS, :])
        g_row_all = _dot(logf_row, triu, precision=HIGHEST)

        q16 = q.astype(BF16)
        k16 = k.astype(BF16)
        v16 = v.astype(BF16)
        num_intra = jnp.zeros((L, W), F32)
        w_inter_f = jnp.zeros((L, W), F32)
        rowsum_f = jnp.zeros((L, W), F32)
        floor_f = jnp.zeros((L, W), F32)
        wa_f = jnp.zeros((L, W), F32)
        dec_f = jnp.zeros((1, W), F32)
        for h in range(M_HEADS):
            mk = masks[h]
            g_col = g_col_all[:, h:h + 1]
            i_col = i_col_all[:, h:h + 1]
            g_row = g_row_all[h:h + 1, :]
            i_row = i_row_all[h:h + 1, :]
            m_old = m_ref[0:1, h:h + 1]
            dmat = jnp.where(causal, g_col - g_row + i_row, -jnp.inf)
            inter = g_col + m_old
            m_row = jnp.maximum(inter, jnp.max(dmat, axis=-1, keepdims=True))
            sc = _dot_nt((q * mk).astype(BF16), k16) * jnp.exp(dmat - m_row)
            w_inter = jnp.exp(inter - m_row)
            num_intra = num_intra + _dot(sc.astype(BF16), v16) * mk
            w_inter_f = w_inter_f + w_inter * mk
            rowsum_f = rowsum_f + jnp.sum(sc, axis=-1, keepdims=True) * mk
            floor_f = floor_f + jnp.exp(-m_row) * mk
            g_end = g_col[L - 1:L, :]
            a = g_end - g_col + i_col
            m_new = jnp.maximum(g_end + m_old, jnp.max(a, axis=0, keepdims=True))
            wa_f = wa_f + jnp.exp(a - m_new) * mk
            dec_f = dec_f + jnp.exp(g_end + m_old - m_new) * mk
            m_ref[0:1, h:h + 1] = m_new

        ct = ct_ref[...]
        nvec = n_ref[...]
        q_c = _dot(q16, ct.astype(BF16))
        q_n = _dot(q * nvec, same_head, precision=HIGHEST)
        num = w_inter_f * q_c + num_intra
        den = w_inter_f * q_n + rowsum_f
        hv = num / jnp.maximum(jnp.abs(den), floor_f)
        ms = _dot(hv * hv, same_head, precision=HIGHEST) * (1.0 / M_HEAD_DIM)
        y = hv * lax.rsqrt(ms + EPS) * ng * _sigmoid(og)
        o_ref[pl.ds(r0, L), :] = y

        upd = _dot_tn(k16, (wa_f * v).astype(BF16))
        ct_ref[...] = dec_f * ct + upd * same_head
        n_ref[...] = dec_f * nvec + jnp.sum(wa_f * k, axis=0, keepdims=True)
        return carry

    lax.fori_loop(0, nc, body, 0)


def _mlstm(proj, gates_t, cw, cb, gbc, gbr, ng, B, S):
    T = B * S
    nc = S // M_CHUNK
    return pl.pallas_call(
        _mlstm_kernel,
        grid=(B,),
        in_specs=[pl.BlockSpec((S, 4 * M_WIDTH), lambda b: (b, 0)),
                  pl.BlockSpec((S, LANES), lambda b: (b, PROJ_SMALL // LANES)),
                  pl.BlockSpec((1, nc, 2 * M_HEADS, M_CHUNK), lambda b: (b, 0, 0, 0)),
                  pl.BlockSpec((M_CONV, 2 * M_WIDTH), lambda b: (0, 0)),
                  pl.BlockSpec((1, 2 * M_WIDTH), lambda b: (0, 0)),
                  pl.BlockSpec((1, LANES), lambda b: (0, 0)),
                  pl.BlockSpec((2 * M_HEADS, 1), lambda b: (0, 0)),
                  pl.BlockSpec((1, M_WIDTH), lambda b: (0, 0))],
        out_specs=pl.BlockSpec((S, M_WIDTH), lambda b: (b, 0)),
        out_shape=jax.ShapeDtypeStruct((T, M_WIDTH), F32),
        scratch_shapes=[pltpu.VMEM((S + SUBLANES, 2 * M_WIDTH), F32),
                        pltpu.VMEM((M_WIDTH, M_WIDTH), F32),
                        pltpu.VMEM((1, M_WIDTH), F32),
                        pltpu.VMEM((1, LANES), F32)],
        compiler_params=_cparams("parallel"),
        name="mlstm",
    )(proj, proj, gates_t, cw, cb, gbc, gbr, ng)


POOL_HALO = 16
POOL_TILE = 256


def _pool_kernel(u_ref, w_ref, sc_ref, o_ref, upad_ref):
    S = u_ref.shape[0]
    upad_ref[0:POOL_HALO, :] = jnp.zeros((POOL_HALO, P_WIDTH), F32)
    upad_ref[POOL_HALO:, :] = u_ref[...]
    grp = lax.broadcasted_iota(jnp.int32, (1, P_WIDTH), 1) // (P_WIDTH // len(P_WINDOWS))
    win_lane = jnp.zeros((1, P_WIDTH), jnp.int32)
    for gi, wn in enumerate(P_WINDOWS):
        win_lane = jnp.where(grp == gi, wn, win_lane)
    w = w_ref[...]
    scale = sc_ref[...]
    rows = POOL_TILE + POOL_HALO

    def body(r, carry):
        r0 = pl.multiple_of(r * POOL_TILE, POOL_TILE)
        a = upad_ref[pl.ds(r0, rows), :]
        sums = []
        cur = a
        span = 1
        for _ in P_WINDOWS:
            cur = cur + pltpu.roll(cur, span, axis=0)
            span *= 2
            sums.append(cur)
        sel = sums[-1]
        for gi in range(len(P_WINDOWS) - 1):
            sel = jnp.where(grp == gi, sums[gi], sel)
        sel = sel[POOL_HALO:, :]
        u = a[POOL_HALO:, :]
        t = r0 + lax.broadcasted_iota(jnp.int32, (POOL_TILE, P_WIDTH), 0)
        cnt = jnp.minimum(t + 1, win_lane).astype(F32)
        pooled = sel / cnt - u
        o_ref[pl.ds(r0, POOL_TILE), :] = _dot(pooled.astype(BF16), w) * scale
        return carry

    lax.fori_loop(0, S // POOL_TILE, body, 0)


def _pool(proj, w_bd, scale, B, S):
    T = B * S
    return pl.pallas_call(
        _pool_kernel,
        grid=(B,),
        in_specs=[pl.BlockSpec((S, P_WIDTH), lambda b: (b, PROJ_UP // P_WIDTH)),
                  pl.BlockSpec((P_WIDTH, P_WIDTH), lambda b: (0, 0)),
                  pl.BlockSpec((1, P_WIDTH), lambda b: (0, 0))],
        out_specs=pl.BlockSpec((S, P_WIDTH), lambda b: (b, 0)),
        out_shape=jax.ShapeDtypeStruct((T, P_WIDTH), F32),
        scratch_shapes=[pltpu.VMEM((S + POOL_HALO, P_WIDTH), F32)],
        compiler_params=_cparams("parallel"),
        name="pool",
    )(proj, w_bd, scale)


def _rope(x, c, s1, s2):
    return x * c + pltpu.roll(x, LANES - A_ROPE // 2, axis=1) * s1 + pltpu.roll(x, A_ROPE // 2, axis=1) * s2


def _mla_prep_kernel(cq_ref, ckv_ref, small_ref, qg_ref, kvg_ref, wq_ref, wk_ref, wv_ref,
                     cq_t_ref, ck_t_ref, s1_ref, s2_ref, q_ref, k_ref, v_ref):
    def rms(x, g):
        return x * lax.rsqrt(jnp.mean(x * x, axis=-1, keepdims=True) + EPS) * g

    cqn = rms(cq_ref[...], qg_ref[...]).astype(BF16)
    ckvn = rms(ckv_ref[...], kvg_ref[...]).astype(BF16)
    scale = (A_NOPE + A_ROPE) ** -0.5
    qf = _dot(cqn, wq_ref[...]) * scale
    kf = _dot(ckvn, wk_ref[...])
    v_ref[...] = _dot(ckvn, wv_ref[...]).astype(BF16)
    cqt = cq_t_ref[...]
    s1 = s1_ref[...]
    s2 = s2_ref[...]
    krot = _rope(small_ref[...], ck_t_ref[...], s1, s2)
    for h in range(A_HEADS):
        sl = slice(h * LANES, (h + 1) * LANES)
        q_ref[:, sl] = _rope(qf[:, sl], cqt, s1, s2).astype(BF16)
        k_ref[:, sl] = (kf[:, sl] + krot).astype(BF16)


def _mla_prep(proj, qg, kvg, wq, wk, wv, cq_t, ck_t, s1_t, s2_t, B, S, ts=512):
    T = B * S
    nst = S // ts
    hw = A_HEADS * LANES
    return pl.pallas_call(
        _mla_prep_kernel,
        grid=(B, nst),
        in_specs=[pl.BlockSpec((ts, A_QRANK), lambda b, s: (b * nst + s, PROJ_CQ // A_QRANK)),
                  pl.BlockSpec((ts, A_KVRANK), lambda b, s: (b * nst + s, PROJ_CKV // A_KVRANK)),
                  pl.BlockSpec((ts, LANES), lambda b, s: (b * nst + s, PROJ_SMALL // LANES)),
                  pl.BlockSpec((1, A_QRANK), lambda b, s: (0, 0)),
                  pl.BlockSpec((1, A_KVRANK), lambda b, s: (0, 0)),
                  pl.BlockSpec((A_QRANK, hw), lambda b, s: (0, 0)),
                  pl.BlockSpec((A_KVRANK, hw), lambda b, s: (0, 0)),
                  pl.BlockSpec((A_KVRANK, A_WIDTH), lambda b, s: (0, 0)),
                  pl.BlockSpec((ts, LANES), lambda b, s: (s, 0)),
                  pl.BlockSpec((ts, LANES), lambda b, s: (s, 0)),
                  pl.BlockSpec((ts, LANES), lambda b, s: (s, 0)),
                  pl.BlockSpec((ts, LANES), lambda b, s: (s, 0))],
        out_specs=[pl.BlockSpec((ts, hw), lambda b, s: (b * nst + s, 0)),
                   pl.BlockSpec((ts, hw), lambda b, s: (b * nst + s, 0)),
                   pl.BlockSpec((ts, A_WIDTH), lambda b, s: (b * nst + s, 0))],
        out_shape=[jax.ShapeDtypeStruct((T, hw), BF16),
                   jax.ShapeDtypeStruct((T, hw), BF16),
                   jax.ShapeDtypeStruct((T, A_WIDTH), BF16)],
        compiler_params=_cparams("parallel", "parallel"),
        name="mla_prep",
    )(proj, proj, proj, qg, kvg, wq, wk, wv, cq_t, ck_t, s1_t, s2_t)


def _attn_kernel(q_ref, k_ref, v_ref, o_ref, *, tq):
    qi = pl.program_id(2)
    ri = lax.broadcasted_iota(jnp.int32, (tq, tq), 0)
    ci = lax.broadcasted_iota(jnp.int32, (tq, tq), 1)
    diag_ok = ri >= ci
    outs = []
    for hh in range(2):
        sl = slice(hh * LANES, (hh + 1) * LANES)
        q = q_ref[:, sl]

        def step(kb, carry, masked):
            m, l, acc = carry
            k0 = pl.multiple_of(kb * tq, tq)
            kk = k_ref[pl.ds(k0, tq), sl]
            vv = v_ref[pl.ds(k0, tq), :]
            s = _dot_nt(q, kk)
            if masked:
                s = jnp.where(diag_ok, s, -jnp.inf)
            m_new = jnp.maximum(m, jnp.max(s, axis=-1, keepdims=True))
            p = jnp.exp(s - m_new)
            alpha = jnp.exp(m - m_new)
            l = alpha * l + jnp.sum(p, axis=-1, keepdims=True)
            acc = alpha * acc + _dot(p.astype(BF16), vv)
            return m_new, l, acc

        init = (jnp.full((tq, 1), -jnp.inf, F32), jnp.zeros((tq, 1), F32), jnp.zeros((tq, LANES), F32))
        carry = lax.fori_loop(0, qi, functools.partial(step, masked=False), init)
        m, l, acc = step(qi, carry, True)
        outs.append(acc / l)
    lane = lax.broadcasted_iota(jnp.int32, (tq, LANES), 1)
    o_ref[...] = jnp.where(lane < A_VDIM, outs[0], outs[1]).astype(o_ref.dtype)


def _attention(q, k, v, B, S, tq=256):
    T = B * S
    nq = S // tq
    return pl.pallas_call(
        functools.partial(_attn_kernel, tq=tq),
        grid=(B, A_HEADS // 2, nq),
        in_specs=[pl.BlockSpec((tq, 2 * LANES), lambda b, p, i: (b * nq + i, p)),
                  pl.BlockSpec((S, 2 * LANES), lambda b, p, i: (b, p)),
                  pl.BlockSpec((S, LANES), lambda b, p, i: (b, p))],
        out_specs=pl.BlockSpec((tq, LANES), lambda b, p, i: (b * nq + i, p)),
        out_shape=jax.ShapeDtypeStruct((T, A_WIDTH), BF16),
        compiler_params=_cparams("parallel", "parallel", "arbitrary"),
        name="attention",
    )(q, k, v)


def _outproj_kernel(ym_ref, ya_ref, yp_ref, h_ref, w_ref, g_ref, wr_ref, br_ref,
                    hn_ref, xn_ref, lg_ref):
    tm = h_ref.shape[0]
    mix = _dot(ym_ref[...].astype(BF16), w_ref[0:M_WIDTH, :])
    mix = mix + _dot(ya_ref[...], w_ref[M_WIDTH:M_WIDTH + A_WIDTH, :])
    mix = mix + _dot(yp_ref[...].astype(BF16), w_ref[M_WIDTH + A_WIDTH:, :])
    hn = h_ref[...] + mix
    hn_ref[...] = hn
    xn = hn * lax.rsqrt(jnp.mean(hn * hn, axis=-1, keepdims=True) + EPS) * g_ref[...]
    lg_ref[...] = _dot_nt(wr_ref[...], xn, precision=HIGHEST) + br_ref[...]
    for s in range(ROW_CHUNKS):
        xn_ref[pl.ds(s, tm, stride=ROW_CHUNKS), :] = xn[:, s * LANES:(s + 1) * LANES]


def _outproj(ym, ya, yp, h2, w, g, wr_t, br, tm=512):
    T = h2.shape[0]
    return pl.pallas_call(
        _outproj_kernel,
        grid=(T // tm,),
        in_specs=[pl.BlockSpec((tm, M_WIDTH), lambda i: (i, 0)),
                  pl.BlockSpec((tm, A_WIDTH), lambda i: (i, 0)),
                  pl.BlockSpec((tm, P_WIDTH), lambda i: (i, 0)),
                  pl.BlockSpec((tm, D_MODEL), lambda i: (i, 0)),
                  pl.BlockSpec((D_MODEL, D_MODEL), lambda i: (0, 0)),
                  pl.BlockSpec((1, D_MODEL), lambda i: (0, 0)),
                  pl.BlockSpec((N_EXPERTS, D_MODEL), lambda i: (0, 0)),
                  pl.BlockSpec((N_EXPERTS, 1), lambda i: (0, 0))],
        out_specs=[pl.BlockSpec((tm, D_MODEL), lambda i: (i, 0)),
                   pl.BlockSpec((tm * ROW_CHUNKS, LANES), lambda i: (i, 0)),
                   pl.BlockSpec((N_EXPERTS, tm), lambda i: (0, i))],
        out_shape=[jax.ShapeDtypeStruct((T, D_MODEL), F32),
                   jax.ShapeDtypeStruct((T * ROW_CHUNKS, LANES), F32),
                   jax.ShapeDtypeStruct((N_EXPERTS, T), F32)],
        compiler_params=_cparams("parallel"),
        name="outproj",
    )(ym, ya, yp, h2, w, g, wr_t, br)


def _router_kernel(lg_ref, tri_ref, e_ref, w_ref, r_ref, cnt_ref, carry_ref):
    tr = lg_ref.shape[1]

    @pl.when(pl.program_id(0) == 0)
    def _():
        carry_ref[...] = jnp.zeros_like(carry_ref)

    x = lg_ref[...]
    eio = lax.broadcasted_iota(jnp.int32, (N_EXPERTS, tr), 0).astype(F32)
    picked = jnp.zeros((N_EXPERTS, tr), F32)
    vals = []
    idxs = []
    for _ in range(TOP_K):
        mx = jnp.max(x, axis=0, keepdims=True)
        idx = jnp.min(jnp.where(x == mx, eio, float(N_EXPERTS)), axis=0, keepdims=True)
        hit = eio == idx
        vals.append(mx)
        idxs.append(idx)
        picked = picked + hit.astype(F32)
        x = jnp.where(hit, -jnp.inf, x)
    exps = [jnp.exp(vv - vals[0]) for vv in vals]
    tot = exps[0] + exps[1] + exps[2] + exps[3]
    before = _dot(picked.astype(BF16), tri_ref[...]) + carry_ref[:, 0:1]
    for kk in range(TOP_K):
        e_ref[kk:kk + 1, :] = idxs[kk].astype(jnp.int32)
        w_ref[kk:kk + 1, :] = exps[kk] / tot
        rk = jnp.sum(jnp.where(eio == idxs[kk], before, 0.0), axis=0, keepdims=True)
        r_ref[kk:kk + 1, :] = rk.astype(jnp.int32)
    carry_ref[...] = carry_ref[...] + jnp.sum(picked, axis=1, keepdims=True)
    cnt_ref[...] = carry_ref[...]


def _router(logits_t, tri, tr=512):
    T = logits_t.shape[1]
    return pl.pallas_call(
        _router_kernel,
        grid=(T // tr,),
        in_specs=[pl.BlockSpec((N_EXPERTS, tr), lambda i: (0, i)),
                  pl.BlockSpec((tr, tr), lambda i: (0, 0))],
        out_specs=[pl.BlockSpec((TOP_K, tr), lambda i: (0, i)),
                   pl.BlockSpec((TOP_K, tr), lambda i: (0, i)),
                   pl.BlockSpec((TOP_K, tr), lambda i: (0, i)),
                   pl.BlockSpec((N_EXPERTS, LANES), lambda i: (0, 0))],
        out_shape=[jax.ShapeDtypeStruct((TOP_K, T), jnp.int32),
                   jax.ShapeDtypeStruct((TOP_K, T), F32),
                   jax.ShapeDtypeStruct((TOP_K, T), jnp.int32),
                   jax.ShapeDtypeStruct((N_EXPERTS, LANES), F32)],
        scratch_shapes=[pltpu.VMEM((N_EXPERTS, LANES), F32)],
        compiler_params=_cparams("arbitrary"),
        name="router",
    )(logits_t, tri)


def _meta_kernel(cnt_ref, e_ref, r_ref, dest_ref, be_ref, bc_ref, nb_ref, *, nb_pad):
    cnt = cnt_ref[...]
    padded = jnp.floor((cnt + (MOE_ROWS - 1)) * (1.0 / MOE_ROWS)) * MOE_ROWS
    ri = lax.broadcasted_iota(jnp.int32, (N_EXPERTS, N_EXPERTS), 0)
    ci = lax.broadcasted_iota(jnp.int32, (N_EXPERTS, N_EXPERTS), 1)
    pad_end = _dot((ri >= ci).astype(F32), padded, precision=HIGHEST)
    pad_start = pad_end - padded
    e = e_ref[...]
    dest = r_ref[...]
    for ex in range(N_EXPERTS):
        ps = pad_start[ex:ex + 1, 0:1].astype(jnp.int32)
        dest = jnp.where(e == ex, dest + ps, dest)
    dest_ref[...] = dest
    blk0 = (lax.broadcasted_iota(jnp.int32, (N_EXPERTS, nb_pad), 1) * MOE_ROWS).astype(F32)
    be = jnp.sum((pad_end[:, 0:1] <= blk0).astype(F32), axis=0, keepdims=True)
    be = jnp.minimum(be, float(N_EXPERTS - 1))
    eio = lax.broadcasted_iota(jnp.int32, (N_EXPERTS, nb_pad), 0).astype(F32)
    seg_end = jnp.sum(jnp.where(eio == be, pad_start[:, 0:1] + cnt[:, 0:1], 0.0), axis=0, keepdims=True)
    bc = jnp.clip(seg_end - blk0[0:1, :], 0.0, float(MOE_ROWS))
    be_ref[...] = be.astype(jnp.int32)
    bc_ref[...] = bc.astype(jnp.int32)
    nb_ref[...] = (pad_end[N_EXPERTS - 1:N_EXPERTS, :] * (1.0 / MOE_ROWS)).astype(jnp.int32)


def _meta(counts, eidx, rank, nb_pad):
    T = eidx.shape[1]
    return pl.pallas_call(
        functools.partial(_meta_kernel, nb_pad=nb_pad),
        out_shape=[jax.ShapeDtypeStruct((TOP_K, T), jnp.int32),
                   jax.ShapeDtypeStruct((1, nb_pad), jnp.int32),
                   jax.ShapeDtypeStruct((1, nb_pad), jnp.int32),
                   jax.ShapeDtypeStruct((1, LANES), jnp.int32)],
        compiler_params=pltpu.CompilerParams(vmem_limit_bytes=VMEM_LIMIT),
        name="route_meta",
    )(counts, eidx, rank)


FF_CHUNK = 512


def _gmm_kernel(be_ref, bc_ref, nb_ref, tok_ref, x_hbm, wgu_ref, bgu_ref, wdn_ref, bdn_ref, y_ref,
                xbuf, wgu16, wdn16, sem):
    i = pl.program_id(0)
    nblk = nb_ref[0]
    bm = MOE_ROWS

    def row_copy(blk, slot, r):
        tok = tok_ref[blk * bm + r]
        src = x_hbm.at[pl.ds(pl.multiple_of(tok * ROW_CHUNKS, ROW_CHUNKS), ROW_CHUNKS)]
        dst = xbuf.at[slot, pl.ds(pl.multiple_of(r * ROW_CHUNKS, ROW_CHUNKS), ROW_CHUNKS)]
        return pltpu.make_async_copy(src, dst, sem.at[slot])

    def issue(blk, slot):
        def body(r, c):
            row_copy(blk, slot, r).start()
            return c
        lax.fori_loop(0, bm, body, 0)

    @pl.when(i == 0)
    def _():
        issue(0, 0)

    @pl.when(i + 1 < nblk)
    def _():
        issue(i + 1, (i + 1) % 2)

    @pl.when(i >= nblk)
    def _():
        y_ref[...] = jnp.zeros_like(y_ref)

    @pl.when(i < nblk)
    def _():
        slot = i % 2
        e_changed = jnp.logical_or(i == 0, be_ref[i] != be_ref[jnp.maximum(i - 1, 0)])

        @pl.when(e_changed)
        def _():
            wgu16[...] = wgu_ref[0].astype(BF16)
            wdn16[...] = wdn_ref[0].astype(BF16)

        pltpu.make_async_copy(x_hbm.at[pl.ds(0, bm * ROW_CHUNKS)], xbuf.at[slot], sem.at[slot]).wait()
        xs = [xbuf[slot, pl.ds(s, bm, stride=ROW_CHUNKS), :] for s in range(ROW_CHUNKS)]
        x = jnp.concatenate(xs, axis=1)
        valid = lax.broadcasted_iota(jnp.int32, (bm, 1), 0) < bc_ref[i]
        x16 = jnp.where(valid, x, 0.0).astype(BF16)
        acc = jnp.zeros((bm, D_MODEL), F32) + bdn_ref[0]
        for c in range(D_FF // FF_CHUNK):
            cs = slice(c * FF_CHUNK, (c + 1) * FF_CHUNK)
            us = slice(D_FF + c * FF_CHUNK, D_FF + (c + 1) * FF_CHUNK)
            gate = _dot(x16, wgu16[:, cs]) + bgu_ref[0, :, cs]
            up = _dot(x16, wgu16[:, us]) + bgu_ref[0, :, us]
            gate = jnp.minimum(gate, SWIGLU_LIMIT)
            up = jnp.clip(up, -SWIGLU_LIMIT, SWIGLU_LIMIT)
            act = (up + 1.0) * gate * _sigmoid(SWIGLU_ALPHA * gate)
            acc = acc + _dot(act.astype(BF16), wdn16[cs, :])
        for s in range(ROW_CHUNKS):
            y_ref[pl.ds(s, bm, stride=ROW_CHUNKS), :] = acc[:, s * LANES:(s + 1) * LANES]


def _gmm(blk_e, blk_cnt, nblk, row_tok, x8, wgu, bgu, wdn, bdn, nb):
    bm = MOE_ROWS

    def clamp(i, nb_ref):
        return jnp.minimum(i, jnp.maximum(nb_ref[0] - 1, 0))

    grid_spec = pltpu.PrefetchScalarGridSpec(
        num_scalar_prefetch=4,
        grid=(nb,),
        in_specs=[pl.BlockSpec(memory_space=pl.ANY),
                  pl.BlockSpec((1, D_MODEL, 2 * D_FF), lambda i, be, bc, nbr, tk: (be[clamp(i, nbr)], 0, 0)),
                  pl.BlockSpec((1, 1, 2 * D_FF), lambda i, be, bc, nbr, tk: (be[clamp(i, nbr)], 0, 0)),
                  pl.BlockSpec((1, D_FF, D_MODEL), lambda i, be, bc, nbr, tk: (be[clamp(i, nbr)], 0, 0)),
                  pl.BlockSpec((1, 1, D_MODEL), lambda i, be, bc, nbr, tk: (be[clamp(i, nbr)], 0, 0))],
        out_specs=pl.BlockSpec((bm * ROW_CHUNKS, LANES), lambda i, be, bc, nbr, tk: (i, 0)),
        scratch_shapes=[pltpu.VMEM((2, bm * ROW_CHUNKS, LANES), F32),
                        pltpu.VMEM((D_MODEL, 2 * D_FF), BF16),
                        pltpu.VMEM((D_FF, D_MODEL), BF16),
                        pltpu.SemaphoreType.DMA((2,))],
    )
    return pl.pallas_call(
        _gmm_kernel,
        grid_spec=grid_spec,
        out_shape=jax.ShapeDtypeStruct((nb * bm * ROW_CHUNKS, LANES), F32),
        compiler_params=_cparams("arbitrary"),
        name="expert_gmm",
    )(blk_e, blk_cnt, nblk, row_tok, x8, wgu, bgu, wdn, bdn)


def _combine_kernel(dest_ref, y_hbm, h_ref, w_ref, o_ref, ybuf, sem, *, tc, T):
    i = pl.program_id(0)
    n = pl.num_programs(0)

    def issue(blk, slot):
        def body(t, c):
            for kk in range(TOP_K):
                d = dest_ref[kk * T + blk * tc + t]
                src = y_hbm.at[pl.ds(pl.multiple_of(d * ROW_CHUNKS, ROW_CHUNKS), ROW_CHUNKS)]
                dst = ybuf.at[slot, pl.ds(pl.multiple_of((kk * tc + t) * ROW_CHUNKS, ROW_CHUNKS), ROW_CHUNKS)]
                pltpu.make_async_copy(src, dst, sem.at[slot]).start()
            return c
        lax.fori_loop(0, tc, body, 0)

    @pl.when(i == 0)
    def _():
        issue(0, 0)

    @pl.when(i + 1 < n)
    def _():
        issue(i + 1, (i + 1) % 2)

    slot = i % 2
    pltpu.make_async_copy(y_hbm.at[pl.ds(0, TOP_K * tc * ROW_CHUNKS)], ybuf.at[slot], sem.at[slot]).wait()
    w = w_ref[...]
    for s in range(ROW_CHUNKS):
        acc = h_ref[:, s * LANES:(s + 1) * LANES]
        for kk in range(TOP_K):
            yk = ybuf[slot, pl.ds(kk * tc * ROW_CHUNKS + s, tc, stride=ROW_CHUNKS), :]
            acc = acc + yk * w[:, kk:kk + 1]
        o_ref[:, s * LANES:(s + 1) * LANES] = acc


def _combine(dest_flat, y8, h2, w_tok, tc=256):
    T = h2.shape[0]
    grid_spec = pltpu.PrefetchScalarGridSpec(
        num_scalar_prefetch=1,
        grid=(T // tc,),
        in_specs=[pl.BlockSpec(memory_space=pl.ANY),
                  pl.BlockSpec((tc, D_MODEL), lambda i, d: (i, 0)),
                  pl.BlockSpec((tc, TOP_K), lambda i, d: (i, 0))],
        out_specs=pl.BlockSpec((tc, D_MODEL), lambda i, d: (i, 0)),
        scratch_shapes=[pltpu.VMEM((2, TOP_K * tc * ROW_CHUNKS, LANES), F32),
                        pltpu.SemaphoreType.DMA((2,))],
    )
    return pl.pallas_call(
        functools.partial(_combine_kernel, tc=tc, T=T),
        grid_spec=grid_spec,
        out_shape=jax.ShapeDtypeStruct((T, D_MODEL), F32),
        compiler_params=_cparams("arbitrary"),
        name="combine",
    )(dest_flat, y8, h2, w_tok)


def _final_norm_kernel(h_ref, g_ref, o_ref):
    x = h_ref[...]
    o_ref[...] = x * lax.rsqrt(jnp.mean(x * x, axis=-1, keepdims=True) + EPS) * g_ref[...]


def _final_norm(h2, g, tm=1024):
    T = h2.shape[0]
    return pl.pallas_call(
        _final_norm_kernel,
        grid=(T // tm,),
        in_specs=[pl.BlockSpec((tm, D_MODEL), lambda i: (i, 0)),
                  pl.BlockSpec((1, D_MODEL), lambda i: (0, 0))],
        out_specs=pl.BlockSpec((tm, D_MODEL), lambda i: (i, 0)),
        out_shape=jax.ShapeDtypeStruct((T, D_MODEL), F32),
        compiler_params=_cparams("parallel"),
        name="final_norm",
    )(h2, g)


def _prep_w_in(w_in):
    o_g = 4 * M_WIDTH
    o_cq = o_g + 2 * M_HEADS
    o_ckv = o_cq + A_QRANK
    o_kr = o_ckv + A_KVRANK
    o_up = o_kr + A_ROPE
    z = lambda n: jnp.zeros(w_in.shape[:-1] + (n,), w_in.dtype)
    small = jnp.concatenate([z(SMALL_KR), w_in[..., o_kr:o_up], w_in[..., o_g:o_cq],
                             z(LANES - SMALL_GATE - 2 * M_HEADS)], axis=-1)
    return jnp.concatenate([w_in[..., 0:o_g], w_in[..., o_cq:o_ckv], w_in[..., o_up:o_up + P_WIDTH],
                            w_in[..., o_ckv:o_kr], small], axis=-1).astype(BF16)


def _rope_tables(seq):
    inv = ROPE_THETA ** (-jnp.arange(0, A_ROPE, 2, dtype=F32) / A_ROPE)
    ang = jnp.arange(seq, dtype=F32)[:, None] * inv[None, :]
    cos, sin = jnp.cos(ang), jnp.sin(ang)
    half = A_ROPE // 2
    zeros = lambda n: jnp.zeros((seq, n), F32)
    ones = lambda n: jnp.ones((seq, n), F32)
    tail = LANES - A_NOPE - A_ROPE
    cq_t = jnp.concatenate([ones(A_NOPE), cos, cos, zeros(tail)], axis=1)
    ck_t = jnp.concatenate([zeros(A_NOPE), cos, cos, zeros(tail)], axis=1)
    s1_t = jnp.concatenate([zeros(A_NOPE), -sin, zeros(half), zeros(tail)], axis=1)
    s2_t = jnp.concatenate([zeros(A_NOPE), zeros(half), sin, zeros(tail)], axis=1)
    return cq_t, ck_t, s1_t, s2_t


def kernel(x, norm1_g, w_in, conv_w, conv_b, gate_b, mlstm_norm_g, q_norm_g, kv_norm_g, w_uq, w_ukv,
           w_pool, pool_scale, w_out, norm2_g, w_router, b_router, w_gate_up, b_gate_up, w_down, b_down,
           final_norm_g):
    B, S, D = x.shape
    depth = w_in.shape[0]
    T = B * S
    nb = (T * TOP_K) // MOE_ROWS + N_EXPERTS
    nb_pad = -(-nb // LANES) * LANES

    w_in_p = _prep_w_in(w_in)
    wq = w_uq.reshape(depth, A_QRANK, A_HEADS, A_NOPE + A_ROPE)
    wq = jnp.pad(wq, ((0, 0), (0, 0), (0, 0), (0, LANES - A_NOPE - A_ROPE)))
    wq = wq.reshape(depth, A_QRANK, A_HEADS * LANES).astype(BF16)
    wkv = w_ukv.reshape(depth, A_KVRANK, A_HEADS, A_NOPE + A_VDIM)
    wk = jnp.pad(wkv[..., :A_NOPE], ((0, 0), (0, 0), (0, 0), (0, LANES - A_NOPE)))
    wk = wk.reshape(depth, A_KVRANK, A_HEADS * LANES).astype(BF16)
    wv = wkv[..., A_NOPE:].reshape(depth, A_KVRANK, A_WIDTH).astype(BF16)
    gsz = P_WIDTH // len(P_WINDOWS)
    w_pool_bd = jnp.zeros((depth, P_WIDTH, P_WIDTH), F32)
    for gi in range(len(P_WINDOWS)):
        w_pool_bd = w_pool_bd.at[:, gi * gsz:(gi + 1) * gsz, gi * gsz:(gi + 1) * gsz].set(w_pool[:, gi])
    w_pool_bd = w_pool_bd.astype(BF16)
    w_out16 = w_out.astype(BF16)
    w_router_t = jnp.swapaxes(w_router, 1, 2)
    gate_b_col = jnp.pad(gate_b, ((0, 0), (SMALL_GATE, LANES - SMALL_GATE - 2 * M_HEADS)))
    cq_t, ck_t, s1_t, s2_t = _rope_tables(S)
    tr = 512
    tri = (jnp.arange(tr)[:, None] < jnp.arange(tr)[None, :]).astype(BF16)

    h = x.reshape(T, D)
    for l in range(depth):
        proj = _inproj(h, norm1_g[l][None, :], w_in_p[l])
        gates_t = proj[:, PROJ_SMALL + SMALL_GATE:PROJ_SMALL + SMALL_GATE + 2 * M_HEADS]
        gates_t = gates_t.reshape(B, S // M_CHUNK, M_CHUNK, 2 * M_HEADS).transpose(0, 1, 3, 2)
        y_m = _mlstm(proj, gates_t, conv_w[l], conv_b[l][None, :], gate_b_col[l][None, :],
                     gate_b[l][:, None], mlstm_norm_g[l][None, :], B, S)
        q16, k16, v16 = _mla_prep(proj, q_norm_g[l][None, :], kv_norm_g[l][None, :], wq[l], wk[l], wv[l],
                                  cq_t, ck_t, s1_t, s2_t, B, S)
        y_a = _attention(q16, k16, v16, B, S)
        y_p = _pool(proj, w_pool_bd[l], pool_scale[l][None, :], B, S)
        h, xn8, logits_t = _outproj(y_m, y_a, y_p, h, w_out16[l], norm2_g[l][None, :],
                                    w_router_t[l], b_router[l][:, None])
        eidx, wts, rank, counts = _router(logits_t, tri, tr)
        dest, blk_e, blk_cnt, nblk = _meta(counts, eidx, rank, nb_pad)
        tok_of = jnp.broadcast_to(jnp.arange(T, dtype=jnp.int32)[None, :], (TOP_K, T))
        row_tok = jnp.zeros((nb * MOE_ROWS,), jnp.int32).at[dest.reshape(-1)].set(tok_of.reshape(-1))
        y8 = _gmm(blk_e[0], blk_cnt[0], nblk[0], row_tok, xn8, w_gate_up[l], b_gate_up[l][:, None, :],
                  w_down[l], b_down[l][:, None, :], nb)
        h = _combine(dest.reshape(-1), y8, h, wts.T)
    return _final_norm(h, final_norm_g[None, :]).reshape(B, S, D)
```

```python
import functools

import jax
import jax.numpy as jnp
import numpy as np
from jax import lax
from jax.experimental import pallas as pl
from jax.experimental.pallas import tpu as pltpu

F32 = jnp.float32
BF16 = jnp.bfloat16
HIGHEST = lax.Precision.HIGHEST

D_MODEL = 1024
M_HEADS = 4
M_HEAD_DIM = 64
M_WIDTH = 256
M_CONV = 4
M_CHUNK = 64
A_HEADS = 8
A_NOPE = 64
A_ROPE = 32
A_VDIM = 64
A_QRANK = 256
A_KVRANK = 128
A_WIDTH = 512
ROPE_THETA = 10000.0
P_WINDOWS = (2, 4, 8, 16)
P_WIDTH = 256
N_EXPERTS = 32
TOP_K = 4
D_FF = 1024
SWIGLU_LIMIT = 7.0
SWIGLU_ALPHA = 1.702
EPS = 1e-6

LANES = 128
SUBLANES = 8
ROW_CHUNKS = D_MODEL // LANES

PROJ_QKVO = 0
PROJ_CQ = 1024
PROJ_UP = 1280
PROJ_CKV = 1536
PROJ_SMALL = 1664
PROJ_WIDTH = 1792
SMALL_KR = 64
SMALL_GATE = 96

MOE_ROWS = 256
VMEM_LIMIT = 56 * 1024 * 1024


def _cparams(*sem):
    return pltpu.CompilerParams(dimension_semantics=sem, vmem_limit_bytes=VMEM_LIMIT)


def _sigmoid(x):
    return 1.0 / (1.0 + jnp.exp(-x))


def _log_sigmoid(x):
    return jnp.minimum(x, 0.0) - jnp.log(1.0 + jnp.exp(-jnp.abs(x)))


def _dot(a, b, **kw):
    return jnp.dot(a, b, preferred_element_type=F32, **kw)


def _dot_nt(a, b, **kw):
    return lax.dot_general(a, b, (((1,), (1,)), ((), ())), preferred_element_type=F32, **kw)


def _dot_tn(a, b, **kw):
    return lax.dot_general(a, b, (((0,), (0,)), ((), ())), preferred_element_type=F32, **kw)


def _inproj_kernel(h_ref, g_ref, w_ref, o_ref):
    x = h_ref[...]
    ms = jnp.mean(x * x, axis=-1, keepdims=True)
    xn = x * lax.rsqrt(ms + EPS) * g_ref[...]
    o_ref[...] = _dot(xn.astype(BF16), w_ref[...])


def _inproj(h2, g, w, tm=512):
    T = h2.shape[0]
    return pl.pallas_call(
        _inproj_kernel,
        grid=(T // tm,),
        in_specs=[pl.BlockSpec((tm, D_MODEL), lambda i: (i, 0)),
                  pl.BlockSpec((1, D_MODEL), lambda i: (0, 0)),
                  pl.BlockSpec((D_MODEL, PROJ_WIDTH), lambda i: (0, 0))],
        out_specs=pl.BlockSpec((tm, PROJ_WIDTH), lambda i: (i, 0)),
        out_shape=jax.ShapeDtypeStruct((T, PROJ_WIDTH), F32),
        compiler_params=_cparams("parallel"),
        name="inproj",
    )(h2, g, w)


def _mlstm_kernel(proj_ref, small_ref, gt_ref, cw_ref, cb_ref, gbc_ref, gbr_ref, ng_ref, o_ref,
                  xpad_ref, ct_ref, n_ref, m_ref):
    S = proj_ref.shape[0]
    L = M_CHUNK
    nc = S // L
    W = M_WIDTH
    xpad_ref[0:SUBLANES, :] = jnp.zeros((SUBLANES, 2 * W), F32)
    xpad_ref[SUBLANES:, :] = proj_ref[:, 0:2 * W]
    ct_ref[...] = jnp.zeros_like(ct_ref)
    n_ref[...] = jnp.zeros_like(n_ref)
    m_ref[...] = jnp.zeros_like(m_ref)

    lane_head = lax.broadcasted_iota(jnp.int32, (1, W), 1) // M_HEAD_DIM
    masks = [(lane_head == h).astype(F32) for h in range(M_HEADS)]
    rh = lax.broadcasted_iota(jnp.int32, (W, W), 0) // M_HEAD_DIM
    chd = lax.broadcasted_iota(jnp.int32, (W, W), 1) // M_HEAD_DIM
    same_head = (rh == chd).astype(F32)
    ri = lax.broadcasted_iota(jnp.int32, (L, L), 0)
    ci = lax.broadcasted_iota(jnp.int32, (L, L), 1)
    causal = ri >= ci
    tril = causal.astype(F32)
    triu = (ri <= ci).astype(F32)
    cw = cw_ref[...]
    cb = cb_ref[...]
    gbc = gbc_ref[...]
    gbr = gbr_ref[...]
    ng = ng_ref[...]

    def body(c, carry):
        r0 = pl.multiple_of(c * L, L)
        win = xpad_ref[pl.ds(r0, L + SUBLANES), :]
        acc = jnp.zeros((L, 2 * W), F32) + cb
        for j in range(M_CONV):
            s = M_CONV - 1 - j
            xs = win if s == 0 else pltpu.roll(win, s, axis=0)
            acc = acc + xs[SUBLANES:, :] * cw[j:j + 1, :]
        qk = acc * _sigmoid(acc)
        q = qk[:, 0:W]
        k = qk[:, W:2 * W] * (M_HEAD_DIM ** -0.5)
        v = proj_ref[pl.ds(r0, L), 2 * W:3 * W]
        og = proj_ref[pl.ds(r0, L), 3 * W:4 * W]
        sm = small_ref[pl.ds(r0, L), :] + gbc
        gt = gt_ref[0, c] + gbr
        i_col_all = sm[:, SMALL_GATE:SMALL_GATE + M_HEADS]
        logf_col = _log_sigmoid(sm[:, SMALL_GATE + M_HEADS:SMALL_GATE + 2 * M_HEADS])
        g_col_all = _dot(tril, logf_col, precision=HIGHEST)
        i_row_all = gt[0:M_HEADS, :]
        logf_row = _log_sigmoid(gt[M_HEADS:2 * M_HEADS, :])
        g_row_all = _dot(logf_row, triu, precision=HIGHEST)

        q16 = q.astype(BF16)
        k16 = k.astype(BF16)
        v16 = v.astype(BF16)
        num_intra = jnp.zeros((L, W), F32)
        w_inter_f = jnp.zeros((L, W), F32)
        rowsum_f = jnp.zeros((L, W), F32)
        floor_f = jnp.zeros((L, W), F32)
        wa_f = jnp.zeros((L, W), F32)
        dec_f = jnp.zeros((1, W), F32)
        for h in range(M_HEADS):
            mk = masks[h]
            g_col = g_col_all[:, h:h + 1]
            i_col = i_col_all[:, h:h + 1]
            g_row = g_row_all[h:h + 1, :]
            i_row = i_row_all[h:h + 1, :]
            m_old = m_ref[0:1, h:h + 1]
            dmat = jnp.where(causal, g_col - g_row + i_row, -jnp.inf)
            inter = g_col + m_old
            m_row = jnp.maximum(inter, jnp.max(dmat, axis=-1, keepdims=True))
            sc = _dot_nt((q * mk).astype(BF16), k16) * jnp.exp(dmat - m_row)
            w_inter = jnp.exp(inter - m_row)
            num_intra = num_intra + _dot(sc.astype(BF16), v16) * mk
            w_inter_f = w_inter_f + w_inter * mk
            rowsum_f = rowsum_f + jnp.sum(sc, axis=-1, keepdims=True) * mk
            floor_f = floor_f + jnp.exp(-m_row) * mk
            g_end = g_col[L - 1:L, :]
            a = g_end - g_col + i_col
            m_new = jnp.maximum(g_end + m_old, jnp.max(a, axis=0, keepdims=True))
            wa_f = wa_f + jnp.exp(a - m_new) * mk
            dec_f = dec_f + jnp.exp(g_end + m_old - m_new) * mk
            m_ref[0:1, h:h + 1] = m_new

        ct = ct_ref[...]
        nvec = n_ref[...]
        q_c = _dot(q16, ct.astype(BF16))
        q_n = _dot(q * nvec, same_head, precision=HIGHEST)
        num = w_inter_f * q_c + num_intra
        den = w_inter_f * q_n + rowsum_f
        hv = num / jnp.maximum(jnp.abs(den), floor_f)
        ms = _dot(hv * hv, same_head, precision=HIGHEST) * (1.0 / M_HEAD_DIM)
        y = hv * lax.rsqrt(ms + EPS) * ng * _sigmoid(og)
        o_ref[pl.ds(r0, L), :] = y

        upd = _dot_tn(k16, (wa_f * v).astype(BF16))
        ct_ref[...] = dec_f * ct + upd * same_head
        n_ref[...] = dec_f * nvec + jnp.sum(wa_f * k, axis=0, keepdims=True)
        return carry

    lax.fori_loop(0, nc, body, 0)


def _mlstm(proj, gates_t, cw, cb, gbc, gbr, ng, B, S):
    T = B * S
    nc = S // M_CHUNK
    return pl.pallas_call(
        _mlstm_kernel,
        grid=(B,),
        in_specs=[pl.BlockSpec((S, 4 * M_WIDTH), lambda b: (b, 0)),
                  pl.BlockSpec((S, LANES), lambda b: (b, PROJ_SMALL // LANES)),
                  pl.BlockSpec((1, nc, 2 * M_HEADS, M_CHUNK), lambda b: (b, 0, 0, 0)),
                  pl.BlockSpec((M_CONV, 2 * M_WIDTH), lambda b: (0, 0)),
                  pl.BlockSpec((1, 2 * M_WIDTH), lambda b: (0, 0)),
                  pl.BlockSpec((1, LANES), lambda b: (0, 0)),
                  pl.BlockSpec((2 * M_HEADS, 1), lambda b: (0, 0)),
                  pl.BlockSpec((1, M_WIDTH), lambda b: (0, 0))],
        out_specs=pl.BlockSpec((S, M_WIDTH), lambda b: (b, 0)),
        out_shape=jax.ShapeDtypeStruct((T, M_WIDTH), F32),
        scratch_shapes=[pltpu.VMEM((S + SUBLANES, 2 * M_WIDTH), F32),
                        pltpu.VMEM((M_WIDTH, M_WIDTH), F32),
                        pltpu.VMEM((1, M_WIDTH), F32),
                        pltpu.VMEM((1, LANES), F32)],
        compiler_params=_cparams("parallel"),
        name="mlstm",
    )(proj, proj, gates_t, cw, cb, gbc, gbr, ng)


POOL_HALO = 16
POOL_TILE = 256


def _pool_kernel(u_ref, w_ref, sc_ref, o_ref, upad_ref):
    S = u_ref.shape[0]
    upad_ref[0:POOL_HALO, :] = jnp.zeros((POOL_HALO, P_WIDTH), F32)
    upad_ref[POOL_HALO:, :] = u_ref[...]
    grp = lax.broadcasted_iota(jnp.int32, (1, P_WIDTH), 1) // (P_WIDTH // len(P_WINDOWS))
    win_lane = jnp.zeros((1, P_WIDTH), jnp.int32)
    for gi, wn in enumerate(P_WINDOWS):
        win_lane = jnp.where(grp == gi, wn, win_lane)
    w = w_ref[...]
    scale = sc_ref[...]
    rows = POOL_TILE + POOL_HALO

    def body(r, carry):
        r0 = pl.multiple_of(r * POOL_TILE, POOL_TILE)
        a = upad_ref[pl.ds(r0, rows), :]
        sums = []
        cur = a
        span = 1
        for _ in P_WINDOWS:
            cur = cur + pltpu.roll(cur, span, axis=0)
            span *= 2
            sums.append(cur)
        sel = sums[-1]
        for gi in range(len(P_WINDOWS) - 1):
            sel = jnp.where(grp == gi, sums[gi], sel)
        sel = sel[POOL_HALO:, :]
        u = a[POOL_HALO:, :]
        t = r0 + lax.broadcasted_iota(jnp.int32, (POOL_TILE, P_WIDTH), 0)
        cnt = jnp.minimum(t + 1, win_lane).astype(F32)
        pooled = sel / cnt - u
        o_ref[pl.ds(r0, POOL_TILE), :] = _dot(pooled.astype(BF16), w) * scale
        return carry

    lax.fori_loop(0, S // POOL_TILE, body, 0)


def _pool(proj, w_bd, scale, B, S):
    T = B * S
    return pl.pallas_call(
        _pool_kernel,
        grid=(B,),
        in_specs=[pl.BlockSpec((S, P_WIDTH), lambda b: (b, PROJ_UP // P_WIDTH)),
                  pl.BlockSpec((P_WIDTH, P_WIDTH), lambda b: (0, 0)),
                  pl.BlockSpec((1, P_WIDTH), lambda b: (0, 0))],
        out_specs=pl.BlockSpec((S, P_WIDTH), lambda b: (b, 0)),
        out_shape=jax.ShapeDtypeStruct((T, P_WIDTH), F32),
        scratch_shapes=[pltpu.VMEM((S + POOL_HALO, P_WIDTH), F32)],
        compiler_params=_cparams("parallel"),
        name="pool",
    )(proj, w_bd, scale)


def _rope(x, c, s1, s2):
    return x * c + pltpu.roll(x, LANES - A_ROPE // 2, axis=1) * s1 + pltpu.roll(x, A_ROPE // 2, axis=1) * s2


def _mla_prep_kernel(cq_ref, ckv_ref, small_ref, qg_ref, kvg_ref, wq_ref, wk_ref, wv_ref,
                     vone_ref, cq_t_ref, ck_t_ref, s1_ref, s2_ref, q_ref, k_ref, v_ref):
    def rms(x, g):
        return x * lax.rsqrt(jnp.mean(x * x, axis=-1, keepdims=True) + EPS) * g

    cqn = rms(cq_ref[...], qg_ref[...]).astype(BF16)
    ckvn = rms(ckv_ref[...], kvg_ref[...]).astype(BF16)
    scale = (A_NOPE + A_ROPE) ** -0.5
    qf = _dot(cqn, wq_ref[...]) * scale
    kf = _dot(ckvn, wk_ref[...])
    v_ref[...] = (_dot(ckvn, wv_ref[...]) + vone_ref[...]).astype(BF16)
    cqt = cq_t_ref[...]
    s1 = s1_ref[...]
    s2 = s2_ref[...]
    krot = _rope(small_ref[...], ck_t_ref[...], s1, s2)
    for h in range(A_HEADS):
        sl = slice(h * LANES, (h + 1) * LANES)
        q_ref[:, sl] = _rope(qf[:, sl], cqt, s1, s2).astype(BF16)
        k_ref[:, sl] = (kf[:, sl] + krot).astype(BF16)


def _mla_prep(proj, qg, kvg, wq, wk, wv, vone, cq_t, ck_t, s1_t, s2_t, B, S, ts=512):
    T = B * S
    nst = S // ts
    hw = A_HEADS * LANES
    return pl.pallas_call(
        _mla_prep_kernel,
        grid=(B, nst),
        in_specs=[pl.BlockSpec((ts, A_QRANK), lambda b, s: (b * nst + s, PROJ_CQ // A_QRANK)),
                  pl.BlockSpec((ts, A_KVRANK), lambda b, s: (b * nst + s, PROJ_CKV // A_KVRANK)),
                  pl.BlockSpec((ts, LANES), lambda b, s: (b * nst + s, PROJ_SMALL // LANES)),
                  pl.BlockSpec((1, A_QRANK), lambda b, s: (0, 0)),
                  pl.BlockSpec((1, A_KVRANK), lambda b, s: (0, 0)),
                  pl.BlockSpec((A_QRANK, hw), lambda b, s: (0, 0)),
                  pl.BlockSpec((A_KVRANK, hw), lambda b, s: (0, 0)),
                  pl.BlockSpec((A_KVRANK, hw), lambda b, s: (0, 0)),
                  pl.BlockSpec((1, hw), lambda b, s: (0, 0)),
                  pl.BlockSpec((ts, LANES), lambda b, s: (s, 0)),
                  pl.BlockSpec((ts, LANES), lambda b, s: (s, 0)),
                  pl.BlockSpec((ts, LANES), lambda b, s: (s, 0)),
                  pl.BlockSpec((ts, LANES), lambda b, s: (s, 0))],
        out_specs=[pl.BlockSpec((ts, hw), lambda b, s: (b * nst + s, 0)),
                   pl.BlockSpec((ts, hw), lambda b, s: (b * nst + s, 0)),
                   pl.BlockSpec((ts, hw), lambda b, s: (b * nst + s, 0))],
        out_shape=[jax.ShapeDtypeStruct((T, hw), BF16),
                   jax.ShapeDtypeStruct((T, hw), BF16),
                   jax.ShapeDtypeStruct((T, hw), BF16)],
        compiler_params=_cparams("parallel", "parallel"),
        name="mla_prep",
    )(proj, proj, proj, qg, kvg, wq, wk, wv, vone, cq_t, ck_t, s1_t, s2_t)


def _attn_kernel(q_ref, k_ref, v_ref, o_ref, *, tq):
    qi = pl.program_id(2)
    ri = lax.broadcasted_iota(jnp.int32, (tq, tq), 0)
    ci = lax.broadcasted_iota(jnp.int32, (tq, tq), 1)
    diag_ok = ri >= ci
    sls = [slice(hh * LANES, (hh + 1) * LANES) for hh in range(2)]
    qs = [q_ref[:, sl] for sl in sls]

    def step(kb, carry, masked):
        k0 = pl.multiple_of(kb * tq, tq)
        new = []
        for hh in range(2):
            m, acc = carry[hh]
            s = _dot_nt(qs[hh], k_ref[pl.ds(k0, tq), sls[hh]])
            if masked:
                s = jnp.where(diag_ok, s, -jnp.inf)
            m_new = jnp.maximum(m, jnp.max(s, axis=-1, keepdims=True))
            p = jnp.exp(s - m_new)
            acc = jnp.exp(m - m_new) * acc + _dot(p.astype(BF16), v_ref[pl.ds(k0, tq), sls[hh]])
            new.append((m_new, acc))
        return tuple(new)

    init = tuple((jnp.full((tq, 1), -jnp.inf, F32), jnp.zeros((tq, LANES), F32)) for _ in range(2))
    carry = lax.fori_loop(0, qi, functools.partial(step, masked=False), init)
    (_, acc0), (_, acc1) = step(qi, carry, True)
    lane = lax.broadcasted_iota(jnp.int32, (tq, LANES), 1)
    acc = jnp.where(lane < A_VDIM, acc0, acc1)
    den = jnp.where(lane < A_VDIM, pltpu.roll(acc0, A_VDIM, axis=1), pltpu.roll(acc1, A_VDIM, axis=1))
    o_ref[...] = (acc / den).astype(o_ref.dtype)


def _attention(q, k, v, B, S, tq=512):
    T = B * S
    nq = S // tq
    return pl.pallas_call(
        functools.partial(_attn_kernel, tq=tq),
        grid=(B, A_HEADS // 2, nq),
        in_specs=[pl.BlockSpec((tq, 2 * LANES), lambda b, p, i: (b * nq + i, p)),
                  pl.BlockSpec((S, 2 * LANES), lambda b, p, i: (b, p)),
                  pl.BlockSpec((S, 2 * LANES), lambda b, p, i: (b, p))],
        out_specs=pl.BlockSpec((tq, LANES), lambda b, p, i: (b * nq + i, p)),
        out_shape=jax.ShapeDtypeStruct((T, A_WIDTH), BF16),
        compiler_params=_cparams("parallel", "parallel", "arbitrary"),
        name="attention",
    )(q, k, v)


def _outproj_kernel(ym_ref, ya_ref, yp_ref, h_ref, w_ref, g_ref, wr_ref, br_ref,
                    hn_ref, xn_ref, lg_ref):
    tm = h_ref.shape[0]
    mix = _dot(ym_ref[...].astype(BF16), w_ref[0:M_WIDTH, :])
    mix = mix + _dot(ya_ref[...], w_ref[M_WIDTH:M_WIDTH + A_WIDTH, :])
    mix = mix + _dot(yp_ref[...].astype(BF16), w_ref[M_WIDTH + A_WIDTH:, :])
    hn = h_ref[...] + mix
    hn_ref[...] = hn
    xn = hn * lax.rsqrt(jnp.mean(hn * hn, axis=-1, keepdims=True) + EPS) * g_ref[...]
    lg_ref[...] = _dot_nt(wr_ref[...], xn, precision=HIGHEST) + br_ref[...]
    for s in range(ROW_CHUNKS):
        xn_ref[pl.ds(s, tm, stride=ROW_CHUNKS), :] = xn[:, s * LANES:(s + 1) * LANES]


def _outproj(ym, ya, yp, h2, w, g, wr_t, br, tm=512):
    T = h2.shape[0]
    return pl.pallas_call(
        _outproj_kernel,
        grid=(T // tm,),
        in_specs=[pl.BlockSpec((tm, M_WIDTH), lambda i: (i, 0)),
                  pl.BlockSpec((tm, A_WIDTH), lambda i: (i, 0)),
                  pl.BlockSpec((tm, P_WIDTH), lambda i: (i, 0)),
                  pl.BlockSpec((tm, D_MODEL), lambda i: (i, 0)),
                  pl.BlockSpec((D_MODEL, D_MODEL), lambda i: (0, 0)),
                  pl.BlockSpec((1, D_MODEL), lambda i: (0, 0)),
                  pl.BlockSpec((N_EXPERTS, D_MODEL), lambda i: (0, 0)),
                  pl.BlockSpec((N_EXPERTS, 1), lambda i: (0, 0))],
        out_specs=[pl.BlockSpec((tm, D_MODEL), lambda i: (i, 0)),
                   pl.BlockSpec((tm * ROW_CHUNKS, LANES), lambda i: (i, 0)),
                   pl.BlockSpec((N_EXPERTS, tm), lambda i: (0, i))],
        out_shape=[jax.ShapeDtypeStruct((T, D_MODEL), F32),
                   jax.ShapeDtypeStruct((T * ROW_CHUNKS, LANES), F32),
                   jax.ShapeDtypeStruct((N_EXPERTS, T), F32)],
        compiler_params=_cparams("parallel"),
        name="outproj",
    )(ym, ya, yp, h2, w, g, wr_t, br)


def _router_kernel(lg_ref, tri_ref, e_ref, w_ref, r_ref, cnt_ref, carry_ref):
    tr = lg_ref.shape[1]

    @pl.when(pl.program_id(0) == 0)
    def _():
        carry_ref[...] = jnp.zeros_like(carry_ref)

    x = lg_ref[...]
    eio = lax.broadcasted_iota(jnp.int32, (N_EXPERTS, tr), 0).astype(F32)
    picked = jnp.zeros((N_EXPERTS, tr), F32)
    vals = []
    idxs = []
    for _ in range(TOP_K):
        mx = jnp.max(x, axis=0, keepdims=True)
        idx = jnp.min(jnp.where(x == mx, eio, float(N_EXPERTS)), axis=0, keepdims=True)
        hit = eio == idx
        vals.append(mx)
        idxs.append(idx)
        picked = picked + hit.astype(F32)
        x = jnp.where(hit, -jnp.inf, x)
    exps = [jnp.exp(vv - vals[0]) for vv in vals]
    tot = exps[0] + exps[1] + exps[2] + exps[3]
    before = _dot(picked.astype(BF16), tri_ref[...]) + carry_ref[:, 0:1]
    for kk in range(TOP_K):
        e_ref[kk:kk + 1, :] = idxs[kk].astype(jnp.int32)
        w_ref[kk:kk + 1, :] = exps[kk] / tot
        rk = jnp.sum(jnp.where(eio == idxs[kk], before, 0.0), axis=0, keepdims=True)
        r_ref[kk:kk + 1, :] = rk.astype(jnp.int32)
    carry_ref[...] = carry_ref[...] + jnp.sum(picked, axis=1, keepdims=True)
    cnt_ref[...] = carry_ref[...]


def _router(logits_t, tri, tr=512):
    T = logits_t.shape[1]
    return pl.pallas_call(
        _router_kernel,
        grid=(T // tr,),
        in_specs=[pl.BlockSpec((N_EXPERTS, tr), lambda i: (0, i)),
                  pl.BlockSpec((tr, tr), lambda i: (0, 0))],
        out_specs=[pl.BlockSpec((TOP_K, tr), lambda i: (0, i)),
                   pl.BlockSpec((TOP_K, tr), lambda i: (0, i)),
                   pl.BlockSpec((TOP_K, tr), lambda i: (0, i)),
                   pl.BlockSpec((N_EXPERTS, LANES), lambda i: (0, 0))],
        out_shape=[jax.ShapeDtypeStruct((TOP_K, T), jnp.int32),
                   jax.ShapeDtypeStruct((TOP_K, T), F32),
                   jax.ShapeDtypeStruct((TOP_K, T), jnp.int32),
                   jax.ShapeDtypeStruct((N_EXPERTS, LANES), F32)],
        scratch_shapes=[pltpu.VMEM((N_EXPERTS, LANES), F32)],
        compiler_params=_cparams("arbitrary"),
        name="router",
    )(logits_t, tri)


def _meta_kernel(cnt_ref, e_ref, r_ref, dest_ref, be_ref, bc_ref, nb_ref, *, nb_pad):
    cnt = cnt_ref[...]
    padded = jnp.floor((cnt + (MOE_ROWS - 1)) * (1.0 / MOE_ROWS)) * MOE_ROWS
    ri = lax.broadcasted_iota(jnp.int32, (N_EXPERTS, N_EXPERTS), 0)
    ci = lax.broadcasted_iota(jnp.int32, (N_EXPERTS, N_EXPERTS), 1)
    pad_end = _dot((ri >= ci).astype(F32), padded, precision=HIGHEST)
    pad_start = pad_end - padded
    e = e_ref[...]
    dest = r_ref[...]
    for ex in range(N_EXPERTS):
        ps = pad_start[ex:ex + 1, 0:1].astype(jnp.int32)
        dest = jnp.where(e == ex, dest + ps, dest)
    dest_ref[...] = dest
    blk0 = (lax.broadcasted_iota(jnp.int32, (N_EXPERTS, nb_pad), 1) * MOE_ROWS).astype(F32)
    be = jnp.sum((pad_end[:, 0:1] <= blk0).astype(F32), axis=0, keepdims=True)
    be = jnp.minimum(be, float(N_EXPERTS - 1))
    eio = lax.broadcasted_iota(jnp.int32, (N_EXPERTS, nb_pad), 0).astype(F32)
    seg_end = jnp.sum(jnp.where(eio == be, pad_start[:, 0:1] + cnt[:, 0:1], 0.0), axis=0, keepdims=True)
    bc = jnp.clip(seg_end - blk0[0:1, :], 0.0, float(MOE_ROWS))
    be_ref[...] = be.astype(jnp.int32)
    bc_ref[...] = bc.astype(jnp.int32)
    nb_ref[...] = (pad_end[N_EXPERTS - 1:N_EXPERTS, :] * (1.0 / MOE_ROWS)).astype(jnp.int32)


def _meta(counts, eidx, rank, nb_pad):
    T = eidx.shape[1]
    return pl.pallas_call(
        functools.partial(_meta_kernel, nb_pad=nb_pad),
        out_shape=[jax.ShapeDtypeStruct((TOP_K, T), jnp.int32),
                   jax.ShapeDtypeStruct((1, nb_pad), jnp.int32),
                   jax.ShapeDtypeStruct((1, nb_pad), jnp.int32),
                   jax.ShapeDtypeStruct((1, LANES), jnp.int32)],
        compiler_params=pltpu.CompilerParams(vmem_limit_bytes=VMEM_LIMIT),
        name="route_meta",
    )(counts, eidx, rank)


FF_CHUNK = 512


def _gmm_kernel(be_ref, bc_ref, nb_ref, tok_ref, x_hbm, wgu_ref, bgu_ref, wdn_ref, bdn_ref, y_ref,
                xbuf, wgu16, wdn16, sem):
    i = pl.program_id(0)
    nblk = nb_ref[0]
    bm = MOE_ROWS

    def row_copy(blk, slot, r):
        tok = tok_ref[blk * bm + r]
        src = x_hbm.at[pl.ds(pl.multiple_of(tok * ROW_CHUNKS, ROW_CHUNKS), ROW_CHUNKS)]
        dst = xbuf.at[slot, pl.ds(pl.multiple_of(r * ROW_CHUNKS, ROW_CHUNKS), ROW_CHUNKS)]
        return pltpu.make_async_copy(src, dst, sem.at[slot])

    def issue(blk, slot):
        def body(r, c):
            row_copy(blk, slot, r).start()
            return c
        lax.fori_loop(0, bm, body, 0)

    @pl.when(i == 0)
    def _():
        issue(0, 0)

    @pl.when(i + 1 < nblk)
    def _():
        issue(i + 1, (i + 1) % 2)

    @pl.when(i >= nblk)
    def _():
        y_ref[...] = jnp.zeros_like(y_ref)

    @pl.when(i < nblk)
    def _():
        slot = i % 2
        e_changed = jnp.logical_or(i == 0, be_ref[i] != be_ref[jnp.maximum(i - 1, 0)])

        @pl.when(e_changed)
        def _():
            wgu16[...] = wgu_ref[0].astype(BF16)
            wdn16[...] = wdn_ref[0].astype(BF16)

        pltpu.make_async_copy(x_hbm.at[pl.ds(0, bm * ROW_CHUNKS)], xbuf.at[slot], sem.at[slot]).wait()
        xs = [xbuf[slot, pl.ds(s, bm, stride=ROW_CHUNKS), :] for s in range(ROW_CHUNKS)]
        x = jnp.concatenate(xs, axis=1)
        valid = lax.broadcasted_iota(jnp.int32, (bm, 1), 0) < bc_ref[i]
        x16 = jnp.where(valid, x, 0.0).astype(BF16)
        acc = jnp.zeros((bm, D_MODEL), F32) + bdn_ref[0]
        for c in range(D_FF // FF_CHUNK):
            cs = slice(c * FF_CHUNK, (c + 1) * FF_CHUNK)
            us = slice(D_FF + c * FF_CHUNK, D_FF + (c + 1) * FF_CHUNK)
            gate = _dot(x16, wgu16[:, cs]) + bgu_ref[0, :, cs]
            up = _dot(x16, wgu16[:, us]) + bgu_ref[0, :, us]
            gate = jnp.minimum(gate, SWIGLU_LIMIT)
            up = jnp.clip(up, -SWIGLU_LIMIT, SWIGLU_LIMIT)
            act = (up + 1.0) * gate * _sigmoid(SWIGLU_ALPHA * gate)
            acc = acc + _dot(act.astype(BF16), wdn16[cs, :])
        for s in range(ROW_CHUNKS):
            y_ref[pl.ds(s, bm, stride=ROW_CHUNKS), :] = acc[:, s * LANES:(s + 1) * LANES]


def _gmm(blk_e, blk_cnt, nblk, row_tok, x8, wgu, bgu, wdn, bdn, nb, layer):
    bm = MOE_ROWS
    e0 = layer * N_EXPERTS

    def expert(i, be, nb_ref):
        return (e0 + be[jnp.minimum(i, jnp.maximum(nb_ref[0] - 1, 0))], 0, 0)

    grid_spec = pltpu.PrefetchScalarGridSpec(
        num_scalar_prefetch=4,
        grid=(nb,),
        in_specs=[pl.BlockSpec(memory_space=pl.ANY),
                  pl.BlockSpec((1, D_MODEL, 2 * D_FF), lambda i, be, bc, nbr, tk: expert(i, be, nbr)),
                  pl.BlockSpec((1, 1, 2 * D_FF), lambda i, be, bc, nbr, tk: expert(i, be, nbr)),
                  pl.BlockSpec((1, D_FF, D_MODEL), lambda i, be, bc, nbr, tk: expert(i, be, nbr)),
                  pl.BlockSpec((1, 1, D_MODEL), lambda i, be, bc, nbr, tk: expert(i, be, nbr))],
        out_specs=pl.BlockSpec((bm * ROW_CHUNKS, LANES), lambda i, be, bc, nbr, tk: (i, 0)),
        scratch_shapes=[pltpu.VMEM((2, bm * ROW_CHUNKS, LANES), F32),
                        pltpu.VMEM((D_MODEL, 2 * D_FF), BF16),
                        pltpu.VMEM((D_FF, D_MODEL), BF16),
                        pltpu.SemaphoreType.DMA((2,))],
    )
    return pl.pallas_call(
        _gmm_kernel,
        grid_spec=grid_spec,
        out_shape=jax.ShapeDtypeStruct((nb * bm * ROW_CHUNKS, LANES), F32),
        compiler_params=_cparams("arbitrary"),
        name="expert_gmm",
    )(blk_e, blk_cnt, nblk, row_tok, x8, wgu, bgu, wdn, bdn)


def _combine_kernel(dest_ref, y_hbm, h_ref, w_ref, o_ref, ybuf, sem, *, tc, T):
    i = pl.program_id(0)
    n = pl.num_programs(0)

    def issue(blk, slot):
        def body(t, c):
            for kk in range(TOP_K):
                d = dest_ref[kk * T + blk * tc + t]
                src = y_hbm.at[pl.ds(pl.multiple_of(d * ROW_CHUNKS, ROW_CHUNKS), ROW_CHUNKS)]
                dst = ybuf.at[slot, pl.ds(pl.multiple_of((kk * tc + t) * ROW_CHUNKS, ROW_CHUNKS), ROW_CHUNKS)]
                pltpu.make_async_copy(src, dst, sem.at[slot]).start()
            return c
        lax.fori_loop(0, tc, body, 0)

    @pl.when(i == 0)
    def _():
        issue(0, 0)

    @pl.when(i + 1 < n)
    def _():
        issue(i + 1, (i + 1) % 2)

    slot = i % 2
    pltpu.make_async_copy(y_hbm.at[pl.ds(0, TOP_K * tc * ROW_CHUNKS)], ybuf.at[slot], sem.at[slot]).wait()
    w = w_ref[...]
    for s in range(ROW_CHUNKS):
        acc = h_ref[:, s * LANES:(s + 1) * LANES]
        for kk in range(TOP_K):
            yk = ybuf[slot, pl.ds(kk * tc * ROW_CHUNKS + s, tc, stride=ROW_CHUNKS), :]
            acc = acc + yk * w[:, kk:kk + 1]
        o_ref[:, s * LANES:(s + 1) * LANES] = acc


def _combine(dest_flat, y8, h2, w_tok, tc=256):
    T = h2.shape[0]
    grid_spec = pltpu.PrefetchScalarGridSpec(
        num_scalar_prefetch=1,
        grid=(T // tc,),
        in_specs=[pl.BlockSpec(memory_space=pl.ANY),
                  pl.BlockSpec((tc, D_MODEL), lambda i, d: (i, 0)),
                  pl.BlockSpec((tc, TOP_K), lambda i, d: (i, 0))],
        out_specs=pl.BlockSpec((tc, D_MODEL), lambda i, d: (i, 0)),
        scratch_shapes=[pltpu.VMEM((2, TOP_K * tc * ROW_CHUNKS, LANES), F32),
                        pltpu.SemaphoreType.DMA((2,))],
    )
    return pl.pallas_call(
        functools.partial(_combine_kernel, tc=tc, T=T),
        grid_spec=grid_spec,
        out_shape=jax.ShapeDtypeStruct((T, D_MODEL), F32),
        compiler_params=_cparams("arbitrary"),
        name="combine",
    )(dest_flat, y8, h2, w_tok)


def _final_norm_kernel(h_ref, g_ref, o_ref):
    x = h_ref[...]
    o_ref[...] = x * lax.rsqrt(jnp.mean(x * x, axis=-1, keepdims=True) + EPS) * g_ref[...]


def _final_norm(h2, g, tm=1024):
    T = h2.shape[0]
    return pl.pallas_call(
        _final_norm_kernel,
        grid=(T // tm,),
        in_specs=[pl.BlockSpec((tm, D_MODEL), lambda i: (i, 0)),
                  pl.BlockSpec((1, D_MODEL), lambda i: (0, 0))],
        out_specs=pl.BlockSpec((tm, D_MODEL), lambda i: (i, 0)),
        out_shape=jax.ShapeDtypeStruct((T, D_MODEL), F32),
        compiler_params=_cparams("parallel"),
        name="final_norm",
    )(h2, g)


def _prep_w_in(w_in):
    o_g = 4 * M_WIDTH
    o_cq = o_g + 2 * M_HEADS
    o_ckv = o_cq + A_QRANK
    o_kr = o_ckv + A_KVRANK
    o_up = o_kr + A_ROPE
    z = lambda n: jnp.zeros(w_in.shape[:-1] + (n,), w_in.dtype)
    small = jnp.concatenate([z(SMALL_KR), w_in[..., o_kr:o_up], w_in[..., o_g:o_cq],
                             z(LANES - SMALL_GATE - 2 * M_HEADS)], axis=-1)
    return jnp.concatenate([w_in[..., 0:o_g], w_in[..., o_cq:o_ckv], w_in[..., o_up:o_up + P_WIDTH],
                            w_in[..., o_ckv:o_kr], small], axis=-1).astype(BF16)


def _rope_tables(seq):
    inv = ROPE_THETA ** (-jnp.arange(0, A_ROPE, 2, dtype=F32) / A_ROPE)
    ang = jnp.arange(seq, dtype=F32)[:, None] * inv[None, :]
    cos, sin = jnp.cos(ang), jnp.sin(ang)
    half = A_ROPE // 2
    zeros = lambda n: jnp.zeros((seq, n), F32)
    ones = lambda n: jnp.ones((seq, n), F32)
    tail = LANES - A_NOPE - A_ROPE
    cq_t = jnp.concatenate([ones(A_NOPE), cos, cos, zeros(tail)], axis=1)
    ck_t = jnp.concatenate([zeros(A_NOPE), cos, cos, zeros(tail)], axis=1)
    s1_t = jnp.concatenate([zeros(A_NOPE), -sin, zeros(half), zeros(tail)], axis=1)
    s2_t = jnp.concatenate([zeros(A_NOPE), zeros(half), sin, zeros(tail)], axis=1)
    return cq_t, ck_t, s1_t, s2_t


def kernel(x, norm1_g, w_in, conv_w, conv_b, gate_b, mlstm_norm_g, q_norm_g, kv_norm_g, w_uq, w_ukv,
           w_pool, pool_scale, w_out, norm2_g, w_router, b_router, w_gate_up, b_gate_up, w_down, b_down,
           final_norm_g):
    B, S, D = x.shape
    depth = w_in.shape[0]
    T = B * S
    nb = (T * TOP_K) // MOE_ROWS + N_EXPERTS
    nb_pad = -(-nb // LANES) * LANES

    w_in_p = _prep_w_in(w_in)
    wq = w_uq.reshape(depth, A_QRANK, A_HEADS, A_NOPE + A_ROPE)
    wq = jnp.pad(wq, ((0, 0), (0, 0), (0, 0), (0, LANES - A_NOPE - A_ROPE)))
    wq = wq.reshape(depth, A_QRANK, A_HEADS * LANES).astype(BF16)
    wkv = w_ukv.reshape(depth, A_KVRANK, A_HEADS, A_NOPE + A_VDIM)
    wk = jnp.pad(wkv[..., :A_NOPE], ((0, 0), (0, 0), (0, 0), (0, LANES - A_NOPE)))
    wk = wk.reshape(depth, A_KVRANK, A_HEADS * LANES).astype(BF16)
    wv_e = jnp.pad(wkv[:, :, 0::2, A_NOPE:], ((0, 0), (0, 0), (0, 0), (0, LANES - A_VDIM)))
    wv_o = jnp.pad(wkv[:, :, 1::2, A_NOPE:], ((0, 0), (0, 0), (0, 0), (LANES - A_VDIM, 0)))
    wv = jnp.stack([wv_e, wv_o], axis=3).reshape(depth, A_KVRANK, A_HEADS * LANES).astype(BF16)
    half = jnp.arange(A_HEADS * LANES) // A_VDIM
    vone = ((half % 4 == 1) | (half % 4 == 2)).astype(F32)[None, :]
    gsz = P_WIDTH // len(P_WINDOWS)
    w_pool_bd = jnp.zeros((depth, P_WIDTH, P_WIDTH), F32)
    for gi in range(len(P_WINDOWS)):
        w_pool_bd = w_pool_bd.at[:, gi * gsz:(gi + 1) * gsz, gi * gsz:(gi + 1) * gsz].set(w_pool[:, gi])
    w_pool_bd = w_pool_bd.astype(BF16)
    w_out16 = w_out.astype(BF16)
    w_router_t = jnp.swapaxes(w_router, 1, 2)
    gate_b_col = jnp.pad(gate_b, ((0, 0), (SMALL_GATE, LANES - SMALL_GATE - 2 * M_HEADS)))
    cq_t, ck_t, s1_t, s2_t = _rope_tables(S)
    tr = 512
    tri = (jnp.arange(tr)[:, None] < jnp.arange(tr)[None, :]).astype(BF16)

    wgu_all = w_gate_up.reshape(depth * N_EXPERTS, D_MODEL, 2 * D_FF)
    bgu_all = b_gate_up.reshape(depth * N_EXPERTS, 1, 2 * D_FF)
    wdn_all = w_down.reshape(depth * N_EXPERTS, D_FF, D_MODEL)
    bdn_all = b_down.reshape(depth * N_EXPERTS, 1, D_MODEL)

    h = x.reshape(T, D)
    for l in range(depth):
        proj = _inproj(h, norm1_g[l][None, :], w_in_p[l])
        gates_t = proj[:, PROJ_SMALL + SMALL_GATE:PROJ_SMALL + SMALL_GATE + 2 * M_HEADS]
        gates_t = gates_t.reshape(B, S // M_CHUNK, M_CHUNK, 2 * M_HEADS).transpose(0, 1, 3, 2)
        y_m = _mlstm(proj, gates_t, conv_w[l], conv_b[l][None, :], gate_b_col[l][None, :],
                     gate_b[l][:, None], mlstm_norm_g[l][None, :], B, S)
        q16, k16, v16 = _mla_prep(proj, q_norm_g[l][None, :], kv_norm_g[l][None, :], wq[l], wk[l], wv[l],
                                  vone, cq_t, ck_t, s1_t, s2_t, B, S)
        y_a = _attention(q16, k16, v16, B, S)
        y_p = _pool(proj, w_pool_bd[l], pool_scale[l][None, :], B, S)
        h, xn8, logits_t = _outproj(y_m, y_a, y_p, h, w_out16[l], norm2_g[l][None, :],
                                    w_router_t[l], b_router[l][:, None])
        eidx, wts, rank, counts = _router(logits_t, tri, tr)
        dest, blk_e, blk_cnt, nblk = _meta(counts, eidx, rank, nb_pad)
        tok_of = jnp.broadcast_to(jnp.arange(T, dtype=jnp.int32)[None, :], (TOP_K, T))
        row_tok = jnp.zeros((nb * MOE_ROWS,), jnp.int32).at[dest.reshape(-1)].set(tok_of.reshape(-1))
        y8 = _gmm(blk_e[0], blk_cnt[0], nblk[0], row_tok, xn8, wgu_all, bgu_all, wdn_all, bdn_all, nb, l)
        h = _combine(dest.reshape(-1), y8, h, wts.T)
    return _final_norm(h, final_norm_g[None, :]).reshape(B, S, D)
```

```python
import functools

import jax
import jax.numpy as jnp
import numpy as np
from jax import lax
from jax.experimental import pallas as pl
from jax.experimental.pallas import tpu as pltpu
from jax.experimental.pallas import tpu_sc as plsc

F32 = jnp.float32
BF16 = jnp.bfloat16
HIGHEST = lax.Precision.HIGHEST

D_MODEL = 1024
M_HEADS = 4
M_HEAD_DIM = 64
M_WIDTH = 256
M_CONV = 4
M_CHUNK = 64
A_HEADS = 8
A_NOPE = 64
A_ROPE = 32
A_VDIM = 64
A_QRANK = 256
A_KVRANK = 128
A_WIDTH = 512
ROPE_THETA = 10000.0
P_WINDOWS = (2, 4, 8, 16)
P_WIDTH = 256
N_EXPERTS = 32
TOP_K = 4
D_FF = 1024
SWIGLU_LIMIT = 7.0
SWIGLU_ALPHA = 1.702
EPS = 1e-6

LANES = 128
SUBLANES = 8

PROJ_QKVO = 0
PROJ_CQ = 1024
PROJ_UP = 1280
PROJ_CKV = 1536
PROJ_SMALL = 1664
PROJ_WIDTH = 1792
SMALL_KR = 64
SMALL_GATE = 96

MOE_ROWS = 256
VMEM_LIMIT = 56 * 1024 * 1024


def _cparams(*sem):
    return pltpu.CompilerParams(dimension_semantics=sem, vmem_limit_bytes=VMEM_LIMIT)


def _sigmoid(x):
    return 1.0 / (1.0 + jnp.exp(-x))


def _log_sigmoid(x):
    return jnp.minimum(x, 0.0) - jnp.log(1.0 + jnp.exp(-jnp.abs(x)))


def _dot(a, b, **kw):
    return jnp.dot(a, b, preferred_element_type=F32, **kw)


def _dot_nt(a, b, **kw):
    return lax.dot_general(a, b, (((1,), (1,)), ((), ())), preferred_element_type=F32, **kw)


def _dot_tn(a, b, **kw):
    return lax.dot_general(a, b, (((0,), (0,)), ((), ())), preferred_element_type=F32, **kw)


def _inproj_kernel(h_ref, g_ref, w_ref, o_ref):
    x = h_ref[...]
    ms = jnp.mean(x * x, axis=-1, keepdims=True)
    xn = x * lax.rsqrt(ms + EPS) * g_ref[...]
    o_ref[...] = _dot(xn.astype(BF16), w_ref[...])


def _inproj(h2, g, w, tm=512):
    T = h2.shape[0]
    return pl.pallas_call(
        _inproj_kernel,
        grid=(T // tm,),
        in_specs=[pl.BlockSpec((tm, D_MODEL), lambda i: (i, 0)),
                  pl.BlockSpec((1, D_MODEL), lambda i: (0, 0)),
                  pl.BlockSpec((D_MODEL, PROJ_WIDTH), lambda i: (0, 0))],
        out_specs=pl.BlockSpec((tm, PROJ_WIDTH), lambda i: (i, 0)),
        out_shape=jax.ShapeDtypeStruct((T, PROJ_WIDTH), F32),
        compiler_params=_cparams("parallel"),
        name="inproj",
    )(h2, g, w)


def _mlstm_kernel(proj_ref, small_ref, gt_ref, cw_ref, cb_ref, gbc_ref, gbr_ref, ng_ref, o_ref,
                  xpad_ref, ct_ref, n_ref, m_ref):
    S = proj_ref.shape[0]
    L = M_CHUNK
    nc = S // L
    W = M_WIDTH
    xpad_ref[0:SUBLANES, :] = jnp.zeros((SUBLANES, 2 * W), F32)
    xpad_ref[SUBLANES:, :] = proj_ref[:, 0:2 * W]
    ct_ref[...] = jnp.zeros_like(ct_ref)
    n_ref[...] = jnp.zeros_like(n_ref)
    m_ref[...] = jnp.zeros_like(m_ref)

    lane_head = lax.broadcasted_iota(jnp.int32, (1, W), 1) // M_HEAD_DIM
    masks = [(lane_head == h).astype(F32) for h in range(M_HEADS)]
    rh = lax.broadcasted_iota(jnp.int32, (W, W), 0) // M_HEAD_DIM
    chd = lax.broadcasted_iota(jnp.int32, (W, W), 1) // M_HEAD_DIM
    same_head = (rh == chd).astype(F32)
    ri = lax.broadcasted_iota(jnp.int32, (L, L), 0)
    ci = lax.broadcasted_iota(jnp.int32, (L, L), 1)
    causal = ri >= ci
    tril = causal.astype(F32)
    triu = (ri <= ci).astype(F32)
    cw = cw_ref[...]
    cb = cb_ref[...]
    gbc = gbc_ref[...]
    gbr = gbr_ref[...]
    ng = ng_ref[...]

    def body(c, carry):
        r0 = pl.multiple_of(c * L, L)
        win = xpad_ref[pl.ds(r0, L + SUBLANES), :]
        acc = jnp.zeros((L, 2 * W), F32) + cb
        for j in range(M_CONV):
            s = M_CONV - 1 - j
            xs = win if s == 0 else pltpu.roll(win, s, axis=0)
            acc = acc + xs[SUBLANES:, :] * cw[j:j + 1, :]
        qk = acc * _sigmoid(acc)
        q = qk[:, 0:W]
        k = qk[:, W:2 * W] * (M_HEAD_DIM ** -0.5)
        v = proj_ref[pl.ds(r0, L), 2 * W:3 * W]
        og = proj_ref[pl.ds(r0, L), 3 * W:4 * W]
        sm = small_ref[pl.ds(r0, L), :] + gbc
        gt = gt_ref[0, c] + gbr
        i_col_all = sm[:, SMALL_GATE:SMALL_GATE + M_HEADS]
        logf_col = _log_sigmoid(sm[:, SMALL_GATE + M_HEADS:SMALL_GATE + 2 * M_HEADS])
        g_col_all = _dot(tril, logf_col, precision=HIGHEST)
        i_row_all = gt[0:M_HEADS, :]
        logf_row = _log_sigmoid(gt[M_HEADS:2 * M_HEADS, :])
        g_row_all = _dot(logf_row, triu, precision=HIGHEST)

        q16 = q.astype(BF16)
        k16 = k.astype(BF16)
        v16 = v.astype(BF16)
        num_intra = jnp.zeros((L, W), F32)
        w_inter_f = jnp.zeros((L, W), F32)
        rowsum_f = jnp.zeros((L, W), F32)
        floor_f = jnp.zeros((L, W), F32)
        wa_f = jnp.zeros((L, W), F32)
        dec_f = jnp.zeros((1, W), F32)
        for h in range(M_HEADS):
            mk = masks[h]
            g_col = g_col_all[:, h:h + 1]
            i_col = i_col_all[:, h:h + 1]
            g_row = g_row_all[h:h + 1, :]
            i_row = i_row_all[h:h + 1, :]
            m_old = m_ref[0:1, h:h + 1]
            dmat = jnp.where(causal, g_col - g_row + i_row, -jnp.inf)
            inter = g_col + m_old
            m_row = jnp.maximum(inter, jnp.max(dmat, axis=-1, keepdims=True))
            sc = _dot_nt((q * mk).astype(BF16), k16) * jnp.exp(dmat - m_row)
            w_inter = jnp.exp(inter - m_row)
            num_intra = num_intra + _dot(sc.astype(BF16), v16) * mk
            w_inter_f = w_inter_f + w_inter * mk
            rowsum_f = rowsum_f + jnp.sum(sc, axis=-1, keepdims=True) * mk
            floor_f = floor_f + jnp.exp(-m_row) * mk
            g_end = g_col[L - 1:L, :]
            a = g_end - g_col + i_col
            m_new = jnp.maximum(g_end + m_old, jnp.max(a, axis=0, keepdims=True))
            wa_f = wa_f + jnp.exp(a - m_new) * mk
            dec_f = dec_f + jnp.exp(g_end + m_old - m_new) * mk
            m_ref[0:1, h:h + 1] = m_new

        ct = ct_ref[...]
        nvec = n_ref[...]
        q_c = _dot(q16, ct.astype(BF16))
        q_n = _dot(q * nvec, same_head, precision=HIGHEST)
        num = w_inter_f * q_c + num_intra
        den = w_inter_f * q_n + rowsum_f
        hv = num / jnp.maximum(jnp.abs(den), floor_f)
        ms = _dot(hv * hv, same_head, precision=HIGHEST) * (1.0 / M_HEAD_DIM)
        y = hv * lax.rsqrt(ms + EPS) * ng * _sigmoid(og)
        o_ref[pl.ds(r0, L), :] = y

        upd = _dot_tn(k16, (wa_f * v).astype(BF16))
        ct_ref[...] = dec_f * ct + upd * same_head
        n_ref[...] = dec_f * nvec + jnp.sum(wa_f * k, axis=0, keepdims=True)
        return carry

    lax.fori_loop(0, nc, body, 0)


def _mlstm(proj, gates_t, cw, cb, gbc, gbr, ng, B, S):
    T = B * S
    nc = S // M_CHUNK
    return pl.pallas_call(
        _mlstm_kernel,
        grid=(B,),
        in_specs=[pl.BlockSpec((S, 4 * M_WIDTH), lambda b: (b, 0)),
                  pl.BlockSpec((S, LANES), lambda b: (b, PROJ_SMALL // LANES)),
                  pl.BlockSpec((1, nc, 2 * M_HEADS, M_CHUNK), lambda b: (b, 0, 0, 0)),
                  pl.BlockSpec((M_CONV, 2 * M_WIDTH), lambda b: (0, 0)),
                  pl.BlockSpec((1, 2 * M_WIDTH), lambda b: (0, 0)),
                  pl.BlockSpec((1, LANES), lambda b: (0, 0)),
                  pl.BlockSpec((2 * M_HEADS, 1), lambda b: (0, 0)),
                  pl.BlockSpec((1, M_WIDTH), lambda b: (0, 0))],
        out_specs=pl.BlockSpec((S, M_WIDTH), lambda b: (b, 0)),
        out_shape=jax.ShapeDtypeStruct((T, M_WIDTH), F32),
        scratch_shapes=[pltpu.VMEM((S + SUBLANES, 2 * M_WIDTH), F32),
                        pltpu.VMEM((M_WIDTH, M_WIDTH), F32),
                        pltpu.VMEM((1, M_WIDTH), F32),
                        pltpu.VMEM((1, LANES), F32)],
        compiler_params=_cparams("parallel"),
        name="mlstm",
    )(proj, proj, gates_t, cw, cb, gbc, gbr, ng)


POOL_HALO = 16
POOL_TILE = 256


def _pool_kernel(u_ref, w_ref, sc_ref, o_ref, upad_ref):
    S = u_ref.shape[0]
    upad_ref[0:POOL_HALO, :] = jnp.zeros((POOL_HALO, P_WIDTH), F32)
    upad_ref[POOL_HALO:, :] = u_ref[...]
    grp = lax.broadcasted_iota(jnp.int32, (1, P_WIDTH), 1) // (P_WIDTH // len(P_WINDOWS))
    win_lane = jnp.zeros((1, P_WIDTH), jnp.int32)
    for gi, wn in enumerate(P_WINDOWS):
        win_lane = jnp.where(grp == gi, wn, win_lane)
    w = w_ref[...]
    scale = sc_ref[...]
    rows = POOL_TILE + POOL_HALO

    def body(r, carry):
        r0 = pl.multiple_of(r * POOL_TILE, POOL_TILE)
        a = upad_ref[pl.ds(r0, rows), :]
        sums = []
        cur = a
        span = 1
        for _ in P_WINDOWS:
            cur = cur + pltpu.roll(cur, span, axis=0)
            span *= 2
            sums.append(cur)
        sel = sums[-1]
        for gi in range(len(P_WINDOWS) - 1):
            sel = jnp.where(grp == gi, sums[gi], sel)
        sel = sel[POOL_HALO:, :]
        u = a[POOL_HALO:, :]
        t = r0 + lax.broadcasted_iota(jnp.int32, (POOL_TILE, P_WIDTH), 0)
        cnt = jnp.minimum(t + 1, win_lane).astype(F32)
        pooled = sel / cnt - u
        o_ref[pl.ds(r0, POOL_TILE), :] = _dot(pooled.astype(BF16), w) * scale
        return carry

    lax.fori_loop(0, S // POOL_TILE, body, 0)


def _pool(proj, w_bd, scale, B, S):
    T = B * S
    return pl.pallas_call(
        _pool_kernel,
        grid=(B,),
        in_specs=[pl.BlockSpec((S, P_WIDTH), lambda b: (b, PROJ_UP // P_WIDTH)),
                  pl.BlockSpec((P_WIDTH, P_WIDTH), lambda b: (0, 0)),
                  pl.BlockSpec((1, P_WIDTH), lambda b: (0, 0))],
        out_specs=pl.BlockSpec((S, P_WIDTH), lambda b: (b, 0)),
        out_shape=jax.ShapeDtypeStruct((T, P_WIDTH), F32),
        scratch_shapes=[pltpu.VMEM((S + POOL_HALO, P_WIDTH), F32)],
        compiler_params=_cparams("parallel"),
        name="pool",
    )(proj, w_bd, scale)


def _rope(x, c, s1, s2):
    return x * c + pltpu.roll(x, LANES - A_ROPE // 2, axis=1) * s1 + pltpu.roll(x, A_ROPE // 2, axis=1) * s2


def _mla_prep_kernel(cq_ref, ckv_ref, small_ref, qg_ref, kvg_ref, wq_ref, wk_ref, wv_ref,
                     vone_ref, cq_t_ref, ck_t_ref, s1_ref, s2_ref, q_ref, k_ref, v_ref):
    def rms(x, g):
        return x * lax.rsqrt(jnp.mean(x * x, axis=-1, keepdims=True) + EPS) * g

    cqn = rms(cq_ref[...], qg_ref[...]).astype(BF16)
    ckvn = rms(ckv_ref[...], kvg_ref[...]).astype(BF16)
    scale = (A_NOPE + A_ROPE) ** -0.5
    qf = _dot(cqn, wq_ref[...]) * scale
    kf = _dot(ckvn, wk_ref[...])
    v_ref[...] = (_dot(ckvn, wv_ref[...]) + vone_ref[...]).astype(BF16)
    cqt = cq_t_ref[...]
    s1 = s1_ref[...]
    s2 = s2_ref[...]
    krot = _rope(small_ref[...], ck_t_ref[...], s1, s2)
    for h in range(A_HEADS):
        sl = slice(h * LANES, (h + 1) * LANES)
        q_ref[:, sl] = _rope(qf[:, sl], cqt, s1, s2).astype(BF16)
        k_ref[:, sl] = (kf[:, sl] + krot).astype(BF16)


def _mla_prep(proj, qg, kvg, wq, wk, wv, vone, cq_t, ck_t, s1_t, s2_t, B, S, ts=512):
    T = B * S
    nst = S // ts
    hw = A_HEADS * LANES
    return pl.pallas_call(
        _mla_prep_kernel,
        grid=(B, nst),
        in_specs=[pl.BlockSpec((ts, A_QRANK), lambda b, s: (b * nst + s, PROJ_CQ // A_QRANK)),
                  pl.BlockSpec((ts, A_KVRANK), lambda b, s: (b * nst + s, PROJ_CKV // A_KVRANK)),
                  pl.BlockSpec((ts, LANES), lambda b, s: (b * nst + s, PROJ_SMALL // LANES)),
                  pl.BlockSpec((1, A_QRANK), lambda b, s: (0, 0)),
                  pl.BlockSpec((1, A_KVRANK), lambda b, s: (0, 0)),
                  pl.BlockSpec((A_QRANK, hw), lambda b, s: (0, 0)),
                  pl.BlockSpec((A_KVRANK, hw), lambda b, s: (0, 0)),
                  pl.BlockSpec((A_KVRANK, hw), lambda b, s: (0, 0)),
                  pl.BlockSpec((1, hw), lambda b, s: (0, 0)),
                  pl.BlockSpec((ts, LANES), lambda b, s: (s, 0)),
                  pl.BlockSpec((ts, LANES), lambda b, s: (s, 0)),
                  pl.BlockSpec((ts, LANES), lambda b, s: (s, 0)),
                  pl.BlockSpec((ts, LANES), lambda b, s: (s, 0))],
        out_specs=[pl.BlockSpec((ts, hw), lambda b, s: (b * nst + s, 0)),
                   pl.BlockSpec((ts, hw), lambda b, s: (b * nst + s, 0)),
                   pl.BlockSpec((ts, hw), lambda b, s: (b * nst + s, 0))],
        out_shape=[jax.ShapeDtypeStruct((T, hw), BF16),
                   jax.ShapeDtypeStruct((T, hw), BF16),
                   jax.ShapeDtypeStruct((T, hw), BF16)],
        compiler_params=_cparams("parallel", "parallel"),
        name="mla_prep",
    )(proj, proj, proj, qg, kvg, wq, wk, wv, vone, cq_t, ck_t, s1_t, s2_t)


def _attn_kernel(q_ref, k_ref, v_ref, o_ref, *, tq):
    qi = pl.program_id(2)
    ri = lax.broadcasted_iota(jnp.int32, (tq, tq), 0)
    ci = lax.broadcasted_iota(jnp.int32, (tq, tq), 1)
    diag_ok = ri >= ci
    sls = [slice(hh * LANES, (hh + 1) * LANES) for hh in range(2)]
    qs = [q_ref[:, sl] for sl in sls]

    def step(kb, carry, masked):
        k0 = pl.multiple_of(kb * tq, tq)
        new = []
        for hh in range(2):
            m, acc = carry[hh]
            s = _dot_nt(qs[hh], k_ref[pl.ds(k0, tq), sls[hh]])
            if masked:
                s = jnp.where(diag_ok, s, -jnp.inf)
            m_new = jnp.maximum(m, jnp.max(s, axis=-1, keepdims=True))
            p = jnp.exp(s - m_new)
            acc = jnp.exp(m - m_new) * acc + _dot(p.astype(BF16), v_ref[pl.ds(k0, tq), sls[hh]])
            new.append((m_new, acc))
        return tuple(new)

    init = tuple((jnp.full((tq, 1), -jnp.inf, F32), jnp.zeros((tq, LANES), F32)) for _ in range(2))
    carry = lax.fori_loop(0, qi, functools.partial(step, masked=False), init)
    (_, acc0), (_, acc1) = step(qi, carry, True)
    lane = lax.broadcasted_iota(jnp.int32, (tq, LANES), 1)
    acc = jnp.where(lane < A_VDIM, acc0, acc1)
    den = jnp.where(lane < A_VDIM, pltpu.roll(acc0, A_VDIM, axis=1), pltpu.roll(acc1, A_VDIM, axis=1))
    o_ref[...] = (acc / den).astype(o_ref.dtype)


def _attention(q, k, v, B, S, tq=512):
    T = B * S
    nq = S // tq
    return pl.pallas_call(
        functools.partial(_attn_kernel, tq=tq),
        grid=(B, A_HEADS // 2, nq),
        in_specs=[pl.BlockSpec((tq, 2 * LANES), lambda b, p, i: (b * nq + i, p)),
                  pl.BlockSpec((S, 2 * LANES), lambda b, p, i: (b, p)),
                  pl.BlockSpec((S, 2 * LANES), lambda b, p, i: (b, p))],
        out_specs=pl.BlockSpec((tq, LANES), lambda b, p, i: (b * nq + i, p)),
        out_shape=jax.ShapeDtypeStruct((T, A_WIDTH), BF16),
        compiler_params=_cparams("parallel", "parallel", "arbitrary"),
        name="attention",
    )(q, k, v)


def _outproj_kernel(ym_ref, ya_ref, yp_ref, h_ref, w_ref, g_ref, wr_ref, br_ref,
                    hn_ref, xn_ref, lg_ref):
    mix = _dot(ym_ref[...].astype(BF16), w_ref[0:M_WIDTH, :])
    mix = mix + _dot(ya_ref[...], w_ref[M_WIDTH:M_WIDTH + A_WIDTH, :])
    mix = mix + _dot(yp_ref[...].astype(BF16), w_ref[M_WIDTH + A_WIDTH:, :])
    hn = h_ref[...] + mix
    hn_ref[...] = hn
    xn = hn * lax.rsqrt(jnp.mean(hn * hn, axis=-1, keepdims=True) + EPS) * g_ref[...]
    lg_ref[...] = _dot_nt(wr_ref[...], xn, precision=HIGHEST) + br_ref[...]
    xn_ref[...] = xn


def _outproj(ym, ya, yp, h2, w, g, wr_t, br, tm=512):
    T = h2.shape[0]
    return pl.pallas_call(
        _outproj_kernel,
        grid=(T // tm,),
        in_specs=[pl.BlockSpec((tm, M_WIDTH), lambda i: (i, 0)),
                  pl.BlockSpec((tm, A_WIDTH), lambda i: (i, 0)),
                  pl.BlockSpec((tm, P_WIDTH), lambda i: (i, 0)),
                  pl.BlockSpec((tm, D_MODEL), lambda i: (i, 0)),
                  pl.BlockSpec((D_MODEL, D_MODEL), lambda i: (0, 0)),
                  pl.BlockSpec((1, D_MODEL), lambda i: (0, 0)),
                  pl.BlockSpec((N_EXPERTS, D_MODEL), lambda i: (0, 0)),
                  pl.BlockSpec((N_EXPERTS, 1), lambda i: (0, 0))],
        out_specs=[pl.BlockSpec((tm, D_MODEL), lambda i: (i, 0)),
                   pl.BlockSpec((tm, D_MODEL), lambda i: (i, 0)),
                   pl.BlockSpec((N_EXPERTS, tm), lambda i: (0, i))],
        out_shape=[jax.ShapeDtypeStruct((T, D_MODEL), F32),
                   jax.ShapeDtypeStruct((T, D_MODEL), F32),
                   jax.ShapeDtypeStruct((N_EXPERTS, T), F32)],
        compiler_params=_cparams("parallel"),
        name="outproj",
    )(ym, ya, yp, h2, w, g, wr_t, br)


def _router_kernel(lg_ref, tri_ref, e_ref, w_ref, r_ref, cnt_ref, carry_ref):
    tr = lg_ref.shape[1]

    @pl.when(pl.program_id(0) == 0)
    def _():
        carry_ref[...] = jnp.zeros_like(carry_ref)

    x = lg_ref[...]
    eio = lax.broadcasted_iota(jnp.int32, (N_EXPERTS, tr), 0).astype(F32)
    picked = jnp.zeros((N_EXPERTS, tr), F32)
    vals = []
    idxs = []
    for _ in range(TOP_K):
        mx = jnp.max(x, axis=0, keepdims=True)
        idx = jnp.min(jnp.where(x == mx, eio, float(N_EXPERTS)), axis=0, keepdims=True)
        hit = eio == idx
        vals.append(mx)
        idxs.append(idx)
        picked = picked + hit.astype(F32)
        x = jnp.where(hit, -jnp.inf, x)
    exps = [jnp.exp(vv - vals[0]) for vv in vals]
    tot = exps[0] + exps[1] + exps[2] + exps[3]
    before = _dot(picked.astype(BF16), tri_ref[...]) + carry_ref[:, 0:1]
    for kk in range(TOP_K):
        e_ref[kk:kk + 1, :] = idxs[kk].astype(jnp.int32)
        w_ref[kk:kk + 1, :] = exps[kk] / tot
        rk = jnp.sum(jnp.where(eio == idxs[kk], before, 0.0), axis=0, keepdims=True)
        r_ref[kk:kk + 1, :] = rk.astype(jnp.int32)
    carry_ref[...] = carry_ref[...] + jnp.sum(picked, axis=1, keepdims=True)
    cnt_ref[...] = carry_ref[...]


def _router(logits_t, tri, tr=512):
    T = logits_t.shape[1]
    return pl.pallas_call(
        _router_kernel,
        grid=(T // tr,),
        in_specs=[pl.BlockSpec((N_EXPERTS, tr), lambda i: (0, i)),
                  pl.BlockSpec((tr, tr), lambda i: (0, 0))],
        out_specs=[pl.BlockSpec((TOP_K, tr), lambda i: (0, i)),
                   pl.BlockSpec((TOP_K, tr), lambda i: (0, i)),
                   pl.BlockSpec((TOP_K, tr), lambda i: (0, i)),
                   pl.BlockSpec((N_EXPERTS, LANES), lambda i: (0, 0))],
        out_shape=[jax.ShapeDtypeStruct((TOP_K, T), jnp.int32),
                   jax.ShapeDtypeStruct((TOP_K, T), F32),
                   jax.ShapeDtypeStruct((TOP_K, T), jnp.int32),
                   jax.ShapeDtypeStruct((N_EXPERTS, LANES), F32)],
        scratch_shapes=[pltpu.VMEM((N_EXPERTS, LANES), F32)],
        compiler_params=_cparams("arbitrary"),
        name="router",
    )(logits_t, tri)


def _meta_kernel(cnt_ref, e_ref, r_ref, dest_ref, be_ref, bc_ref, nb_ref, *, nb_pad):
    cnt = cnt_ref[...]
    padded = jnp.floor((cnt + (MOE_ROWS - 1)) * (1.0 / MOE_ROWS)) * MOE_ROWS
    ri = lax.broadcasted_iota(jnp.int32, (N_EXPERTS, N_EXPERTS), 0)
    ci = lax.broadcasted_iota(jnp.int32, (N_EXPERTS, N_EXPERTS), 1)
    pad_end = _dot((ri >= ci).astype(F32), padded, precision=HIGHEST)
    pad_start = pad_end - padded
    e = e_ref[...]
    dest = r_ref[...]
    for ex in range(N_EXPERTS):
        ps = pad_start[ex:ex + 1, 0:1].astype(jnp.int32)
        dest = jnp.where(e == ex, dest + ps, dest)
    dest_ref[...] = dest
    blk0 = (lax.broadcasted_iota(jnp.int32, (N_EXPERTS, nb_pad), 1) * MOE_ROWS).astype(F32)
    be = jnp.sum((pad_end[:, 0:1] <= blk0).astype(F32), axis=0, keepdims=True)
    be = jnp.minimum(be, float(N_EXPERTS - 1))
    eio = lax.broadcasted_iota(jnp.int32, (N_EXPERTS, nb_pad), 0).astype(F32)
    seg_end = jnp.sum(jnp.where(eio == be, pad_start[:, 0:1] + cnt[:, 0:1], 0.0), axis=0, keepdims=True)
    bc = jnp.clip(seg_end - blk0[0:1, :], 0.0, float(MOE_ROWS))
    be_ref[...] = be.astype(jnp.int32)
    bc_ref[...] = bc.astype(jnp.int32)
    nb_ref[...] = (pad_end[N_EXPERTS - 1:N_EXPERTS, :] * (1.0 / MOE_ROWS)).astype(jnp.int32)


def _meta(counts, eidx, rank, nb_pad):
    T = eidx.shape[1]
    return pl.pallas_call(
        functools.partial(_meta_kernel, nb_pad=nb_pad),
        out_shape=[jax.ShapeDtypeStruct((TOP_K, T), jnp.int32),
                   jax.ShapeDtypeStruct((1, nb_pad), jnp.int32),
                   jax.ShapeDtypeStruct((1, nb_pad), jnp.int32),
                   jax.ShapeDtypeStruct((1, LANES), jnp.int32)],
        compiler_params=pltpu.CompilerParams(vmem_limit_bytes=VMEM_LIMIT),
        name="route_meta",
    )(counts, eidx, rank)


FF_CHUNK = 512


def _gmm_kernel(be_ref, bc_ref, nb_ref, x_ref, wgu_ref, bgu_ref, wdn_ref, bdn_ref, y_ref, wgu16, wdn16):
    i = pl.program_id(0)
    nblk = nb_ref[0]
    bm = MOE_ROWS

    @pl.when(i >= nblk)
    def _():
        y_ref[...] = jnp.zeros_like(y_ref)

    @pl.when(i < nblk)
    def _():
        e_changed = jnp.logical_or(i == 0, be_ref[i] != be_ref[jnp.maximum(i - 1, 0)])

        @pl.when(e_changed)
        def _():
            wgu16[...] = wgu_ref[0].astype(BF16)
            wdn16[...] = wdn_ref[0].astype(BF16)

        valid = lax.broadcasted_iota(jnp.int32, (bm, 1), 0) < bc_ref[i]
        x16 = jnp.where(valid, x_ref[...], 0.0).astype(BF16)
        acc = jnp.zeros((bm, D_MODEL), F32) + bdn_ref[0]
        for c in range(D_FF // FF_CHUNK):
            cs = slice(c * FF_CHUNK, (c + 1) * FF_CHUNK)
            us = slice(D_FF + c * FF_CHUNK, D_FF + (c + 1) * FF_CHUNK)
            gate = _dot(x16, wgu16[:, cs]) + bgu_ref[0, :, cs]
            up = _dot(x16, wgu16[:, us]) + bgu_ref[0, :, us]
            gate = jnp.minimum(gate, SWIGLU_LIMIT)
            up = jnp.clip(up, -SWIGLU_LIMIT, SWIGLU_LIMIT)
            act = (up + 1.0) * gate * _sigmoid(SWIGLU_ALPHA * gate)
            acc = acc + _dot(act.astype(BF16), wdn16[cs, :])
        y_ref[...] = acc


def _gmm(blk_e, blk_cnt, nblk, x_rows, wgu, bgu, wdn, bdn, nb, layer):
    bm = MOE_ROWS
    e0 = layer * N_EXPERTS

    def expert(i, be, nb_ref):
        return (e0 + be[jnp.minimum(i, jnp.maximum(nb_ref[0] - 1, 0))], 0, 0)

    def rows(i, nb_ref):
        return (jnp.minimum(i, jnp.maximum(nb_ref[0] - 1, 0)), 0)

    grid_spec = pltpu.PrefetchScalarGridSpec(
        num_scalar_prefetch=3,
        grid=(nb,),
        in_specs=[pl.BlockSpec((bm, D_MODEL), lambda i, be, bc, nbr: rows(i, nbr)),
                  pl.BlockSpec((1, D_MODEL, 2 * D_FF), lambda i, be, bc, nbr: expert(i, be, nbr)),
                  pl.BlockSpec((1, 1, 2 * D_FF), lambda i, be, bc, nbr: expert(i, be, nbr)),
                  pl.BlockSpec((1, D_FF, D_MODEL), lambda i, be, bc, nbr: expert(i, be, nbr)),
                  pl.BlockSpec((1, 1, D_MODEL), lambda i, be, bc, nbr: expert(i, be, nbr))],
        out_specs=pl.BlockSpec((bm, D_MODEL), lambda i, be, bc, nbr: (i, 0)),
        scratch_shapes=[pltpu.VMEM((D_MODEL, 2 * D_FF), BF16),
                        pltpu.VMEM((D_FF, D_MODEL), BF16)],
    )
    return pl.pallas_call(
        _gmm_kernel,
        grid_spec=grid_spec,
        out_shape=jax.ShapeDtypeStruct((nb * bm, D_MODEL), F32),
        compiler_params=_cparams("arbitrary"),
        name="expert_gmm",
    )(blk_e, blk_cnt, nblk, x_rows, wgu, bgu, wdn, bdn)


def _combine_kernel(h_ref, y0_ref, y1_ref, y2_ref, y3_ref, w_ref, o_ref):
    w = w_ref[...]
    acc = h_ref[...]
    for kk, y_ref in enumerate((y0_ref, y1_ref, y2_ref, y3_ref)):
        acc = acc + y_ref[...] * w[:, kk:kk + 1]
    o_ref[...] = acc


def _combine(y_tok, h2, w_tok, tc=512):
    T = h2.shape[0]
    nt = T // tc
    y_specs = [pl.BlockSpec((tc, D_MODEL), functools.partial(lambda i, kk: (kk * nt + i, 0), kk=kk))
               for kk in range(TOP_K)]
    return pl.pallas_call(
        _combine_kernel,
        grid=(nt,),
        in_specs=[pl.BlockSpec((tc, D_MODEL), lambda i: (i, 0))] + y_specs
                 + [pl.BlockSpec((tc, TOP_K), lambda i: (i, 0))],
        out_specs=pl.BlockSpec((tc, D_MODEL), lambda i: (i, 0)),
        out_shape=jax.ShapeDtypeStruct((T, D_MODEL), F32),
        compiler_params=_cparams("parallel"),
        name="combine",
    )(h2, y_tok, y_tok, y_tok, y_tok, w_tok)


SC_CORES = 2
SC_SUBCORES = 16
SC_LANES = 16
SC_WORKERS = SC_CORES * SC_SUBCORES
SC_WINDOW = 32


def _sc_mesh():
    return plsc.VectorSubcoreMesh(core_axis_name="c", subcore_axis_name="s")


def _sc_worker():
    return lax.axis_index("s") * SC_CORES + lax.axis_index("c")


def _sc_dispatch(xn, dest_flat, n_rows):
    T = xn.shape[0]
    tpw = T // SC_WORKERS
    nchunk = tpw // SC_WINDOW
    nvec = SC_WINDOW // SC_LANES

    @functools.partial(
        pl.kernel, out_type=jax.ShapeDtypeStruct((n_rows, D_MODEL), xn.dtype), mesh=_sc_mesh(),
        scratch_types=[pltpu.VMEM((TOP_K * tpw,), jnp.int32),
                       pltpu.VMEM((SC_WINDOW, D_MODEL), xn.dtype),
                       pltpu.VMEM((SC_WINDOW, D_MODEL), xn.dtype),
                       pltpu.SemaphoreType.DMA, pltpu.SemaphoreType.DMA, pltpu.SemaphoreType.DMA],
        name="sc_dispatch")
    def run(x_hbm, d_hbm, o_hbm, idx_v, buf0, buf1, sem0, sem1, sem_out):
        base = _sc_worker() * tpw
        for kk in range(TOP_K):
            pltpu.sync_copy(d_hbm.at[pl.ds(kk * T + base, tpw)], idx_v.at[pl.ds(kk * tpw, tpw)])
        bufs = (buf0, buf1)
        sems = (sem0, sem1)

        def load(c, slot):
            return pltpu.make_async_copy(x_hbm.at[pl.ds(base + c * SC_WINDOW, SC_WINDOW)], bufs[slot], sems[slot])

        load(0, 0).start()

        @pl.loop(0, nchunk, step=2)
        def _(c0):
            for slot in range(2):
                c = c0 + slot
                load(c, slot).wait()

                @pl.when(c + 1 < nchunk)
                def _():
                    load(c + 1, 1 - slot).start()

                copies = []
                for kk in range(TOP_K):
                    for q in range(nvec):
                        off = pl.multiple_of(kk * tpw + c * SC_WINDOW + q * SC_LANES, SC_LANES)
                        rows = idx_v[pl.ds(off, SC_LANES)]
                        cp = pltpu.make_async_copy(bufs[slot].at[pl.ds(q * SC_LANES, SC_LANES)],
                                                   o_hbm.at[rows], sem_out)
                        cp.start()
                        copies.append(cp)
                for cp in copies:
                    cp.wait()

    return run(xn, dest_flat)


def _sc_gather(y_rows, dest_flat):
    n = dest_flat.shape[0]
    rpw = n // SC_WORKERS
    nchunk = rpw // SC_WINDOW
    nvec = SC_WINDOW // SC_LANES

    @functools.partial(
        pl.kernel, out_type=jax.ShapeDtypeStruct((n, D_MODEL), y_rows.dtype), mesh=_sc_mesh(),
        scratch_types=[pltpu.VMEM((rpw,), jnp.int32),
                       pltpu.VMEM((SC_WINDOW, D_MODEL), y_rows.dtype),
                       pltpu.VMEM((SC_WINDOW, D_MODEL), y_rows.dtype),
                       pltpu.SemaphoreType.DMA, pltpu.SemaphoreType.DMA, pltpu.SemaphoreType.DMA],
        name="sc_gather")
    def run(y_hbm, d_hbm, o_hbm, idx_v, buf0, buf1, sem0, sem1, sem_in):
        base = _sc_worker() * rpw
        pltpu.sync_copy(d_hbm.at[pl.ds(base, rpw)], idx_v)
        bufs = (buf0, buf1)
        sems = (sem0, sem1)

        def store(c, slot):
            return pltpu.make_async_copy(bufs[slot], o_hbm.at[pl.ds(base + c * SC_WINDOW, SC_WINDOW)], sems[slot])

        @pl.loop(0, nchunk, step=2)
        def _(c0):
            for slot in range(2):
                c = c0 + slot

                @pl.when(c >= 2)
                def _():
                    store(c - 2, slot).wait()

                copies = []
                for q in range(nvec):
                    off = pl.multiple_of(c * SC_WINDOW + q * SC_LANES, SC_LANES)
                    rows = idx_v[pl.ds(off, SC_LANES)]
                    cp = pltpu.make_async_copy(y_hbm.at[rows], bufs[slot].at[pl.ds(q * SC_LANES, SC_LANES)], sem_in)
                    cp.start()
                    copies.append(cp)
                for cp in copies:
                    cp.wait()
                store(c, slot).start()

        store(nchunk - 2, 0).wait()
        store(nchunk - 1, 1).wait()

    return run(y_rows, dest_flat)


def _final_norm_kernel(h_ref, g_ref, o_ref):
    x = h_ref[...]
    o_ref[...] = x * lax.rsqrt(jnp.mean(x * x, axis=-1, keepdims=True) + EPS) * g_ref[...]


def _final_norm(h2, g, tm=1024):
    T = h2.shape[0]
    return pl.pallas_call(
        _final_norm_kernel,
        grid=(T // tm,),
        in_specs=[pl.BlockSpec((tm, D_MODEL), lambda i: (i, 0)),
                  pl.BlockSpec((1, D_MODEL), lambda i: (0, 0))],
        out_specs=pl.BlockSpec((tm, D_MODEL), lambda i: (i, 0)),
        out_shape=jax.ShapeDtypeStruct((T, D_MODEL), F32),
        compiler_params=_cparams("parallel"),
        name="final_norm",
    )(h2, g)


def _prep_w_in(w_in):
    o_g = 4 * M_WIDTH
    o_cq = o_g + 2 * M_HEADS
    o_ckv = o_cq + A_QRANK
    o_kr = o_ckv + A_KVRANK
    o_up = o_kr + A_ROPE
    z = lambda n: jnp.zeros(w_in.shape[:-1] + (n,), w_in.dtype)
    small = jnp.concatenate([z(SMALL_KR), w_in[..., o_kr:o_up], w_in[..., o_g:o_cq],
                             z(LANES - SMALL_GATE - 2 * M_HEADS)], axis=-1)
    return jnp.concatenate([w_in[..., 0:o_g], w_in[..., o_cq:o_ckv], w_in[..., o_up:o_up + P_WIDTH],
                            w_in[..., o_ckv:o_kr], small], axis=-1).astype(BF16)


def _rope_tables(seq):
    inv = ROPE_THETA ** (-jnp.arange(0, A_ROPE, 2, dtype=F32) / A_ROPE)
    ang = jnp.arange(seq, dtype=F32)[:, None] * inv[None, :]
    cos, sin = jnp.cos(ang), jnp.sin(ang)
    half = A_ROPE // 2
    zeros = lambda n: jnp.zeros((seq, n), F32)
    ones = lambda n: jnp.ones((seq, n), F32)
    tail = LANES - A_NOPE - A_ROPE
    cq_t = jnp.concatenate([ones(A_NOPE), cos, cos, zeros(tail)], axis=1)
    ck_t = jnp.concatenate([zeros(A_NOPE), cos, cos, zeros(tail)], axis=1)
    s1_t = jnp.concatenate([zeros(A_NOPE), -sin, zeros(half), zeros(tail)], axis=1)
    s2_t = jnp.concatenate([zeros(A_NOPE), zeros(half), sin, zeros(tail)], axis=1)
    return cq_t, ck_t, s1_t, s2_t


def kernel(x, norm1_g, w_in, conv_w, conv_b, gate_b, mlstm_norm_g, q_norm_g, kv_norm_g, w_uq, w_ukv,
           w_pool, pool_scale, w_out, norm2_g, w_router, b_router, w_gate_up, b_gate_up, w_down, b_down,
           final_norm_g):
    B, S, D = x.shape
    depth = w_in.shape[0]
    T = B * S
    nb = (T * TOP_K) // MOE_ROWS + N_EXPERTS
    nb_pad = -(-nb // LANES) * LANES

    w_in_p = _prep_w_in(w_in)
    wq = w_uq.reshape(depth, A_QRANK, A_HEADS, A_NOPE + A_ROPE)
    wq = jnp.pad(wq, ((0, 0), (0, 0), (0, 0), (0, LANES - A_NOPE - A_ROPE)))
    wq = wq.reshape(depth, A_QRANK, A_HEADS * LANES).astype(BF16)
    wkv = w_ukv.reshape(depth, A_KVRANK, A_HEADS, A_NOPE + A_VDIM)
    wk = jnp.pad(wkv[..., :A_NOPE], ((0, 0), (0, 0), (0, 0), (0, LANES - A_NOPE)))
    wk = wk.reshape(depth, A_KVRANK, A_HEADS * LANES).astype(BF16)
    wv_e = jnp.pad(wkv[:, :, 0::2, A_NOPE:], ((0, 0), (0, 0), (0, 0), (0, LANES - A_VDIM)))
    wv_o = jnp.pad(wkv[:, :, 1::2, A_NOPE:], ((0, 0), (0, 0), (0, 0), (LANES - A_VDIM, 0)))
    wv = jnp.stack([wv_e, wv_o], axis=3).reshape(depth, A_KVRANK, A_HEADS * LANES).astype(BF16)
    half = jnp.arange(A_HEADS * LANES) // A_VDIM
    vone = ((half % 4 == 1) | (half % 4 == 2)).astype(F32)[None, :]
    gsz = P_WIDTH // len(P_WINDOWS)
    w_pool_bd = jnp.zeros((depth, P_WIDTH, P_WIDTH), F32)
    for gi in range(len(P_WINDOWS)):
        w_pool_bd = w_pool_bd.at[:, gi * gsz:(gi + 1) * gsz, gi * gsz:(gi + 1) * gsz].set(w_pool[:, gi])
    w_pool_bd = w_pool_bd.astype(BF16)
    w_out16 = w_out.astype(BF16)
    w_router_t = jnp.swapaxes(w_router, 1, 2)
    gate_b_col = jnp.pad(gate_b, ((0, 0), (SMALL_GATE, LANES - SMALL_GATE - 2 * M_HEADS)))
    cq_t, ck_t, s1_t, s2_t = _rope_tables(S)
    tr = 512
    tri = (jnp.arange(tr)[:, None] < jnp.arange(tr)[None, :]).astype(BF16)

    wgu_all = w_gate_up.reshape(depth * N_EXPERTS, D_MODEL, 2 * D_FF)
    bgu_all = b_gate_up.reshape(depth * N_EXPERTS, 1, 2 * D_FF)
    wdn_all = w_down.reshape(depth * N_EXPERTS, D_FF, D_MODEL)
    bdn_all = b_down.reshape(depth * N_EXPERTS, 1, D_MODEL)

    h = x.reshape(T, D)
    for l in range(depth):
        proj = _inproj(h, norm1_g[l][None, :], w_in_p[l])
        gates_t = proj[:, PROJ_SMALL + SMALL_GATE:PROJ_SMALL + SMALL_GATE + 2 * M_HEADS]
        gates_t = gates_t.reshape(B, S // M_CHUNK, M_CHUNK, 2 * M_HEADS).transpose(0, 1, 3, 2)
        y_m = _mlstm(proj, gates_t, conv_w[l], conv_b[l][None, :], gate_b_col[l][None, :],
                     gate_b[l][:, None], mlstm_norm_g[l][None, :], B, S)
        q16, k16, v16 = _mla_prep(proj, q_norm_g[l][None, :], kv_norm_g[l][None, :], wq[l], wk[l], wv[l],
                                  vone, cq_t, ck_t, s1_t, s2_t, B, S)
        y_a = _attention(q16, k16, v16, B, S)
        y_p = _pool(proj, w_pool_bd[l], pool_scale[l][None, :], B, S)
        h, xn, logits_t = _outproj(y_m, y_a, y_p, h, w_out16[l], norm2_g[l][None, :],
                                   w_router_t[l], b_router[l][:, None])
        eidx, wts, rank, counts = _router(logits_t, tri, tr)
        dest, blk_e, blk_cnt, nblk = _meta(counts, eidx, rank, nb_pad)
        dest_flat = dest.reshape(TOP_K * T)
        x_rows = _sc_dispatch(xn, dest_flat, nb * MOE_ROWS)
        y_rows = _gmm(blk_e[0], blk_cnt[0], nblk[0], x_rows, wgu_all, bgu_all, wdn_all, bdn_all, nb, l)
        y_tok = _sc_gather(y_rows, dest_flat)
        h = _combine(y_tok, h, wts.T)
    return _final_norm(h, final_norm_g[None, :]).reshape(B, S, D)
```

```python
import functools

import jax
import jax.numpy as jnp
import numpy as np
from jax import lax
from jax.experimental import pallas as pl
from jax.experimental.pallas import tpu as pltpu
from jax.experimental.pallas import tpu_sc as plsc

F32 = jnp.float32
BF16 = jnp.bfloat16
HIGHEST = lax.Precision.HIGHEST

D_MODEL = 1024
M_HEADS = 4
M_HEAD_DIM = 64
M_WIDTH = 256
M_CONV = 4
M_CHUNK = 64
A_HEADS = 8
A_NOPE = 64
A_ROPE = 32
A_VDIM = 64
A_QRANK = 256
A_KVRANK = 128
A_WIDTH = 512
ROPE_THETA = 10000.0
P_WINDOWS = (2, 4, 8, 16)
P_WIDTH = 256
N_EXPERTS = 32
TOP_K = 4
D_FF = 1024
SWIGLU_LIMIT = 7.0
SWIGLU_ALPHA = 1.702
EPS = 1e-6

LANES = 128
SUBLANES = 8

PROJ_QKVO = 0
PROJ_CQ = 1024
PROJ_UP = 1280
PROJ_CKV = 1536
PROJ_SMALL = 1664
PROJ_WIDTH = 1792
SMALL_KR = 64
SMALL_GATE = 96

MOE_ROWS = 256
VMEM_LIMIT = 56 * 1024 * 1024


def _cparams(*sem):
    return pltpu.CompilerParams(dimension_semantics=sem, vmem_limit_bytes=VMEM_LIMIT)


def _sigmoid(x):
    return 1.0 / (1.0 + jnp.exp(-x))


def _log_sigmoid(x):
    return jnp.minimum(x, 0.0) - jnp.log(1.0 + jnp.exp(-jnp.abs(x)))


def _dot(a, b, **kw):
    return jnp.dot(a, b, preferred_element_type=F32, **kw)


def _dot_nt(a, b, **kw):
    return lax.dot_general(a, b, (((1,), (1,)), ((), ())), preferred_element_type=F32, **kw)


def _dot_tn(a, b, **kw):
    return lax.dot_general(a, b, (((0,), (0,)), ((), ())), preferred_element_type=F32, **kw)


def _inproj_kernel(h_ref, g_ref, w_ref, o_ref):
    x = h_ref[...]
    ms = jnp.mean(x * x, axis=-1, keepdims=True)
    xn = x * lax.rsqrt(ms + EPS) * g_ref[...]
    o_ref[...] = _dot(xn.astype(BF16), w_ref[...])


def _inproj(h2, g, w, tm=512):
    T = h2.shape[0]
    return pl.pallas_call(
        _inproj_kernel,
        grid=(T // tm,),
        in_specs=[pl.BlockSpec((tm, D_MODEL), lambda i: (i, 0)),
                  pl.BlockSpec((1, D_MODEL), lambda i: (0, 0)),
                  pl.BlockSpec((D_MODEL, PROJ_WIDTH), lambda i: (0, 0))],
        out_specs=pl.BlockSpec((tm, PROJ_WIDTH), lambda i: (i, 0)),
        out_shape=jax.ShapeDtypeStruct((T, PROJ_WIDTH), F32),
        compiler_params=_cparams("parallel"),
        name="inproj",
    )(h2, g, w)


def _mlstm_kernel(proj_ref, small_ref, gt_ref, cw_ref, cb_ref, gbc_ref, gbr_ref, ng_ref, o_ref,
                  xpad_ref, ct_ref, n_ref, m_ref):
    S = proj_ref.shape[0]
    L = M_CHUNK
    nc = S // L
    W = M_WIDTH
    xpad_ref[0:SUBLANES, :] = jnp.zeros((SUBLANES, 2 * W), F32)
    xpad_ref[SUBLANES:, :] = proj_ref[:, 0:2 * W]
    ct_ref[...] = jnp.zeros_like(ct_ref)
    n_ref[...] = jnp.zeros_like(n_ref)
    m_ref[...] = jnp.zeros_like(m_ref)

    lane_head = lax.broadcasted_iota(jnp.int32, (1, W), 1) // M_HEAD_DIM
    masks = [(lane_head == h).astype(F32) for h in range(M_HEADS)]
    rh = lax.broadcasted_iota(jnp.int32, (W, W), 0) // M_HEAD_DIM
    chd = lax.broadcasted_iota(jnp.int32, (W, W), 1) // M_HEAD_DIM
    same_head = (rh == chd).astype(F32)
    ri = lax.broadcasted_iota(jnp.int32, (L, L), 0)
    ci = lax.broadcasted_iota(jnp.int32, (L, L), 1)
    causal = ri >= ci
    tril = causal.astype(F32)
    triu = (ri <= ci).astype(F32)
    cw = cw_ref[...]
    cb = cb_ref[...]
    gbc = gbc_ref[...]
    gbr = gbr_ref[...]
    ng = ng_ref[...]

    def body(c, carry):
        r0 = pl.multiple_of(c * L, L)
        win = xpad_ref[pl.ds(r0, L + SUBLANES), :]
        acc = jnp.zeros((L, 2 * W), F32) + cb
        for j in range(M_CONV):
            s = M_CONV - 1 - j
            xs = win if s == 0 else pltpu.roll(win, s, axis=0)
            acc = acc + xs[SUBLANES:, :] * cw[j:j + 1, :]
        qk = acc * _sigmoid(acc)
        q = qk[:, 0:W]
        k = qk[:, W:2 * W] * (M_HEAD_DIM ** -0.5)
        v = proj_ref[pl.ds(r0, L), 2 * W:3 * W]
        og = proj_ref[pl.ds(r0, L), 3 * W:4 * W]
        sm = small_ref[pl.ds(r0, L), :] + gbc
        gt = gt_ref[0, c] + gbr
        i_col_all = sm[:, SMALL_GATE:SMALL_GATE + M_HEADS]
        logf_col = _log_sigmoid(sm[:, SMALL_GATE + M_HEADS:SMALL_GATE + 2 * M_HEADS])
        g_col_all = _dot(tril, logf_col, precision=HIGHEST)
        i_row_all = gt[0:M_HEADS, :]
        logf_row = _log_sigmoid(gt[M_HEADS:2 * M_HEADS, :])
        g_row_all = _dot(logf_row, triu, precision=HIGHEST)

        q16 = q.astype(BF16)
        k16 = k.astype(BF16)
        v16 = v.astype(BF16)
        num_intra = jnp.zeros((L, W), F32)
        w_inter_f = jnp.zeros((L, W), F32)
        rowsum_f = jnp.zeros((L, W), F32)
        floor_f = jnp.zeros((L, W), F32)
        wa_f = jnp.zeros((L, W), F32)
        dec_f = jnp.zeros((1, W), F32)
        for h in range(M_HEADS):
            mk = masks[h]
            g_col = g_col_all[:, h:h + 1]
            i_col = i_col_all[:, h:h + 1]
            g_row = g_row_all[h:h + 1, :]
            i_row = i_row_all[h:h + 1, :]
            m_old = m_ref[0:1, h:h + 1]
            dmat = jnp.where(causal, g_col - g_row + i_row, -jnp.inf)
            inter = g_col + m_old
            m_row = jnp.maximum(inter, jnp.max(dmat, axis=-1, keepdims=True))
            sc = _dot_nt((q * mk).astype(BF16), k16) * jnp.exp(dmat - m_row)
            w_inter = jnp.exp(inter - m_row)
            num_intra = num_intra + _dot(sc.astype(BF16), v16) * mk
            w_inter_f = w_inter_f + w_inter * mk
            rowsum_f = rowsum_f + jnp.sum(sc, axis=-1, keepdims=True) * mk
            floor_f = floor_f + jnp.exp(-m_row) * mk
            g_end = g_col[L - 1:L, :]
            a = g_end - g_col + i_col
            m_new = jnp.maximum(g_end + m_old, jnp.max(a, axis=0, keepdims=True))
            wa_f = wa_f + jnp.exp(a - m_new) * mk
            dec_f = dec_f + jnp.exp(g_end + m_old - m_new) * mk
            m_ref[0:1, h:h + 1] = m_new

        ct = ct_ref[...]
        nvec = n_ref[...]
        q_c = _dot(q16, ct.astype(BF16))
        q_n = _dot(q * nvec, same_head, precision=HIGHEST)
        num = w_inter_f * q_c + num_intra
        den = w_inter_f * q_n + rowsum_f
        hv = num / jnp.maximum(jnp.abs(den), floor_f)
        ms = _dot(hv * hv, same_head, precision=HIGHEST) * (1.0 / M_HEAD_DIM)
        y = hv * lax.rsqrt(ms + EPS) * ng * _sigmoid(og)
        o_ref[pl.ds(r0, L), :] = y

        upd = _dot_tn(k16, (wa_f * v).astype(BF16))
        ct_ref[...] = dec_f * ct + upd * same_head
        n_ref[...] = dec_f * nvec + jnp.sum(wa_f * k, axis=0, keepdims=True)
        return carry

    lax.fori_loop(0, nc, body, 0)


def _mlstm(proj, gates_t, cw, cb, gbc, gbr, ng, B, S):
    T = B * S
    nc = S // M_CHUNK
    return pl.pallas_call(
        _mlstm_kernel,
        grid=(B,),
        in_specs=[pl.BlockSpec((S, 4 * M_WIDTH), lambda b: (b, 0)),
                  pl.BlockSpec((S, LANES), lambda b: (b, PROJ_SMALL // LANES)),
                  pl.BlockSpec((1, nc, 2 * M_HEADS, M_CHUNK), lambda b: (b, 0, 0, 0)),
                  pl.BlockSpec((M_CONV, 2 * M_WIDTH), lambda b: (0, 0)),
                  pl.BlockSpec((1, 2 * M_WIDTH), lambda b: (0, 0)),
                  pl.BlockSpec((1, LANES), lambda b: (0, 0)),
                  pl.BlockSpec((2 * M_HEADS, 1), lambda b: (0, 0)),
                  pl.BlockSpec((1, M_WIDTH), lambda b: (0, 0))],
        out_specs=pl.BlockSpec((S, M_WIDTH), lambda b: (b, 0)),
        out_shape=jax.ShapeDtypeStruct((T, M_WIDTH), F32),
        scratch_shapes=[pltpu.VMEM((S + SUBLANES, 2 * M_WIDTH), F32),
                        pltpu.VMEM((M_WIDTH, M_WIDTH), F32),
                        pltpu.VMEM((1, M_WIDTH), F32),
                        pltpu.VMEM((1, LANES), F32)],
        compiler_params=_cparams("parallel"),
        name="mlstm",
    )(proj, proj, gates_t, cw, cb, gbc, gbr, ng)


POOL_HALO = 16
POOL_TILE = 256


def _pool_kernel(u_ref, w_ref, sc_ref, o_ref, upad_ref):
    S = u_ref.shape[0]
    upad_ref[0:POOL_HALO, :] = jnp.zeros((POOL_HALO, P_WIDTH), F32)
    upad_ref[POOL_HALO:, :] = u_ref[...]
    grp = lax.broadcasted_iota(jnp.int32, (1, P_WIDTH), 1) // (P_WIDTH // len(P_WINDOWS))
    win_lane = jnp.zeros((1, P_WIDTH), jnp.int32)
    for gi, wn in enumerate(P_WINDOWS):
        win_lane = jnp.where(grp == gi, wn, win_lane)
    w = w_ref[...]
    scale = sc_ref[...]
    rows = POOL_TILE + POOL_HALO

    def body(r, carry):
        r0 = pl.multiple_of(r * POOL_TILE, POOL_TILE)
        a = upad_ref[pl.ds(r0, rows), :]
        sums = []
        cur = a
        span = 1
        for _ in P_WINDOWS:
            cur = cur + pltpu.roll(cur, span, axis=0)
            span *= 2
            sums.append(cur)
        sel = sums[-1]
        for gi in range(len(P_WINDOWS) - 1):
            sel = jnp.where(grp == gi, sums[gi], sel)
        sel = sel[POOL_HALO:, :]
        u = a[POOL_HALO:, :]
        t = r0 + lax.broadcasted_iota(jnp.int32, (POOL_TILE, P_WIDTH), 0)
        cnt = jnp.minimum(t + 1, win_lane).astype(F32)
        pooled = sel / cnt - u
        o_ref[pl.ds(r0, POOL_TILE), :] = _dot(pooled.astype(BF16), w) * scale
        return carry

    lax.fori_loop(0, S // POOL_TILE, body, 0)


def _pool(proj, w_bd, scale, B, S):
    T = B * S
    return pl.pallas_call(
        _pool_kernel,
        grid=(B,),
        in_specs=[pl.BlockSpec((S, P_WIDTH), lambda b: (b, PROJ_UP // P_WIDTH)),
                  pl.BlockSpec((P_WIDTH, P_WIDTH), lambda b: (0, 0)),
                  pl.BlockSpec((1, P_WIDTH), lambda b: (0, 0))],
        out_specs=pl.BlockSpec((S, P_WIDTH), lambda b: (b, 0)),
        out_shape=jax.ShapeDtypeStruct((T, P_WIDTH), F32),
        scratch_shapes=[pltpu.VMEM((S + POOL_HALO, P_WIDTH), F32)],
        compiler_params=_cparams("parallel"),
        name="pool",
    )(proj, w_bd, scale)


def _rope(x, c, s1, s2):
    return x * c + pltpu.roll(x, LANES - A_ROPE // 2, axis=1) * s1 + pltpu.roll(x, A_ROPE // 2, axis=1) * s2


def _mla_prep_kernel(cq_ref, ckv_ref, small_ref, qg_ref, kvg_ref, wq_ref, wk_ref, wv_ref,
                     vone_ref, cq_t_ref, ck_t_ref, s1_ref, s2_ref, q_ref, k_ref, v_ref):
    def rms(x, g):
        return x * lax.rsqrt(jnp.mean(x * x, axis=-1, keepdims=True) + EPS) * g

    cqn = rms(cq_ref[...], qg_ref[...]).astype(BF16)
    ckvn = rms(ckv_ref[...], kvg_ref[...]).astype(BF16)
    scale = (A_NOPE + A_ROPE) ** -0.5
    qf = _dot(cqn, wq_ref[...]) * scale
    kf = _dot(ckvn, wk_ref[...])
    v_ref[...] = (_dot(ckvn, wv_ref[...]) + vone_ref[...]).astype(BF16)
    cqt = cq_t_ref[...]
    s1 = s1_ref[...]
    s2 = s2_ref[...]
    krot = _rope(small_ref[...], ck_t_ref[...], s1, s2)
    for h in range(A_HEADS):
        sl = slice(h * LANES, (h + 1) * LANES)
        q_ref[:, sl] = _rope(qf[:, sl], cqt, s1, s2).astype(BF16)
        k_ref[:, sl] = (kf[:, sl] + krot).astype(BF16)


def _mla_prep(proj, qg, kvg, wq, wk, wv, vone, cq_t, ck_t, s1_t, s2_t, B, S, ts=512):
    T = B * S
    nst = S // ts
    hw = A_HEADS * LANES
    return pl.pallas_call(
        _mla_prep_kernel,
        grid=(B, nst),
        in_specs=[pl.BlockSpec((ts, A_QRANK), lambda b, s: (b * nst + s, PROJ_CQ // A_QRANK)),
                  pl.BlockSpec((ts, A_KVRANK), lambda b, s: (b * nst + s, PROJ_CKV // A_KVRANK)),
                  pl.BlockSpec((ts, LANES), lambda b, s: (b * nst + s, PROJ_SMALL // LANES)),
                  pl.BlockSpec((1, A_QRANK), lambda b, s: (0, 0)),
                  pl.BlockSpec((1, A_KVRANK), lambda b, s: (0, 0)),
                  pl.BlockSpec((A_QRANK, hw), lambda b, s: (0, 0)),
                  pl.BlockSpec((A_KVRANK, hw), lambda b, s: (0, 0)),
                  pl.BlockSpec((A_KVRANK, hw), lambda b, s: (0, 0)),
                  pl.BlockSpec((1, hw), lambda b, s: (0, 0)),
                  pl.BlockSpec((ts, LANES), lambda b, s: (s, 0)),
                  pl.BlockSpec((ts, LANES), lambda b, s: (s, 0)),
                  pl.BlockSpec((ts, LANES), lambda b, s: (s, 0)),
                  pl.BlockSpec((ts, LANES), lambda b, s: (s, 0))],
        out_specs=[pl.BlockSpec((ts, hw), lambda b, s: (b * nst + s, 0)),
                   pl.BlockSpec((ts, hw), lambda b, s: (b * nst + s, 0)),
                   pl.BlockSpec((ts, hw), lambda b, s: (b * nst + s, 0))],
        out_shape=[jax.ShapeDtypeStruct((T, hw), BF16),
                   jax.ShapeDtypeStruct((T, hw), BF16),
                   jax.ShapeDtypeStruct((T, hw), BF16)],
        compiler_params=_cparams("parallel", "parallel"),
        name="mla_prep",
    )(proj, proj, proj, qg, kvg, wq, wk, wv, vone, cq_t, ck_t, s1_t, s2_t)


def _attn_kernel(q_ref, k_ref, v_ref, o_ref, *, tq):
    qi = pl.program_id(2)
    ri = lax.broadcasted_iota(jnp.int32, (tq, tq), 0)
    ci = lax.broadcasted_iota(jnp.int32, (tq, tq), 1)
    diag_ok = ri >= ci
    sls = [slice(hh * LANES, (hh + 1) * LANES) for hh in range(2)]
    qs = [q_ref[:, sl] for sl in sls]

    def step(kb, carry, masked):
        k0 = pl.multiple_of(kb * tq, tq)
        new = []
        for hh in range(2):
            m, acc = carry[hh]
            s = _dot_nt(qs[hh], k_ref[pl.ds(k0, tq), sls[hh]])
            if masked:
                s = jnp.where(diag_ok, s, -jnp.inf)
            m_new = jnp.maximum(m, jnp.max(s, axis=-1, keepdims=True))
            p = jnp.exp(s - m_new)
            acc = jnp.exp(m - m_new) * acc + _dot(p.astype(BF16), v_ref[pl.ds(k0, tq), sls[hh]])
            new.append((m_new, acc))
        return tuple(new)

    init = tuple((jnp.full((tq, 1), -jnp.inf, F32), jnp.zeros((tq, LANES), F32)) for _ in range(2))
    carry = lax.fori_loop(0, qi, functools.partial(step, masked=False), init)
    (_, acc0), (_, acc1) = step(qi, carry, True)
    lane = lax.broadcasted_iota(jnp.int32, (tq, LANES), 1)
    acc = jnp.where(lane < A_VDIM, acc0, acc1)
    den = jnp.where(lane < A_VDIM, pltpu.roll(acc0, A_VDIM, axis=1), pltpu.roll(acc1, A_VDIM, axis=1))
    o_ref[...] = (acc / den).astype(o_ref.dtype)


def _attention(q, k, v, B, S, tq=512):
    T = B * S
    nq = S // tq
    return pl.pallas_call(
        functools.partial(_attn_kernel, tq=tq),
        grid=(B, A_HEADS // 2, nq),
        in_specs=[pl.BlockSpec((tq, 2 * LANES), lambda b, p, i: (b * nq + i, p)),
                  pl.BlockSpec((S, 2 * LANES), lambda b, p, i: (b, p)),
                  pl.BlockSpec((S, 2 * LANES), lambda b, p, i: (b, p))],
        out_specs=pl.BlockSpec((tq, LANES), lambda b, p, i: (b * nq + i, p)),
        out_shape=jax.ShapeDtypeStruct((T, A_WIDTH), BF16),
        compiler_params=_cparams("parallel", "parallel", "arbitrary"),
        name="attention",
    )(q, k, v)


def _outproj_kernel(ym_ref, ya_ref, yp_ref, h_ref, w_ref, g_ref, wr_ref, br_ref,
                    hn_ref, xn_ref, lg_ref):
    mix = _dot(ym_ref[...].astype(BF16), w_ref[0:M_WIDTH, :])
    mix = mix + _dot(ya_ref[...], w_ref[M_WIDTH:M_WIDTH + A_WIDTH, :])
    mix = mix + _dot(yp_ref[...].astype(BF16), w_ref[M_WIDTH + A_WIDTH:, :])
    hn = h_ref[...] + mix
    hn_ref[...] = hn
    xn = hn * lax.rsqrt(jnp.mean(hn * hn, axis=-1, keepdims=True) + EPS) * g_ref[...]
    x_hi = xn.astype(BF16)
    x_lo = (xn - x_hi.astype(F32)).astype(BF16)
    wr = wr_ref[...]
    w_hi = wr.astype(BF16)
    w_lo = (wr - w_hi.astype(F32)).astype(BF16)
    lg_ref[...] = _dot_nt(w_hi, x_hi) + _dot_nt(w_hi, x_lo) + _dot_nt(w_lo, x_hi) + br_ref[...]
    xn_ref[...] = _pack_bf16_pairs(xn)


def _outproj(ym, ya, yp, h2, w, g, wr_t, br, tm=512):
    T = h2.shape[0]
    return pl.pallas_call(
        _outproj_kernel,
        grid=(T // tm,),
        in_specs=[pl.BlockSpec((tm, M_WIDTH), lambda i: (i, 0)),
                  pl.BlockSpec((tm, A_WIDTH), lambda i: (i, 0)),
                  pl.BlockSpec((tm, P_WIDTH), lambda i: (i, 0)),
                  pl.BlockSpec((tm, D_MODEL), lambda i: (i, 0)),
                  pl.BlockSpec((D_MODEL, D_MODEL), lambda i: (0, 0)),
                  pl.BlockSpec((1, D_MODEL), lambda i: (0, 0)),
                  pl.BlockSpec((N_EXPERTS, D_MODEL), lambda i: (0, 0)),
                  pl.BlockSpec((N_EXPERTS, 1), lambda i: (0, 0))],
        out_specs=[pl.BlockSpec((tm, D_MODEL), lambda i: (i, 0)),
                   pl.BlockSpec((tm, D_MODEL // 2), lambda i: (i, 0)),
                   pl.BlockSpec((N_EXPERTS, tm), lambda i: (0, i))],
        out_shape=[jax.ShapeDtypeStruct((T, D_MODEL), F32),
                   jax.ShapeDtypeStruct((T, D_MODEL // 2), jnp.uint32),
                   jax.ShapeDtypeStruct((N_EXPERTS, T), F32)],
        compiler_params=_cparams("parallel"),
        name="outproj",
    )(ym, ya, yp, h2, w, g, wr_t, br)


def _router_kernel(lg_ref, tri_ref, e_ref, w_ref, r_ref, cnt_ref, carry_ref):
    tr = lg_ref.shape[1]

    @pl.when(pl.program_id(0) == 0)
    def _():
        carry_ref[...] = jnp.zeros_like(carry_ref)

    x = lg_ref[...]
    eio = lax.broadcasted_iota(jnp.int32, (N_EXPERTS, tr), 0).astype(F32)
    picked = jnp.zeros((N_EXPERTS, tr), F32)
    vals = []
    idxs = []
    for _ in range(TOP_K):
        mx = jnp.max(x, axis=0, keepdims=True)
        idx = jnp.min(jnp.where(x == mx, eio, float(N_EXPERTS)), axis=0, keepdims=True)
        hit = eio == idx
        vals.append(mx)
        idxs.append(idx)
        picked = picked + hit.astype(F32)
        x = jnp.where(hit, -jnp.inf, x)
    exps = [jnp.exp(vv - vals[0]) for vv in vals]
    tot = exps[0] + exps[1] + exps[2] + exps[3]
    before = _dot(picked.astype(BF16), tri_ref[...]) + carry_ref[:, 0:1]
    for kk in range(TOP_K):
        e_ref[kk:kk + 1, :] = idxs[kk].astype(jnp.int32)
        w_ref[kk:kk + 1, :] = exps[kk] / tot
        rk = jnp.sum(jnp.where(eio == idxs[kk], before, 0.0), axis=0, keepdims=True)
        r_ref[kk:kk + 1, :] = rk.astype(jnp.int32)
    carry_ref[...] = carry_ref[...] + jnp.sum(picked, axis=1, keepdims=True)
    cnt_ref[...] = carry_ref[...]


def _router(logits_t, tri, tr=512):
    T = logits_t.shape[1]
    return pl.pallas_call(
        _router_kernel,
        grid=(T // tr,),
        in_specs=[pl.BlockSpec((N_EXPERTS, tr), lambda i: (0, i)),
                  pl.BlockSpec((tr, tr), lambda i: (0, 0))],
        out_specs=[pl.BlockSpec((TOP_K, tr), lambda i: (0, i)),
                   pl.BlockSpec((TOP_K, tr), lambda i: (0, i)),
                   pl.BlockSpec((TOP_K, tr), lambda i: (0, i)),
                   pl.BlockSpec((N_EXPERTS, LANES), lambda i: (0, 0))],
        out_shape=[jax.ShapeDtypeStruct((TOP_K, T), jnp.int32),
                   jax.ShapeDtypeStruct((TOP_K, T), F32),
                   jax.ShapeDtypeStruct((TOP_K, T), jnp.int32),
                   jax.ShapeDtypeStruct((N_EXPERTS, LANES), F32)],
        scratch_shapes=[pltpu.VMEM((N_EXPERTS, LANES), F32)],
        compiler_params=_cparams("arbitrary"),
        name="router",
    )(logits_t, tri)


def _meta_kernel(cnt_ref, e_ref, r_ref, dest_ref, be_ref, bc_ref, bn_ref, nb_ref, *, nb_pad):
    cnt = cnt_ref[...]
    padded = jnp.floor((cnt + (MOE_ROWS - 1)) * (1.0 / MOE_ROWS)) * MOE_ROWS
    ri = lax.broadcasted_iota(jnp.int32, (N_EXPERTS, N_EXPERTS), 0)
    ci = lax.broadcasted_iota(jnp.int32, (N_EXPERTS, N_EXPERTS), 1)
    pad_end = _dot((ri >= ci).astype(F32), padded, precision=HIGHEST)
    pad_start = pad_end - padded
    e = e_ref[...]
    dest = r_ref[...]
    for ex in range(N_EXPERTS):
        ps = pad_start[ex:ex + 1, 0:1].astype(jnp.int32)
        dest = jnp.where(e == ex, dest + ps, dest)
    dest_ref[...] = dest
    blk0 = (lax.broadcasted_iota(jnp.int32, (N_EXPERTS, nb_pad), 1) * MOE_ROWS).astype(F32)
    be = jnp.sum((pad_end[:, 0:1] <= blk0).astype(F32), axis=0, keepdims=True)
    be = jnp.minimum(be, float(N_EXPERTS - 1))
    eio = lax.broadcasted_iota(jnp.int32, (N_EXPERTS, nb_pad), 0).astype(F32)
    seg_end = jnp.sum(jnp.where(eio == be, pad_start[:, 0:1] + cnt[:, 0:1], 0.0), axis=0, keepdims=True)
    bc = jnp.clip(seg_end - blk0[0:1, :], 0.0, float(MOE_ROWS))
    nxt0 = jnp.sum(jnp.where(eio == be, pad_end[:, 0:1], 0.0), axis=0, keepdims=True)
    bn = jnp.sum((pad_end[:, 0:1] <= nxt0).astype(F32), axis=0, keepdims=True)
    bn = jnp.where(nxt0 < pad_end[N_EXPERTS - 1:N_EXPERTS, 0:1], bn, -1.0)
    be_ref[...] = be.astype(jnp.int32)
    bc_ref[...] = bc.astype(jnp.int32)
    bn_ref[...] = bn.astype(jnp.int32)
    nb_ref[...] = (pad_end[N_EXPERTS - 1:N_EXPERTS, :] * (1.0 / MOE_ROWS)).astype(jnp.int32)


def _meta(counts, eidx, rank, nb_pad):
    T = eidx.shape[1]
    return pl.pallas_call(
        functools.partial(_meta_kernel, nb_pad=nb_pad),
        out_shape=[jax.ShapeDtypeStruct((TOP_K, T), jnp.int32),
                   jax.ShapeDtypeStruct((1, nb_pad), jnp.int32),
                   jax.ShapeDtypeStruct((1, nb_pad), jnp.int32),
                   jax.ShapeDtypeStruct((1, nb_pad), jnp.int32),
                   jax.ShapeDtypeStruct((1, LANES), jnp.int32)],
        compiler_params=pltpu.CompilerParams(vmem_limit_bytes=VMEM_LIMIT),
        name="route_meta",
    )(counts, eidx, rank)


FF_CHUNK = 512


def _pack_bf16_pairs(x):
    n = x.shape[1] // 2
    lo = lax.bitcast_convert_type(x[:, :n].astype(BF16).astype(F32), jnp.uint32)
    hi = lax.bitcast_convert_type(x[:, n:].astype(BF16).astype(F32), jnp.uint32)
    return (lo >> 16) | (hi & jnp.uint32(0xFFFF0000))


def _unpack_bf16_pairs(p):
    lo = lax.bitcast_convert_type(p << 16, F32)
    hi = lax.bitcast_convert_type(p & jnp.uint32(0xFFFF0000), F32)
    return lo, hi


def _gmm_kernel(be_ref, bc_ref, bn_ref, nb_ref, x_ref, wgu_hbm, bgu_ref, wdn_hbm, bdn_ref, y_ref,
                wgu_st, wdn_st, wgu16, wdn16, sem, *, e0):
    i = pl.program_id(0)
    nblk = nb_ref[0]
    bm = MOE_ROWS

    def weight_copies(e):
        return (pltpu.make_async_copy(wgu_hbm.at[e0 + e], wgu_st, sem.at[0]),
                pltpu.make_async_copy(wdn_hbm.at[e0 + e], wdn_st, sem.at[1]))

    @pl.when(i == 0)
    def _():
        for cp in weight_copies(be_ref[0]):
            cp.start()

    @pl.when(i >= nblk)
    def _():
        y_ref[...] = jnp.zeros_like(y_ref)

    @pl.when(i < nblk)
    def _():
        e_changed = jnp.logical_or(i == 0, be_ref[i] != be_ref[jnp.maximum(i - 1, 0)])

        @pl.when(e_changed)
        def _():
            for cp in weight_copies(be_ref[i]):
                cp.wait()
            wgu16[...] = wgu_st[...].astype(BF16)
            wdn16[...] = wdn_st[...].astype(BF16)

            @pl.when(bn_ref[i] >= 0)
            def _():
                for cp in weight_copies(bn_ref[i]):
                    cp.start()

        valid = lax.broadcasted_iota(jnp.int32, (bm, 1), 0) < bc_ref[i]
        lo, hi = _unpack_bf16_pairs(jnp.where(valid, x_ref[...], jnp.uint32(0)))
        x16 = jnp.concatenate([lo.astype(BF16), hi.astype(BF16)], axis=1)
        acc = jnp.zeros((bm, D_MODEL), F32) + bdn_ref[0]
        for c in range(D_FF // FF_CHUNK):
            cs = slice(c * FF_CHUNK, (c + 1) * FF_CHUNK)
            us = slice(D_FF + c * FF_CHUNK, D_FF + (c + 1) * FF_CHUNK)
            gate = _dot(x16, wgu16[:, cs]) + bgu_ref[0, :, cs]
            up = _dot(x16, wgu16[:, us]) + bgu_ref[0, :, us]
            gate = jnp.minimum(gate, SWIGLU_LIMIT)
            up = jnp.clip(up, -SWIGLU_LIMIT, SWIGLU_LIMIT)
            act = (up + 1.0) * gate * _sigmoid(SWIGLU_ALPHA * gate)
            acc = acc + _dot(act.astype(BF16), wdn16[cs, :])
        y_ref[...] = _pack_bf16_pairs(acc)


def _gmm(blk_e, blk_cnt, blk_next, nblk, x_rows, wgu, bgu, wdn, bdn, nb, layer):
    bm = MOE_ROWS
    e0 = layer * N_EXPERTS
    dp = D_MODEL // 2

    def expert(i, be, nb_ref):
        return (e0 + be[jnp.minimum(i, jnp.maximum(nb_ref[0] - 1, 0))], 0, 0)

    def rows(i, nb_ref):
        return (jnp.minimum(i, jnp.maximum(nb_ref[0] - 1, 0)), 0)

    grid_spec = pltpu.PrefetchScalarGridSpec(
        num_scalar_prefetch=4,
        grid=(nb,),
        in_specs=[pl.BlockSpec((bm, dp), lambda i, be, bc, bn, nbr: rows(i, nbr)),
                  pl.BlockSpec(memory_space=pl.ANY),
                  pl.BlockSpec((1, 1, 2 * D_FF), lambda i, be, bc, bn, nbr: expert(i, be, nbr)),
                  pl.BlockSpec(memory_space=pl.ANY),
                  pl.BlockSpec((1, 1, D_MODEL), lambda i, be, bc, bn, nbr: expert(i, be, nbr))],
        out_specs=pl.BlockSpec((bm, dp), lambda i, be, bc, bn, nbr: (i, 0)),
        scratch_shapes=[pltpu.VMEM((D_MODEL, 2 * D_FF), F32),
                        pltpu.VMEM((D_FF, D_MODEL), F32),
                        pltpu.VMEM((D_MODEL, 2 * D_FF), BF16),
                        pltpu.VMEM((D_FF, D_MODEL), BF16),
                        pltpu.SemaphoreType.DMA((2,))],
    )
    return pl.pallas_call(
        functools.partial(_gmm_kernel, e0=e0),
        grid_spec=grid_spec,
        out_shape=jax.ShapeDtypeStruct((nb * bm, dp), jnp.uint32),
        compiler_params=_cparams("arbitrary"),
        name="expert_gmm",
    )(blk_e, blk_cnt, blk_next, nblk, x_rows, wgu, bgu, wdn, bdn)


def _combine_kernel(h_ref, y0_ref, y1_ref, y2_ref, y3_ref, w_ref, o_ref):
    w = w_ref[...]
    dp = D_MODEL // 2
    acc_lo = h_ref[:, :dp]
    acc_hi = h_ref[:, dp:]
    for kk, y_ref in enumerate((y0_ref, y1_ref, y2_ref, y3_ref)):
        lo, hi = _unpack_bf16_pairs(y_ref[...])
        acc_lo = acc_lo + lo * w[:, kk:kk + 1]
        acc_hi = acc_hi + hi * w[:, kk:kk + 1]
    o_ref[:, :dp] = acc_lo
    o_ref[:, dp:] = acc_hi


def _combine(y_tok, h2, w_tok, tc=512):
    T = h2.shape[0]
    nt = T // tc
    y_specs = [pl.BlockSpec((tc, D_MODEL // 2), functools.partial(lambda i, kk: (kk * nt + i, 0), kk=kk))
               for kk in range(TOP_K)]
    return pl.pallas_call(
        _combine_kernel,
        grid=(nt,),
        in_specs=[pl.BlockSpec((tc, D_MODEL), lambda i: (i, 0))] + y_specs
                 + [pl.BlockSpec((tc, TOP_K), lambda i: (i, 0))],
        out_specs=pl.BlockSpec((tc, D_MODEL), lambda i: (i, 0)),
        out_shape=jax.ShapeDtypeStruct((T, D_MODEL), F32),
        compiler_params=_cparams("parallel"),
        name="combine",
    )(h2, y_tok, y_tok, y_tok, y_tok, w_tok)


SC_CORES = 2
SC_SUBCORES = 16
SC_LANES = 16
SC_WORKERS = SC_CORES * SC_SUBCORES
SC_WINDOW = 64


def _sc_mesh():
    return plsc.VectorSubcoreMesh(core_axis_name="c", subcore_axis_name="s")


def _sc_worker():
    return lax.axis_index("s") * SC_CORES + lax.axis_index("c")


def _sc_dispatch(xn, dest_flat, n_rows):
    T = xn.shape[0]
    tpw = T // SC_WORKERS
    nchunk = tpw // SC_WINDOW
    nvec = SC_WINDOW // SC_LANES

    @functools.partial(
        pl.kernel, out_type=jax.ShapeDtypeStruct((n_rows, xn.shape[1]), xn.dtype), mesh=_sc_mesh(),
        scratch_types=[pltpu.VMEM((TOP_K * tpw,), jnp.int32),
                       pltpu.VMEM((SC_WINDOW, xn.shape[1]), xn.dtype),
                       pltpu.VMEM((SC_WINDOW, xn.shape[1]), xn.dtype),
                       pltpu.SemaphoreType.DMA, pltpu.SemaphoreType.DMA, pltpu.SemaphoreType.DMA],
        name="sc_dispatch")
    def run(x_hbm, d_hbm, o_hbm, idx_v, buf0, buf1, sem0, sem1, sem_out):
        base = _sc_worker() * tpw
        for kk in range(TOP_K):
            pltpu.sync_copy(d_hbm.at[pl.ds(kk * T + base, tpw)], idx_v.at[pl.ds(kk * tpw, tpw)])
        bufs = (buf0, buf1)
        sems = (sem0, sem1)

        def load(c, slot):
            return pltpu.make_async_copy(x_hbm.at[pl.ds(base + c * SC_WINDOW, SC_WINDOW)], bufs[slot], sems[slot])

        load(0, 0).start()

        @pl.loop(0, nchunk, step=2)
        def _(c0):
            for slot in range(2):
                c = c0 + slot
                load(c, slot).wait()

                @pl.when(c + 1 < nchunk)
                def _():
                    load(c + 1, 1 - slot).start()

                copies = []
                for kk in range(TOP_K):
                    for q in range(nvec):
                        off = pl.multiple_of(kk * tpw + c * SC_WINDOW + q * SC_LANES, SC_LANES)
                        rows = idx_v[pl.ds(off, SC_LANES)]
                        cp = pltpu.make_async_copy(bufs[slot].at[pl.ds(q * SC_LANES, SC_LANES)],
                                                   o_hbm.at[rows], sem_out)
                        cp.start()
                        copies.append(cp)
                for cp in copies:
                    cp.wait()

    return run(xn, dest_flat)


def _sc_gather(y_rows, dest_flat):
    n = dest_flat.shape[0]
    rpw = n // SC_WORKERS
    nchunk = rpw // SC_WINDOW
    nvec = SC_WINDOW // SC_LANES

    @functools.partial(
        pl.kernel, out_type=jax.ShapeDtypeStruct((n, y_rows.shape[1]), y_rows.dtype), mesh=_sc_mesh(),
        scratch_types=[pltpu.VMEM((rpw,), jnp.int32),
                       pltpu.VMEM((SC_WINDOW, y_rows.shape[1]), y_rows.dtype),
                       pltpu.VMEM((SC_WINDOW, y_rows.shape[1]), y_rows.dtype),
                       pltpu.SemaphoreType.DMA, pltpu.SemaphoreType.DMA, pltpu.SemaphoreType.DMA],
        name="sc_gather")
    def run(y_hbm, d_hbm, o_hbm, idx_v, buf0, buf1, sem0, sem1, sem_in):
        base = _sc_worker() * rpw
        pltpu.sync_copy(d_hbm.at[pl.ds(base, rpw)], idx_v)
        bufs = (buf0, buf1)
        sems = (sem0, sem1)

        def store(c, slot):
            return pltpu.make_async_copy(bufs[slot], o_hbm.at[pl.ds(base + c * SC_WINDOW, SC_WINDOW)], sems[slot])

        @pl.loop(0, nchunk, step=2)
        def _(c0):
            for slot in range(2):
                c = c0 + slot

                @pl.when(c >= 2)
                def _():
                    store(c - 2, slot).wait()

                copies = []
                for q in range(nvec):
                    off = pl.multiple_of(c * SC_WINDOW + q * SC_LANES, SC_LANES)
                    rows = idx_v[pl.ds(off, SC_LANES)]
                    cp = pltpu.make_async_copy(y_hbm.at[rows], bufs[slot].at[pl.ds(q * SC_LANES, SC_LANES)], sem_in)
                    cp.start()
                    copies.append(cp)
                for cp in copies:
                    cp.wait()
                store(c, slot).start()

        store(nchunk - 2, 0).wait()
        store(nchunk - 1, 1).wait()

    return run(y_rows, dest_flat)


def _final_norm_kernel(h_ref, g_ref, o_ref):
    x = h_ref[...]
    o_ref[...] = x * lax.rsqrt(jnp.mean(x * x, axis=-1, keepdims=True) + EPS) * g_ref[...]


def _final_norm(h2, g, tm=1024):
    T = h2.shape[0]
    return pl.pallas_call(
        _final_norm_kernel,
        grid=(T // tm,),
        in_specs=[pl.BlockSpec((tm, D_MODEL), lambda i: (i, 0)),
                  pl.BlockSpec((1, D_MODEL), lambda i: (0, 0))],
        out_specs=pl.BlockSpec((tm, D_MODEL), lambda i: (i, 0)),
        out_shape=jax.ShapeDtypeStruct((T, D_MODEL), F32),
        compiler_params=_cparams("parallel"),
        name="final_norm",
    )(h2, g)


def _prep_w_in(w_in):
    o_g = 4 * M_WIDTH
    o_cq = o_g + 2 * M_HEADS
    o_ckv = o_cq + A_QRANK
    o_kr = o_ckv + A_KVRANK
    o_up = o_kr + A_ROPE
    z = lambda n: jnp.zeros(w_in.shape[:-1] + (n,), w_in.dtype)
    small = jnp.concatenate([z(SMALL_KR), w_in[..., o_kr:o_up], w_in[..., o_g:o_cq],
                             z(LANES - SMALL_GATE - 2 * M_HEADS)], axis=-1)
    return jnp.concatenate([w_in[..., 0:o_g], w_in[..., o_cq:o_ckv], w_in[..., o_up:o_up + P_WIDTH],
                            w_in[..., o_ckv:o_kr], small], axis=-1).astype(BF16)


def _rope_tables(seq):
    inv = ROPE_THETA ** (-jnp.arange(0, A_ROPE, 2, dtype=F32) / A_ROPE)
    ang = jnp.arange(seq, dtype=F32)[:, None] * inv[None, :]
    cos, sin = jnp.cos(ang), jnp.sin(ang)
    half = A_ROPE // 2
    zeros = lambda n: jnp.zeros((seq, n), F32)
    ones = lambda n: jnp.ones((seq, n), F32)
    tail = LANES - A_NOPE - A_ROPE
    cq_t = jnp.concatenate([ones(A_NOPE), cos, cos, zeros(tail)], axis=1)
    ck_t = jnp.concatenate([zeros(A_NOPE), cos, cos, zeros(tail)], axis=1)
    s1_t = jnp.concatenate([zeros(A_NOPE), -sin, zeros(half), zeros(tail)], axis=1)
    s2_t = jnp.concatenate([zeros(A_NOPE), zeros(half), sin, zeros(tail)], axis=1)
    return cq_t, ck_t, s1_t, s2_t


def kernel(x, norm1_g, w_in, conv_w, conv_b, gate_b, mlstm_norm_g, q_norm_g, kv_norm_g, w_uq, w_ukv,
           w_pool, pool_scale, w_out, norm2_g, w_router, b_router, w_gate_up, b_gate_up, w_down, b_down,
           final_norm_g):
    B, S, D = x.shape
    depth = w_in.shape[0]
    T = B * S
    nb = (T * TOP_K) // MOE_ROWS + N_EXPERTS
    nb_pad = -(-nb // LANES) * LANES

    w_in_p = _prep_w_in(w_in)
    wq = w_uq.reshape(depth, A_QRANK, A_HEADS, A_NOPE + A_ROPE)
    wq = jnp.pad(wq, ((0, 0), (0, 0), (0, 0), (0, LANES - A_NOPE - A_ROPE)))
    wq = wq.reshape(depth, A_QRANK, A_HEADS * LANES).astype(BF16)
    wkv = w_ukv.reshape(depth, A_KVRANK, A_HEADS, A_NOPE + A_VDIM)
    wk = jnp.pad(wkv[..., :A_NOPE], ((0, 0), (0, 0), (0, 0), (0, LANES - A_NOPE)))
    wk = wk.reshape(depth, A_KVRANK, A_HEADS * LANES).astype(BF16)
    wv_e = jnp.pad(wkv[:, :, 0::2, A_NOPE:], ((0, 0), (0, 0), (0, 0), (0, LANES - A_VDIM)))
    wv_o = jnp.pad(wkv[:, :, 1::2, A_NOPE:], ((0, 0), (0, 0), (0, 0), (LANES - A_VDIM, 0)))
    wv = jnp.stack([wv_e, wv_o], axis=3).reshape(depth, A_KVRANK, A_HEADS * LANES).astype(BF16)
    half = jnp.arange(A_HEADS * LANES) // A_VDIM
    vone = ((half % 4 == 1) | (half % 4 == 2)).astype(F32)[None, :]
    gsz = P_WIDTH // len(P_WINDOWS)
    w_pool_bd = jnp.zeros((depth, P_WIDTH, P_WIDTH), F32)
    for gi in range(len(P_WINDOWS)):
        w_pool_bd = w_pool_bd.at[:, gi * gsz:(gi + 1) * gsz, gi * gsz:(gi + 1) * gsz].set(w_pool[:, gi])
    w_pool_bd = w_pool_bd.astype(BF16)
    w_out16 = w_out.astype(BF16)
    w_router_t = jnp.swapaxes(w_router, 1, 2)
    gate_b_col = jnp.pad(gate_b, ((0, 0), (SMALL_GATE, LANES - SMALL_GATE - 2 * M_HEADS)))
    cq_t, ck_t, s1_t, s2_t = _rope_tables(S)
    tr = 512
    tri = (jnp.arange(tr)[:, None] < jnp.arange(tr)[None, :]).astype(BF16)

    wgu_all = w_gate_up.reshape(depth * N_EXPERTS, D_MODEL, 2 * D_FF)
    bgu_all = b_gate_up.reshape(depth * N_EXPERTS, 1, 2 * D_FF)
    wdn_all = w_down.reshape(depth * N_EXPERTS, D_FF, D_MODEL)
    bdn_all = b_down.reshape(depth * N_EXPERTS, 1, D_MODEL)

    h = x.reshape(T, D)
    for l in range(depth):
        proj = _inproj(h, norm1_g[l][None, :], w_in_p[l])
        gates_t = proj[:, PROJ_SMALL + SMALL_GATE:PROJ_SMALL + SMALL_GATE + 2 * M_HEADS]
        gates_t = gates_t.reshape(B, S // M_CHUNK, M_CHUNK, 2 * M_HEADS).transpose(0, 1, 3, 2)
        y_m = _mlstm(proj, gates_t, conv_w[l], conv_b[l][None, :], gate_b_col[l][None, :],
                     gate_b[l][:, None], mlstm_norm_g[l][None, :], B, S)
        q16, k16, v16 = _mla_prep(proj, q_norm_g[l][None, :], kv_norm_g[l][None, :], wq[l], wk[l], wv[l],
                                  vone, cq_t, ck_t, s1_t, s2_t, B, S)
        y_a = _attention(q16, k16, v16, B, S)
        y_p = _pool(proj, w_pool_bd[l], pool_scale[l][None, :], B, S)
        h, xn, logits_t = _outproj(y_m, y_a, y_p, h, w_out16[l], norm2_g[l][None, :],
                                   w_router_t[l], b_router[l][:, None])
        eidx, wts, rank, counts = _router(logits_t, tri, tr)
        dest, blk_e, blk_cnt, blk_next, nblk = _meta(counts, eidx, rank, nb_pad)
        dest_flat = dest.reshape(TOP_K * T)
        x_rows = _sc_dispatch(xn, dest_flat, nb * MOE_ROWS)
        y_rows = _gmm(blk_e[0], blk_cnt[0], blk_next[0], nblk[0], x_rows, wgu_all, bgu_all, wdn_all,
                      bdn_all, nb, l)
        y_tok = _sc_gather(y_rows, dest_flat)
        h = _combine(y_tok, h, wts.T)
    return _final_norm(h, final_norm_g[None, :]).reshape(B, S, D)
```

```python
import functools

import jax
import jax.numpy as jnp
import numpy as np
from jax import lax
from jax.experimental import pallas as pl
from jax.experimental.pallas import tpu as pltpu
from jax.experimental.pallas import tpu_sc as plsc

F32 = jnp.float32
BF16 = jnp.bfloat16
HIGHEST = lax.Precision.HIGHEST

D_MODEL = 1024
M_HEADS = 4
M_HEAD_DIM = 64
M_WIDTH = 256
M_CONV = 4
M_CHUNK = 64
A_HEADS = 8
A_NOPE = 64
A_ROPE = 32
A_VDIM = 64
A_QRANK = 256
A_KVRANK = 128
A_WIDTH = 512
ROPE_THETA = 10000.0
P_WINDOWS = (2, 4, 8, 16)
P_WIDTH = 256
N_EXPERTS = 32
TOP_K = 4
D_FF = 1024
SWIGLU_LIMIT = 7.0
SWIGLU_ALPHA = 1.702
EPS = 1e-6

LANES = 128
SUBLANES = 8

PROJ_QKVO = 0
PROJ_CQ = 1024
PROJ_UP = 1280
PROJ_CKV = 1536
PROJ_SMALL = 1664
PROJ_WIDTH = 1792
SMALL_KR = 64
SMALL_GATE = 96

MOE_ROWS = 512
VMEM_LIMIT = 56 * 1024 * 1024


def _cparams(*sem):
    return pltpu.CompilerParams(dimension_semantics=sem, vmem_limit_bytes=VMEM_LIMIT)


def _sigmoid(x):
    return 1.0 / (1.0 + jnp.exp(-x))


def _log_sigmoid(x):
    return jnp.minimum(x, 0.0) - jnp.log(1.0 + jnp.exp(-jnp.abs(x)))


def _dot(a, b, **kw):
    return jnp.dot(a, b, preferred_element_type=F32, **kw)


def _dot_nt(a, b, **kw):
    return lax.dot_general(a, b, (((1,), (1,)), ((), ())), preferred_element_type=F32, **kw)


def _dot_tn(a, b, **kw):
    return lax.dot_general(a, b, (((0,), (0,)), ((), ())), preferred_element_type=F32, **kw)


def _inproj_kernel(h_ref, g_ref, w_ref, o_ref):
    x = h_ref[...]
    ms = jnp.mean(x * x, axis=-1, keepdims=True)
    xn = x * lax.rsqrt(ms + EPS) * g_ref[...]
    o_ref[...] = _dot(xn.astype(BF16), w_ref[...])


def _inproj(h2, g, w, tm=512):
    T = h2.shape[0]
    return pl.pallas_call(
        _inproj_kernel,
        grid=(T // tm,),
        in_specs=[pl.BlockSpec((tm, D_MODEL), lambda i: (i, 0)),
                  pl.BlockSpec((1, D_MODEL), lambda i: (0, 0)),
                  pl.BlockSpec((D_MODEL, PROJ_WIDTH), lambda i: (0, 0))],
        out_specs=pl.BlockSpec((tm, PROJ_WIDTH), lambda i: (i, 0)),
        out_shape=jax.ShapeDtypeStruct((T, PROJ_WIDTH), F32),
        compiler_params=_cparams("parallel"),
        name="inproj",
    )(h2, g, w)


M_SEQS = 2


def _mlstm_kernel(proj_ref, small_ref, gt_ref, cw_ref, cb_ref, gbc_ref, gbr_ref, ng_ref, o_ref,
                  ct_ref, n_ref, m_ref):
    S = proj_ref.shape[0] // M_SEQS
    L = M_CHUNK
    nc = S // L
    W = M_WIDTH
    ct_ref[...] = jnp.zeros_like(ct_ref)
    n_ref[...] = jnp.zeros_like(n_ref)
    m_ref[...] = jnp.zeros_like(m_ref)

    lane_head = lax.broadcasted_iota(jnp.int32, (1, W), 1) // M_HEAD_DIM
    masks = [(lane_head == h).astype(F32) for h in range(M_HEADS)]
    rh = lax.broadcasted_iota(jnp.int32, (W, W), 0) // M_HEAD_DIM
    chd = lax.broadcasted_iota(jnp.int32, (W, W), 1) // M_HEAD_DIM
    same_head = (rh == chd).astype(F32)
    ri = lax.broadcasted_iota(jnp.int32, (L, L), 0)
    ci = lax.broadcasted_iota(jnp.int32, (L, L), 1)
    causal = ri >= ci
    tril = causal.astype(F32)
    triu = (ri <= ci).astype(F32)
    cw = cw_ref[...]
    cb = cb_ref[...]
    gbc = gbc_ref[...]
    gbr = gbr_ref[...]
    ng = ng_ref[...]

    halo_row = lax.broadcasted_iota(jnp.int32, (L + SUBLANES, 1), 0) >= SUBLANES

    def chunk(sq, c):
        r0 = pl.multiple_of(sq * S + c * L, L)
        w0 = pl.multiple_of(sq * S + jnp.maximum(c * L - SUBLANES, 0), SUBLANES)
        win = proj_ref[pl.ds(w0, L + SUBLANES), 0:2 * W]
        first = jnp.where(halo_row, pltpu.roll(win, SUBLANES, axis=0), 0.0)
        win = jnp.where(c == 0, first, win)
        acc = jnp.zeros((L, 2 * W), F32) + cb
        for j in range(M_CONV):
            s = M_CONV - 1 - j
            xs = win if s == 0 else pltpu.roll(win, s, axis=0)
            acc = acc + xs[SUBLANES:, :] * cw[j:j + 1, :]
        qk = acc * _sigmoid(acc)
        q = qk[:, 0:W]
        k = qk[:, W:2 * W] * (M_HEAD_DIM ** -0.5)
        v = proj_ref[pl.ds(r0, L), 2 * W:3 * W]
        og = proj_ref[pl.ds(r0, L), 3 * W:4 * W]
        sm = small_ref[pl.ds(r0, L), :] + gbc
        gt = gt_ref[sq, c] + gbr
        i_col_all = sm[:, SMALL_GATE:SMALL_GATE + M_HEADS]
        logf_col = _log_sigmoid(sm[:, SMALL_GATE + M_HEADS:SMALL_GATE + 2 * M_HEADS])
        g_col_all = _dot(tril, logf_col, precision=HIGHEST)
        i_row_all = gt[0:M_HEADS, :]
        logf_row = _log_sigmoid(gt[M_HEADS:2 * M_HEADS, :])
        g_row_all = _dot(logf_row, triu, precision=HIGHEST)
        yield

        q16 = q.astype(BF16)
        k16 = k.astype(BF16)
        v16 = v.astype(BF16)
        num_intra = jnp.zeros((L, W), F32)
        w_inter_f = jnp.zeros((L, W), F32)
        rowsum_f = jnp.zeros((L, W), F32)
        floor_f = jnp.zeros((L, W), F32)
        wa_f = jnp.zeros((L, W), F32)
        dec_f = jnp.zeros((1, W), F32)
        for h in range(M_HEADS):
            mk = masks[h]
            g_col = g_col_all[:, h:h + 1]
            i_col = i_col_all[:, h:h + 1]
            g_row = g_row_all[h:h + 1, :]
            i_row = i_row_all[h:h + 1, :]
            m_old = m_ref[sq, 0:1, h:h + 1]
            dmat = jnp.where(causal, g_col - g_row + i_row, -jnp.inf)
            inter = g_col + m_old
            m_row = jnp.maximum(inter, jnp.max(dmat, axis=-1, keepdims=True))
            sc = _dot_nt((q * mk).astype(BF16), k16) * jnp.exp(dmat - m_row)
            yield
            w_inter = jnp.exp(inter - m_row)
            num_intra = num_intra + _dot(sc.astype(BF16), v16) * mk
            w_inter_f = w_inter_f + w_inter * mk
            rowsum_f = rowsum_f + jnp.sum(sc, axis=-1, keepdims=True) * mk
            floor_f = floor_f + jnp.exp(-m_row) * mk
            g_end = g_col[L - 1:L, :]
            a = g_end - g_col + i_col
            m_new = jnp.maximum(g_end + m_old, jnp.max(a, axis=0, keepdims=True))
            wa_f = wa_f + jnp.exp(a - m_new) * mk
            dec_f = dec_f + jnp.exp(g_end + m_old - m_new) * mk
            m_ref[sq, 0:1, h:h + 1] = m_new
            yield

        ct = ct_ref[sq]
        nvec = n_ref[sq]
        q_c = _dot(q16, ct.astype(BF16))
        q_n = _dot(q * nvec, same_head, precision=HIGHEST)
        yield
        num = w_inter_f * q_c + num_intra
        den = w_inter_f * q_n + rowsum_f
        hv = num / jnp.maximum(jnp.abs(den), floor_f)
        ms = _dot(hv * hv, same_head, precision=HIGHEST) * (1.0 / M_HEAD_DIM)
        yield
        y = hv * lax.rsqrt(ms + EPS) * ng * _sigmoid(og)
        o_ref[pl.ds(r0, L), :] = y

        upd = _dot_tn(k16, (wa_f * v).astype(BF16))
        ct_ref[sq] = dec_f * ct + upd * same_head
        n_ref[sq] = dec_f * nvec + jnp.sum(wa_f * k, axis=0, keepdims=True)
        yield

    def body(c, carry):
        for _ in zip(*[chunk(sq, c) for sq in range(M_SEQS)]):
            pass
        return carry

    lax.fori_loop(0, nc, body, 0)


def _mlstm(proj, gates_t, cw, cb, gbc, gbr, ng, B, S):
    T = B * S
    nc = S // M_CHUNK
    rows = M_SEQS * S
    return pl.pallas_call(
        _mlstm_kernel,
        grid=(B // M_SEQS,),
        in_specs=[pl.BlockSpec((rows, 4 * M_WIDTH), lambda b: (b, 0)),
                  pl.BlockSpec((rows, LANES), lambda b: (b, PROJ_SMALL // LANES)),
                  pl.BlockSpec((M_SEQS, nc, 2 * M_HEADS, M_CHUNK), lambda b: (b, 0, 0, 0)),
                  pl.BlockSpec((M_CONV, 2 * M_WIDTH), lambda b: (0, 0)),
                  pl.BlockSpec((1, 2 * M_WIDTH), lambda b: (0, 0)),
                  pl.BlockSpec((1, LANES), lambda b: (0, 0)),
                  pl.BlockSpec((2 * M_HEADS, 1), lambda b: (0, 0)),
                  pl.BlockSpec((1, M_WIDTH), lambda b: (0, 0))],
        out_specs=pl.BlockSpec((rows, M_WIDTH), lambda b: (b, 0)),
        out_shape=jax.ShapeDtypeStruct((T, M_WIDTH), F32),
        scratch_shapes=[pltpu.VMEM((M_SEQS, M_WIDTH, M_WIDTH), F32),
                        pltpu.VMEM((M_SEQS, 1, M_WIDTH), F32),
                        pltpu.VMEM((M_SEQS, 1, LANES), F32)],
        compiler_params=_cparams("parallel"),
        name="mlstm",
    )(proj, proj, gates_t, cw, cb, gbc, gbr, ng)


POOL_HALO = 16
POOL_TILE = 256


def _pool_kernel(u_ref, w_ref, sc_ref, o_ref, upad_ref):
    S = u_ref.shape[0]
    upad_ref[0:POOL_HALO, :] = jnp.zeros((POOL_HALO, P_WIDTH), F32)
    upad_ref[POOL_HALO:, :] = u_ref[...]
    grp = lax.broadcasted_iota(jnp.int32, (1, P_WIDTH), 1) // (P_WIDTH // len(P_WINDOWS))
    win_lane = jnp.zeros((1, P_WIDTH), jnp.int32)
    for gi, wn in enumerate(P_WINDOWS):
        win_lane = jnp.where(grp == gi, wn, win_lane)
    w = w_ref[...]
    scale = sc_ref[...]
    rows = POOL_TILE + POOL_HALO

    def body(r, carry):
        r0 = pl.multiple_of(r * POOL_TILE, POOL_TILE)
        a = upad_ref[pl.ds(r0, rows), :]
        sums = []
        cur = a
        span = 1
        for _ in P_WINDOWS:
            cur = cur + pltpu.roll(cur, span, axis=0)
            span *= 2
            sums.append(cur)
        sel = sums[-1]
        for gi in range(len(P_WINDOWS) - 1):
            sel = jnp.where(grp == gi, sums[gi], sel)
        sel = sel[POOL_HALO:, :]
        u = a[POOL_HALO:, :]
        t = r0 + lax.broadcasted_iota(jnp.int32, (POOL_TILE, P_WIDTH), 0)
        cnt = jnp.minimum(t + 1, win_lane).astype(F32)
        pooled = sel / cnt - u
        o_ref[pl.ds(r0, POOL_TILE), :] = _dot(pooled.astype(BF16), w) * scale
        return carry

    lax.fori_loop(0, S // POOL_TILE, body, 0)


def _pool(proj, w_bd, scale, B, S):
    T = B * S
    return pl.pallas_call(
        _pool_kernel,
        grid=(B,),
        in_specs=[pl.BlockSpec((S, P_WIDTH), lambda b: (b, PROJ_UP // P_WIDTH)),
                  pl.BlockSpec((P_WIDTH, P_WIDTH), lambda b: (0, 0)),
                  pl.BlockSpec((1, P_WIDTH), lambda b: (0, 0))],
        out_specs=pl.BlockSpec((S, P_WIDTH), lambda b: (b, 0)),
        out_shape=jax.ShapeDtypeStruct((T, P_WIDTH), F32),
        scratch_shapes=[pltpu.VMEM((S + POOL_HALO, P_WIDTH), F32)],
        compiler_params=_cparams("parallel"),
        name="pool",
    )(proj, w_bd, scale)


def _rope(x, c, s1, s2):
    return x * c + pltpu.roll(x, LANES - A_ROPE // 2, axis=1) * s1 + pltpu.roll(x, A_ROPE // 2, axis=1) * s2


def _mla_prep_kernel(cq_ref, ckv_ref, small_ref, qg_ref, kvg_ref, wq_ref, wk_ref, wv_ref,
                     vone_ref, cq_t_ref, ck_t_ref, s1_ref, s2_ref, q_ref, k_ref, v_ref):
    def rms(x, g):
        return x * lax.rsqrt(jnp.mean(x * x, axis=-1, keepdims=True) + EPS) * g

    cqn = rms(cq_ref[...], qg_ref[...]).astype(BF16)
    ckvn = rms(ckv_ref[...], kvg_ref[...]).astype(BF16)
    scale = (A_NOPE + A_ROPE) ** -0.5
    qf = _dot(cqn, wq_ref[...]) * scale
    kf = _dot(ckvn, wk_ref[...])
    v_ref[...] = (_dot(ckvn, wv_ref[...]) + vone_ref[...]).astype(BF16)
    cqt = cq_t_ref[...]
    s1 = s1_ref[...]
    s2 = s2_ref[...]
    krot = _rope(small_ref[...], ck_t_ref[...], s1, s2)
    for h in range(A_HEADS):
        sl = slice(h * LANES, (h + 1) * LANES)
        q_ref[:, sl] = _rope(qf[:, sl], cqt, s1, s2).astype(BF16)
        k_ref[:, sl] = (kf[:, sl] + krot).astype(BF16)


def _mla_prep(proj, qg, kvg, wq, wk, wv, vone, cq_t, ck_t, s1_t, s2_t, B, S, ts=512):
    T = B * S
    nst = S // ts
    hw = A_HEADS * LANES
    return pl.pallas_call(
        _mla_prep_kernel,
        grid=(B, nst),
        in_specs=[pl.BlockSpec((ts, A_QRANK), lambda b, s: (b * nst + s, PROJ_CQ // A_QRANK)),
                  pl.BlockSpec((ts, A_KVRANK), lambda b, s: (b * nst + s, PROJ_CKV // A_KVRANK)),
                  pl.BlockSpec((ts, LANES), lambda b, s: (b * nst + s, PROJ_SMALL // LANES)),
                  pl.BlockSpec((1, A_QRANK), lambda b, s: (0, 0)),
                  pl.BlockSpec((1, A_KVRANK), lambda b, s: (0, 0)),
                  pl.BlockSpec((A_QRANK, hw), lambda b, s: (0, 0)),
                  pl.BlockSpec((A_KVRANK, hw), lambda b, s: (0, 0)),
                  pl.BlockSpec((A_KVRANK, hw), lambda b, s: (0, 0)),
                  pl.BlockSpec((1, hw), lambda b, s: (0, 0)),
                  pl.BlockSpec((ts, LANES), lambda b, s: (s, 0)),
                  pl.BlockSpec((ts, LANES), lambda b, s: (s, 0)),
                  pl.BlockSpec((ts, LANES), lambda b, s: (s, 0)),
                  pl.BlockSpec((ts, LANES), lambda b, s: (s, 0))],
        out_specs=[pl.BlockSpec((ts, hw), lambda b, s: (b * nst + s, 0)),
                   pl.BlockSpec((ts, hw), lambda b, s: (b * nst + s, 0)),
                   pl.BlockSpec((ts, hw), lambda b, s: (b * nst + s, 0))],
        out_shape=[jax.ShapeDtypeStruct((T, hw), BF16),
                   jax.ShapeDtypeStruct((T, hw), BF16),
                   jax.ShapeDtypeStruct((T, hw), BF16)],
        compiler_params=_cparams("parallel", "parallel"),
        name="mla_prep",
    )(proj, proj, proj, qg, kvg, wq, wk, wv, vone, cq_t, ck_t, s1_t, s2_t)


def _attn_kernel(q_ref, k_ref, v_ref, o_ref, *, tq):
    qi = pl.program_id(2)
    ri = lax.broadcasted_iota(jnp.int32, (tq, tq), 0)
    ci = lax.broadcasted_iota(jnp.int32, (tq, tq), 1)
    diag_ok = ri >= ci
    sls = [slice(hh * LANES, (hh + 1) * LANES) for hh in range(2)]
    qs = [q_ref[:, sl] for sl in sls]

    def step(kb, carry, masked):
        k0 = pl.multiple_of(kb * tq, tq)
        heads = range(2)
        s = [_dot_nt(qs[hh], k_ref[pl.ds(k0, tq), sls[hh]]) for hh in heads]
        if masked:
            s = [jnp.where(diag_ok, s[hh], -jnp.inf) for hh in heads]
        m_new = [jnp.maximum(carry[hh][0], jnp.max(s[hh], axis=-1, keepdims=True)) for hh in heads]
        p = [jnp.exp(s[hh] - m_new[hh]).astype(BF16) for hh in heads]
        pv = [_dot(p[hh], v_ref[pl.ds(k0, tq), sls[hh]]) for hh in heads]
        acc = [jnp.exp(carry[hh][0] - m_new[hh]) * carry[hh][1] + pv[hh] for hh in heads]
        return tuple((m_new[hh], acc[hh]) for hh in heads)

    init = tuple((jnp.full((tq, 1), -jnp.inf, F32), jnp.zeros((tq, LANES), F32)) for _ in range(2))
    carry = lax.fori_loop(0, qi, functools.partial(step, masked=False), init)
    (_, acc0), (_, acc1) = step(qi, carry, True)
    lane = lax.broadcasted_iota(jnp.int32, (tq, LANES), 1)
    acc = jnp.where(lane < A_VDIM, acc0, acc1)
    den = jnp.where(lane < A_VDIM, pltpu.roll(acc0, A_VDIM, axis=1), pltpu.roll(acc1, A_VDIM, axis=1))
    o_ref[...] = (acc / den).astype(o_ref.dtype)


def _attention(q, k, v, B, S, tq=512):
    T = B * S
    nq = S // tq
    return pl.pallas_call(
        functools.partial(_attn_kernel, tq=tq),
        grid=(B, A_HEADS // 2, nq),
        in_specs=[pl.BlockSpec((tq, 2 * LANES), lambda b, p, i: (b * nq + i, p)),
                  pl.BlockSpec((S, 2 * LANES), lambda b, p, i: (b, p)),
                  pl.BlockSpec((S, 2 * LANES), lambda b, p, i: (b, p))],
        out_specs=pl.BlockSpec((tq, LANES), lambda b, p, i: (b * nq + i, p)),
        out_shape=jax.ShapeDtypeStruct((T, A_WIDTH), BF16),
        compiler_params=_cparams("parallel", "parallel", "arbitrary"),
        name="attention",
    )(q, k, v)


def _outproj_kernel(ym_ref, ya_ref, yp_ref, h_ref, w_ref, g_ref, wr_ref, br_ref,
                    hn_ref, xn_ref, lg_ref):
    mix = _dot(ym_ref[...].astype(BF16), w_ref[0:M_WIDTH, :])
    mix = mix + _dot(ya_ref[...], w_ref[M_WIDTH:M_WIDTH + A_WIDTH, :])
    mix = mix + _dot(yp_ref[...].astype(BF16), w_ref[M_WIDTH + A_WIDTH:, :])
    hn = h_ref[...] + mix
    hn_ref[...] = hn
    xn = hn * lax.rsqrt(jnp.mean(hn * hn, axis=-1, keepdims=True) + EPS) * g_ref[...]
    x_hi = xn.astype(BF16)
    x_lo = (xn - x_hi.astype(F32)).astype(BF16)
    wr = wr_ref[...]
    w_hi = wr.astype(BF16)
    w_lo = (wr - w_hi.astype(F32)).astype(BF16)
    lg_ref[...] = _dot_nt(w_hi, x_hi) + _dot_nt(w_hi, x_lo) + _dot_nt(w_lo, x_hi) + br_ref[...]
    xn_ref[...] = _pack_bf16_pairs(xn)


def _outproj(ym, ya, yp, h2, w, g, wr_t, br, tm=512):
    T = h2.shape[0]
    return pl.pallas_call(
        _outproj_kernel,
        grid=(T // tm,),
        in_specs=[pl.BlockSpec((tm, M_WIDTH), lambda i: (i, 0)),
                  pl.BlockSpec((tm, A_WIDTH), lambda i: (i, 0)),
                  pl.BlockSpec((tm, P_WIDTH), lambda i: (i, 0)),
                  pl.BlockSpec((tm, D_MODEL), lambda i: (i, 0)),
                  pl.BlockSpec((D_MODEL, D_MODEL), lambda i: (0, 0)),
                  pl.BlockSpec((1, D_MODEL), lambda i: (0, 0)),
                  pl.BlockSpec((N_EXPERTS, D_MODEL), lambda i: (0, 0)),
                  pl.BlockSpec((N_EXPERTS, 1), lambda i: (0, 0))],
        out_specs=[pl.BlockSpec((tm, D_MODEL), lambda i: (i, 0)),
                   pl.BlockSpec((tm, D_MODEL // 2), lambda i: (i, 0)),
                   pl.BlockSpec((N_EXPERTS, tm), lambda i: (0, i))],
        out_shape=[jax.ShapeDtypeStruct((T, D_MODEL), F32),
                   jax.ShapeDtypeStruct((T, D_MODEL // 2), jnp.uint32),
                   jax.ShapeDtypeStruct((N_EXPERTS, T), F32)],
        compiler_params=_cparams("parallel"),
        name="outproj",
    )(ym, ya, yp, h2, w, g, wr_t, br)


def _router_kernel(lg_ref, tri_ref, e_ref, w_ref, r_ref, cnt_ref, carry_ref):
    tr = lg_ref.shape[1]

    @pl.when(pl.program_id(0) == 0)
    def _():
        carry_ref[...] = jnp.zeros_like(carry_ref)

    x = lg_ref[...]
    eio = lax.broadcasted_iota(jnp.int32, (N_EXPERTS, tr), 0).astype(F32)
    picked = jnp.zeros((N_EXPERTS, tr), F32)
    vals = []
    idxs = []
    for _ in range(TOP_K):
        mx = jnp.max(x, axis=0, keepdims=True)
        idx = jnp.min(jnp.where(x == mx, eio, float(N_EXPERTS)), axis=0, keepdims=True)
        hit = eio == idx
        vals.append(mx)
        idxs.append(idx)
        picked = picked + hit.astype(F32)
        x = jnp.where(hit, -jnp.inf, x)
    exps = [jnp.exp(vv - vals[0]) for vv in vals]
    tot = exps[0] + exps[1] + exps[2] + exps[3]
    before = _dot(picked.astype(BF16), tri_ref[...]) + carry_ref[:, 0:1]
    for kk in range(TOP_K):
        e_ref[kk:kk + 1, :] = idxs[kk].astype(jnp.int32)
        w_ref[kk:kk + 1, :] = exps[kk] / tot
        rk = jnp.sum(jnp.where(eio == idxs[kk], before, 0.0), axis=0, keepdims=True)
        r_ref[kk:kk + 1, :] = rk.astype(jnp.int32)
    carry_ref[...] = carry_ref[...] + jnp.sum(picked, axis=1, keepdims=True)
    cnt_ref[...] = carry_ref[...]


def _router(logits_t, tri, tr=512):
    T = logits_t.shape[1]
    return pl.pallas_call(
        _router_kernel,
        grid=(T // tr,),
        in_specs=[pl.BlockSpec((N_EXPERTS, tr), lambda i: (0, i)),
                  pl.BlockSpec((tr, tr), lambda i: (0, 0))],
        out_specs=[pl.BlockSpec((TOP_K, tr), lambda i: (0, i)),
                   pl.BlockSpec((TOP_K, tr), lambda i: (0, i)),
                   pl.BlockSpec((TOP_K, tr), lambda i: (0, i)),
                   pl.BlockSpec((N_EXPERTS, LANES), lambda i: (0, 0))],
        out_shape=[jax.ShapeDtypeStruct((TOP_K, T), jnp.int32),
                   jax.ShapeDtypeStruct((TOP_K, T), F32),
                   jax.ShapeDtypeStruct((TOP_K, T), jnp.int32),
                   jax.ShapeDtypeStruct((N_EXPERTS, LANES), F32)],
        scratch_shapes=[pltpu.VMEM((N_EXPERTS, LANES), F32)],
        compiler_params=_cparams("arbitrary"),
        name="router",
    )(logits_t, tri)


def _meta_kernel(cnt_ref, e_ref, r_ref, dest_ref, be_ref, bc_ref, bn_ref, nb_ref, *, nb_pad):
    cnt = cnt_ref[...]
    padded = jnp.floor((cnt + (MOE_ROWS - 1)) * (1.0 / MOE_ROWS)) * MOE_ROWS
    ri = lax.broadcasted_iota(jnp.int32, (N_EXPERTS, N_EXPERTS), 0)
    ci = lax.broadcasted_iota(jnp.int32, (N_EXPERTS, N_EXPERTS), 1)
    pad_end = _dot((ri >= ci).astype(F32), padded, precision=HIGHEST)
    pad_start = pad_end - padded
    e = e_ref[...]
    dest = r_ref[...]
    for ex in range(N_EXPERTS):
        ps = pad_start[ex:ex + 1, 0:1].astype(jnp.int32)
        dest = jnp.where(e == ex, dest + ps, dest)
    dest_ref[...] = dest
    blk0 = (lax.broadcasted_iota(jnp.int32, (N_EXPERTS, nb_pad), 1) * MOE_ROWS).astype(F32)
    be = jnp.sum((pad_end[:, 0:1] <= blk0).astype(F32), axis=0, keepdims=True)
    be = jnp.minimum(be, float(N_EXPERTS - 1))
    eio = lax.broadcasted_iota(jnp.int32, (N_EXPERTS, nb_pad), 0).astype(F32)
    seg_end = jnp.sum(jnp.where(eio == be, pad_start[:, 0:1] + cnt[:, 0:1], 0.0), axis=0, keepdims=True)
    bc = jnp.clip(seg_end - blk0[0:1, :], 0.0, float(MOE_ROWS))
    nxt0 = jnp.sum(jnp.where(eio == be, pad_end[:, 0:1], 0.0), axis=0, keepdims=True)
    bn = jnp.sum((pad_end[:, 0:1] <= nxt0).astype(F32), axis=0, keepdims=True)
    bn = jnp.where(nxt0 < pad_end[N_EXPERTS - 1:N_EXPERTS, 0:1], bn, -1.0)
    be_ref[...] = be.astype(jnp.int32)
    bc_ref[...] = bc.astype(jnp.int32)
    bn_ref[...] = bn.astype(jnp.int32)
    nb_ref[...] = (pad_end[N_EXPERTS - 1:N_EXPERTS, :] * (1.0 / MOE_ROWS)).astype(jnp.int32)


def _meta(counts, eidx, rank, nb_pad):
    T = eidx.shape[1]
    return pl.pallas_call(
        functools.partial(_meta_kernel, nb_pad=nb_pad),
        out_shape=[jax.ShapeDtypeStruct((TOP_K, T), jnp.int32),
                   jax.ShapeDtypeStruct((1, nb_pad), jnp.int32),
                   jax.ShapeDtypeStruct((1, nb_pad), jnp.int32),
                   jax.ShapeDtypeStruct((1, nb_pad), jnp.int32),
                   jax.ShapeDtypeStruct((1, LANES), jnp.int32)],
        compiler_params=pltpu.CompilerParams(vmem_limit_bytes=VMEM_LIMIT),
        name="route_meta",
    )(counts, eidx, rank)


FF_CHUNK = 512


def _pack_bf16_pairs(x):
    n = x.shape[1] // 2
    lo = lax.bitcast_convert_type(x[:, :n].astype(BF16).astype(F32), jnp.uint32)
    hi = lax.bitcast_convert_type(x[:, n:].astype(BF16).astype(F32), jnp.uint32)
    return (lo >> 16) | (hi & jnp.uint32(0xFFFF0000))


def _unpack_bf16_pairs(p):
    lo = lax.bitcast_convert_type(p << 16, F32)
    hi = lax.bitcast_convert_type(p & jnp.uint32(0xFFFF0000), F32)
    return lo, hi


def _gmm_kernel(be_ref, bc_ref, bn_ref, nb_ref, x_ref, wgu_hbm, bgu_ref, wdn_hbm, bdn_ref, y_ref,
                wgu_st, wdn_st, wgu16, wdn16, sem, *, e0):
    i = pl.program_id(0)
    nblk = nb_ref[0]
    bm = MOE_ROWS

    def weight_copies(e):
        return (pltpu.make_async_copy(wgu_hbm.at[e0 + e], wgu_st, sem.at[0]),
                pltpu.make_async_copy(wdn_hbm.at[e0 + e], wdn_st, sem.at[1]))

    @pl.when(i == 0)
    def _():
        for cp in weight_copies(be_ref[0]):
            cp.start()

    @pl.when(i >= nblk)
    def _():
        y_ref[...] = jnp.zeros_like(y_ref)

    @pl.when(i < nblk)
    def _():
        e_changed = jnp.logical_or(i == 0, be_ref[i] != be_ref[jnp.maximum(i - 1, 0)])

        @pl.when(e_changed)
        def _():
            for cp in weight_copies(be_ref[i]):
                cp.wait()
            wgu16[...] = wgu_st[...].astype(BF16)
            wdn16[...] = wdn_st[...].astype(BF16)

            @pl.when(bn_ref[i] >= 0)
            def _():
                for cp in weight_copies(bn_ref[i]):
                    cp.start()

        valid = lax.broadcasted_iota(jnp.int32, (bm, 1), 0) < bc_ref[i]
        lo, hi = _unpack_bf16_pairs(jnp.where(valid, x_ref[...], jnp.uint32(0)))
        x16 = jnp.concatenate([lo.astype(BF16), hi.astype(BF16)], axis=1)
        acc = jnp.zeros((bm, D_MODEL), F32) + bdn_ref[0]
        for c in range(D_FF // FF_CHUNK):
            cs = slice(c * FF_CHUNK, (c + 1) * FF_CHUNK)
            us = slice(D_FF + c * FF_CHUNK, D_FF + (c + 1) * FF_CHUNK)
            gate = _dot(x16, wgu16[:, cs]) + bgu_ref[0, :, cs]
            up = _dot(x16, wgu16[:, us]) + bgu_ref[0, :, us]
            gate = jnp.minimum(gate, SWIGLU_LIMIT)
            up = jnp.clip(up, -SWIGLU_LIMIT, SWIGLU_LIMIT)
            act = (up + 1.0) * gate * _sigmoid(SWIGLU_ALPHA * gate)
            acc = acc + _dot(act.astype(BF16), wdn16[cs, :])
        y_ref[...] = _pack_bf16_pairs(acc)


def _gmm(blk_e, blk_cnt, blk_next, nblk, x_rows, wgu, bgu, wdn, bdn, nb, layer):
    bm = MOE_ROWS
    e0 = layer * N_EXPERTS
    dp = D_MODEL // 2

    def expert(i, be, nb_ref):
        return (e0 + be[jnp.minimum(i, jnp.maximum(nb_ref[0] - 1, 0))], 0, 0)

    def rows(i, nb_ref):
        return (jnp.minimum(i, jnp.maximum(nb_ref[0] - 1, 0)), 0)

    grid_spec = pltpu.PrefetchScalarGridSpec(
        num_scalar_prefetch=4,
        grid=(nb,),
        in_specs=[pl.BlockSpec((bm, dp), lambda i, be, bc, bn, nbr: rows(i, nbr)),
                  pl.BlockSpec(memory_space=pl.ANY),
                  pl.BlockSpec((1, 1, 2 * D_FF), lambda i, be, bc, bn, nbr: expert(i, be, nbr)),
                  pl.BlockSpec(memory_space=pl.ANY),
                  pl.BlockSpec((1, 1, D_MODEL), lambda i, be, bc, bn, nbr: expert(i, be, nbr))],
        out_specs=pl.BlockSpec((bm, dp), lambda i, be, bc, bn, nbr: (i, 0)),
        scratch_shapes=[pltpu.VMEM((D_MODEL, 2 * D_FF), F32),
                        pltpu.VMEM((D_FF, D_MODEL), F32),
                        pltpu.VMEM((D_MODEL, 2 * D_FF), BF16),
                        pltpu.VMEM((D_FF, D_MODEL), BF16),
                        pltpu.SemaphoreType.DMA((2,))],
    )
    return pl.pallas_call(
        functools.partial(_gmm_kernel, e0=e0),
        grid_spec=grid_spec,
        out_shape=jax.ShapeDtypeStruct((nb * bm, dp), jnp.uint32),
        compiler_params=_cparams("arbitrary"),
        name="expert_gmm",
    )(blk_e, blk_cnt, blk_next, nblk, x_rows, wgu, bgu, wdn, bdn)


def _combine_kernel(h_ref, y0_ref, y1_ref, y2_ref, y3_ref, w_ref, o_ref):
    w = w_ref[...]
    dp = D_MODEL // 2
    acc_lo = h_ref[:, :dp]
    acc_hi = h_ref[:, dp:]
    for kk, y_ref in enumerate((y0_ref, y1_ref, y2_ref, y3_ref)):
        lo, hi = _unpack_bf16_pairs(y_ref[...])
        acc_lo = acc_lo + lo * w[:, kk:kk + 1]
        acc_hi = acc_hi + hi * w[:, kk:kk + 1]
    o_ref[:, :dp] = acc_lo
    o_ref[:, dp:] = acc_hi


def _combine(y_tok, h2, w_tok, tc=512):
    T = h2.shape[0]
    nt = T // tc
    y_specs = [pl.BlockSpec((tc, D_MODEL // 2), functools.partial(lambda i, kk: (kk * nt + i, 0), kk=kk))
               for kk in range(TOP_K)]
    return pl.pallas_call(
        _combine_kernel,
        grid=(nt,),
        in_specs=[pl.BlockSpec((tc, D_MODEL), lambda i: (i, 0))] + y_specs
                 + [pl.BlockSpec((tc, TOP_K), lambda i: (i, 0))],
        out_specs=pl.BlockSpec((tc, D_MODEL), lambda i: (i, 0)),
        out_shape=jax.ShapeDtypeStruct((T, D_MODEL), F32),
        compiler_params=_cparams("parallel"),
        name="combine",
    )(h2, y_tok, y_tok, y_tok, y_tok, w_tok)


SC_CORES = 2
SC_SUBCORES = 16
SC_LANES = 16
SC_WORKERS = SC_CORES * SC_SUBCORES
SC_WINDOW = 64


def _sc_mesh():
    return plsc.VectorSubcoreMesh(core_axis_name="c", subcore_axis_name="s")


def _sc_worker():
    return lax.axis_index("s") * SC_CORES + lax.axis_index("c")


def _sc_dispatch(xn, dest_flat, n_rows):
    T = xn.shape[0]
    tpw = T // SC_WORKERS
    nchunk = tpw // SC_WINDOW
    nvec = SC_WINDOW // SC_LANES

    @functools.partial(
        pl.kernel, out_type=jax.ShapeDtypeStruct((n_rows, xn.shape[1]), xn.dtype), mesh=_sc_mesh(),
        scratch_types=[pltpu.VMEM((TOP_K * tpw,), jnp.int32),
                       pltpu.VMEM((SC_WINDOW, xn.shape[1]), xn.dtype),
                       pltpu.VMEM((SC_WINDOW, xn.shape[1]), xn.dtype),
                       pltpu.SemaphoreType.DMA, pltpu.SemaphoreType.DMA, pltpu.SemaphoreType.DMA],
        name="sc_dispatch")
    def run(x_hbm, d_hbm, o_hbm, idx_v, buf0, buf1, sem0, sem1, sem_out):
        base = _sc_worker() * tpw
        for kk in range(TOP_K):
            pltpu.sync_copy(d_hbm.at[pl.ds(kk * T + base, tpw)], idx_v.at[pl.ds(kk * tpw, tpw)])
        bufs = (buf0, buf1)
        sems = (sem0, sem1)

        def load(c, slot):
            return pltpu.make_async_copy(x_hbm.at[pl.ds(base + c * SC_WINDOW, SC_WINDOW)], bufs[slot], sems[slot])

        load(0, 0).start()

        @pl.loop(0, nchunk, step=2)
        def _(c0):
            for slot in range(2):
                c = c0 + slot
                load(c, slot).wait()

                @pl.when(c + 1 < nchunk)
                def _():
                    load(c + 1, 1 - slot).start()

                copies = []
                for kk in range(TOP_K):
                    for q in range(nvec):
                        off = pl.multiple_of(kk * tpw + c * SC_WINDOW + q * SC_LANES, SC_LANES)
                        rows = idx_v[pl.ds(off, SC_LANES)]
                        cp = pltpu.make_async_copy(bufs[slot].at[pl.ds(q * SC_LANES, SC_LANES)],
                                                   o_hbm.at[rows], sem_out)
                        cp.start()
                        copies.append(cp)
                for cp in copies:
                    cp.wait()

    return run(xn, dest_flat)


def _sc_gather(y_rows, dest_flat):
    n = dest_flat.shape[0]
    rpw = n // SC_WORKERS
    nchunk = rpw // SC_WINDOW
    nvec = SC_WINDOW // SC_LANES

    @functools.partial(
        pl.kernel, out_type=jax.ShapeDtypeStruct((n, y_rows.shape[1]), y_rows.dtype), mesh=_sc_mesh(),
        scratch_types=[pltpu.VMEM((rpw,), jnp.int32),
                       pltpu.VMEM((SC_WINDOW, y_rows.shape[1]), y_rows.dtype),
                       pltpu.VMEM((SC_WINDOW, y_rows.shape[1]), y_rows.dtype),
                       pltpu.SemaphoreType.DMA, pltpu.SemaphoreType.DMA, pltpu.SemaphoreType.DMA],
        name="sc_gather")
    def run(y_hbm, d_hbm, o_hbm, idx_v, buf0, buf1, sem0, sem1, sem_in):
        base = _sc_worker() * rpw
        pltpu.sync_copy(d_hbm.at[pl.ds(base, rpw)], idx_v)
        bufs = (buf0, buf1)
        sems = (sem0, sem1)

        def store(c, slot):
            return pltpu.make_async_copy(bufs[slot], o_hbm.at[pl.ds(base + c * SC_WINDOW, SC_WINDOW)], sems[slot])

        @pl.loop(0, nchunk, step=2)
        def _(c0):
            for slot in range(2):
                c = c0 + slot

                @pl.when(c >= 2)
                def _():
                    store(c - 2, slot).wait()

                copies = []
                for q in range(nvec):
                    off = pl.multiple_of(c * SC_WINDOW + q * SC_LANES, SC_LANES)
                    rows = idx_v[pl.ds(off, SC_LANES)]
                    cp = pltpu.make_async_copy(y_hbm.at[rows], bufs[slot].at[pl.ds(q * SC_LANES, SC_LANES)], sem_in)
                    cp.start()
                    copies.append(cp)
                for cp in copies:
                    cp.wait()
                store(c, slot).start()

        store(nchunk - 2, 0).wait()
        store(nchunk - 1, 1).wait()

    return run(y_rows, dest_flat)


def _final_norm_kernel(h_ref, g_ref, o_ref):
    x = h_ref[...]
    o_ref[...] = x * lax.rsqrt(jnp.mean(x * x, axis=-1, keepdims=True) + EPS) * g_ref[...]


def _final_norm(h2, g, tm=1024):
    T = h2.shape[0]
    return pl.pallas_call(
        _final_norm_kernel,
        grid=(T // tm,),
        in_specs=[pl.BlockSpec((tm, D_MODEL), lambda i: (i, 0)),
                  pl.BlockSpec((1, D_MODEL), lambda i: (0, 0))],
        out_specs=pl.BlockSpec((tm, D_MODEL), lambda i: (i, 0)),
        out_shape=jax.ShapeDtypeStruct((T, D_MODEL), F32),
        compiler_params=_cparams("parallel"),
        name="final_norm",
    )(h2, g)


def _prep_w_in(w_in):
    o_g = 4 * M_WIDTH
    o_cq = o_g + 2 * M_HEADS
    o_ckv = o_cq + A_QRANK
    o_kr = o_ckv + A_KVRANK
    o_up = o_kr + A_ROPE
    z = lambda n: jnp.zeros(w_in.shape[:-1] + (n,), w_in.dtype)
    small = jnp.concatenate([z(SMALL_KR), w_in[..., o_kr:o_up], w_in[..., o_g:o_cq],
                             z(LANES - SMALL_GATE - 2 * M_HEADS)], axis=-1)
    return jnp.concatenate([w_in[..., 0:o_g], w_in[..., o_cq:o_ckv], w_in[..., o_up:o_up + P_WIDTH],
                            w_in[..., o_ckv:o_kr], small], axis=-1).astype(BF16)


def _rope_tables(seq):
    inv = ROPE_THETA ** (-jnp.arange(0, A_ROPE, 2, dtype=F32) / A_ROPE)
    ang = jnp.arange(seq, dtype=F32)[:, None] * inv[None, :]
    cos, sin = jnp.cos(ang), jnp.sin(ang)
    half = A_ROPE // 2
    zeros = lambda n: jnp.zeros((seq, n), F32)
    ones = lambda n: jnp.ones((seq, n), F32)
    tail = LANES - A_NOPE - A_ROPE
    cq_t = jnp.concatenate([ones(A_NOPE), cos, cos, zeros(tail)], axis=1)
    ck_t = jnp.concatenate([zeros(A_NOPE), cos, cos, zeros(tail)], axis=1)
    s1_t = jnp.concatenate([zeros(A_NOPE), -sin, zeros(half), zeros(tail)], axis=1)
    s2_t = jnp.concatenate([zeros(A_NOPE), zeros(half), sin, zeros(tail)], axis=1)
    return cq_t, ck_t, s1_t, s2_t


def kernel(x, norm1_g, w_in, conv_w, conv_b, gate_b, mlstm_norm_g, q_norm_g, kv_norm_g, w_uq, w_ukv,
           w_pool, pool_scale, w_out, norm2_g, w_router, b_router, w_gate_up, b_gate_up, w_down, b_down,
           final_norm_g):
    B, S, D = x.shape
    depth = w_in.shape[0]
    T = B * S
    nb = (T * TOP_K) // MOE_ROWS + N_EXPERTS
    nb_pad = -(-nb // LANES) * LANES

    w_in_p = _prep_w_in(w_in)
    wq = w_uq.reshape(depth, A_QRANK, A_HEADS, A_NOPE + A_ROPE)
    wq = jnp.pad(wq, ((0, 0), (0, 0), (0, 0), (0, LANES - A_NOPE - A_ROPE)))
    wq = wq.reshape(depth, A_QRANK, A_HEADS * LANES).astype(BF16)
    wkv = w_ukv.reshape(depth, A_KVRANK, A_HEADS, A_NOPE + A_VDIM)
    wk = jnp.pad(wkv[..., :A_NOPE], ((0, 0), (0, 0), (0, 0), (0, LANES - A_NOPE)))
    wk = wk.reshape(depth, A_KVRANK, A_HEADS * LANES).astype(BF16)
    wv_e = jnp.pad(wkv[:, :, 0::2, A_NOPE:], ((0, 0), (0, 0), (0, 0), (0, LANES - A_VDIM)))
    wv_o = jnp.pad(wkv[:, :, 1::2, A_NOPE:], ((0, 0), (0, 0), (0, 0), (LANES - A_VDIM, 0)))
    wv = jnp.stack([wv_e, wv_o], axis=3).reshape(depth, A_KVRANK, A_HEADS * LANES).astype(BF16)
    half = jnp.arange(A_HEADS * LANES) // A_VDIM
    vone = ((half % 4 == 1) | (half % 4 == 2)).astype(F32)[None, :]
    gsz = P_WIDTH // len(P_WINDOWS)
    w_pool_bd = jnp.zeros((depth, P_WIDTH, P_WIDTH), F32)
    for gi in range(len(P_WINDOWS)):
        w_pool_bd = w_pool_bd.at[:, gi * gsz:(gi + 1) * gsz, gi * gsz:(gi + 1) * gsz].set(w_pool[:, gi])
    w_pool_bd = w_pool_bd.astype(BF16)
    w_out16 = w_out.astype(BF16)
    w_router_t = jnp.swapaxes(w_router, 1, 2)
    gate_b_col = jnp.pad(gate_b, ((0, 0), (SMALL_GATE, LANES - SMALL_GATE - 2 * M_HEADS)))
    cq_t, ck_t, s1_t, s2_t = _rope_tables(S)
    tr = 512
    tri = (jnp.arange(tr)[:, None] < jnp.arange(tr)[None, :]).astype(BF16)

    wgu_all = w_gate_up.reshape(depth * N_EXPERTS, D_MODEL, 2 * D_FF)
    bgu_all = b_gate_up.reshape(depth * N_EXPERTS, 1, 2 * D_FF)
    wdn_all = w_down.reshape(depth * N_EXPERTS, D_FF, D_MODEL)
    bdn_all = b_down.reshape(depth * N_EXPERTS, 1, D_MODEL)

    h = x.reshape(T, D)
    for l in range(depth):
        proj = _inproj(h, norm1_g[l][None, :], w_in_p[l])
        gates_t = proj[:, PROJ_SMALL + SMALL_GATE:PROJ_SMALL + SMALL_GATE + 2 * M_HEADS]
        gates_t = gates_t.reshape(B, S // M_CHUNK, M_CHUNK, 2 * M_HEADS).transpose(0, 1, 3, 2)
        y_m = _mlstm(proj, gates_t, conv_w[l], conv_b[l][None, :], gate_b_col[l][None, :],
                     gate_b[l][:, None], mlstm_norm_g[l][None, :], B, S)
        q16, k16, v16 = _mla_prep(proj, q_norm_g[l][None, :], kv_norm_g[l][None, :], wq[l], wk[l], wv[l],
                                  vone, cq_t, ck_t, s1_t, s2_t, B, S)
        y_a = _attention(q16, k16, v16, B, S)
        y_p = _pool(proj, w_pool_bd[l], pool_scale[l][None, :], B, S)
        h, xn, logits_t = _outproj(y_m, y_a, y_p, h, w_out16[l], norm2_g[l][None, :],
                                   w_router_t[l], b_router[l][:, None])
        eidx, wts, rank, counts = _router(logits_t, tri, tr)
        dest, blk_e, blk_cnt, blk_next, nblk = _meta(counts, eidx, rank, nb_pad)
        dest_flat = dest.reshape(TOP_K * T)
        x_rows = _sc_dispatch(xn, dest_flat, nb * MOE_ROWS)
        y_rows = _gmm(blk_e[0], blk_cnt[0], blk_next[0], nblk[0], x_rows, wgu_all, bgu_all, wdn_all,
                      bdn_all, nb, l)
        y_tok = _sc_gather(y_rows, dest_flat)
        h = _combine(y_tok, h, wts.T)
    return _final_norm(h, final_norm_g[None, :]).reshape(B, S, D)
```

```python
import functools

import jax
import jax.numpy as jnp
import numpy as np
from jax import lax
from jax.experimental import pallas as pl
from jax.experimental.pallas import tpu as pltpu
from jax.experimental.pallas import tpu_sc as plsc

F32 = jnp.float32
BF16 = jnp.bfloat16
HIGHEST = lax.Precision.HIGHEST

D_MODEL = 1024
M_HEADS = 4
M_HEAD_DIM = 64
M_WIDTH = 256
M_CONV = 4
M_CHUNK = 64
A_HEADS = 8
A_NOPE = 64
A_ROPE = 32
A_VDIM = 64
A_QRANK = 256
A_KVRANK = 128
A_WIDTH = 512
ROPE_THETA = 10000.0
P_WINDOWS = (2, 4, 8, 16)
P_WIDTH = 256
N_EXPERTS = 32
TOP_K = 4
D_FF = 1024
SWIGLU_LIMIT = 7.0
SWIGLU_ALPHA = 1.702
EPS = 1e-6

LANES = 128
SUBLANES = 8

PROJ_QKVO = 0
PROJ_CQ = 1024
PROJ_UP = 1280
PROJ_CKV = 1536
PROJ_SMALL = 1664
PROJ_WIDTH = 1792
SMALL_KR = 64
SMALL_GATE = 96

MOE_ROWS = 512
VMEM_LIMIT = 56 * 1024 * 1024


def _cparams(*sem):
    return pltpu.CompilerParams(dimension_semantics=sem, vmem_limit_bytes=VMEM_LIMIT)


def _sigmoid(x):
    return 1.0 / (1.0 + jnp.exp(-x))


def _log_sigmoid(x):
    return jnp.minimum(x, 0.0) - jnp.log(1.0 + jnp.exp(-jnp.abs(x)))


def _dot(a, b, **kw):
    return jnp.dot(a, b, preferred_element_type=F32, **kw)


def _dot_nt(a, b, **kw):
    return lax.dot_general(a, b, (((1,), (1,)), ((), ())), preferred_element_type=F32, **kw)


def _dot_tn(a, b, **kw):
    return lax.dot_general(a, b, (((0,), (0,)), ((), ())), preferred_element_type=F32, **kw)


def _bf16_terms(x, terms=3):
    out = []
    for _ in range(terms):
        piece = x.astype(BF16)
        out.append(piece)
        x = x - piece.astype(F32)
    return out


def _dot_sel(x, sel16):
    return sum(_dot(p, sel16) for p in _bf16_terms(x))


def _sel_dot(sel16, x):
    return sum(_dot(sel16, p) for p in _bf16_terms(x))


def _pack_bf16_pairs(x):
    n = x.shape[1] // 2
    lo = lax.bitcast_convert_type(x[:, :n].astype(BF16).astype(F32), jnp.uint32)
    hi = lax.bitcast_convert_type(x[:, n:].astype(BF16).astype(F32), jnp.uint32)
    return (lo >> 16) | (hi & jnp.uint32(0xFFFF0000))


def _unpack_bf16_pairs(p):
    lo = lax.bitcast_convert_type(p << 16, F32)
    hi = lax.bitcast_convert_type(p & jnp.uint32(0xFFFF0000), F32)
    return lo, hi


def _moe_combine(h_ref, y_refs, w_ref):
    w = w_ref[...]
    dp = D_MODEL // 2
    acc_lo = h_ref[:, :dp]
    acc_hi = h_ref[:, dp:]
    for kk, y_ref in enumerate(y_refs):
        lo, hi = _unpack_bf16_pairs(y_ref[...])
        acc_lo = acc_lo + lo * w[:, kk:kk + 1]
        acc_hi = acc_hi + hi * w[:, kk:kk + 1]
    return jnp.concatenate([acc_lo, acc_hi], axis=1)


def _combine_specs(tm, nt):
    y_specs = [pl.BlockSpec((tm, D_MODEL // 2), functools.partial(lambda i, kk: (kk * nt + i, 0), kk=kk))
               for kk in range(TOP_K)]
    return [pl.BlockSpec((tm, D_MODEL), lambda i: (i, 0))] + y_specs + [pl.BlockSpec((tm, TOP_K), lambda i: (i, 0))]


def _inproj_kernel(*refs, combine):
    if combine:
        h_ref, y0, y1, y2, y3, wk_ref, g_ref, w_ref, hn_ref, o_ref = refs
        x = _moe_combine(h_ref, (y0, y1, y2, y3), wk_ref)
        hn_ref[...] = x
    else:
        h_ref, g_ref, w_ref, o_ref = refs
        x = h_ref[...]
    ms = jnp.mean(x * x, axis=-1, keepdims=True)
    xn = x * lax.rsqrt(ms + EPS) * g_ref[...]
    o_ref[...] = _dot(xn.astype(BF16), w_ref[...])


def _inproj(h2, g, w, moe=None, tm=512):
    T = h2.shape[0]
    nt = T // tm
    w_specs = [pl.BlockSpec((1, D_MODEL), lambda i: (0, 0)),
               pl.BlockSpec((D_MODEL, PROJ_WIDTH), lambda i: (0, 0))]
    proj_spec = pl.BlockSpec((tm, PROJ_WIDTH), lambda i: (i, 0))
    proj_shape = jax.ShapeDtypeStruct((T, PROJ_WIDTH), F32)
    if moe is None:
        return h2, pl.pallas_call(
            functools.partial(_inproj_kernel, combine=False),
            grid=(nt,),
            in_specs=[pl.BlockSpec((tm, D_MODEL), lambda i: (i, 0))] + w_specs,
            out_specs=proj_spec,
            out_shape=proj_shape,
            compiler_params=_cparams("parallel"),
            name="inproj",
        )(h2, g, w)
    y_tok, w_tok = moe
    return pl.pallas_call(
        functools.partial(_inproj_kernel, combine=True),
        grid=(nt,),
        in_specs=_combine_specs(tm, nt) + w_specs,
        out_specs=[pl.BlockSpec((tm, D_MODEL), lambda i: (i, 0)), proj_spec],
        out_shape=[jax.ShapeDtypeStruct((T, D_MODEL), F32), proj_shape],
        compiler_params=_cparams("parallel"),
        name="combine_inproj",
    )(h2, y_tok, y_tok, y_tok, y_tok, w_tok, g, w)


M_SEQS = 2


def _mlstm_kernel(proj_ref, small_ref, gt_ref, cw_ref, cb_ref, gbc_ref, gbr_ref, ng_ref, o_ref,
                  ct_ref, n_ref, m_ref):
    S = proj_ref.shape[0] // M_SEQS
    L = M_CHUNK
    nc = S // L
    W = M_WIDTH
    ct_ref[...] = jnp.zeros_like(ct_ref)
    n_ref[...] = jnp.zeros_like(n_ref)
    m_ref[...] = jnp.zeros_like(m_ref)

    lane_head = lax.broadcasted_iota(jnp.int32, (1, W), 1) // M_HEAD_DIM
    head_lanes = [lane_head == h for h in range(M_HEADS)]
    masks = [hl.astype(F32) for hl in head_lanes]
    rh = lax.broadcasted_iota(jnp.int32, (W, W), 0) // M_HEAD_DIM
    chd = lax.broadcasted_iota(jnp.int32, (W, W), 1) // M_HEAD_DIM
    same_head = (rh == chd).astype(F32)
    ri = lax.broadcasted_iota(jnp.int32, (L, L), 0)
    ci = lax.broadcasted_iota(jnp.int32, (L, L), 1)
    causal = ri >= ci
    tril16 = causal.astype(BF16)
    triu16 = (ri <= ci).astype(BF16)
    same_head16 = same_head.astype(BF16)
    cw = cw_ref[...]
    cb = cb_ref[...]
    gbc = gbc_ref[...]
    gbr = gbr_ref[...]
    ng = ng_ref[...]

    halo_row = lax.broadcasted_iota(jnp.int32, (L + SUBLANES, 1), 0) >= SUBLANES

    def chunk(sq, c):
        r0 = pl.multiple_of(sq * S + c * L, L)
        w0 = pl.multiple_of(sq * S + jnp.maximum(c * L - SUBLANES, 0), SUBLANES)
        win = proj_ref[pl.ds(w0, L + SUBLANES), 0:2 * W]
        first = jnp.where(halo_row, pltpu.roll(win, SUBLANES, axis=0), 0.0)
        win = jnp.where(c == 0, first, win)
        acc = jnp.zeros((L, 2 * W), F32) + cb
        for j in range(M_CONV):
            s = M_CONV - 1 - j
            xs = win if s == 0 else pltpu.roll(win, s, axis=0)
            acc = acc + xs[SUBLANES:, :] * cw[j:j + 1, :]
        qk = acc * _sigmoid(acc)
        q = qk[:, 0:W]
        k = qk[:, W:2 * W] * (M_HEAD_DIM ** -0.5)
        v = proj_ref[pl.ds(r0, L), 2 * W:3 * W]
        og = proj_ref[pl.ds(r0, L), 3 * W:4 * W]
        sm = small_ref[pl.ds(r0, L), :] + gbc
        gt = gt_ref[sq, c] + gbr
        i_col_all = sm[:, SMALL_GATE:SMALL_GATE + M_HEADS]
        logf_col = _log_sigmoid(sm[:, SMALL_GATE + M_HEADS:SMALL_GATE + 2 * M_HEADS])
        g_col_all = _sel_dot(tril16, logf_col)
        i_row_all = gt[0:M_HEADS, :]
        logf_row = _log_sigmoid(gt[M_HEADS:2 * M_HEADS, :])
        g_row_all = _dot_sel(logf_row, triu16)
        yield

        q16 = q.astype(BF16)
        k16 = k.astype(BF16)
        v16 = v.astype(BF16)
        num_intra = jnp.zeros((L, W), F32)
        w_inter_f = jnp.zeros((L, W), F32)
        rowsum_f = jnp.zeros((L, W), F32)
        floor_f = jnp.zeros((L, W), F32)
        wa_f = jnp.zeros((L, W), F32)
        dec_f = jnp.zeros((1, W), F32)
        for h in range(M_HEADS):
            mk = masks[h]
            g_col = g_col_all[:, h:h + 1]
            i_col = i_col_all[:, h:h + 1]
            g_row = g_row_all[h:h + 1, :]
            i_row = i_row_all[h:h + 1, :]
            m_old = m_ref[sq, 0:1, h:h + 1]
            dmat = jnp.where(causal, g_col - g_row + i_row, -jnp.inf)
            inter = g_col + m_old
            m_row = jnp.maximum(inter, jnp.max(dmat, axis=-1, keepdims=True))
            sc = _dot_nt((q * mk).astype(BF16), k16) * jnp.exp(dmat - m_row)
            yield
            mine = head_lanes[h]
            w_inter = jnp.exp(inter - m_row)
            num_intra = jnp.where(mine, _dot(sc.astype(BF16), v16), num_intra)
            w_inter_f = jnp.where(mine, w_inter, w_inter_f)
            rowsum_f = jnp.where(mine, jnp.sum(sc, axis=-1, keepdims=True), rowsum_f)
            floor_f = jnp.where(mine, jnp.exp(-m_row), floor_f)
            g_end = g_col[L - 1:L, :]
            a = g_end - g_col + i_col
            m_new = jnp.maximum(g_end + m_old, jnp.max(a, axis=0, keepdims=True))
            wa_f = jnp.where(mine, jnp.exp(a - m_new), wa_f)
            dec_f = jnp.where(mine, jnp.exp(g_end + m_old - m_new), dec_f)
            m_ref[sq, 0:1, h:h + 1] = m_new
            yield

        ct = ct_ref[sq]
        nvec = n_ref[sq]
        q_c = _dot(q16, ct.astype(BF16))
        q_n = _dot_sel(q * nvec, same_head16)
        yield
        num = w_inter_f * q_c + num_intra
        den = w_inter_f * q_n + rowsum_f
        hv = num / jnp.maximum(jnp.abs(den), floor_f)
        ms = _dot_sel(hv * hv, same_head16) * (1.0 / M_HEAD_DIM)
        yield
        y = hv * lax.rsqrt(ms + EPS) * ng * _sigmoid(og)
        o_ref[pl.ds(r0, L), :] = y

        upd = _dot_tn(k16, (wa_f * v).astype(BF16))
        ct_ref[sq] = dec_f * ct + upd * same_head
        n_ref[sq] = dec_f * nvec + jnp.sum(wa_f * k, axis=0, keepdims=True)
        yield

    def body(c, carry):
        for _ in zip(*[chunk(sq, c) for sq in range(M_SEQS)]):
            pass
        return carry

    lax.fori_loop(0, nc, body, 0)


def _mlstm(proj, gates_t, cw, cb, gbc, gbr, ng, B, S):
    T = B * S
    nc = S // M_CHUNK
    rows = M_SEQS * S
    return pl.pallas_call(
        _mlstm_kernel,
        grid=(B // M_SEQS,),
        in_specs=[pl.BlockSpec((rows, 4 * M_WIDTH), lambda b: (b, 0)),
                  pl.BlockSpec((rows, LANES), lambda b: (b, PROJ_SMALL // LANES)),
                  pl.BlockSpec((M_SEQS, nc, 2 * M_HEADS, M_CHUNK), lambda b: (b, 0, 0, 0)),
                  pl.BlockSpec((M_CONV, 2 * M_WIDTH), lambda b: (0, 0)),
                  pl.BlockSpec((1, 2 * M_WIDTH), lambda b: (0, 0)),
                  pl.BlockSpec((1, LANES), lambda b: (0, 0)),
                  pl.BlockSpec((2 * M_HEADS, 1), lambda b: (0, 0)),
                  pl.BlockSpec((1, M_WIDTH), lambda b: (0, 0))],
        out_specs=pl.BlockSpec((rows, M_WIDTH), lambda b: (b, 0)),
        out_shape=jax.ShapeDtypeStruct((T, M_WIDTH), F32),
        scratch_shapes=[pltpu.VMEM((M_SEQS, M_WIDTH, M_WIDTH), F32),
                        pltpu.VMEM((M_SEQS, 1, M_WIDTH), F32),
                        pltpu.VMEM((M_SEQS, 1, LANES), F32)],
        compiler_params=_cparams("parallel"),
        name="mlstm",
    )(proj, proj, gates_t, cw, cb, gbc, gbr, ng)


POOL_HALO = 16
POOL_TILE = 256


def _pool_kernel(u_ref, w_ref, sc_ref, o_ref, upad_ref):
    S = u_ref.shape[0]
    upad_ref[0:POOL_HALO, :] = jnp.zeros((POOL_HALO, P_WIDTH), F32)
    upad_ref[POOL_HALO:, :] = u_ref[...]
    grp = lax.broadcasted_iota(jnp.int32, (1, P_WIDTH), 1) // (P_WIDTH // len(P_WINDOWS))
    win_lane = jnp.zeros((1, P_WIDTH), jnp.int32)
    for gi, wn in enumerate(P_WINDOWS):
        win_lane = jnp.where(grp == gi, wn, win_lane)
    w = w_ref[...]
    scale = sc_ref[...]
    rows = POOL_TILE + POOL_HALO

    def body(r, carry):
        r0 = pl.multiple_of(r * POOL_TILE, POOL_TILE)
        a = upad_ref[pl.ds(r0, rows), :]
        sums = []
        cur = a
        span = 1
        for _ in P_WINDOWS:
            cur = cur + pltpu.roll(cur, span, axis=0)
            span *= 2
            sums.append(cur)
        sel = sums[-1]
        for gi in range(len(P_WINDOWS) - 1):
            sel = jnp.where(grp == gi, sums[gi], sel)
        sel = sel[POOL_HALO:, :]
        u = a[POOL_HALO:, :]
        t = r0 + lax.broadcasted_iota(jnp.int32, (POOL_TILE, P_WIDTH), 0)
        cnt = jnp.minimum(t + 1, win_lane).astype(F32)
        pooled = sel / cnt - u
        o_ref[pl.ds(r0, POOL_TILE), :] = _dot(pooled.astype(BF16), w) * scale
        return carry

    lax.fori_loop(0, S // POOL_TILE, body, 0)


def _pool(proj, w_bd, scale, B, S):
    T = B * S
    return pl.pallas_call(
        _pool_kernel,
        grid=(B,),
        in_specs=[pl.BlockSpec((S, P_WIDTH), lambda b: (b, PROJ_UP // P_WIDTH)),
                  pl.BlockSpec((P_WIDTH, P_WIDTH), lambda b: (0, 0)),
                  pl.BlockSpec((1, P_WIDTH), lambda b: (0, 0))],
        out_specs=pl.BlockSpec((S, P_WIDTH), lambda b: (b, 0)),
        out_shape=jax.ShapeDtypeStruct((T, P_WIDTH), F32),
        scratch_shapes=[pltpu.VMEM((S + POOL_HALO, P_WIDTH), F32)],
        compiler_params=_cparams("parallel"),
        name="pool",
    )(proj, w_bd, scale)


def _rope(x, c, s1, s2):
    return x * c + pltpu.roll(x, LANES - A_ROPE // 2, axis=1) * s1 + pltpu.roll(x, A_ROPE // 2, axis=1) * s2


def _mla_prep_kernel(cq_ref, ckv_ref, small_ref, qg_ref, kvg_ref, wq_ref, wk_ref, wv_ref,
                     vone_ref, cq_t_ref, ck_t_ref, s1_ref, s2_ref, q_ref, k_ref, v_ref):
    def rms(x, g):
        return x * lax.rsqrt(jnp.mean(x * x, axis=-1, keepdims=True) + EPS) * g

    cqn = rms(cq_ref[...], qg_ref[...]).astype(BF16)
    ckvn = rms(ckv_ref[...], kvg_ref[...]).astype(BF16)
    scale = (A_NOPE + A_ROPE) ** -0.5
    qf = _dot(cqn, wq_ref[...]) * scale
    kf = _dot(ckvn, wk_ref[...])
    v_ref[...] = (_dot(ckvn, wv_ref[...]) + vone_ref[...]).astype(BF16)
    cqt = cq_t_ref[...]
    s1 = s1_ref[...]
    s2 = s2_ref[...]
    krot = _rope(small_ref[...], ck_t_ref[...], s1, s2)
    for h in range(A_HEADS):
        sl = slice(h * LANES, (h + 1) * LANES)
        q_ref[:, sl] = _rope(qf[:, sl], cqt, s1, s2).astype(BF16)
        k_ref[:, sl] = (kf[:, sl] + krot).astype(BF16)


def _mla_prep(proj, qg, kvg, wq, wk, wv, vone, cq_t, ck_t, s1_t, s2_t, B, S, ts=512):
    T = B * S
    nst = S // ts
    hw = A_HEADS * LANES
    return pl.pallas_call(
        _mla_prep_kernel,
        grid=(B, nst),
        in_specs=[pl.BlockSpec((ts, A_QRANK), lambda b, s: (b * nst + s, PROJ_CQ // A_QRANK)),
                  pl.BlockSpec((ts, A_KVRANK), lambda b, s: (b * nst + s, PROJ_CKV // A_KVRANK)),
                  pl.BlockSpec((ts, LANES), lambda b, s: (b * nst + s, PROJ_SMALL // LANES)),
                  pl.BlockSpec((1, A_QRANK), lambda b, s: (0, 0)),
                  pl.BlockSpec((1, A_KVRANK), lambda b, s: (0, 0)),
                  pl.BlockSpec((A_QRANK, hw), lambda b, s: (0, 0)),
                  pl.BlockSpec((A_KVRANK, hw), lambda b, s: (0, 0)),
                  pl.BlockSpec((A_KVRANK, hw), lambda b, s: (0, 0)),
                  pl.BlockSpec((1, hw), lambda b, s: (0, 0)),
                  pl.BlockSpec((ts, LANES), lambda b, s: (s, 0)),
                  pl.BlockSpec((ts, LANES), lambda b, s: (s, 0)),
                  pl.BlockSpec((ts, LANES), lambda b, s: (s, 0)),
                  pl.BlockSpec((ts, LANES), lambda b, s: (s, 0))],
        out_specs=[pl.BlockSpec((ts, hw), lambda b, s: (b * nst + s, 0)),
                   pl.BlockSpec((ts, hw), lambda b, s: (b * nst + s, 0)),
                   pl.BlockSpec((ts, hw), lambda b, s: (b * nst + s, 0))],
        out_shape=[jax.ShapeDtypeStruct((T, hw), BF16),
                   jax.ShapeDtypeStruct((T, hw), BF16),
                   jax.ShapeDtypeStruct((T, hw), BF16)],
        compiler_params=_cparams("parallel", "parallel"),
        name="mla_prep",
    )(proj, proj, proj, qg, kvg, wq, wk, wv, vone, cq_t, ck_t, s1_t, s2_t)


def _attn_kernel(q_ref, k_ref, v_ref, o_ref, *, tq):
    qi = pl.program_id(2)
    heads = range(2)
    sls = [slice(hh * LANES, (hh + 1) * LANES) for hh in heads]
    qs = [q_ref[:, sl] for sl in sls]

    def update(k0, ncols, state, causal=False):
        s = [_dot_nt(qs[hh], k_ref[pl.ds(k0, ncols), sls[hh]]) for hh in heads]
        if causal:
            ok = (lax.broadcasted_iota(jnp.int32, s[0].shape, 0) >= lax.broadcasted_iota(jnp.int32, s[0].shape, 1))
            s = [jnp.where(ok, s[hh], -jnp.inf) for hh in heads]
        m_new = [jnp.maximum(state[hh][0], jnp.max(s[hh], axis=-1, keepdims=True)) for hh in heads]
        p = [jnp.exp((s[hh] - m_new[hh]).astype(BF16)) for hh in heads]
        pv = [_dot(p[hh], v_ref[pl.ds(k0, ncols), sls[hh]]) for hh in heads]
        acc = [jnp.exp(state[hh][0] - m_new[hh]) * state[hh][1] + pv[hh] for hh in heads]
        return tuple((m_new[hh], acc[hh]) for hh in heads)

    init = tuple((jnp.full((tq, 1), -jnp.inf, F32), jnp.zeros((tq, LANES), F32)) for _ in heads)
    carry = lax.fori_loop(0, qi, lambda kb, st: update(pl.multiple_of(kb * tq, tq), tq, st), init)
    (_, acc0), (_, acc1) = update(pl.multiple_of(qi * tq, tq), tq, carry, causal=True)
    lane = lax.broadcasted_iota(jnp.int32, (tq, LANES), 1)
    acc = jnp.where(lane < A_VDIM, acc0, acc1)
    den = jnp.where(lane < A_VDIM, pltpu.roll(acc0, A_VDIM, axis=1), pltpu.roll(acc1, A_VDIM, axis=1))
    o_ref[...] = (acc / den).astype(o_ref.dtype)


def _attention(q, k, v, B, S, tq=512):
    T = B * S
    nq = S // tq
    return pl.pallas_call(
        functools.partial(_attn_kernel, tq=tq),
        grid=(B, A_HEADS // 2, nq),
        in_specs=[pl.BlockSpec((tq, 2 * LANES), lambda b, p, i: (b * nq + i, p)),
                  pl.BlockSpec((S, 2 * LANES), lambda b, p, i: (b, p)),
                  pl.BlockSpec((S, 2 * LANES), lambda b, p, i: (b, p))],
        out_specs=pl.BlockSpec((tq, LANES), lambda b, p, i: (b * nq + i, p)),
        out_shape=jax.ShapeDtypeStruct((T, A_WIDTH), BF16),
        compiler_params=_cparams("parallel", "parallel", "arbitrary"),
        name="attention",
    )(q, k, v)


def _outproj_kernel(ym_ref, ya_ref, yp_ref, h_ref, w_ref, g_ref, wr_ref, br_ref,
                    hn_ref, xn_ref, lg_ref):
    mix = _dot(ym_ref[...].astype(BF16), w_ref[0:M_WIDTH, :])
    mix = mix + _dot(ya_ref[...], w_ref[M_WIDTH:M_WIDTH + A_WIDTH, :])
    mix = mix + _dot(yp_ref[...].astype(BF16), w_ref[M_WIDTH + A_WIDTH:, :])
    hn = h_ref[...] + mix
    hn_ref[...] = hn
    xn = hn * lax.rsqrt(jnp.mean(hn * hn, axis=-1, keepdims=True) + EPS) * g_ref[...]
    x_hi = xn.astype(BF16)
    x_lo = (xn - x_hi.astype(F32)).astype(BF16)
    wr = wr_ref[...]
    w_hi = wr.astype(BF16)
    w_lo = (wr - w_hi.astype(F32)).astype(BF16)
    lg_ref[...] = _dot_nt(w_hi, x_hi) + _dot_nt(w_hi, x_lo) + _dot_nt(w_lo, x_hi) + br_ref[...]
    xn_ref[...] = _pack_bf16_pairs(xn)


def _outproj(ym, ya, yp, h2, w, g, wr_t, br, tm=512):
    T = h2.shape[0]
    return pl.pallas_call(
        _outproj_kernel,
        grid=(T // tm,),
        in_specs=[pl.BlockSpec((tm, M_WIDTH), lambda i: (i, 0)),
                  pl.BlockSpec((tm, A_WIDTH), lambda i: (i, 0)),
                  pl.BlockSpec((tm, P_WIDTH), lambda i: (i, 0)),
                  pl.BlockSpec((tm, D_MODEL), lambda i: (i, 0)),
                  pl.BlockSpec((D_MODEL, D_MODEL), lambda i: (0, 0)),
                  pl.BlockSpec((1, D_MODEL), lambda i: (0, 0)),
                  pl.BlockSpec((N_EXPERTS, D_MODEL), lambda i: (0, 0)),
                  pl.BlockSpec((N_EXPERTS, 1), lambda i: (0, 0))],
        out_specs=[pl.BlockSpec((tm, D_MODEL), lambda i: (i, 0)),
                   pl.BlockSpec((tm, D_MODEL // 2), lambda i: (i, 0)),
                   pl.BlockSpec((N_EXPERTS, tm), lambda i: (0, i))],
        out_shape=[jax.ShapeDtypeStruct((T, D_MODEL), F32),
                   jax.ShapeDtypeStruct((T, D_MODEL // 2), jnp.uint32),
                   jax.ShapeDtypeStruct((N_EXPERTS, T), F32)],
        compiler_params=_cparams("parallel"),
        name="outproj",
    )(ym, ya, yp, h2, w, g, wr_t, br)


def _router_kernel(lg_ref, tri_ref, e_ref, w_ref, r_ref, cnt_ref, carry_ref):
    tr = lg_ref.shape[1]

    @pl.when(pl.program_id(0) == 0)
    def _():
        carry_ref[...] = jnp.zeros_like(carry_ref)

    x = lg_ref[...]
    eio = lax.broadcasted_iota(jnp.int32, (N_EXPERTS, tr), 0).astype(F32)
    picked = jnp.zeros((N_EXPERTS, tr), F32)
    vals = []
    idxs = []
    for _ in range(TOP_K):
        mx = jnp.max(x, axis=0, keepdims=True)
        idx = jnp.min(jnp.where(x == mx, eio, float(N_EXPERTS)), axis=0, keepdims=True)
        hit = eio == idx
        vals.append(mx)
        idxs.append(idx)
        picked = picked + hit.astype(F32)
        x = jnp.where(hit, -jnp.inf, x)
    exps = [jnp.exp(vv - vals[0]) for vv in vals]
    tot = exps[0] + exps[1] + exps[2] + exps[3]
    before = _dot(picked.astype(BF16), tri_ref[...]) + carry_ref[:, 0:1]
    for kk in range(TOP_K):
        e_ref[kk:kk + 1, :] = idxs[kk].astype(jnp.int32)
        w_ref[kk:kk + 1, :] = exps[kk] / tot
        rk = jnp.sum(jnp.where(eio == idxs[kk], before, 0.0), axis=0, keepdims=True)
        r_ref[kk:kk + 1, :] = rk.astype(jnp.int32)
    carry_ref[...] = carry_ref[...] + jnp.sum(picked, axis=1, keepdims=True)
    cnt_ref[...] = carry_ref[...]


def _router(logits_t, tri, tr=512):
    T = logits_t.shape[1]
    return pl.pallas_call(
        _router_kernel,
        grid=(T // tr,),
        in_specs=[pl.BlockSpec((N_EXPERTS, tr), lambda i: (0, i)),
                  pl.BlockSpec((tr, tr), lambda i: (0, 0))],
        out_specs=[pl.BlockSpec((TOP_K, tr), lambda i: (0, i)),
                   pl.BlockSpec((TOP_K, tr), lambda i: (0, i)),
                   pl.BlockSpec((TOP_K, tr), lambda i: (0, i)),
                   pl.BlockSpec((N_EXPERTS, LANES), lambda i: (0, 0))],
        out_shape=[jax.ShapeDtypeStruct((TOP_K, T), jnp.int32),
                   jax.ShapeDtypeStruct((TOP_K, T), F32),
                   jax.ShapeDtypeStruct((TOP_K, T), jnp.int32),
                   jax.ShapeDtypeStruct((N_EXPERTS, LANES), F32)],
        scratch_shapes=[pltpu.VMEM((N_EXPERTS, LANES), F32)],
        compiler_params=_cparams("arbitrary"),
        name="router",
    )(logits_t, tri)


def _meta_kernel(cnt_ref, e_ref, r_ref, dest_ref, be_ref, bc_ref, bn_ref, nb_ref, *, nb_pad):
    cnt = cnt_ref[...]
    padded = jnp.floor((cnt + (MOE_ROWS - 1)) * (1.0 / MOE_ROWS)) * MOE_ROWS
    ri = lax.broadcasted_iota(jnp.int32, (N_EXPERTS, N_EXPERTS), 0)
    ci = lax.broadcasted_iota(jnp.int32, (N_EXPERTS, N_EXPERTS), 1)
    pad_end = _dot((ri >= ci).astype(F32), padded, precision=HIGHEST)
    pad_start = pad_end - padded
    e = e_ref[...]
    dest = r_ref[...]
    for ex in range(N_EXPERTS):
        ps = pad_start[ex:ex + 1, 0:1].astype(jnp.int32)
        dest = jnp.where(e == ex, dest + ps, dest)
    dest_ref[...] = dest
    blk0 = (lax.broadcasted_iota(jnp.int32, (N_EXPERTS, nb_pad), 1) * MOE_ROWS).astype(F32)
    be = jnp.sum((pad_end[:, 0:1] <= blk0).astype(F32), axis=0, keepdims=True)
    be = jnp.minimum(be, float(N_EXPERTS - 1))
    eio = lax.broadcasted_iota(jnp.int32, (N_EXPERTS, nb_pad), 0).astype(F32)
    seg_end = jnp.sum(jnp.where(eio == be, pad_start[:, 0:1] + cnt[:, 0:1], 0.0), axis=0, keepdims=True)
    bc = jnp.clip(seg_end - blk0[0:1, :], 0.0, float(MOE_ROWS))
    nxt0 = jnp.sum(jnp.where(eio == be, pad_end[:, 0:1], 0.0), axis=0, keepdims=True)
    bn = jnp.sum((pad_end[:, 0:1] <= nxt0).astype(F32), axis=0, keepdims=True)
    bn = jnp.where(nxt0 < pad_end[N_EXPERTS - 1:N_EXPERTS, 0:1], bn, -1.0)
    be_ref[...] = be.astype(jnp.int32)
    bc_ref[...] = bc.astype(jnp.int32)
    bn_ref[...] = bn.astype(jnp.int32)
    nb_ref[...] = (pad_end[N_EXPERTS - 1:N_EXPERTS, :] * (1.0 / MOE_ROWS)).astype(jnp.int32)


def _meta(counts, eidx, rank, nb_pad):
    T = eidx.shape[1]
    return pl.pallas_call(
        functools.partial(_meta_kernel, nb_pad=nb_pad),
        out_shape=[jax.ShapeDtypeStruct((TOP_K, T), jnp.int32),
                   jax.ShapeDtypeStruct((1, nb_pad), jnp.int32),
                   jax.ShapeDtypeStruct((1, nb_pad), jnp.int32),
                   jax.ShapeDtypeStruct((1, nb_pad), jnp.int32),
                   jax.ShapeDtypeStruct((1, LANES), jnp.int32)],
        compiler_params=pltpu.CompilerParams(vmem_limit_bytes=VMEM_LIMIT),
        name="route_meta",
    )(counts, eidx, rank)


FF_CHUNK = 512


def _gmm_kernel(be_ref, bc_ref, bn_ref, nb_ref, x_ref, wgu_hbm, bgu_ref, wdn_hbm, bdn_ref, y_ref,
                wgu_st, wdn_st, wgu16, wdn16, sem, *, e0):
    i = pl.program_id(0)
    nblk = nb_ref[0]
    bm = MOE_ROWS

    def weight_copies(e):
        return (pltpu.make_async_copy(wgu_hbm.at[e0 + e], wgu_st, sem.at[0]),
                pltpu.make_async_copy(wdn_hbm.at[e0 + e], wdn_st, sem.at[1]))

    @pl.when(i == 0)
    def _():
        for cp in weight_copies(be_ref[0]):
            cp.start()

    @pl.when(i >= nblk)
    def _():
        y_ref[...] = jnp.zeros_like(y_ref)

    @pl.when(i < nblk)
    def _():
        e_changed = jnp.logical_or(i == 0, be_ref[i] != be_ref[jnp.maximum(i - 1, 0)])

        @pl.when(e_changed)
        def _():
            for cp in weight_copies(be_ref[i]):
                cp.wait()
            wgu16[...] = wgu_st[...].astype(BF16)
            wdn16[...] = wdn_st[...].astype(BF16)

            @pl.when(bn_ref[i] >= 0)
            def _():
                for cp in weight_copies(bn_ref[i]):
                    cp.start()

        valid = lax.broadcasted_iota(jnp.int32, (bm, 1), 0) < bc_ref[i]
        lo, hi = _unpack_bf16_pairs(jnp.where(valid, x_ref[...], jnp.uint32(0)))
        x16 = jnp.concatenate([lo.astype(BF16), hi.astype(BF16)], axis=1)
        acc = jnp.zeros((bm, D_MODEL), F32) + bdn_ref[0]
        for c in range(D_FF // FF_CHUNK):
            cs = slice(c * FF_CHUNK, (c + 1) * FF_CHUNK)
            us = slice(D_FF + c * FF_CHUNK, D_FF + (c + 1) * FF_CHUNK)
            gate = _dot(x16, wgu16[:, cs]) + bgu_ref[0, :, cs]
            up = _dot(x16, wgu16[:, us]) + bgu_ref[0, :, us]
            gate = jnp.minimum(gate, SWIGLU_LIMIT)
            up = jnp.clip(up, -SWIGLU_LIMIT, SWIGLU_LIMIT)
            act = (up + 1.0) * gate * _sigmoid(SWIGLU_ALPHA * gate)
            acc = acc + _dot(act.astype(BF16), wdn16[cs, :])
        y_ref[...] = _pack_bf16_pairs(acc)


def _gmm(blk_e, blk_cnt, blk_next, nblk, x_rows, wgu, bgu, wdn, bdn, nb, layer):
    bm = MOE_ROWS
    e0 = layer * N_EXPERTS
    dp = D_MODEL // 2

    def expert(i, be, nb_ref):
        return (e0 + be[jnp.minimum(i, jnp.maximum(nb_ref[0] - 1, 0))], 0, 0)

    def rows(i, nb_ref):
        return (jnp.minimum(i, jnp.maximum(nb_ref[0] - 1, 0)), 0)

    grid_spec = pltpu.PrefetchScalarGridSpec(
        num_scalar_prefetch=4,
        grid=(nb,),
        in_specs=[pl.BlockSpec((bm, dp), lambda i, be, bc, bn, nbr: rows(i, nbr)),
                  pl.BlockSpec(memory_space=pl.ANY),
                  pl.BlockSpec((1, 1, 2 * D_FF), lambda i, be, bc, bn, nbr: expert(i, be, nbr)),
                  pl.BlockSpec(memory_space=pl.ANY),
                  pl.BlockSpec((1, 1, D_MODEL), lambda i, be, bc, bn, nbr: expert(i, be, nbr))],
        out_specs=pl.BlockSpec((bm, dp), lambda i, be, bc, bn, nbr: (i, 0)),
        scratch_shapes=[pltpu.VMEM((D_MODEL, 2 * D_FF), F32),
                        pltpu.VMEM((D_FF, D_MODEL), F32),
                        pltpu.VMEM((D_MODEL, 2 * D_FF), BF16),
                        pltpu.VMEM((D_FF, D_MODEL), BF16),
                        pltpu.SemaphoreType.DMA((2,))],
    )
    return pl.pallas_call(
        functools.partial(_gmm_kernel, e0=e0),
        grid_spec=grid_spec,
        out_shape=jax.ShapeDtypeStruct((nb * bm, dp), jnp.uint32),
        compiler_params=_cparams("arbitrary"),
        name="expert_gmm",
    )(blk_e, blk_cnt, blk_next, nblk, x_rows, wgu, bgu, wdn, bdn)


def _final_kernel(h_ref, y0, y1, y2, y3, wk_ref, g_ref, o_ref):
    x = _moe_combine(h_ref, (y0, y1, y2, y3), wk_ref)
    o_ref[...] = x * lax.rsqrt(jnp.mean(x * x, axis=-1, keepdims=True) + EPS) * g_ref[...]


def _final(h2, y_tok, w_tok, g, tm=512):
    T = h2.shape[0]
    nt = T // tm
    return pl.pallas_call(
        _final_kernel,
        grid=(nt,),
        in_specs=_combine_specs(tm, nt) + [pl.BlockSpec((1, D_MODEL), lambda i: (0, 0))],
        out_specs=pl.BlockSpec((tm, D_MODEL), lambda i: (i, 0)),
        out_shape=jax.ShapeDtypeStruct((T, D_MODEL), F32),
        compiler_params=_cparams("parallel"),
        name="combine_final_norm",
    )(h2, y_tok, y_tok, y_tok, y_tok, w_tok, g)


SC_CORES = 2
SC_SUBCORES = 16
SC_LANES = 16
SC_WORKERS = SC_CORES * SC_SUBCORES
SC_WINDOW = 64


def _sc_mesh():
    return plsc.VectorSubcoreMesh(core_axis_name="c", subcore_axis_name="s")


def _sc_worker():
    return lax.axis_index("s") * SC_CORES + lax.axis_index("c")


def _sc_dispatch(xn, dest_flat, n_rows):
    T = xn.shape[0]
    tpw = T // SC_WORKERS
    nchunk = tpw // SC_WINDOW
    nvec = SC_WINDOW // SC_LANES

    @functools.partial(
        pl.kernel, out_type=jax.ShapeDtypeStruct((n_rows, xn.shape[1]), xn.dtype), mesh=_sc_mesh(),
        scratch_types=[pltpu.VMEM((TOP_K * tpw,), jnp.int32),
                       pltpu.VMEM((SC_WINDOW, xn.shape[1]), xn.dtype),
                       pltpu.VMEM((SC_WINDOW, xn.shape[1]), xn.dtype),
                       pltpu.SemaphoreType.DMA, pltpu.SemaphoreType.DMA, pltpu.SemaphoreType.DMA],
        name="sc_dispatch")
    def run(x_hbm, d_hbm, o_hbm, idx_v, buf0, buf1, sem0, sem1, sem_out):
        base = _sc_worker() * tpw
        for kk in range(TOP_K):
            pltpu.sync_copy(d_hbm.at[pl.ds(kk * T + base, tpw)], idx_v.at[pl.ds(kk * tpw, tpw)])
        bufs = (buf0, buf1)
        sems = (sem0, sem1)

        def load(c, slot):
            return pltpu.make_async_copy(x_hbm.at[pl.ds(base + c * SC_WINDOW, SC_WINDOW)], bufs[slot], sems[slot])

        load(0, 0).start()

        @pl.loop(0, nchunk, step=2)
        def _(c0):
            for slot in range(2):
                c = c0 + slot
                load(c, slot).wait()

                @pl.when(c + 1 < nchunk)
                def _():
                    load(c + 1, 1 - slot).start()

                copies = []
                for kk in range(TOP_K):
                    for q in range(nvec):
                        off = pl.multiple_of(kk * tpw + c * SC_WINDOW + q * SC_LANES, SC_LANES)
                        rows = idx_v[pl.ds(off, SC_LANES)]
                        cp = pltpu.make_async_copy(bufs[slot].at[pl.ds(q * SC_LANES, SC_LANES)],
                                                   o_hbm.at[rows], sem_out)
                        cp.start()
                        copies.append(cp)
                for cp in copies:
                    cp.wait()

    return run(xn, dest_flat)


def _sc_gather(y_rows, dest_flat):
    n = dest_flat.shape[0]
    rpw = n // SC_WORKERS
    nchunk = rpw // SC_WINDOW
    nvec = SC_WINDOW // SC_LANES

    @functools.partial(
        pl.kernel, out_type=jax.ShapeDtypeStruct((n, y_rows.shape[1]), y_rows.dtype), mesh=_sc_mesh(),
        scratch_types=[pltpu.VMEM((rpw,), jnp.int32),
                       pltpu.VMEM((SC_WINDOW, y_rows.shape[1]), y_rows.dtype),
                       pltpu.VMEM((SC_WINDOW, y_rows.shape[1]), y_rows.dtype),
                       pltpu.SemaphoreType.DMA, pltpu.SemaphoreType.DMA, pltpu.SemaphoreType.DMA],
        name="sc_gather")
    def run(y_hbm, d_hbm, o_hbm, idx_v, buf0, buf1, sem0, sem1, sem_in):
        base = _sc_worker() * rpw
        pltpu.sync_copy(d_hbm.at[pl.ds(base, rpw)], idx_v)
        bufs = (buf0, buf1)
        sems = (sem0, sem1)

        def store(c, slot):
            return pltpu.make_async_copy(bufs[slot], o_hbm.at[pl.ds(base + c * SC_WINDOW, SC_WINDOW)], sems[slot])

        @pl.loop(0, nchunk, step=2)
        def _(c0):
            for slot in range(2):
                c = c0 + slot

                @pl.when(c >= 2)
                def _():
                    store(c - 2, slot).wait()

                copies = []
                for q in range(nvec):
                    off = pl.multiple_of(c * SC_WINDOW + q * SC_LANES, SC_LANES)
                    rows = idx_v[pl.ds(off, SC_LANES)]
                    cp = pltpu.make_async_copy(y_hbm.at[rows], bufs[slot].at[pl.ds(q * SC_LANES, SC_LANES)], sem_in)
                    cp.start()
                    copies.append(cp)
                for cp in copies:
                    cp.wait()
                store(c, slot).start()

        store(nchunk - 2, 0).wait()
        store(nchunk - 1, 1).wait()

    return run(y_rows, dest_flat)


def _prep_w_in(w_in):
    o_g = 4 * M_WIDTH
    o_cq = o_g + 2 * M_HEADS
    o_ckv = o_cq + A_QRANK
    o_kr = o_ckv + A_KVRANK
    o_up = o_kr + A_ROPE
    z = lambda n: jnp.zeros(w_in.shape[:-1] + (n,), w_in.dtype)
    small = jnp.concatenate([z(SMALL_KR), w_in[..., o_kr:o_up], w_in[..., o_g:o_cq],
                             z(LANES - SMALL_GATE - 2 * M_HEADS)], axis=-1)
    return jnp.concatenate([w_in[..., 0:o_g], w_in[..., o_cq:o_ckv], w_in[..., o_up:o_up + P_WIDTH],
                            w_in[..., o_ckv:o_kr], small], axis=-1).astype(BF16)


def _rope_tables(seq):
    inv = ROPE_THETA ** (-jnp.arange(0, A_ROPE, 2, dtype=F32) / A_ROPE)
    ang = jnp.arange(seq, dtype=F32)[:, None] * inv[None, :]
    cos, sin = jnp.cos(ang), jnp.sin(ang)
    half = A_ROPE // 2
    zeros = lambda n: jnp.zeros((seq, n), F32)
    ones = lambda n: jnp.ones((seq, n), F32)
    tail = LANES - A_NOPE - A_ROPE
    cq_t = jnp.concatenate([ones(A_NOPE), cos, cos, zeros(tail)], axis=1)
    ck_t = jnp.concatenate([zeros(A_NOPE), cos, cos, zeros(tail)], axis=1)
    s1_t = jnp.concatenate([zeros(A_NOPE), -sin, zeros(half), zeros(tail)], axis=1)
    s2_t = jnp.concatenate([zeros(A_NOPE), zeros(half), sin, zeros(tail)], axis=1)
    return cq_t, ck_t, s1_t, s2_t


def kernel(x, norm1_g, w_in, conv_w, conv_b, gate_b, mlstm_norm_g, q_norm_g, kv_norm_g, w_uq, w_ukv,
           w_pool, pool_scale, w_out, norm2_g, w_router, b_router, w_gate_up, b_gate_up, w_down, b_down,
           final_norm_g):
    B, S, D = x.shape
    depth = w_in.shape[0]
    T = B * S
    nb = (T * TOP_K) // MOE_ROWS + N_EXPERTS
    nb_pad = -(-nb // LANES) * LANES

    w_in_p = _prep_w_in(w_in)
    wq = w_uq.reshape(depth, A_QRANK, A_HEADS, A_NOPE + A_ROPE)
    wq = jnp.pad(wq, ((0, 0), (0, 0), (0, 0), (0, LANES - A_NOPE - A_ROPE)))
    wq = wq.reshape(depth, A_QRANK, A_HEADS * LANES).astype(BF16)
    wkv = w_ukv.reshape(depth, A_KVRANK, A_HEADS, A_NOPE + A_VDIM)
    wk = jnp.pad(wkv[..., :A_NOPE], ((0, 0), (0, 0), (0, 0), (0, LANES - A_NOPE)))
    wk = wk.reshape(depth, A_KVRANK, A_HEADS * LANES).astype(BF16)
    wv_e = jnp.pad(wkv[:, :, 0::2, A_NOPE:], ((0, 0), (0, 0), (0, 0), (0, LANES - A_VDIM)))
    wv_o = jnp.pad(wkv[:, :, 1::2, A_NOPE:], ((0, 0), (0, 0), (0, 0), (LANES - A_VDIM, 0)))
    wv = jnp.stack([wv_e, wv_o], axis=3).reshape(depth, A_KVRANK, A_HEADS * LANES).astype(BF16)
    half = jnp.arange(A_HEADS * LANES) // A_VDIM
    vone = ((half % 4 == 1) | (half % 4 == 2)).astype(F32)[None, :]
    gsz = P_WIDTH // len(P_WINDOWS)
    w_pool_bd = jnp.zeros((depth, P_WIDTH, P_WIDTH), F32)
    for gi in range(len(P_WINDOWS)):
        w_pool_bd = w_pool_bd.at[:, gi * gsz:(gi + 1) * gsz, gi * gsz:(gi + 1) * gsz].set(w_pool[:, gi])
    w_pool_bd = w_pool_bd.astype(BF16)
    w_out16 = w_out.astype(BF16)
    w_router_t = jnp.swapaxes(w_router, 1, 2)
    gate_b_col = jnp.pad(gate_b, ((0, 0), (SMALL_GATE, LANES - SMALL_GATE - 2 * M_HEADS)))
    cq_t, ck_t, s1_t, s2_t = _rope_tables(S)
    tr = 512
    tri = (jnp.arange(tr)[:, None] < jnp.arange(tr)[None, :]).astype(BF16)

    wgu_all = w_gate_up.reshape(depth * N_EXPERTS, D_MODEL, 2 * D_FF)
    bgu_all = b_gate_up.reshape(depth * N_EXPERTS, 1, 2 * D_FF)
    wdn_all = w_down.reshape(depth * N_EXPERTS, D_FF, D_MODEL)
    bdn_all = b_down.reshape(depth * N_EXPERTS, 1, D_MODEL)

    h = x.reshape(T, D)
    moe = None
    for l in range(depth):
        h, proj = _inproj(h, norm1_g[l][None, :], w_in_p[l], moe)
        gates_t = proj[:, PROJ_SMALL + SMALL_GATE:PROJ_SMALL + SMALL_GATE + 2 * M_HEADS]
        gates_t = gates_t.reshape(B, S // M_CHUNK, M_CHUNK, 2 * M_HEADS).transpose(0, 1, 3, 2)
        y_m = _mlstm(proj, gates_t, conv_w[l], conv_b[l][None, :], gate_b_col[l][None, :],
                     gate_b[l][:, None], mlstm_norm_g[l][None, :], B, S)
        q16, k16, v16 = _mla_prep(proj, q_norm_g[l][None, :], kv_norm_g[l][None, :], wq[l], wk[l], wv[l],
                                  vone, cq_t, ck_t, s1_t, s2_t, B, S)
        y_a = _attention(q16, k16, v16, B, S)
        y_p = _pool(proj, w_pool_bd[l], pool_scale[l][None, :], B, S)
        h, xn, logits_t = _outproj(y_m, y_a, y_p, h, w_out16[l], norm2_g[l][None, :],
                                   w_router_t[l], b_router[l][:, None])
        eidx, wts, rank, counts = _router(logits_t, tri, tr)
        dest, blk_e, blk_cnt, blk_next, nblk = _meta(counts, eidx, rank, nb_pad)
        dest_flat = dest.reshape(TOP_K * T)
        x_rows = _sc_dispatch(xn, dest_flat, nb * MOE_ROWS)
        y_rows = _gmm(blk_e[0], blk_cnt[0], blk_next[0], nblk[0], x_rows, wgu_all, bgu_all, wdn_all,
                      bdn_all, nb, l)
        moe = (_sc_gather(y_rows, dest_flat), wts.T)
    return _final(h, moe[0], moe[1], final_norm_g[None, :]).reshape(B, S, D)
```

```python
import functools

import jax
import jax.numpy as jnp
import numpy as np
from jax import lax
from jax.experimental import pallas as pl
from jax.experimental.pallas import tpu as pltpu
from jax.experimental.pallas import tpu_sc as plsc

F32 = jnp.float32
BF16 = jnp.bfloat16
HIGHEST = lax.Precision.HIGHEST

D_MODEL = 1024
M_HEADS = 4
M_HEAD_DIM = 64
M_WIDTH = 256
M_CONV = 4
M_CHUNK = 64
A_HEADS = 8
A_NOPE = 64
A_ROPE = 32
A_VDIM = 64
A_QRANK = 256
A_KVRANK = 128
A_WIDTH = 512
ROPE_THETA = 10000.0
P_WINDOWS = (2, 4, 8, 16)
P_WIDTH = 256
N_EXPERTS = 32
TOP_K = 4
D_FF = 1024
SWIGLU_LIMIT = 7.0
SWIGLU_ALPHA = 1.702
EPS = 1e-6

LANES = 128
SUBLANES = 8

PROJ_QKVO = 0
PROJ_CQ = 1024
PROJ_UP = 1280
PROJ_CKV = 1536
PROJ_SMALL = 1664
PROJ_WIDTH = 1792
SMALL_KR = 64
SMALL_GATE = 96

MOE_ROWS = 512
VMEM_LIMIT = 56 * 1024 * 1024


def _cparams(*sem):
    return pltpu.CompilerParams(dimension_semantics=sem, vmem_limit_bytes=VMEM_LIMIT)


def _sigmoid(x):
    return 1.0 / (1.0 + jnp.exp(-x))


def _log_sigmoid(x):
    return jnp.minimum(x, 0.0) - jnp.log(1.0 + jnp.exp(-jnp.abs(x)))


def _dot(a, b, **kw):
    return jnp.dot(a, b, preferred_element_type=F32, **kw)


def _dot_nt(a, b, **kw):
    return lax.dot_general(a, b, (((1,), (1,)), ((), ())), preferred_element_type=F32, **kw)


def _dot_tn(a, b, **kw):
    return lax.dot_general(a, b, (((0,), (0,)), ((), ())), preferred_element_type=F32, **kw)


def _bf16_terms(x, terms=3):
    out = []
    for _ in range(terms):
        piece = x.astype(BF16)
        out.append(piece)
        x = x - piece.astype(F32)
    return out


def _dot_sel(x, sel16, terms=3):
    return sum(_dot(p, sel16) for p in _bf16_terms(x, terms))


def _sel_dot(sel16, x, terms=3):
    return sum(_dot(sel16, p) for p in _bf16_terms(x, terms))


def _pack_bf16_pairs(x):
    n = x.shape[1] // 2
    lo = lax.bitcast_convert_type(x[:, :n].astype(BF16).astype(F32), jnp.uint32)
    hi = lax.bitcast_convert_type(x[:, n:].astype(BF16).astype(F32), jnp.uint32)
    return (lo >> 16) | (hi & jnp.uint32(0xFFFF0000))


def _unpack_bf16_pairs(p):
    lo = lax.bitcast_convert_type(p << 16, F32)
    hi = lax.bitcast_convert_type(p & jnp.uint32(0xFFFF0000), F32)
    return lo, hi


def _moe_combine(h_ref, y_refs, w_ref):
    w = w_ref[...]
    dp = D_MODEL // 2
    acc_lo = h_ref[:, :dp]
    acc_hi = h_ref[:, dp:]
    for kk, y_ref in enumerate(y_refs):
        lo, hi = _unpack_bf16_pairs(y_ref[...])
        acc_lo = acc_lo + lo * w[:, kk:kk + 1]
        acc_hi = acc_hi + hi * w[:, kk:kk + 1]
    return jnp.concatenate([acc_lo, acc_hi], axis=1)


def _combine_specs(tm, nt):
    y_specs = [pl.BlockSpec((tm, D_MODEL // 2), functools.partial(lambda i, kk: (kk * nt + i, 0), kk=kk))
               for kk in range(TOP_K)]
    return [pl.BlockSpec((tm, D_MODEL), lambda i: (i, 0))] + y_specs + [pl.BlockSpec((tm, TOP_K), lambda i: (i, 0))]


def _inproj_kernel(*refs, combine):
    if combine:
        h_ref, y0, y1, y2, y3, wk_ref, g_ref, w_ref, hn_ref, o_ref = refs
        x = _moe_combine(h_ref, (y0, y1, y2, y3), wk_ref)
        hn_ref[...] = x
    else:
        h_ref, g_ref, w_ref, o_ref = refs
        x = h_ref[...]
    ms = jnp.mean(x * x, axis=-1, keepdims=True)
    xn = x * lax.rsqrt(ms + EPS) * g_ref[...]
    o_ref[...] = _dot(xn.astype(BF16), w_ref[...])


def _inproj(h2, g, w, moe=None, tm=512):
    T = h2.shape[0]
    nt = T // tm
    w_specs = [pl.BlockSpec((1, D_MODEL), lambda i: (0, 0)),
               pl.BlockSpec((D_MODEL, PROJ_WIDTH), lambda i: (0, 0))]
    proj_spec = pl.BlockSpec((tm, PROJ_WIDTH), lambda i: (i, 0))
    proj_shape = jax.ShapeDtypeStruct((T, PROJ_WIDTH), F32)
    if moe is None:
        return h2, pl.pallas_call(
            functools.partial(_inproj_kernel, combine=False),
            grid=(nt,),
            in_specs=[pl.BlockSpec((tm, D_MODEL), lambda i: (i, 0))] + w_specs,
            out_specs=proj_spec,
            out_shape=proj_shape,
            compiler_params=_cparams("parallel"),
            name="inproj",
        )(h2, g, w)
    y_tok, w_tok = moe
    return pl.pallas_call(
        functools.partial(_inproj_kernel, combine=True),
        grid=(nt,),
        in_specs=_combine_specs(tm, nt) + w_specs,
        out_specs=[pl.BlockSpec((tm, D_MODEL), lambda i: (i, 0)), proj_spec],
        out_shape=[jax.ShapeDtypeStruct((T, D_MODEL), F32), proj_shape],
        compiler_params=_cparams("parallel"),
        name="combine_inproj",
    )(h2, y_tok, y_tok, y_tok, y_tok, w_tok, g, w)


M_SEQS = 2


def _mlstm_kernel(proj_ref, small_ref, cw_ref, cb_ref, gbc_ref, ng_ref, o_ref, ct_ref, n_ref, m_ref):
    S = proj_ref.shape[0] // M_SEQS
    L = M_CHUNK
    nc = S // L
    W = M_WIDTH
    ct_ref[...] = jnp.zeros_like(ct_ref)
    n_ref[...] = jnp.zeros_like(n_ref)
    m_ref[...] = jnp.zeros_like(m_ref)

    rh = lax.broadcasted_iota(jnp.int32, (W, W), 0) // M_HEAD_DIM
    chd = lax.broadcasted_iota(jnp.int32, (W, W), 1) // M_HEAD_DIM
    same_head = (rh == chd).astype(F32)
    same_head16 = same_head.astype(BF16)
    tril16 = (lax.broadcasted_iota(jnp.int32, (L, L), 0) >= lax.broadcasted_iota(jnp.int32, (L, L), 1)).astype(BF16)
    row = lax.broadcasted_iota(jnp.int32, (L, W), 0)
    key = lax.broadcasted_iota(jnp.int32, (L, W), 1) % M_HEAD_DIM
    causal = key <= row
    diag = (key == row).astype(F32)
    er = lax.broadcasted_iota(jnp.int32, (LANES, 2 * W), 0)
    ec = lax.broadcasted_iota(jnp.int32, (LANES, 2 * W), 1)
    spread16 = (er == SMALL_GATE + ec // M_HEAD_DIM).astype(BF16)
    cw = cw_ref[...]
    cb = cb_ref[...]
    gbc = gbc_ref[...]
    ng = ng_ref[...]

    halo_row = lax.broadcasted_iota(jnp.int32, (L + SUBLANES, 1), 0) >= SUBLANES

    def prefix_max(x):
        s = 1
        while s < L:
            x = jnp.maximum(x, jnp.where(row >= s, pltpu.roll(x, s, axis=0), -jnp.inf))
            s *= 2
        return x

    def chunk(sq, c):
        r0 = pl.multiple_of(sq * S + c * L, L)
        w0 = pl.multiple_of(sq * S + jnp.maximum(c * L - SUBLANES, 0), SUBLANES)
        win = proj_ref[pl.ds(w0, L + SUBLANES), 0:2 * W]
        first = jnp.where(halo_row, pltpu.roll(win, SUBLANES, axis=0), 0.0)
        win = jnp.where(c == 0, first, win)
        acc = jnp.zeros((L, 2 * W), F32) + cb
        for j in range(M_CONV):
            s = M_CONV - 1 - j
            xs = win if s == 0 else pltpu.roll(win, s, axis=0)
            acc = acc + xs[SUBLANES:, :] * cw[j:j + 1, :]
        qk = acc * _sigmoid(acc)
        q = qk[:, 0:W]
        k = qk[:, W:2 * W] * (M_HEAD_DIM ** -0.5)
        v = proj_ref[pl.ds(r0, L), 2 * W:3 * W]
        og = proj_ref[pl.ds(r0, L), 3 * W:4 * W]
        q16 = q.astype(BF16)
        k16 = k.astype(BF16)
        v16 = v.astype(BF16)
        ct = ct_ref[sq]
        nvec = n_ref[sq]
        m_old = m_ref[sq]
        gates = _dot_sel(small_ref[pl.ds(r0, L), :] + gbc, spread16)
        kb = jnp.concatenate([k16] * M_HEADS, axis=0) * same_head16
        vb = jnp.concatenate([v16] * M_HEADS, axis=0) * same_head16
        qk_all = _dot_nt(q16, kb)
        q_c = _dot(q16, ct.astype(BF16))
        q_n = _dot_sel(q * nvec, same_head16, terms=2)
        yield

        i_pre = gates[:, :W]
        g = _sel_dot(tril16, _log_sigmoid(gates[:, W:]))
        yield
        a = i_pre - g
        a_key = jnp.sum(a * diag, axis=0, keepdims=True)
        a_max = prefix_max(a)
        m_row = g + jnp.maximum(m_old, a_max)
        p = qk_all * jnp.exp(jnp.where(causal, g + a_key, -jnp.inf) - m_row)
        p16 = p.astype(BF16)
        num_intra = _dot(p16, vb)
        rowsum = _dot_sel(p, same_head16, terms=2)
        yield

        w_inter = jnp.exp(g + m_old - m_row)
        num = w_inter * q_c + num_intra
        den = w_inter * q_n + rowsum
        hv = num / jnp.maximum(jnp.abs(den), jnp.exp(-m_row))
        ms = _dot_sel(hv * hv, same_head16, terms=2) * (1.0 / M_HEAD_DIM)
        yield
        y = hv * lax.rsqrt(ms + EPS) * ng * _sigmoid(og)
        o_ref[pl.ds(r0, L), :] = y

        g_end = g[L - 1:L, :]
        m_new = g_end + jnp.maximum(m_old, a_max[L - 1:L, :])
        wa = jnp.exp(g_end + a - m_new)
        decay = jnp.exp(g_end + m_old - m_new)
        upd = _dot_tn(k16, (wa * v).astype(BF16))
        ct_ref[sq] = decay * ct + upd * same_head
        n_ref[sq] = decay * nvec + jnp.sum(wa * k, axis=0, keepdims=True)
        m_ref[sq] = m_new
        yield

    def body(c, carry):
        for _ in zip(*[chunk(sq, c) for sq in range(M_SEQS)]):
            pass
        return carry

    lax.fori_loop(0, nc, body, 0)


def _mlstm(proj, cw, cb, gbc, ng, B, S):
    assert M_CHUNK == M_HEAD_DIM
    T = B * S
    rows = M_SEQS * S
    return pl.pallas_call(
        _mlstm_kernel,
        grid=(B // M_SEQS,),
        in_specs=[pl.BlockSpec((rows, 4 * M_WIDTH), lambda b: (b, 0)),
                  pl.BlockSpec((rows, LANES), lambda b: (b, PROJ_SMALL // LANES)),
                  pl.BlockSpec((M_CONV, 2 * M_WIDTH), lambda b: (0, 0)),
                  pl.BlockSpec((1, 2 * M_WIDTH), lambda b: (0, 0)),
                  pl.BlockSpec((1, LANES), lambda b: (0, 0)),
                  pl.BlockSpec((1, M_WIDTH), lambda b: (0, 0))],
        out_specs=pl.BlockSpec((rows, M_WIDTH), lambda b: (b, 0)),
        out_shape=jax.ShapeDtypeStruct((T, M_WIDTH), F32),
        scratch_shapes=[pltpu.VMEM((M_SEQS, M_WIDTH, M_WIDTH), F32),
                        pltpu.VMEM((M_SEQS, 1, M_WIDTH), F32),
                        pltpu.VMEM((M_SEQS, 1, M_WIDTH), F32)],
        compiler_params=_cparams("parallel"),
        name="mlstm",
    )(proj, proj, cw, cb, gbc, ng)


POOL_HALO = 16
POOL_TILE = 256


def _pool_kernel(u_ref, w_ref, sc_ref, o_ref, upad_ref):
    S = u_ref.shape[0]
    upad_ref[0:POOL_HALO, :] = jnp.zeros((POOL_HALO, P_WIDTH), F32)
    upad_ref[POOL_HALO:, :] = u_ref[...]
    grp = lax.broadcasted_iota(jnp.int32, (1, P_WIDTH), 1) // (P_WIDTH // len(P_WINDOWS))
    win_lane = jnp.zeros((1, P_WIDTH), jnp.int32)
    for gi, wn in enumerate(P_WINDOWS):
        win_lane = jnp.where(grp == gi, wn, win_lane)
    w = w_ref[...]
    scale = sc_ref[...]
    rows = POOL_TILE + POOL_HALO

    def body(r, carry):
        r0 = pl.multiple_of(r * POOL_TILE, POOL_TILE)
        a = upad_ref[pl.ds(r0, rows), :]
        sums = []
        cur = a
        span = 1
        for _ in P_WINDOWS:
            cur = cur + pltpu.roll(cur, span, axis=0)
            span *= 2
            sums.append(cur)
        sel = sums[-1]
        for gi in range(len(P_WINDOWS) - 1):
            sel = jnp.where(grp == gi, sums[gi], sel)
        sel = sel[POOL_HALO:, :]
        u = a[POOL_HALO:, :]
        t = r0 + lax.broadcasted_iota(jnp.int32, (POOL_TILE, P_WIDTH), 0)
        cnt = jnp.minimum(t + 1, win_lane).astype(F32)
        pooled = sel / cnt - u
        o_ref[pl.ds(r0, POOL_TILE), :] = _dot(pooled.astype(BF16), w) * scale
        return carry

    lax.fori_loop(0, S // POOL_TILE, body, 0)


def _pool(proj, w_bd, scale, B, S):
    T = B * S
    return pl.pallas_call(
        _pool_kernel,
        grid=(B,),
        in_specs=[pl.BlockSpec((S, P_WIDTH), lambda b: (b, PROJ_UP // P_WIDTH)),
                  pl.BlockSpec((P_WIDTH, P_WIDTH), lambda b: (0, 0)),
                  pl.BlockSpec((1, P_WIDTH), lambda b: (0, 0))],
        out_specs=pl.BlockSpec((S, P_WIDTH), lambda b: (b, 0)),
        out_shape=jax.ShapeDtypeStruct((T, P_WIDTH), F32),
        scratch_shapes=[pltpu.VMEM((S + POOL_HALO, P_WIDTH), F32)],
        compiler_params=_cparams("parallel"),
        name="pool",
    )(proj, w_bd, scale)


def _rope(x, c, s1, s2):
    return x * c + pltpu.roll(x, LANES - A_ROPE // 2, axis=1) * s1 + pltpu.roll(x, A_ROPE // 2, axis=1) * s2


def _mla_prep_kernel(cq_ref, ckv_ref, small_ref, qg_ref, kvg_ref, wq_ref, wk_ref, wv_ref,
                     vone_ref, cq_t_ref, ck_t_ref, s1_ref, s2_ref, q_ref, k_ref, v_ref):
    def rms(x, g):
        return x * lax.rsqrt(jnp.mean(x * x, axis=-1, keepdims=True) + EPS) * g

    cqn = rms(cq_ref[...], qg_ref[...]).astype(BF16)
    ckvn = rms(ckv_ref[...], kvg_ref[...]).astype(BF16)
    scale = (A_NOPE + A_ROPE) ** -0.5
    qf = _dot(cqn, wq_ref[...]) * scale
    kf = _dot(ckvn, wk_ref[...])
    v_ref[...] = (_dot(ckvn, wv_ref[...]) + vone_ref[...]).astype(BF16)
    cqt = cq_t_ref[...]
    s1 = s1_ref[...]
    s2 = s2_ref[...]
    krot = _rope(small_ref[...], ck_t_ref[...], s1, s2)
    for h in range(A_HEADS):
        sl = slice(h * LANES, (h + 1) * LANES)
        q_ref[:, sl] = _rope(qf[:, sl], cqt, s1, s2).astype(BF16)
        k_ref[:, sl] = (kf[:, sl] + krot).astype(BF16)


def _mla_prep(proj, qg, kvg, wq, wk, wv, vone, cq_t, ck_t, s1_t, s2_t, B, S, ts=512):
    T = B * S
    nst = S // ts
    hw = A_HEADS * LANES
    return pl.pallas_call(
        _mla_prep_kernel,
        grid=(B, nst),
        in_specs=[pl.BlockSpec((ts, A_QRANK), lambda b, s: (b * nst + s, PROJ_CQ // A_QRANK)),
                  pl.BlockSpec((ts, A_KVRANK), lambda b, s: (b * nst + s, PROJ_CKV // A_KVRANK)),
                  pl.BlockSpec((ts, LANES), lambda b, s: (b * nst + s, PROJ_SMALL // LANES)),
                  pl.BlockSpec((1, A_QRANK), lambda b, s: (0, 0)),
                  pl.BlockSpec((1, A_KVRANK), lambda b, s: (0, 0)),
                  pl.BlockSpec((A_QRANK, hw), lambda b, s: (0, 0)),
                  pl.BlockSpec((A_KVRANK, hw), lambda b, s: (0, 0)),
                  pl.BlockSpec((A_KVRANK, hw), lambda b, s: (0, 0)),
                  pl.BlockSpec((1, hw), lambda b, s: (0, 0)),
                  pl.BlockSpec((ts, LANES), lambda b, s: (s, 0)),
                  pl.BlockSpec((ts, LANES), lambda b, s: (s, 0)),
                  pl.BlockSpec((ts, LANES), lambda b, s: (s, 0)),
                  pl.BlockSpec((ts, LANES), lambda b, s: (s, 0))],
        out_specs=[pl.BlockSpec((ts, hw), lambda b, s: (b * nst + s, 0)),
                   pl.BlockSpec((ts, hw), lambda b, s: (b * nst + s, 0)),
                   pl.BlockSpec((ts, hw), lambda b, s: (b * nst + s, 0))],
        out_shape=[jax.ShapeDtypeStruct((T, hw), BF16),
                   jax.ShapeDtypeStruct((T, hw), BF16),
                   jax.ShapeDtypeStruct((T, hw), BF16)],
        compiler_params=_cparams("parallel", "parallel"),
        name="mla_prep",
    )(proj, proj, proj, qg, kvg, wq, wk, wv, vone, cq_t, ck_t, s1_t, s2_t)


def _attn_kernel(q_ref, k_ref, v_ref, o_ref, *, tq):
    qi = pl.program_id(2)
    heads = range(2)
    sls = [slice(hh * LANES, (hh + 1) * LANES) for hh in heads]
    qs = [q_ref[:, sl] for sl in sls]

    def update(k0, ncols, state, causal=False):
        s = [_dot_nt(qs[hh], k_ref[pl.ds(k0, ncols), sls[hh]]) for hh in heads]
        if causal:
            ok = (lax.broadcasted_iota(jnp.int32, s[0].shape, 0) >= lax.broadcasted_iota(jnp.int32, s[0].shape, 1))
            s = [jnp.where(ok, s[hh], -jnp.inf) for hh in heads]
        m_new = [jnp.maximum(state[hh][0], jnp.max(s[hh], axis=-1, keepdims=True)) for hh in heads]
        p = [jnp.exp((s[hh] - m_new[hh]).astype(BF16)) for hh in heads]
        pv = [_dot(p[hh], v_ref[pl.ds(k0, ncols), sls[hh]]) for hh in heads]
        acc = [jnp.exp(state[hh][0] - m_new[hh]) * state[hh][1] + pv[hh] for hh in heads]
        return tuple((m_new[hh], acc[hh]) for hh in heads)

    init = tuple((jnp.full((tq, 1), -jnp.inf, F32), jnp.zeros((tq, LANES), F32)) for _ in heads)
    carry = lax.fori_loop(0, qi, lambda kb, st: update(pl.multiple_of(kb * tq, tq), tq, st), init)
    (_, acc0), (_, acc1) = update(pl.multiple_of(qi * tq, tq), tq, carry, causal=True)
    lane = lax.broadcasted_iota(jnp.int32, (tq, LANES), 1)
    acc = jnp.where(lane < A_VDIM, acc0, acc1)
    den = jnp.where(lane < A_VDIM, pltpu.roll(acc0, A_VDIM, axis=1), pltpu.roll(acc1, A_VDIM, axis=1))
    o_ref[...] = (acc / den).astype(o_ref.dtype)


def _attention(q, k, v, B, S, tq=512):
    T = B * S
    nq = S // tq
    return pl.pallas_call(
        functools.partial(_attn_kernel, tq=tq),
        grid=(B, A_HEADS // 2, nq),
        in_specs=[pl.BlockSpec((tq, 2 * LANES), lambda b, p, i: (b * nq + i, p)),
                  pl.BlockSpec((S, 2 * LANES), lambda b, p, i: (b, p)),
                  pl.BlockSpec((S, 2 * LANES), lambda b, p, i: (b, p))],
        out_specs=pl.BlockSpec((tq, LANES), lambda b, p, i: (b * nq + i, p)),
        out_shape=jax.ShapeDtypeStruct((T, A_WIDTH), BF16),
        compiler_params=_cparams("parallel", "parallel", "arbitrary"),
        name="attention",
    )(q, k, v)


def _outproj_kernel(ym_ref, ya_ref, yp_ref, h_ref, w_ref, g_ref, wr_ref, br_ref,
                    hn_ref, xn_ref, lg_ref):
    mix = _dot(ym_ref[...].astype(BF16), w_ref[0:M_WIDTH, :])
    mix = mix + _dot(ya_ref[...], w_ref[M_WIDTH:M_WIDTH + A_WIDTH, :])
    mix = mix + _dot(yp_ref[...].astype(BF16), w_ref[M_WIDTH + A_WIDTH:, :])
    hn = h_ref[...] + mix
    hn_ref[...] = hn
    xn = hn * lax.rsqrt(jnp.mean(hn * hn, axis=-1, keepdims=True) + EPS) * g_ref[...]
    x_hi = xn.astype(BF16)
    x_lo = (xn - x_hi.astype(F32)).astype(BF16)
    wr = wr_ref[...]
    w_hi = wr.astype(BF16)
    w_lo = (wr - w_hi.astype(F32)).astype(BF16)
    lg_ref[...] = _dot_nt(w_hi, x_hi) + _dot_nt(w_hi, x_lo) + _dot_nt(w_lo, x_hi) + br_ref[...]
    xn_ref[...] = _pack_bf16_pairs(xn)


def _outproj(ym, ya, yp, h2, w, g, wr_t, br, tm=512):
    T = h2.shape[0]
    return pl.pallas_call(
        _outproj_kernel,
        grid=(T // tm,),
        in_specs=[pl.BlockSpec((tm, M_WIDTH), lambda i: (i, 0)),
                  pl.BlockSpec((tm, A_WIDTH), lambda i: (i, 0)),
                  pl.BlockSpec((tm, P_WIDTH), lambda i: (i, 0)),
                  pl.BlockSpec((tm, D_MODEL), lambda i: (i, 0)),
                  pl.BlockSpec((D_MODEL, D_MODEL), lambda i: (0, 0)),
                  pl.BlockSpec((1, D_MODEL), lambda i: (0, 0)),
                  pl.BlockSpec((N_EXPERTS, D_MODEL), lambda i: (0, 0)),
                  pl.BlockSpec((N_EXPERTS, 1), lambda i: (0, 0))],
        out_specs=[pl.BlockSpec((tm, D_MODEL), lambda i: (i, 0)),
                   pl.BlockSpec((tm, D_MODEL // 2), lambda i: (i, 0)),
                   pl.BlockSpec((N_EXPERTS, tm), lambda i: (0, i))],
        out_shape=[jax.ShapeDtypeStruct((T, D_MODEL), F32),
                   jax.ShapeDtypeStruct((T, D_MODEL // 2), jnp.uint32),
                   jax.ShapeDtypeStruct((N_EXPERTS, T), F32)],
        compiler_params=_cparams("parallel"),
        name="outproj",
    )(ym, ya, yp, h2, w, g, wr_t, br)


def _router_kernel(lg_ref, tri_ref, e_ref, w_ref, r_ref, cnt_ref, carry_ref):
    tr = lg_ref.shape[1]

    @pl.when(pl.program_id(0) == 0)
    def _():
        carry_ref[...] = jnp.zeros_like(carry_ref)

    x = lg_ref[...]
    eio = lax.broadcasted_iota(jnp.int32, (N_EXPERTS, tr), 0).astype(F32)
    picked = jnp.zeros((N_EXPERTS, tr), F32)
    vals = []
    idxs = []
    for _ in range(TOP_K):
        mx = jnp.max(x, axis=0, keepdims=True)
        idx = jnp.min(jnp.where(x == mx, eio, float(N_EXPERTS)), axis=0, keepdims=True)
        hit = eio == idx
        vals.append(mx)
        idxs.append(idx)
        picked = picked + hit.astype(F32)
        x = jnp.where(hit, -jnp.inf, x)
    exps = [jnp.exp(vv - vals[0]) for vv in vals]
    tot = exps[0] + exps[1] + exps[2] + exps[3]
    before = _dot(picked.astype(BF16), tri_ref[...]) + carry_ref[:, 0:1]
    for kk in range(TOP_K):
        e_ref[kk:kk + 1, :] = idxs[kk].astype(jnp.int32)
        w_ref[kk:kk + 1, :] = exps[kk] / tot
        rk = jnp.sum(jnp.where(eio == idxs[kk], before, 0.0), axis=0, keepdims=True)
        r_ref[kk:kk + 1, :] = rk.astype(jnp.int32)
    carry_ref[...] = carry_ref[...] + jnp.sum(picked, axis=1, keepdims=True)
    cnt_ref[...] = carry_ref[...]


def _router(logits_t, tri, tr=512):
    T = logits_t.shape[1]
    return pl.pallas_call(
        _router_kernel,
        grid=(T // tr,),
        in_specs=[pl.BlockSpec((N_EXPERTS, tr), lambda i: (0, i)),
                  pl.BlockSpec((tr, tr), lambda i: (0, 0))],
        out_specs=[pl.BlockSpec((TOP_K, tr), lambda i: (0, i)),
                   pl.BlockSpec((TOP_K, tr), lambda i: (0, i)),
                   pl.BlockSpec((TOP_K, tr), lambda i: (0, i)),
                   pl.BlockSpec((N_EXPERTS, LANES), lambda i: (0, 0))],
        out_shape=[jax.ShapeDtypeStruct((TOP_K, T), jnp.int32),
                   jax.ShapeDtypeStruct((TOP_K, T), F32),
                   jax.ShapeDtypeStruct((TOP_K, T), jnp.int32),
                   jax.ShapeDtypeStruct((N_EXPERTS, LANES), F32)],
        scratch_shapes=[pltpu.VMEM((N_EXPERTS, LANES), F32)],
        compiler_params=_cparams("arbitrary"),
        name="router",
    )(logits_t, tri)


def _meta_kernel(cnt_ref, e_ref, r_ref, dest_ref, be_ref, bc_ref, bn_ref, nb_ref, *, nb_pad):
    cnt = cnt_ref[...]
    padded = jnp.floor((cnt + (MOE_ROWS - 1)) * (1.0 / MOE_ROWS)) * MOE_ROWS
    ri = lax.broadcasted_iota(jnp.int32, (N_EXPERTS, N_EXPERTS), 0)
    ci = lax.broadcasted_iota(jnp.int32, (N_EXPERTS, N_EXPERTS), 1)
    pad_end = _dot((ri >= ci).astype(F32), padded, precision=HIGHEST)
    pad_start = pad_end - padded
    e = e_ref[...]
    dest = r_ref[...]
    for ex in range(N_EXPERTS):
        ps = pad_start[ex:ex + 1, 0:1].astype(jnp.int32)
        dest = jnp.where(e == ex, dest + ps, dest)
    dest_ref[...] = dest
    blk0 = (lax.broadcasted_iota(jnp.int32, (N_EXPERTS, nb_pad), 1) * MOE_ROWS).astype(F32)
    be = jnp.sum((pad_end[:, 0:1] <= blk0).astype(F32), axis=0, keepdims=True)
    be = jnp.minimum(be, float(N_EXPERTS - 1))
    eio = lax.broadcasted_iota(jnp.int32, (N_EXPERTS, nb_pad), 0).astype(F32)
    seg_end = jnp.sum(jnp.where(eio == be, pad_start[:, 0:1] + cnt[:, 0:1], 0.0), axis=0, keepdims=True)
    bc = jnp.clip(seg_end - blk0[0:1, :], 0.0, float(MOE_ROWS))
    nxt0 = jnp.sum(jnp.where(eio == be, pad_end[:, 0:1], 0.0), axis=0, keepdims=True)
    bn = jnp.sum((pad_end[:, 0:1] <= nxt0).astype(F32), axis=0, keepdims=True)
    bn = jnp.where(nxt0 < pad_end[N_EXPERTS - 1:N_EXPERTS, 0:1], bn, -1.0)
    be_ref[...] = be.astype(jnp.int32)
    bc_ref[...] = bc.astype(jnp.int32)
    bn_ref[...] = bn.astype(jnp.int32)
    nb_ref[...] = (pad_end[N_EXPERTS - 1:N_EXPERTS, :] * (1.0 / MOE_ROWS)).astype(jnp.int32)


def _meta(counts, eidx, rank, nb_pad):
    T = eidx.shape[1]
    return pl.pallas_call(
        functools.partial(_meta_kernel, nb_pad=nb_pad),
        out_shape=[jax.ShapeDtypeStruct((TOP_K, T), jnp.int32),
                   jax.ShapeDtypeStruct((1, nb_pad), jnp.int32),
                   jax.ShapeDtypeStruct((1, nb_pad), jnp.int32),
                   jax.ShapeDtypeStruct((1, nb_pad), jnp.int32),
                   jax.ShapeDtypeStruct((1, LANES), jnp.int32)],
        compiler_params=pltpu.CompilerParams(vmem_limit_bytes=VMEM_LIMIT),
        name="route_meta",
    )(counts, eidx, rank)


FF_CHUNK = 512


def _gmm_kernel(be_ref, bc_ref, bn_ref, nb_ref, x_ref, wgu_hbm, bgu_ref, wdn_hbm, bdn_ref, y_ref,
                wgu_st, wdn_st, wgu16, wdn16, sem, *, e0):
    i = pl.program_id(0)
    nblk = nb_ref[0]
    bm = MOE_ROWS

    def weight_copies(e):
        return (pltpu.make_async_copy(wgu_hbm.at[e0 + e], wgu_st, sem.at[0]),
                pltpu.make_async_copy(wdn_hbm.at[e0 + e], wdn_st, sem.at[1]))

    @pl.when(i == 0)
    def _():
        for cp in weight_copies(be_ref[0]):
            cp.start()

    @pl.when(i >= nblk)
    def _():
        y_ref[...] = jnp.zeros_like(y_ref)

    @pl.when(i < nblk)
    def _():
        e_changed = jnp.logical_or(i == 0, be_ref[i] != be_ref[jnp.maximum(i - 1, 0)])

        @pl.when(e_changed)
        def _():
            for cp in weight_copies(be_ref[i]):
                cp.wait()
            wgu16[...] = wgu_st[...].astype(BF16)
            wdn16[...] = wdn_st[...].astype(BF16)

            @pl.when(bn_ref[i] >= 0)
            def _():
                for cp in weight_copies(bn_ref[i]):
                    cp.start()

        def expert_rows(nrows):
            valid = lax.broadcasted_iota(jnp.int32, (nrows, 1), 0) < bc_ref[i]
            lo, hi = _unpack_bf16_pairs(jnp.where(valid, x_ref[0:nrows, :], jnp.uint32(0)))
            x16 = jnp.concatenate([lo.astype(BF16), hi.astype(BF16)], axis=1)
            acc = jnp.zeros((nrows, D_MODEL), F32) + bdn_ref[0]
            for c in range(D_FF // FF_CHUNK):
                cs = slice(c * FF_CHUNK, (c + 1) * FF_CHUNK)
                us = slice(D_FF + c * FF_CHUNK, D_FF + (c + 1) * FF_CHUNK)
                gate = _dot(x16, wgu16[:, cs]) + bgu_ref[0, :, cs]
                up = _dot(x16, wgu16[:, us]) + bgu_ref[0, :, us]
                gate = jnp.minimum(gate, SWIGLU_LIMIT)
                up = jnp.clip(up, -SWIGLU_LIMIT, SWIGLU_LIMIT)
                act = (up + 1.0) * gate * _sigmoid(SWIGLU_ALPHA * gate)
                acc = acc + _dot(act.astype(BF16), wdn16[cs, :])
            y_ref[0:nrows, :] = _pack_bf16_pairs(acc)

        @pl.when(bc_ref[i] > bm // 2)
        def _():
            expert_rows(bm)

        @pl.when(bc_ref[i] <= bm // 2)
        def _():
            expert_rows(bm // 2)
            y_ref[bm // 2:, :] = jnp.zeros((bm // 2, D_MODEL // 2), jnp.uint32)


def _gmm(blk_e, blk_cnt, blk_next, nblk, x_rows, wgu, bgu, wdn, bdn, nb, layer):
    bm = MOE_ROWS
    e0 = layer * N_EXPERTS
    dp = D_MODEL // 2

    def expert(i, be, nb_ref):
        return (e0 + be[jnp.minimum(i, jnp.maximum(nb_ref[0] - 1, 0))], 0, 0)

    def rows(i, nb_ref):
        return (jnp.minimum(i, jnp.maximum(nb_ref[0] - 1, 0)), 0)

    grid_spec = pltpu.PrefetchScalarGridSpec(
        num_scalar_prefetch=4,
        grid=(nb,),
        in_specs=[pl.BlockSpec((bm, dp), lambda i, be, bc, bn, nbr: rows(i, nbr)),
                  pl.BlockSpec(memory_space=pl.ANY),
                  pl.BlockSpec((1, 1, 2 * D_FF), lambda i, be, bc, bn, nbr: expert(i, be, nbr)),
                  pl.BlockSpec(memory_space=pl.ANY),
                  pl.BlockSpec((1, 1, D_MODEL), lambda i, be, bc, bn, nbr: expert(i, be, nbr))],
        out_specs=pl.BlockSpec((bm, dp), lambda i, be, bc, bn, nbr: (i, 0)),
        scratch_shapes=[pltpu.VMEM((D_MODEL, 2 * D_FF), F32),
                        pltpu.VMEM((D_FF, D_MODEL), F32),
                        pltpu.VMEM((D_MODEL, 2 * D_FF), BF16),
                        pltpu.VMEM((D_FF, D_MODEL), BF16),
                        pltpu.SemaphoreType.DMA((2,))],
    )
    return pl.pallas_call(
        functools.partial(_gmm_kernel, e0=e0),
        grid_spec=grid_spec,
        out_shape=jax.ShapeDtypeStruct((nb * bm, dp), jnp.uint32),
        compiler_params=_cparams("arbitrary"),
        name="expert_gmm",
    )(blk_e, blk_cnt, blk_next, nblk, x_rows, wgu, bgu, wdn, bdn)


def _final_kernel(h_ref, y0, y1, y2, y3, wk_ref, g_ref, o_ref):
    x = _moe_combine(h_ref, (y0, y1, y2, y3), wk_ref)
    o_ref[...] = x * lax.rsqrt(jnp.mean(x * x, axis=-1, keepdims=True) + EPS) * g_ref[...]


def _final(h2, y_tok, w_tok, g, tm=512):
    T = h2.shape[0]
    nt = T // tm
    return pl.pallas_call(
        _final_kernel,
        grid=(nt,),
        in_specs=_combine_specs(tm, nt) + [pl.BlockSpec((1, D_MODEL), lambda i: (0, 0))],
        out_specs=pl.BlockSpec((tm, D_MODEL), lambda i: (i, 0)),
        out_shape=jax.ShapeDtypeStruct((T, D_MODEL), F32),
        compiler_params=_cparams("parallel"),
        name="combine_final_norm",
    )(h2, y_tok, y_tok, y_tok, y_tok, w_tok, g)


SC_CORES = 2
SC_SUBCORES = 16
SC_LANES = 16
SC_WORKERS = SC_CORES * SC_SUBCORES
SC_WINDOW = 64


def _sc_mesh():
    return plsc.VectorSubcoreMesh(core_axis_name="c", subcore_axis_name="s")


def _sc_worker():
    return lax.axis_index("s") * SC_CORES + lax.axis_index("c")


def _sc_dispatch(xn, dest_flat, n_rows):
    T = xn.shape[0]
    tpw = T // SC_WORKERS
    nchunk = tpw // SC_WINDOW
    nvec = SC_WINDOW // SC_LANES

    @functools.partial(
        pl.kernel, out_type=jax.ShapeDtypeStruct((n_rows, xn.shape[1]), xn.dtype), mesh=_sc_mesh(),
        scratch_types=[pltpu.VMEM((TOP_K * tpw,), jnp.int32),
                       pltpu.VMEM((SC_WINDOW, xn.shape[1]), xn.dtype),
                       pltpu.VMEM((SC_WINDOW, xn.shape[1]), xn.dtype),
                       pltpu.SemaphoreType.DMA, pltpu.SemaphoreType.DMA, pltpu.SemaphoreType.DMA],
        name="sc_dispatch")
    def run(x_hbm, d_hbm, o_hbm, idx_v, buf0, buf1, sem0, sem1, sem_out):
        base = _sc_worker() * tpw
        for kk in range(TOP_K):
            pltpu.sync_copy(d_hbm.at[pl.ds(kk * T + base, tpw)], idx_v.at[pl.ds(kk * tpw, tpw)])
        bufs = (buf0, buf1)
        sems = (sem0, sem1)

        def load(c, slot):
            return pltpu.make_async_copy(x_hbm.at[pl.ds(base + c * SC_WINDOW, SC_WINDOW)], bufs[slot], sems[slot])

        load(0, 0).start()

        @pl.loop(0, nchunk, step=2)
        def _(c0):
            for slot in range(2):
                c = c0 + slot
                load(c, slot).wait()

                @pl.when(c + 1 < nchunk)
                def _():
                    load(c + 1, 1 - slot).start()

                copies = []
                for kk in range(TOP_K):
                    for q in range(nvec):
                        off = pl.multiple_of(kk * tpw + c * SC_WINDOW + q * SC_LANES, SC_LANES)
                        rows = idx_v[pl.ds(off, SC_LANES)]
                        cp = pltpu.make_async_copy(bufs[slot].at[pl.ds(q * SC_LANES, SC_LANES)],
                                                   o_hbm.at[rows], sem_out)
                        cp.start()
                        copies.append(cp)
                for cp in copies:
                    cp.wait()

    return run(xn, dest_flat)


def _sc_gather(y_rows, dest_flat):
    n = dest_flat.shape[0]
    rpw = n // SC_WORKERS
    nchunk = rpw // SC_WINDOW
    nvec = SC_WINDOW // SC_LANES

    @functools.partial(
        pl.kernel, out_type=jax.ShapeDtypeStruct((n, y_rows.shape[1]), y_rows.dtype), mesh=_sc_mesh(),
        scratch_types=[pltpu.VMEM((rpw,), jnp.int32),
                       pltpu.VMEM((SC_WINDOW, y_rows.shape[1]), y_rows.dtype),
                       pltpu.VMEM((SC_WINDOW, y_rows.shape[1]), y_rows.dtype),
                       pltpu.SemaphoreType.DMA, pltpu.SemaphoreType.DMA, pltpu.SemaphoreType.DMA],
        name="sc_gather")
    def run(y_hbm, d_hbm, o_hbm, idx_v, buf0, buf1, sem0, sem1, sem_in):
        base = _sc_worker() * rpw
        pltpu.sync_copy(d_hbm.at[pl.ds(base, rpw)], idx_v)
        bufs = (buf0, buf1)
        sems = (sem0, sem1)

        def store(c, slot):
            return pltpu.make_async_copy(bufs[slot], o_hbm.at[pl.ds(base + c * SC_WINDOW, SC_WINDOW)], sems[slot])

        @pl.loop(0, nchunk, step=2)
        def _(c0):
            for slot in range(2):
                c = c0 + slot

                @pl.when(c >= 2)
                def _():
                    store(c - 2, slot).wait()

                copies = []
                for q in range(nvec):
                    off = pl.multiple_of(c * SC_WINDOW + q * SC_LANES, SC_LANES)
                    rows = idx_v[pl.ds(off, SC_LANES)]
                    cp = pltpu.make_async_copy(y_hbm.at[rows], bufs[slot].at[pl.ds(q * SC_LANES, SC_LANES)], sem_in)
                    cp.start()
                    copies.append(cp)
                for cp in copies:
                    cp.wait()
                store(c, slot).start()

        store(nchunk - 2, 0).wait()
        store(nchunk - 1, 1).wait()

    return run(y_rows, dest_flat)


def _prep_w_in(w_in):
    o_g = 4 * M_WIDTH
    o_cq = o_g + 2 * M_HEADS
    o_ckv = o_cq + A_QRANK
    o_kr = o_ckv + A_KVRANK
    o_up = o_kr + A_ROPE
    z = lambda n: jnp.zeros(w_in.shape[:-1] + (n,), w_in.dtype)
    small = jnp.concatenate([z(SMALL_KR), w_in[..., o_kr:o_up], w_in[..., o_g:o_cq],
                             z(LANES - SMALL_GATE - 2 * M_HEADS)], axis=-1)
    return jnp.concatenate([w_in[..., 0:o_g], w_in[..., o_cq:o_ckv], w_in[..., o_up:o_up + P_WIDTH],
                            w_in[..., o_ckv:o_kr], small], axis=-1).astype(BF16)


def _rope_tables(seq):
    inv = ROPE_THETA ** (-jnp.arange(0, A_ROPE, 2, dtype=F32) / A_ROPE)
    ang = jnp.arange(seq, dtype=F32)[:, None] * inv[None, :]
    cos, sin = jnp.cos(ang), jnp.sin(ang)
    half = A_ROPE // 2
    zeros = lambda n: jnp.zeros((seq, n), F32)
    ones = lambda n: jnp.ones((seq, n), F32)
    tail = LANES - A_NOPE - A_ROPE
    cq_t = jnp.concatenate([ones(A_NOPE), cos, cos, zeros(tail)], axis=1)
    ck_t = jnp.concatenate([zeros(A_NOPE), cos, cos, zeros(tail)], axis=1)
    s1_t = jnp.concatenate([zeros(A_NOPE), -sin, zeros(half), zeros(tail)], axis=1)
    s2_t = jnp.concatenate([zeros(A_NOPE), zeros(half), sin, zeros(tail)], axis=1)
    return cq_t, ck_t, s1_t, s2_t


def kernel(x, norm1_g, w_in, conv_w, conv_b, gate_b, mlstm_norm_g, q_norm_g, kv_norm_g, w_uq, w_ukv,
           w_pool, pool_scale, w_out, norm2_g, w_router, b_router, w_gate_up, b_gate_up, w_down, b_down,
           final_norm_g):
    B, S, D = x.shape
    depth = w_in.shape[0]
    T = B * S
    nb = (T * TOP_K) // MOE_ROWS + N_EXPERTS
    nb_pad = -(-nb // LANES) * LANES

    w_in_p = _prep_w_in(w_in)
    wq = w_uq.reshape(depth, A_QRANK, A_HEADS, A_NOPE + A_ROPE)
    wq = jnp.pad(wq, ((0, 0), (0, 0), (0, 0), (0, LANES - A_NOPE - A_ROPE)))
    wq = wq.reshape(depth, A_QRANK, A_HEADS * LANES).astype(BF16)
    wkv = w_ukv.reshape(depth, A_KVRANK, A_HEADS, A_NOPE + A_VDIM)
    wk = jnp.pad(wkv[..., :A_NOPE], ((0, 0), (0, 0), (0, 0), (0, LANES - A_NOPE)))
    wk = wk.reshape(depth, A_KVRANK, A_HEADS * LANES).astype(BF16)
    wv_e = jnp.pad(wkv[:, :, 0::2, A_NOPE:], ((0, 0), (0, 0), (0, 0), (0, LANES - A_VDIM)))
    wv_o = jnp.pad(wkv[:, :, 1::2, A_NOPE:], ((0, 0), (0, 0), (0, 0), (LANES - A_VDIM, 0)))
    wv = jnp.stack([wv_e, wv_o], axis=3).reshape(depth, A_KVRANK, A_HEADS * LANES).astype(BF16)
    half = jnp.arange(A_HEADS * LANES) // A_VDIM
    vone = ((half % 4 == 1) | (half % 4 == 2)).astype(F32)[None, :]
    gsz = P_WIDTH // len(P_WINDOWS)
    w_pool_bd = jnp.zeros((depth, P_WIDTH, P_WIDTH), F32)
    for gi in range(len(P_WINDOWS)):
        w_pool_bd = w_pool_bd.at[:, gi * gsz:(gi + 1) * gsz, gi * gsz:(gi + 1) * gsz].set(w_pool[:, gi])
    w_pool_bd = w_pool_bd.astype(BF16)
    w_out16 = w_out.astype(BF16)
    w_router_t = jnp.swapaxes(w_router, 1, 2)
    gate_b_col = jnp.pad(gate_b, ((0, 0), (SMALL_GATE, LANES - SMALL_GATE - 2 * M_HEADS)))
    cq_t, ck_t, s1_t, s2_t = _rope_tables(S)
    tr = 512
    tri = (jnp.arange(tr)[:, None] < jnp.arange(tr)[None, :]).astype(BF16)

    wgu_all = w_gate_up.reshape(depth * N_EXPERTS, D_MODEL, 2 * D_FF)
    bgu_all = b_gate_up.reshape(depth * N_EXPERTS, 1, 2 * D_FF)
    wdn_all = w_down.reshape(depth * N_EXPERTS, D_FF, D_MODEL)
    bdn_all = b_down.reshape(depth * N_EXPERTS, 1, D_MODEL)

    h = x.reshape(T, D)
    moe = None
    for l in range(depth):
        h, proj = _inproj(h, norm1_g[l][None, :], w_in_p[l], moe)
        y_m = _mlstm(proj, conv_w[l], conv_b[l][None, :], gate_b_col[l][None, :],
                     mlstm_norm_g[l][None, :], B, S)
        q16, k16, v16 = _mla_prep(proj, q_norm_g[l][None, :], kv_norm_g[l][None, :], wq[l], wk[l], wv[l],
                                  vone, cq_t, ck_t, s1_t, s2_t, B, S)
        y_a = _attention(q16, k16, v16, B, S)
        y_p = _pool(proj, w_pool_bd[l], pool_scale[l][None, :], B, S)
        h, xn, logits_t = _outproj(y_m, y_a, y_p, h, w_out16[l], norm2_g[l][None, :],
                                   w_router_t[l], b_router[l][:, None])
        eidx, wts, rank, counts = _router(logits_t, tri, tr)
        dest, blk_e, blk_cnt, blk_next, nblk = _meta(counts, eidx, rank, nb_pad)
        dest_flat = dest.reshape(TOP_K * T)
        x_rows = _sc_dispatch(xn, dest_flat, nb * MOE_ROWS)
        y_rows = _gmm(blk_e[0], blk_cnt[0], blk_next[0], nblk[0], x_rows, wgu_all, bgu_all, wdn_all,
                      bdn_all, nb, l)
        moe = (_sc_gather(y_rows, dest_flat), wts.T)
    return _final(h, moe[0], moe[1], final_norm_g[None, :]).reshape(B, S, D)
```

```python
import functools

import jax
import jax.numpy as jnp
import numpy as np
from jax import lax
from jax.experimental import pallas as pl
from jax.experimental.pallas import tpu as pltpu
from jax.experimental.pallas import tpu_sc as plsc

F32 = jnp.float32
BF16 = jnp.bfloat16
HIGHEST = lax.Precision.HIGHEST

D_MODEL = 1024
M_HEADS = 4
M_HEAD_DIM = 64
M_WIDTH = 256
M_CONV = 4
M_CHUNK = 64
A_HEADS = 8
A_NOPE = 64
A_ROPE = 32
A_VDIM = 64
A_QRANK = 256
A_KVRANK = 128
A_WIDTH = 512
ROPE_THETA = 10000.0
P_WINDOWS = (2, 4, 8, 16)
P_WIDTH = 256
N_EXPERTS = 32
TOP_K = 4
D_FF = 1024
SWIGLU_LIMIT = 7.0
SWIGLU_ALPHA = 1.702
EPS = 1e-6

LANES = 128
SUBLANES = 8

PROJ_QKVO = 0
PROJ_CQ = 1024
PROJ_UP = 1280
PROJ_CKV = 1536
PROJ_SMALL = 1664
PROJ_WIDTH = 1792
SMALL_KR = 64
SMALL_GATE = 96

MOE_ROWS = 512
VMEM_LIMIT = 56 * 1024 * 1024


def _cparams(*sem):
    return pltpu.CompilerParams(dimension_semantics=sem, vmem_limit_bytes=VMEM_LIMIT)


def _sigmoid(x):
    return 1.0 / (1.0 + jnp.exp(-x))


def _log_sigmoid(x):
    return jnp.minimum(x, 0.0) - jnp.log(1.0 + jnp.exp(-jnp.abs(x)))


def _dot(a, b, **kw):
    return jnp.dot(a, b, preferred_element_type=F32, **kw)


def _dot_nt(a, b, **kw):
    return lax.dot_general(a, b, (((1,), (1,)), ((), ())), preferred_element_type=F32, **kw)


def _dot_tn(a, b, **kw):
    return lax.dot_general(a, b, (((0,), (0,)), ((), ())), preferred_element_type=F32, **kw)


def _bf16_terms(x, terms=3):
    out = []
    for _ in range(terms):
        piece = x.astype(BF16)
        out.append(piece)
        x = x - piece.astype(F32)
    return out


def _dot_sel(x, sel16, terms=3):
    return sum(_dot(p, sel16) for p in _bf16_terms(x, terms))


def _sel_dot(sel16, x, terms=3):
    return sum(_dot(sel16, p) for p in _bf16_terms(x, terms))


def _pack_bf16_pairs(x):
    n = x.shape[1] // 2
    lo = lax.bitcast_convert_type(x[:, :n].astype(BF16).astype(F32), jnp.uint32)
    hi = lax.bitcast_convert_type(x[:, n:].astype(BF16).astype(F32), jnp.uint32)
    return (lo >> 16) | (hi & jnp.uint32(0xFFFF0000))


def _unpack_bf16_pairs(p):
    lo = lax.bitcast_convert_type(p << 16, F32)
    hi = lax.bitcast_convert_type(p & jnp.uint32(0xFFFF0000), F32)
    return lo, hi


ROW_SUBTILE = 128


def _row_slices(tm):
    return [slice(r, r + ROW_SUBTILE) for r in range(0, tm, ROW_SUBTILE)]


def _moe_combine(h_ref, y_refs, w_ref, rows):
    w = w_ref[rows, :]
    dp = D_MODEL // 2
    acc_lo = h_ref[rows, :dp]
    acc_hi = h_ref[rows, dp:]
    for kk, y_ref in enumerate(y_refs):
        lo, hi = _unpack_bf16_pairs(y_ref[rows, :])
        acc_lo = acc_lo + lo * w[:, kk:kk + 1]
        acc_hi = acc_hi + hi * w[:, kk:kk + 1]
    return jnp.concatenate([acc_lo, acc_hi], axis=1)


def _combine_specs(tm, nt):
    y_specs = [pl.BlockSpec((tm, D_MODEL // 2), functools.partial(lambda i, kk: (kk * nt + i, 0), kk=kk))
               for kk in range(TOP_K)]
    return [pl.BlockSpec((tm, D_MODEL), lambda i: (i, 0))] + y_specs + [pl.BlockSpec((tm, TOP_K), lambda i: (i, 0))]


def _inproj_kernel(*refs, combine):
    if combine:
        h_ref, y0, y1, y2, y3, wk_ref, g_ref, w_ref, hn_ref, o_ref = refs
    else:
        h_ref, g_ref, w_ref, o_ref = refs
    for rows in _row_slices(h_ref.shape[0]):
        if combine:
            x = _moe_combine(h_ref, (y0, y1, y2, y3), wk_ref, rows)
            hn_ref[rows, :] = x
        else:
            x = h_ref[rows, :]
        ms = jnp.mean(x * x, axis=-1, keepdims=True)
        xn = x * lax.rsqrt(ms + EPS) * g_ref[...]
        o_ref[rows, :] = _dot(xn.astype(BF16), w_ref[...])


def _inproj(h2, g, w, moe=None, tm=512):
    T = h2.shape[0]
    nt = T // tm
    w_specs = [pl.BlockSpec((1, D_MODEL), lambda i: (0, 0)),
               pl.BlockSpec((D_MODEL, PROJ_WIDTH), lambda i: (0, 0))]
    proj_spec = pl.BlockSpec((tm, PROJ_WIDTH), lambda i: (i, 0))
    proj_shape = jax.ShapeDtypeStruct((T, PROJ_WIDTH), F32)
    if moe is None:
        return h2, pl.pallas_call(
            functools.partial(_inproj_kernel, combine=False),
            grid=(nt,),
            in_specs=[pl.BlockSpec((tm, D_MODEL), lambda i: (i, 0))] + w_specs,
            out_specs=proj_spec,
            out_shape=proj_shape,
            compiler_params=_cparams("parallel"),
            name="inproj",
        )(h2, g, w)
    y_tok, w_tok = moe
    return pl.pallas_call(
        functools.partial(_inproj_kernel, combine=True),
        grid=(nt,),
        in_specs=_combine_specs(tm, nt) + w_specs,
        out_specs=[pl.BlockSpec((tm, D_MODEL), lambda i: (i, 0)), proj_spec],
        out_shape=[jax.ShapeDtypeStruct((T, D_MODEL), F32), proj_shape],
        compiler_params=_cparams("parallel"),
        name="combine_inproj",
    )(h2, y_tok, y_tok, y_tok, y_tok, w_tok, g, w)


M_SEQS = 2


def _mlstm_kernel(proj_ref, small_ref, cw_ref, cb_ref, gbc_ref, ng_ref, o_ref, ct_ref, n_ref, m_ref):
    S = proj_ref.shape[0] // M_SEQS
    L = M_CHUNK
    nc = S // L
    W = M_WIDTH
    ct_ref[...] = jnp.zeros_like(ct_ref)
    n_ref[...] = jnp.zeros_like(n_ref)
    m_ref[...] = jnp.zeros_like(m_ref)

    rh = lax.broadcasted_iota(jnp.int32, (W, W), 0) // M_HEAD_DIM
    chd = lax.broadcasted_iota(jnp.int32, (W, W), 1) // M_HEAD_DIM
    same_head = (rh == chd).astype(F32)
    same_head16 = same_head.astype(BF16)
    tril16 = (lax.broadcasted_iota(jnp.int32, (L, L), 0) >= lax.broadcasted_iota(jnp.int32, (L, L), 1)).astype(BF16)
    row = lax.broadcasted_iota(jnp.int32, (L, W), 0)
    key = lax.broadcasted_iota(jnp.int32, (L, W), 1) % M_HEAD_DIM
    causal = key <= row
    diag = (key == row).astype(F32)
    er = lax.broadcasted_iota(jnp.int32, (LANES, 2 * W), 0)
    ec = lax.broadcasted_iota(jnp.int32, (LANES, 2 * W), 1)
    spread16 = (er == SMALL_GATE + ec // M_HEAD_DIM).astype(BF16)
    cw = cw_ref[...]
    cb = cb_ref[...]
    gbc = gbc_ref[...]
    ng = ng_ref[...]

    halo_row = lax.broadcasted_iota(jnp.int32, (L + SUBLANES, 1), 0) >= SUBLANES

    def prefix_max(x):
        s = 1
        while s < L:
            x = jnp.maximum(x, jnp.where(row >= s, pltpu.roll(x, s, axis=0), -jnp.inf))
            s *= 2
        return x

    def chunk(sq, c):
        r0 = pl.multiple_of(sq * S + c * L, L)
        w0 = pl.multiple_of(sq * S + jnp.maximum(c * L - SUBLANES, 0), SUBLANES)
        win = proj_ref[pl.ds(w0, L + SUBLANES), 0:2 * W]
        first = jnp.where(halo_row, pltpu.roll(win, SUBLANES, axis=0), 0.0)
        win = jnp.where(c == 0, first, win)
        acc = jnp.zeros((L, 2 * W), F32) + cb
        for j in range(M_CONV):
            s = M_CONV - 1 - j
            xs = win if s == 0 else pltpu.roll(win, s, axis=0)
            acc = acc + xs[SUBLANES:, :] * cw[j:j + 1, :]
        qk = acc * _sigmoid(acc)
        q = qk[:, 0:W]
        k = qk[:, W:2 * W] * (M_HEAD_DIM ** -0.5)
        v = proj_ref[pl.ds(r0, L), 2 * W:3 * W]
        og = proj_ref[pl.ds(r0, L), 3 * W:4 * W]
        q16 = q.astype(BF16)
        k16 = k.astype(BF16)
        v16 = v.astype(BF16)
        ct = ct_ref[sq]
        nvec = n_ref[sq]
        m_old = m_ref[sq]
        gates = _dot_sel(small_ref[pl.ds(r0, L), :] + gbc, spread16)
        kb = jnp.concatenate([k16] * M_HEADS, axis=0) * same_head16
        vb = jnp.concatenate([v16] * M_HEADS, axis=0) * same_head16
        qk_all = _dot_nt(q16, kb)
        q_c = _dot(q16, ct.astype(BF16))
        q_n = _dot_sel(q * nvec, same_head16, terms=2)
        yield

        i_pre = gates[:, :W]
        g = _sel_dot(tril16, _log_sigmoid(gates[:, W:]))
        yield
        a = i_pre - g
        a_key = jnp.sum(a * diag, axis=0, keepdims=True)
        a_max = prefix_max(a)
        m_row = g + jnp.maximum(m_old, a_max)
        p = qk_all * jnp.exp(jnp.where(causal, g + a_key, -jnp.inf) - m_row)
        p16 = p.astype(BF16)
        num_intra = _dot(p16, vb)
        rowsum = _dot_sel(p, same_head16, terms=2)
        yield

        w_inter = jnp.exp(g + m_old - m_row)
        num = w_inter * q_c + num_intra
        den = w_inter * q_n + rowsum
        hv = num / jnp.maximum(jnp.abs(den), jnp.exp(-m_row))
        ms = _dot_sel(hv * hv, same_head16, terms=2) * (1.0 / M_HEAD_DIM)
        yield
        y = hv * lax.rsqrt(ms + EPS) * ng * _sigmoid(og)
        o_ref[pl.ds(r0, L), :] = y

        g_end = g[L - 1:L, :]
        m_new = g_end + jnp.maximum(m_old, a_max[L - 1:L, :])
        wa = jnp.exp(g_end + a - m_new)
        decay = jnp.exp(g_end + m_old - m_new)
        upd = _dot_tn(k16, (wa * v).astype(BF16))
        ct_ref[sq] = decay * ct + upd * same_head
        n_ref[sq] = decay * nvec + jnp.sum(wa * k, axis=0, keepdims=True)
        m_ref[sq] = m_new
        yield

    def body(c, carry):
        for _ in zip(*[chunk(sq, c) for sq in range(M_SEQS)]):
            pass
        return carry

    lax.fori_loop(0, nc, body, 0)


def _mlstm(proj, cw, cb, gbc, ng, B, S):
    assert M_CHUNK == M_HEAD_DIM
    T = B * S
    rows = M_SEQS * S
    return pl.pallas_call(
        _mlstm_kernel,
        grid=(B // M_SEQS,),
        in_specs=[pl.BlockSpec((rows, 4 * M_WIDTH), lambda b: (b, 0)),
                  pl.BlockSpec((rows, LANES), lambda b: (b, PROJ_SMALL // LANES)),
                  pl.BlockSpec((M_CONV, 2 * M_WIDTH), lambda b: (0, 0)),
                  pl.BlockSpec((1, 2 * M_WIDTH), lambda b: (0, 0)),
                  pl.BlockSpec((1, LANES), lambda b: (0, 0)),
                  pl.BlockSpec((1, M_WIDTH), lambda b: (0, 0))],
        out_specs=pl.BlockSpec((rows, M_WIDTH), lambda b: (b, 0)),
        out_shape=jax.ShapeDtypeStruct((T, M_WIDTH), F32),
        scratch_shapes=[pltpu.VMEM((M_SEQS, M_WIDTH, M_WIDTH), F32),
                        pltpu.VMEM((M_SEQS, 1, M_WIDTH), F32),
                        pltpu.VMEM((M_SEQS, 1, M_WIDTH), F32)],
        compiler_params=_cparams("parallel"),
        name="mlstm",
    )(proj, proj, cw, cb, gbc, ng)


POOL_HALO = 16
POOL_TILE = 256


def _pool_kernel(u_ref, w_ref, sc_ref, o_ref, upad_ref):
    S = u_ref.shape[0]
    upad_ref[0:POOL_HALO, :] = jnp.zeros((POOL_HALO, P_WIDTH), F32)
    upad_ref[POOL_HALO:, :] = u_ref[...]
    grp = lax.broadcasted_iota(jnp.int32, (1, P_WIDTH), 1) // (P_WIDTH // len(P_WINDOWS))
    win_lane = jnp.zeros((1, P_WIDTH), jnp.int32)
    for gi, wn in enumerate(P_WINDOWS):
        win_lane = jnp.where(grp == gi, wn, win_lane)
    w = w_ref[...]
    scale = sc_ref[...]
    rows = POOL_TILE + POOL_HALO

    def body(r, carry):
        r0 = pl.multiple_of(r * POOL_TILE, POOL_TILE)
        a = upad_ref[pl.ds(r0, rows), :]
        sums = []
        cur = a
        span = 1
        for _ in P_WINDOWS:
            cur = cur + pltpu.roll(cur, span, axis=0)
            span *= 2
            sums.append(cur)
        sel = sums[-1]
        for gi in range(len(P_WINDOWS) - 1):
            sel = jnp.where(grp == gi, sums[gi], sel)
        sel = sel[POOL_HALO:, :]
        u = a[POOL_HALO:, :]
        t = r0 + lax.broadcasted_iota(jnp.int32, (POOL_TILE, P_WIDTH), 0)
        cnt = jnp.minimum(t + 1, win_lane).astype(F32)
        pooled = sel / cnt - u
        o_ref[pl.ds(r0, POOL_TILE), :] = _dot(pooled.astype(BF16), w) * scale
        return carry

    lax.fori_loop(0, S // POOL_TILE, body, 0)


def _pool(proj, w_bd, scale, B, S):
    T = B * S
    return pl.pallas_call(
        _pool_kernel,
        grid=(B,),
        in_specs=[pl.BlockSpec((S, P_WIDTH), lambda b: (b, PROJ_UP // P_WIDTH)),
                  pl.BlockSpec((P_WIDTH, P_WIDTH), lambda b: (0, 0)),
                  pl.BlockSpec((1, P_WIDTH), lambda b: (0, 0))],
        out_specs=pl.BlockSpec((S, P_WIDTH), lambda b: (b, 0)),
        out_shape=jax.ShapeDtypeStruct((T, P_WIDTH), F32),
        scratch_shapes=[pltpu.VMEM((S + POOL_HALO, P_WIDTH), F32)],
        compiler_params=_cparams("parallel"),
        name="pool",
    )(proj, w_bd, scale)


def _rope(x, c, s1, s2):
    return x * c + pltpu.roll(x, LANES - A_ROPE // 2, axis=1) * s1 + pltpu.roll(x, A_ROPE // 2, axis=1) * s2


def _mla_prep_kernel(cq_ref, ckv_ref, small_ref, qg_ref, kvg_ref, wq_ref, wqs_ref, wk_ref, wv_ref,
                     vone_ref, cq_t_ref, ck_t_ref, s1_ref, s2_ref, q_ref, k_ref, v_ref):
    def rms(x, g):
        return x * lax.rsqrt(jnp.mean(x * x, axis=-1, keepdims=True) + EPS) * g

    cqn = rms(cq_ref[...], qg_ref[...]).astype(BF16)
    ckvn = rms(ckv_ref[...], kvg_ref[...]).astype(BF16)
    scale = (A_NOPE + A_ROPE) ** -0.5
    qf = _dot(cqn, wq_ref[...]) * scale
    qp = _dot(cqn, wqs_ref[...]) * scale
    kf = _dot(ckvn, wk_ref[...])
    v_ref[...] = (_dot(ckvn, wv_ref[...]) + vone_ref[...]).astype(BF16)
    cqt = cq_t_ref[...]
    s1 = s1_ref[...]
    s2 = s2_ref[...]
    krot = _rope(small_ref[...], ck_t_ref[...], s1, s2)
    sq = s1 + s2
    for h in range(A_HEADS):
        sl = slice(h * LANES, (h + 1) * LANES)
        q_ref[:, sl] = (qf[:, sl] * cqt + qp[:, sl] * sq).astype(BF16)
        k_ref[:, sl] = (kf[:, sl] + krot).astype(BF16)


def _mla_prep(proj, qg, kvg, wq, wqs, wk, wv, vone, cq_t, ck_t, s1_t, s2_t, B, S, ts=512):
    T = B * S
    nst = S // ts
    hw = A_HEADS * LANES
    return pl.pallas_call(
        _mla_prep_kernel,
        grid=(B, nst),
        in_specs=[pl.BlockSpec((ts, A_QRANK), lambda b, s: (b * nst + s, PROJ_CQ // A_QRANK)),
                  pl.BlockSpec((ts, A_KVRANK), lambda b, s: (b * nst + s, PROJ_CKV // A_KVRANK)),
                  pl.BlockSpec((ts, LANES), lambda b, s: (b * nst + s, PROJ_SMALL // LANES)),
                  pl.BlockSpec((1, A_QRANK), lambda b, s: (0, 0)),
                  pl.BlockSpec((1, A_KVRANK), lambda b, s: (0, 0)),
                  pl.BlockSpec((A_QRANK, hw), lambda b, s: (0, 0)),
                  pl.BlockSpec((A_QRANK, hw), lambda b, s: (0, 0)),
                  pl.BlockSpec((A_KVRANK, hw), lambda b, s: (0, 0)),
                  pl.BlockSpec((A_KVRANK, hw), lambda b, s: (0, 0)),
                  pl.BlockSpec((1, hw), lambda b, s: (0, 0)),
                  pl.BlockSpec((ts, LANES), lambda b, s: (s, 0)),
                  pl.BlockSpec((ts, LANES), lambda b, s: (s, 0)),
                  pl.BlockSpec((ts, LANES), lambda b, s: (s, 0)),
                  pl.BlockSpec((ts, LANES), lambda b, s: (s, 0))],
        out_specs=[pl.BlockSpec((ts, hw), lambda b, s: (b * nst + s, 0)),
                   pl.BlockSpec((ts, hw), lambda b, s: (b * nst + s, 0)),
                   pl.BlockSpec((ts, hw), lambda b, s: (b * nst + s, 0))],
        out_shape=[jax.ShapeDtypeStruct((T, hw), BF16),
                   jax.ShapeDtypeStruct((T, hw), BF16),
                   jax.ShapeDtypeStruct((T, hw), BF16)],
        compiler_params=_cparams("parallel", "parallel"),
        name="mla_prep",
    )(proj, proj, proj, qg, kvg, wq, wqs, wk, wv, vone, cq_t, ck_t, s1_t, s2_t)


def _attn_kernel(q_ref, k_ref, v_ref, o_ref, *, tq):
    qi = pl.program_id(2)
    heads = range(2)
    sls = [slice(hh * LANES, (hh + 1) * LANES) for hh in heads]
    qs = [q_ref[:, sl] for sl in sls]

    def update(k0, ncols, state, causal=False):
        s = [_dot_nt(qs[hh], k_ref[pl.ds(k0, ncols), sls[hh]]) for hh in heads]
        if causal:
            ok = (lax.broadcasted_iota(jnp.int32, s[0].shape, 0) >= lax.broadcasted_iota(jnp.int32, s[0].shape, 1))
            s = [jnp.where(ok, s[hh], -jnp.inf) for hh in heads]
        m_new = [jnp.maximum(state[hh][0], jnp.max(s[hh], axis=-1, keepdims=True)) for hh in heads]
        p = [jnp.exp((s[hh] - m_new[hh]).astype(BF16)) for hh in heads]
        pv = [_dot(p[hh], v_ref[pl.ds(k0, ncols), sls[hh]]) for hh in heads]
        acc = [jnp.exp(state[hh][0] - m_new[hh]) * state[hh][1] + pv[hh] for hh in heads]
        return tuple((m_new[hh], acc[hh]) for hh in heads)

    init = tuple((jnp.full((tq, 1), -jnp.inf, F32), jnp.zeros((tq, LANES), F32)) for _ in heads)
    carry = lax.fori_loop(0, qi, lambda kb, st: update(pl.multiple_of(kb * tq, tq), tq, st), init)
    (_, acc0), (_, acc1) = update(pl.multiple_of(qi * tq, tq), tq, carry, causal=True)
    lane = lax.broadcasted_iota(jnp.int32, (tq, LANES), 1)
    acc = jnp.where(lane < A_VDIM, acc0, acc1)
    den = jnp.where(lane < A_VDIM, pltpu.roll(acc0, A_VDIM, axis=1), pltpu.roll(acc1, A_VDIM, axis=1))
    o_ref[...] = (acc / den).astype(o_ref.dtype)


def _attention(q, k, v, B, S, tq=512):
    T = B * S
    nq = S // tq
    return pl.pallas_call(
        functools.partial(_attn_kernel, tq=tq),
        grid=(B, A_HEADS // 2, nq),
        in_specs=[pl.BlockSpec((tq, 2 * LANES), lambda b, p, i: (b * nq + i, p)),
                  pl.BlockSpec((S, 2 * LANES), lambda b, p, i: (b, p)),
                  pl.BlockSpec((S, 2 * LANES), lambda b, p, i: (b, p))],
        out_specs=pl.BlockSpec((tq, LANES), lambda b, p, i: (b * nq + i, p)),
        out_shape=jax.ShapeDtypeStruct((T, A_WIDTH), BF16),
        compiler_params=_cparams("parallel", "parallel", "arbitrary"),
        name="attention",
    )(q, k, v)


def _outproj_kernel(ym_ref, ya_ref, yp_ref, h_ref, w_ref, g_ref, wr_ref, br_ref,
                    hn_ref, xn_ref, lg_ref):
    mix = _dot(ym_ref[...].astype(BF16), w_ref[0:M_WIDTH, :])
    mix = mix + _dot(ya_ref[...], w_ref[M_WIDTH:M_WIDTH + A_WIDTH, :])
    mix = mix + _dot(yp_ref[...].astype(BF16), w_ref[M_WIDTH + A_WIDTH:, :])
    hn = h_ref[...] + mix
    hn_ref[...] = hn
    xn = hn * lax.rsqrt(jnp.mean(hn * hn, axis=-1, keepdims=True) + EPS) * g_ref[...]
    x_hi = xn.astype(BF16)
    x_lo = (xn - x_hi.astype(F32)).astype(BF16)
    wr = wr_ref[...]
    w_hi = wr.astype(BF16)
    w_lo = (wr - w_hi.astype(F32)).astype(BF16)
    lg_ref[...] = _dot_nt(w_hi, x_hi) + _dot_nt(w_hi, x_lo) + _dot_nt(w_lo, x_hi) + br_ref[...]
    xn_ref[...] = _pack_bf16_pairs(xn)


def _outproj(ym, ya, yp, h2, w, g, wr_t, br, tm=512):
    T = h2.shape[0]
    return pl.pallas_call(
        _outproj_kernel,
        grid=(T // tm,),
        in_specs=[pl.BlockSpec((tm, M_WIDTH), lambda i: (i, 0)),
                  pl.BlockSpec((tm, A_WIDTH), lambda i: (i, 0)),
                  pl.BlockSpec((tm, P_WIDTH), lambda i: (i, 0)),
                  pl.BlockSpec((tm, D_MODEL), lambda i: (i, 0)),
                  pl.BlockSpec((D_MODEL, D_MODEL), lambda i: (0, 0)),
                  pl.BlockSpec((1, D_MODEL), lambda i: (0, 0)),
                  pl.BlockSpec((N_EXPERTS, D_MODEL), lambda i: (0, 0)),
                  pl.BlockSpec((N_EXPERTS, 1), lambda i: (0, 0))],
        out_specs=[pl.BlockSpec((tm, D_MODEL), lambda i: (i, 0)),
                   pl.BlockSpec((tm, D_MODEL // 2), lambda i: (i, 0)),
                   pl.BlockSpec((N_EXPERTS, tm), lambda i: (0, i))],
        out_shape=[jax.ShapeDtypeStruct((T, D_MODEL), F32),
                   jax.ShapeDtypeStruct((T, D_MODEL // 2), jnp.uint32),
                   jax.ShapeDtypeStruct((N_EXPERTS, T), F32)],
        compiler_params=_cparams("parallel"),
        name="outproj",
    )(ym, ya, yp, h2, w, g, wr_t, br)


def _router_kernel(lg_ref, tri_ref, e_ref, w_ref, r_ref, cnt_ref, carry_ref):
    tr = lg_ref.shape[1]

    @pl.when(pl.program_id(0) == 0)
    def _():
        carry_ref[...] = jnp.zeros_like(carry_ref)

    x = lg_ref[...]
    eio = lax.broadcasted_iota(jnp.int32, (N_EXPERTS, tr), 0).astype(F32)
    picked = jnp.zeros((N_EXPERTS, tr), F32)
    vals = []
    idxs = []
    for _ in range(TOP_K):
        mx = jnp.max(x, axis=0, keepdims=True)
        idx = jnp.min(jnp.where(x == mx, eio, float(N_EXPERTS)), axis=0, keepdims=True)
        hit = eio == idx
        vals.append(mx)
        idxs.append(idx)
        picked = picked + hit.astype(F32)
        x = jnp.where(hit, -jnp.inf, x)
    exps = [jnp.exp(vv - vals[0]) for vv in vals]
    tot = exps[0] + exps[1] + exps[2] + exps[3]
    before = _dot(picked.astype(BF16), tri_ref[...]) + carry_ref[:, 0:1]
    for kk in range(TOP_K):
        e_ref[kk:kk + 1, :] = idxs[kk].astype(jnp.int32)
        w_ref[kk:kk + 1, :] = exps[kk] / tot
        rk = jnp.sum(jnp.where(eio == idxs[kk], before, 0.0), axis=0, keepdims=True)
        r_ref[kk:kk + 1, :] = rk.astype(jnp.int32)
    carry_ref[...] = carry_ref[...] + jnp.sum(picked, axis=1, keepdims=True)
    cnt_ref[...] = carry_ref[...]


def _router(logits_t, tri, tr=512):
    T = logits_t.shape[1]
    return pl.pallas_call(
        _router_kernel,
        grid=(T // tr,),
        in_specs=[pl.BlockSpec((N_EXPERTS, tr), lambda i: (0, i)),
                  pl.BlockSpec((tr, tr), lambda i: (0, 0))],
        out_specs=[pl.BlockSpec((TOP_K, tr), lambda i: (0, i)),
                   pl.BlockSpec((TOP_K, tr), lambda i: (0, i)),
                   pl.BlockSpec((TOP_K, tr), lambda i: (0, i)),
                   pl.BlockSpec((N_EXPERTS, LANES), lambda i: (0, 0))],
        out_shape=[jax.ShapeDtypeStruct((TOP_K, T), jnp.int32),
                   jax.ShapeDtypeStruct((TOP_K, T), F32),
                   jax.ShapeDtypeStruct((TOP_K, T), jnp.int32),
                   jax.ShapeDtypeStruct((N_EXPERTS, LANES), F32)],
        scratch_shapes=[pltpu.VMEM((N_EXPERTS, LANES), F32)],
        compiler_params=_cparams("arbitrary"),
        name="router",
    )(logits_t, tri)


def _meta_kernel(cnt_ref, e_ref, r_ref, dest_ref, be_ref, bc_ref, bn_ref, nb_ref, *, nb_pad):
    cnt = cnt_ref[...]
    padded = jnp.floor((cnt + (MOE_ROWS - 1)) * (1.0 / MOE_ROWS)) * MOE_ROWS
    ri = lax.broadcasted_iota(jnp.int32, (N_EXPERTS, N_EXPERTS), 0)
    ci = lax.broadcasted_iota(jnp.int32, (N_EXPERTS, N_EXPERTS), 1)
    pad_end = _dot((ri >= ci).astype(F32), padded, precision=HIGHEST)
    pad_start = pad_end - padded
    e = e_ref[...]
    dest = r_ref[...]
    for ex in range(N_EXPERTS):
        ps = pad_start[ex:ex + 1, 0:1].astype(jnp.int32)
        dest = jnp.where(e == ex, dest + ps, dest)
    dest_ref[...] = dest
    blk0 = (lax.broadcasted_iota(jnp.int32, (N_EXPERTS, nb_pad), 1) * MOE_ROWS).astype(F32)
    be = jnp.sum((pad_end[:, 0:1] <= blk0).astype(F32), axis=0, keepdims=True)
    be = jnp.minimum(be, float(N_EXPERTS - 1))
    eio = lax.broadcasted_iota(jnp.int32, (N_EXPERTS, nb_pad), 0).astype(F32)
    seg_end = jnp.sum(jnp.where(eio == be, pad_start[:, 0:1] + cnt[:, 0:1], 0.0), axis=0, keepdims=True)
    bc = jnp.clip(seg_end - blk0[0:1, :], 0.0, float(MOE_ROWS))
    nxt0 = jnp.sum(jnp.where(eio == be, pad_end[:, 0:1], 0.0), axis=0, keepdims=True)
    bn = jnp.sum((pad_end[:, 0:1] <= nxt0).astype(F32), axis=0, keepdims=True)
    bn = jnp.where(nxt0 < pad_end[N_EXPERTS - 1:N_EXPERTS, 0:1], bn, -1.0)
    be_ref[...] = be.astype(jnp.int32)
    bc_ref[...] = bc.astype(jnp.int32)
    bn_ref[...] = bn.astype(jnp.int32)
    nb_ref[...] = (pad_end[N_EXPERTS - 1:N_EXPERTS, :] * (1.0 / MOE_ROWS)).astype(jnp.int32)


def _meta(counts, eidx, rank, nb_pad):
    T = eidx.shape[1]
    return pl.pallas_call(
        functools.partial(_meta_kernel, nb_pad=nb_pad),
        out_shape=[jax.ShapeDtypeStruct((TOP_K, T), jnp.int32),
                   jax.ShapeDtypeStruct((1, nb_pad), jnp.int32),
                   jax.ShapeDtypeStruct((1, nb_pad), jnp.int32),
                   jax.ShapeDtypeStruct((1, nb_pad), jnp.int32),
                   jax.ShapeDtypeStruct((1, LANES), jnp.int32)],
        compiler_params=pltpu.CompilerParams(vmem_limit_bytes=VMEM_LIMIT),
        name="route_meta",
    )(counts, eidx, rank)


FF_CHUNK = 512


def _gmm_kernel(be_ref, bc_ref, bn_ref, nb_ref, x_ref, wgu_hbm, bgu_ref, wdn_hbm, bdn_ref, y_ref,
                wgu_st, wdn_st, wgu16, wdn16, sem, *, e0):
    i = pl.program_id(0)
    nblk = nb_ref[0]
    bm = MOE_ROWS

    def weight_copies(e):
        return (pltpu.make_async_copy(wgu_hbm.at[e0 + e], wgu_st, sem.at[0]),
                pltpu.make_async_copy(wdn_hbm.at[e0 + e], wdn_st, sem.at[1]))

    @pl.when(i == 0)
    def _():
        for cp in weight_copies(be_ref[0]):
            cp.start()

    @pl.when(i >= nblk)
    def _():
        y_ref[...] = jnp.zeros_like(y_ref)

    @pl.when(i < nblk)
    def _():
        e_changed = jnp.logical_or(i == 0, be_ref[i] != be_ref[jnp.maximum(i - 1, 0)])

        @pl.when(e_changed)
        def _():
            for cp in weight_copies(be_ref[i]):
                cp.wait()
            wgu16[...] = wgu_st[...].astype(BF16)
            wdn16[...] = wdn_st[...].astype(BF16)

            @pl.when(bn_ref[i] >= 0)
            def _():
                for cp in weight_copies(bn_ref[i]):
                    cp.start()

        def expert_rows(nrows):
            valid = lax.broadcasted_iota(jnp.int32, (nrows, 1), 0) < bc_ref[i]
            lo, hi = _unpack_bf16_pairs(jnp.where(valid, x_ref[0:nrows, :], jnp.uint32(0)))
            x16 = jnp.concatenate([lo.astype(BF16), hi.astype(BF16)], axis=1)
            acc = jnp.zeros((nrows, D_MODEL), F32) + bdn_ref[0]
            for c in range(D_FF // FF_CHUNK):
                cs = slice(c * FF_CHUNK, (c + 1) * FF_CHUNK)
                us = slice(D_FF + c * FF_CHUNK, D_FF + (c + 1) * FF_CHUNK)
                gate = _dot(x16, wgu16[:, cs]) + bgu_ref[0, :, cs]
                up = _dot(x16, wgu16[:, us]) + bgu_ref[0, :, us]
                gate = jnp.minimum(gate, SWIGLU_LIMIT)
                up = jnp.clip(up, -SWIGLU_LIMIT, SWIGLU_LIMIT)
                act = (up + 1.0) * gate * _sigmoid(SWIGLU_ALPHA * gate)
                acc = acc + _dot(act.astype(BF16), wdn16[cs, :])
            y_ref[0:nrows, :] = _pack_bf16_pairs(acc)

        @pl.when(bc_ref[i] > bm // 2)
        def _():
            expert_rows(bm)

        @pl.when(bc_ref[i] <= bm // 2)
        def _():
            expert_rows(bm // 2)
            y_ref[bm // 2:, :] = jnp.zeros((bm // 2, D_MODEL // 2), jnp.uint32)


def _gmm(blk_e, blk_cnt, blk_next, nblk, x_rows, wgu, bgu, wdn, bdn, nb, layer):
    bm = MOE_ROWS
    e0 = layer * N_EXPERTS
    dp = D_MODEL // 2

    def expert(i, be, nb_ref):
        return (e0 + be[jnp.minimum(i, jnp.maximum(nb_ref[0] - 1, 0))], 0, 0)

    def rows(i, nb_ref):
        return (jnp.minimum(i, jnp.maximum(nb_ref[0] - 1, 0)), 0)

    grid_spec = pltpu.PrefetchScalarGridSpec(
        num_scalar_prefetch=4,
        grid=(nb,),
        in_specs=[pl.BlockSpec((bm, dp), lambda i, be, bc, bn, nbr: rows(i, nbr)),
                  pl.BlockSpec(memory_space=pl.ANY),
                  pl.BlockSpec((1, 1, 2 * D_FF), lambda i, be, bc, bn, nbr: expert(i, be, nbr)),
                  pl.BlockSpec(memory_space=pl.ANY),
                  pl.BlockSpec((1, 1, D_MODEL), lambda i, be, bc, bn, nbr: expert(i, be, nbr))],
        out_specs=pl.BlockSpec((bm, dp), lambda i, be, bc, bn, nbr: (i, 0)),
        scratch_shapes=[pltpu.VMEM((D_MODEL, 2 * D_FF), F32),
                        pltpu.VMEM((D_FF, D_MODEL), F32),
                        pltpu.VMEM((D_MODEL, 2 * D_FF), BF16),
                        pltpu.VMEM((D_FF, D_MODEL), BF16),
                        pltpu.SemaphoreType.DMA((2,))],
    )
    return pl.pallas_call(
        functools.partial(_gmm_kernel, e0=e0),
        grid_spec=grid_spec,
        out_shape=jax.ShapeDtypeStruct((nb * bm, dp), jnp.uint32),
        compiler_params=_cparams("arbitrary"),
        name="expert_gmm",
    )(blk_e, blk_cnt, blk_next, nblk, x_rows, wgu, bgu, wdn, bdn)


def _final_kernel(h_ref, y0, y1, y2, y3, wk_ref, g_ref, o_ref):
    for rows in _row_slices(h_ref.shape[0]):
        x = _moe_combine(h_ref, (y0, y1, y2, y3), wk_ref, rows)
        o_ref[rows, :] = x * lax.rsqrt(jnp.mean(x * x, axis=-1, keepdims=True) + EPS) * g_ref[...]


def _final(h2, y_tok, w_tok, g, tm=512):
    T = h2.shape[0]
    nt = T // tm
    return pl.pallas_call(
        _final_kernel,
        grid=(nt,),
        in_specs=_combine_specs(tm, nt) + [pl.BlockSpec((1, D_MODEL), lambda i: (0, 0))],
        out_specs=pl.BlockSpec((tm, D_MODEL), lambda i: (i, 0)),
        out_shape=jax.ShapeDtypeStruct((T, D_MODEL), F32),
        compiler_params=_cparams("parallel"),
        name="combine_final_norm",
    )(h2, y_tok, y_tok, y_tok, y_tok, w_tok, g)


SC_CORES = 2
SC_SUBCORES = 16
SC_LANES = 16
SC_WORKERS = SC_CORES * SC_SUBCORES
SC_WINDOW = 64


def _sc_mesh():
    return plsc.VectorSubcoreMesh(core_axis_name="c", subcore_axis_name="s")


def _sc_worker():
    return lax.axis_index("s") * SC_CORES + lax.axis_index("c")


def _sc_dispatch(xn, dest_flat, n_rows):
    T = xn.shape[0]
    tpw = T // SC_WORKERS
    nchunk = tpw // SC_WINDOW
    nvec = SC_WINDOW // SC_LANES

    @functools.partial(
        pl.kernel, out_type=jax.ShapeDtypeStruct((n_rows, xn.shape[1]), xn.dtype), mesh=_sc_mesh(),
        scratch_types=[pltpu.VMEM((TOP_K * tpw,), jnp.int32),
                       pltpu.VMEM((SC_WINDOW, xn.shape[1]), xn.dtype),
                       pltpu.VMEM((SC_WINDOW, xn.shape[1]), xn.dtype),
                       pltpu.SemaphoreType.DMA, pltpu.SemaphoreType.DMA, pltpu.SemaphoreType.DMA],
        name="sc_dispatch")
    def run(x_hbm, d_hbm, o_hbm, idx_v, buf0, buf1, sem0, sem1, sem_out):
        base = _sc_worker() * tpw
        for kk in range(TOP_K):
            pltpu.sync_copy(d_hbm.at[pl.ds(kk * T + base, tpw)], idx_v.at[pl.ds(kk * tpw, tpw)])
        bufs = (buf0, buf1)
        sems = (sem0, sem1)

        def load(c, slot):
            return pltpu.make_async_copy(x_hbm.at[pl.ds(base + c * SC_WINDOW, SC_WINDOW)], bufs[slot], sems[slot])

        load(0, 0).start()

        @pl.loop(0, nchunk, step=2)
        def _(c0):
            for slot in range(2):
                c = c0 + slot
                load(c, slot).wait()

                @pl.when(c + 1 < nchunk)
                def _():
                    load(c + 1, 1 - slot).start()

                copies = []
                for kk in range(TOP_K):
                    for q in range(nvec):
                        off = pl.multiple_of(kk * tpw + c * SC_WINDOW + q * SC_LANES, SC_LANES)
                        rows = idx_v[pl.ds(off, SC_LANES)]
                        cp = pltpu.make_async_copy(bufs[slot].at[pl.ds(q * SC_LANES, SC_LANES)],
                                                   o_hbm.at[rows], sem_out)
                        cp.start()
                        copies.append(cp)
                for cp in copies:
                    cp.wait()

    return run(xn, dest_flat)


def _sc_gather(y_rows, dest_flat):
    n = dest_flat.shape[0]
    rpw = n // SC_WORKERS
    nchunk = rpw // SC_WINDOW
    nvec = SC_WINDOW // SC_LANES

    @functools.partial(
        pl.kernel, out_type=jax.ShapeDtypeStruct((n, y_rows.shape[1]), y_rows.dtype), mesh=_sc_mesh(),
        scratch_types=[pltpu.VMEM((rpw,), jnp.int32),
                       pltpu.VMEM((SC_WINDOW, y_rows.shape[1]), y_rows.dtype),
                       pltpu.VMEM((SC_WINDOW, y_rows.shape[1]), y_rows.dtype),
                       pltpu.SemaphoreType.DMA, pltpu.SemaphoreType.DMA, pltpu.SemaphoreType.DMA],
        name="sc_gather")
    def run(y_hbm, d_hbm, o_hbm, idx_v, buf0, buf1, sem0, sem1, sem_in):
        base = _sc_worker() * rpw
        pltpu.sync_copy(d_hbm.at[pl.ds(base, rpw)], idx_v)
        bufs = (buf0, buf1)
        sems = (sem0, sem1)

        def store(c, slot):
            return pltpu.make_async_copy(bufs[slot], o_hbm.at[pl.ds(base + c * SC_WINDOW, SC_WINDOW)], sems[slot])

        @pl.loop(0, nchunk, step=2)
        def _(c0):
            for slot in range(2):
                c = c0 + slot

                @pl.when(c >= 2)
                def _():
                    store(c - 2, slot).wait()

                copies = []
                for q in range(nvec):
                    off = pl.multiple_of(c * SC_WINDOW + q * SC_LANES, SC_LANES)
                    rows = idx_v[pl.ds(off, SC_LANES)]
                    cp = pltpu.make_async_copy(y_hbm.at[rows], bufs[slot].at[pl.ds(q * SC_LANES, SC_LANES)], sem_in)
                    cp.start()
                    copies.append(cp)
                for cp in copies:
                    cp.wait()
                store(c, slot).start()

        store(nchunk - 2, 0).wait()
        store(nchunk - 1, 1).wait()

    return run(y_rows, dest_flat)


def _prep_w_in(w_in):
    o_g = 4 * M_WIDTH
    o_cq = o_g + 2 * M_HEADS
    o_ckv = o_cq + A_QRANK
    o_kr = o_ckv + A_KVRANK
    o_up = o_kr + A_ROPE
    z = lambda n: jnp.zeros(w_in.shape[:-1] + (n,), w_in.dtype)
    small = jnp.concatenate([z(SMALL_KR), w_in[..., o_kr:o_up], w_in[..., o_g:o_cq],
                             z(LANES - SMALL_GATE - 2 * M_HEADS)], axis=-1)
    return jnp.concatenate([w_in[..., 0:o_g], w_in[..., o_cq:o_ckv], w_in[..., o_up:o_up + P_WIDTH],
                            w_in[..., o_ckv:o_kr], small], axis=-1).astype(BF16)


def _rope_tables(seq):
    inv = ROPE_THETA ** (-jnp.arange(0, A_ROPE, 2, dtype=F32) / A_ROPE)
    ang = jnp.arange(seq, dtype=F32)[:, None] * inv[None, :]
    cos, sin = jnp.cos(ang), jnp.sin(ang)
    half = A_ROPE // 2
    zeros = lambda n: jnp.zeros((seq, n), F32)
    ones = lambda n: jnp.ones((seq, n), F32)
    tail = LANES - A_NOPE - A_ROPE
    cq_t = jnp.concatenate([ones(A_NOPE), cos, cos, zeros(tail)], axis=1)
    ck_t = jnp.concatenate([zeros(A_NOPE), cos, cos, zeros(tail)], axis=1)
    s1_t = jnp.concatenate([zeros(A_NOPE), -sin, zeros(half), zeros(tail)], axis=1)
    s2_t = jnp.concatenate([zeros(A_NOPE), zeros(half), sin, zeros(tail)], axis=1)
    return cq_t, ck_t, s1_t, s2_t


def kernel(x, norm1_g, w_in, conv_w, conv_b, gate_b, mlstm_norm_g, q_norm_g, kv_norm_g, w_uq, w_ukv,
           w_pool, pool_scale, w_out, norm2_g, w_router, b_router, w_gate_up, b_gate_up, w_down, b_down,
           final_norm_g):
    B, S, D = x.shape
    depth = w_in.shape[0]
    T = B * S
    nb = (T * TOP_K) // MOE_ROWS + N_EXPERTS
    nb_pad = -(-nb // LANES) * LANES

    w_in_p = _prep_w_in(w_in)
    wq = w_uq.reshape(depth, A_QRANK, A_HEADS, A_NOPE + A_ROPE)
    wq = jnp.pad(wq, ((0, 0), (0, 0), (0, 0), (0, LANES - A_NOPE - A_ROPE)))
    r0, r1, r2 = A_NOPE, A_NOPE + A_ROPE // 2, A_NOPE + A_ROPE
    wqs = jnp.concatenate([jnp.zeros_like(wq[..., :r0]), wq[..., r1:r2], wq[..., r0:r1],
                           jnp.zeros_like(wq[..., r2:])], axis=-1)
    wq = wq.reshape(depth, A_QRANK, A_HEADS * LANES).astype(BF16)
    wqs = wqs.reshape(depth, A_QRANK, A_HEADS * LANES).astype(BF16)
    wkv = w_ukv.reshape(depth, A_KVRANK, A_HEADS, A_NOPE + A_VDIM)
    wk = jnp.pad(wkv[..., :A_NOPE], ((0, 0), (0, 0), (0, 0), (0, LANES - A_NOPE)))
    wk = wk.reshape(depth, A_KVRANK, A_HEADS * LANES).astype(BF16)
    wv_e = jnp.pad(wkv[:, :, 0::2, A_NOPE:], ((0, 0), (0, 0), (0, 0), (0, LANES - A_VDIM)))
    wv_o = jnp.pad(wkv[:, :, 1::2, A_NOPE:], ((0, 0), (0, 0), (0, 0), (LANES - A_VDIM, 0)))
    wv = jnp.stack([wv_e, wv_o], axis=3).reshape(depth, A_KVRANK, A_HEADS * LANES).astype(BF16)
    half = jnp.arange(A_HEADS * LANES) // A_VDIM
    vone = ((half % 4 == 1) | (half % 4 == 2)).astype(F32)[None, :]
    gsz = P_WIDTH // len(P_WINDOWS)
    w_pool_bd = jnp.zeros((depth, P_WIDTH, P_WIDTH), F32)
    for gi in range(len(P_WINDOWS)):
        w_pool_bd = w_pool_bd.at[:, gi * gsz:(gi + 1) * gsz, gi * gsz:(gi + 1) * gsz].set(w_pool[:, gi])
    w_pool_bd = w_pool_bd.astype(BF16)
    w_out16 = w_out.astype(BF16)
    w_router_t = jnp.swapaxes(w_router, 1, 2)
    gate_b_col = jnp.pad(gate_b, ((0, 0), (SMALL_GATE, LANES - SMALL_GATE - 2 * M_HEADS)))
    cq_t, ck_t, s1_t, s2_t = _rope_tables(S)
    tr = 512
    tri = (jnp.arange(tr)[:, None] < jnp.arange(tr)[None, :]).astype(BF16)

    wgu_all = w_gate_up.reshape(depth * N_EXPERTS, D_MODEL, 2 * D_FF)
    bgu_all = b_gate_up.reshape(depth * N_EXPERTS, 1, 2 * D_FF)
    wdn_all = w_down.reshape(depth * N_EXPERTS, D_FF, D_MODEL)
    bdn_all = b_down.reshape(depth * N_EXPERTS, 1, D_MODEL)

    h = x.reshape(T, D)
    moe = None
    for l in range(depth):
        h, proj = _inproj(h, norm1_g[l][None, :], w_in_p[l], moe)
        y_m = _mlstm(proj, conv_w[l], conv_b[l][None, :], gate_b_col[l][None, :],
                     mlstm_norm_g[l][None, :], B, S)
        q16, k16, v16 = _mla_prep(proj, q_norm_g[l][None, :], kv_norm_g[l][None, :], wq[l], wqs[l], wk[l], wv[l],
                                  vone, cq_t, ck_t, s1_t, s2_t, B, S)
        y_a = _attention(q16, k16, v16, B, S)
        y_p = _pool(proj, w_pool_bd[l], pool_scale[l][None, :], B, S)
        h, xn, logits_t = _outproj(y_m, y_a, y_p, h, w_out16[l], norm2_g[l][None, :],
                                   w_router_t[l], b_router[l][:, None])
        eidx, wts, rank, counts = _router(logits_t, tri, tr)
        dest, blk_e, blk_cnt, blk_next, nblk = _meta(counts, eidx, rank, nb_pad)
        dest_flat = dest.reshape(TOP_K * T)
        x_rows = _sc_dispatch(xn, dest_flat, nb * MOE_ROWS)
        y_rows = _gmm(blk_e[0], blk_cnt[0], blk_next[0], nblk[0], x_rows, wgu_all, bgu_all, wdn_all,
                      bdn_all, nb, l)
        moe = (_sc_gather(y_rows, dest_flat), wts.T)
    return _final(h, moe[0], moe[1], final_norm_g[None, :]).reshape(B, S, D)
```

```python
import functools

import jax
import jax.numpy as jnp
import numpy as np
from jax import lax
from jax.experimental import pallas as pl
from jax.experimental.pallas import tpu as pltpu
from jax.experimental.pallas import tpu_sc as plsc

F32 = jnp.float32
BF16 = jnp.bfloat16
HIGHEST = lax.Precision.HIGHEST

D_MODEL = 1024
M_HEADS = 4
M_HEAD_DIM = 64
M_WIDTH = 256
M_CONV = 4
M_CHUNK = 64
A_HEADS = 8
A_NOPE = 64
A_ROPE = 32
A_VDIM = 64
A_QRANK = 256
A_KVRANK = 128
A_WIDTH = 512
ROPE_THETA = 10000.0
P_WINDOWS = (2, 4, 8, 16)
P_WIDTH = 256
N_EXPERTS = 32
TOP_K = 4
D_FF = 1024
SWIGLU_LIMIT = 7.0
SWIGLU_ALPHA = 1.702
EPS = 1e-6

LANES = 128
SUBLANES = 8

PROJ_QKVO = 0
PROJ_CQ = 1024
PROJ_UP = 1280
PROJ_CKV = 1536
PROJ_SMALL = 1664
PROJ_WIDTH = 1792
SMALL_KR = 64
SMALL_GATE = 96

MOE_ROWS = 512
VMEM_LIMIT = 56 * 1024 * 1024


def _cparams(*sem):
    return pltpu.CompilerParams(dimension_semantics=sem, vmem_limit_bytes=VMEM_LIMIT)


def _sigmoid(x):
    return 1.0 / (1.0 + jnp.exp(-x))


def _log_sigmoid(x):
    return jnp.minimum(x, 0.0) - jnp.log(1.0 + jnp.exp(-jnp.abs(x)))


def _dot(a, b, **kw):
    return jnp.dot(a, b, preferred_element_type=F32, **kw)


def _dot_nt(a, b, **kw):
    return lax.dot_general(a, b, (((1,), (1,)), ((), ())), preferred_element_type=F32, **kw)


def _dot_tn(a, b, **kw):
    return lax.dot_general(a, b, (((0,), (0,)), ((), ())), preferred_element_type=F32, **kw)


def _bf16_terms(x, terms=3):
    out = []
    for _ in range(terms):
        piece = x.astype(BF16)
        out.append(piece)
        x = x - piece.astype(F32)
    return out


def _dot_sel(x, sel16, terms=3):
    return sum(_dot(p, sel16) for p in _bf16_terms(x, terms))


def _sel_dot(sel16, x, terms=3):
    return sum(_dot(sel16, p) for p in _bf16_terms(x, terms))


def _pack_bf16_pairs(x):
    n = x.shape[1] // 2
    lo = lax.bitcast_convert_type(x[:, :n].astype(BF16).astype(F32), jnp.uint32)
    hi = lax.bitcast_convert_type(x[:, n:].astype(BF16).astype(F32), jnp.uint32)
    return (lo >> 16) | (hi & jnp.uint32(0xFFFF0000))


def _unpack_bf16_pairs(p):
    lo = lax.bitcast_convert_type(p << 16, F32)
    hi = lax.bitcast_convert_type(p & jnp.uint32(0xFFFF0000), F32)
    return lo, hi


ROW_SUBTILE = 128


def _row_slices(tm):
    return [slice(r, r + ROW_SUBTILE) for r in range(0, tm, ROW_SUBTILE)]


def _moe_combine(h_ref, y_refs, w_ref, rows):
    w = w_ref[rows, :]
    dp = D_MODEL // 2
    acc_lo = h_ref[rows, :dp]
    acc_hi = h_ref[rows, dp:]
    for kk, y_ref in enumerate(y_refs):
        lo, hi = _unpack_bf16_pairs(y_ref[rows, :])
        acc_lo = acc_lo + lo * w[:, kk:kk + 1]
        acc_hi = acc_hi + hi * w[:, kk:kk + 1]
    return jnp.concatenate([acc_lo, acc_hi], axis=1)


def _combine_specs(tm, nt):
    y_specs = [pl.BlockSpec((tm, D_MODEL // 2), functools.partial(lambda i, kk: (kk * nt + i, 0), kk=kk))
               for kk in range(TOP_K)]
    return [pl.BlockSpec((tm, D_MODEL), lambda i: (i, 0))] + y_specs + [pl.BlockSpec((tm, TOP_K), lambda i: (i, 0))]


def _inproj_kernel(*refs, combine):
    if combine:
        h_ref, y0, y1, y2, y3, wk_ref, g_ref, w_ref, hn_ref, o_ref = refs
    else:
        h_ref, g_ref, w_ref, o_ref = refs
    for rows in _row_slices(h_ref.shape[0]):
        if combine:
            x = _moe_combine(h_ref, (y0, y1, y2, y3), wk_ref, rows)
            hn_ref[rows, :] = x
        else:
            x = h_ref[rows, :]
        ms = jnp.mean(x * x, axis=-1, keepdims=True)
        xn = x * lax.rsqrt(ms + EPS) * g_ref[...]
        o_ref[rows, :] = _dot(xn.astype(BF16), w_ref[...])


def _inproj(h2, g, w, moe=None, tm=512):
    T = h2.shape[0]
    nt = T // tm
    w_specs = [pl.BlockSpec((1, D_MODEL), lambda i: (0, 0)),
               pl.BlockSpec((D_MODEL, PROJ_WIDTH), lambda i: (0, 0))]
    proj_spec = pl.BlockSpec((tm, PROJ_WIDTH), lambda i: (i, 0))
    proj_shape = jax.ShapeDtypeStruct((T, PROJ_WIDTH), F32)
    if moe is None:
        return h2, pl.pallas_call(
            functools.partial(_inproj_kernel, combine=False),
            grid=(nt,),
            in_specs=[pl.BlockSpec((tm, D_MODEL), lambda i: (i, 0))] + w_specs,
            out_specs=proj_spec,
            out_shape=proj_shape,
            compiler_params=_cparams("parallel"),
            name="inproj",
        )(h2, g, w)
    y_tok, w_tok = moe
    return pl.pallas_call(
        functools.partial(_inproj_kernel, combine=True),
        grid=(nt,),
        in_specs=_combine_specs(tm, nt) + w_specs,
        out_specs=[pl.BlockSpec((tm, D_MODEL), lambda i: (i, 0)), proj_spec],
        out_shape=[jax.ShapeDtypeStruct((T, D_MODEL), F32), proj_shape],
        compiler_params=_cparams("parallel"),
        name="combine_inproj",
    )(h2, y_tok, y_tok, y_tok, y_tok, w_tok, g, w)


M_SEQS = 2
M_UNROLL = 2


def _mlstm_kernel(proj_ref, small_ref, cw_ref, cb_ref, gbc_ref, ng_ref, o_ref, ct_ref, n_ref, m_ref):
    S = proj_ref.shape[0] // M_SEQS
    L = M_CHUNK
    nc = S // L
    W = M_WIDTH
    ct_ref[...] = jnp.zeros_like(ct_ref)
    n_ref[...] = jnp.zeros_like(n_ref)
    m_ref[...] = jnp.zeros_like(m_ref)

    rh = lax.broadcasted_iota(jnp.int32, (W, W), 0) // M_HEAD_DIM
    chd = lax.broadcasted_iota(jnp.int32, (W, W), 1) // M_HEAD_DIM
    same_head = (rh == chd).astype(F32)
    same_head16 = same_head.astype(BF16)
    tril16 = (lax.broadcasted_iota(jnp.int32, (L, L), 0) >= lax.broadcasted_iota(jnp.int32, (L, L), 1)).astype(BF16)
    row = lax.broadcasted_iota(jnp.int32, (L, W), 0)
    key = lax.broadcasted_iota(jnp.int32, (L, W), 1) % M_HEAD_DIM
    causal = key <= row
    diag = (key == row).astype(F32)
    er = lax.broadcasted_iota(jnp.int32, (LANES, 2 * W), 0)
    ec = lax.broadcasted_iota(jnp.int32, (LANES, 2 * W), 1)
    spread16 = (er == SMALL_GATE + ec // M_HEAD_DIM).astype(BF16)
    cw = cw_ref[...]
    cb = cb_ref[...]
    gbc = gbc_ref[...]
    ng = ng_ref[...]

    halo_row = lax.broadcasted_iota(jnp.int32, (L + SUBLANES, 1), 0) >= SUBLANES

    def prefix_max(x):
        s = 1
        while s < L:
            x = jnp.maximum(x, jnp.where(row >= s, pltpu.roll(x, s, axis=0), -jnp.inf))
            s *= 2
        return x

    def chunk(sq, c):
        r0 = pl.multiple_of(sq * S + c * L, L)
        w0 = pl.multiple_of(sq * S + jnp.maximum(c * L - SUBLANES, 0), SUBLANES)
        win = proj_ref[pl.ds(w0, L + SUBLANES), 0:2 * W]
        first = jnp.where(halo_row, pltpu.roll(win, SUBLANES, axis=0), 0.0)
        win = jnp.where(c == 0, first, win)
        acc = jnp.zeros((L, 2 * W), F32) + cb
        for j in range(M_CONV):
            s = M_CONV - 1 - j
            xs = win if s == 0 else pltpu.roll(win, s, axis=0)
            acc = acc + xs[SUBLANES:, :] * cw[j:j + 1, :]
        qk = acc * _sigmoid(acc)
        q = qk[:, 0:W]
        k = qk[:, W:2 * W] * (M_HEAD_DIM ** -0.5)
        v = proj_ref[pl.ds(r0, L), 2 * W:3 * W]
        og = proj_ref[pl.ds(r0, L), 3 * W:4 * W]
        q16 = q.astype(BF16)
        k16 = k.astype(BF16)
        v16 = v.astype(BF16)
        ct = ct_ref[sq]
        nvec = n_ref[sq]
        m_old = m_ref[sq]
        gates = _dot_sel(small_ref[pl.ds(r0, L), :] + gbc, spread16)
        kb = jnp.concatenate([k16] * M_HEADS, axis=0) * same_head16
        vb = jnp.concatenate([v16] * M_HEADS, axis=0) * same_head16
        qk_all = _dot_nt(q16, kb)
        q_c = _dot(q16, ct.astype(BF16))
        q_n = _dot_sel(q * nvec, same_head16, terms=2)
        yield

        i_pre = gates[:, :W]
        g = _sel_dot(tril16, _log_sigmoid(gates[:, W:]))
        yield
        a = i_pre - g
        a_key = jnp.sum(a * diag, axis=0, keepdims=True)
        a_max = prefix_max(a)
        m_row = g + jnp.maximum(m_old, a_max)
        p = qk_all * jnp.exp(jnp.where(causal, g + a_key, -jnp.inf) - m_row)
        p16 = p.astype(BF16)
        num_intra = _dot(p16, vb)
        rowsum = _dot_sel(p, same_head16, terms=2)
        yield

        w_inter = jnp.exp(g + m_old - m_row)
        num = w_inter * q_c + num_intra
        den = w_inter * q_n + rowsum
        hv = num / jnp.maximum(jnp.abs(den), jnp.exp(-m_row))
        ms = _dot_sel(hv * hv, same_head16, terms=2) * (1.0 / M_HEAD_DIM)
        yield
        y = hv * lax.rsqrt(ms + EPS) * ng * _sigmoid(og)
        o_ref[pl.ds(r0, L), :] = y

        g_end = g[L - 1:L, :]
        m_new = g_end + jnp.maximum(m_old, a_max[L - 1:L, :])
        wa = jnp.exp(g_end + a - m_new)
        decay = jnp.exp(g_end + m_old - m_new)
        upd = _dot_tn(k16, (wa * v).astype(BF16))
        ct_ref[sq] = decay * ct + upd * same_head
        n_ref[sq] = decay * nvec + jnp.sum(wa * k, axis=0, keepdims=True)
        m_ref[sq] = m_new
        yield

    def body(cc, carry):
        for u in range(M_UNROLL):
            for _ in zip(*[chunk(sq, cc * M_UNROLL + u) for sq in range(M_SEQS)]):
                pass
        return carry

    lax.fori_loop(0, nc // M_UNROLL, body, 0)


def _mlstm(proj, cw, cb, gbc, ng, B, S):
    assert M_CHUNK == M_HEAD_DIM
    T = B * S
    rows = M_SEQS * S
    return pl.pallas_call(
        _mlstm_kernel,
        grid=(B // M_SEQS,),
        in_specs=[pl.BlockSpec((rows, 4 * M_WIDTH), lambda b: (b, 0)),
                  pl.BlockSpec((rows, LANES), lambda b: (b, PROJ_SMALL // LANES)),
                  pl.BlockSpec((M_CONV, 2 * M_WIDTH), lambda b: (0, 0)),
                  pl.BlockSpec((1, 2 * M_WIDTH), lambda b: (0, 0)),
                  pl.BlockSpec((1, LANES), lambda b: (0, 0)),
                  pl.BlockSpec((1, M_WIDTH), lambda b: (0, 0))],
        out_specs=pl.BlockSpec((rows, M_WIDTH), lambda b: (b, 0)),
        out_shape=jax.ShapeDtypeStruct((T, M_WIDTH), F32),
        scratch_shapes=[pltpu.VMEM((M_SEQS, M_WIDTH, M_WIDTH), F32),
                        pltpu.VMEM((M_SEQS, 1, M_WIDTH), F32),
                        pltpu.VMEM((M_SEQS, 1, M_WIDTH), F32)],
        compiler_params=_cparams("parallel"),
        name="mlstm",
    )(proj, proj, cw, cb, gbc, ng)


POOL_HALO = 16
POOL_TILE = 256


def _pool_kernel(u_ref, w_ref, sc_ref, o_ref, upad_ref):
    S = u_ref.shape[0]
    upad_ref[0:POOL_HALO, :] = jnp.zeros((POOL_HALO, P_WIDTH), F32)
    upad_ref[POOL_HALO:, :] = u_ref[...]
    grp = lax.broadcasted_iota(jnp.int32, (1, P_WIDTH), 1) // (P_WIDTH // len(P_WINDOWS))
    win_lane = jnp.zeros((1, P_WIDTH), jnp.int32)
    for gi, wn in enumerate(P_WINDOWS):
        win_lane = jnp.where(grp == gi, wn, win_lane)
    w = w_ref[...]
    scale = sc_ref[...]
    rows = POOL_TILE + POOL_HALO

    def body(r, carry):
        r0 = pl.multiple_of(r * POOL_TILE, POOL_TILE)
        a = upad_ref[pl.ds(r0, rows), :]
        sums = []
        cur = a
        span = 1
        for _ in P_WINDOWS:
            cur = cur + pltpu.roll(cur, span, axis=0)
            span *= 2
            sums.append(cur)
        sel = sums[-1]
        for gi in range(len(P_WINDOWS) - 1):
            sel = jnp.where(grp == gi, sums[gi], sel)
        sel = sel[POOL_HALO:, :]
        u = a[POOL_HALO:, :]
        t = r0 + lax.broadcasted_iota(jnp.int32, (POOL_TILE, P_WIDTH), 0)
        cnt = jnp.minimum(t + 1, win_lane).astype(F32)
        pooled = sel / cnt - u
        o_ref[pl.ds(r0, POOL_TILE), :] = _dot(pooled.astype(BF16), w) * scale
        return carry

    lax.fori_loop(0, S // POOL_TILE, body, 0)


def _pool(proj, w_bd, scale, B, S):
    T = B * S
    return pl.pallas_call(
        _pool_kernel,
        grid=(B,),
        in_specs=[pl.BlockSpec((S, P_WIDTH), lambda b: (b, PROJ_UP // P_WIDTH)),
                  pl.BlockSpec((P_WIDTH, P_WIDTH), lambda b: (0, 0)),
                  pl.BlockSpec((1, P_WIDTH), lambda b: (0, 0))],
        out_specs=pl.BlockSpec((S, P_WIDTH), lambda b: (b, 0)),
        out_shape=jax.ShapeDtypeStruct((T, P_WIDTH), F32),
        scratch_shapes=[pltpu.VMEM((S + POOL_HALO, P_WIDTH), F32)],
        compiler_params=_cparams("parallel"),
        name="pool",
    )(proj, w_bd, scale)


def _rope(x, c, s1, s2):
    return x * c + pltpu.roll(x, LANES - A_ROPE // 2, axis=1) * s1 + pltpu.roll(x, A_ROPE // 2, axis=1) * s2


def _mla_prep_kernel(cq_ref, ckv_ref, small_ref, qg_ref, kvg_ref, wq_ref, wqs_ref, wk_ref, wv_ref,
                     vone_ref, cq_t_ref, ck_t_ref, s1_ref, s2_ref, q_ref, k_ref, v_ref):
    def rms(x, g):
        return x * lax.rsqrt(jnp.mean(x * x, axis=-1, keepdims=True) + EPS) * g

    cqn = rms(cq_ref[...], qg_ref[...]).astype(BF16)
    ckvn = rms(ckv_ref[...], kvg_ref[...]).astype(BF16)
    scale = (A_NOPE + A_ROPE) ** -0.5
    qf = _dot(cqn, wq_ref[...]) * scale
    qp = _dot(cqn, wqs_ref[...]) * scale
    kf = _dot(ckvn, wk_ref[...])
    v_ref[...] = (_dot(ckvn, wv_ref[...]) + vone_ref[...]).astype(BF16)
    cqt = cq_t_ref[...]
    s1 = s1_ref[...]
    s2 = s2_ref[...]
    krot = _rope(small_ref[...], ck_t_ref[...], s1, s2)
    sq = s1 + s2
    for h in range(A_HEADS):
        sl = slice(h * LANES, (h + 1) * LANES)
        q_ref[:, sl] = (qf[:, sl] * cqt + qp[:, sl] * sq).astype(BF16)
        k_ref[:, sl] = (kf[:, sl] + krot).astype(BF16)


def _mla_prep(proj, qg, kvg, wq, wqs, wk, wv, vone, cq_t, ck_t, s1_t, s2_t, B, S, ts=512):
    T = B * S
    nst = S // ts
    hw = A_HEADS * LANES
    return pl.pallas_call(
        _mla_prep_kernel,
        grid=(B, nst),
        in_specs=[pl.BlockSpec((ts, A_QRANK), lambda b, s: (b * nst + s, PROJ_CQ // A_QRANK)),
                  pl.BlockSpec((ts, A_KVRANK), lambda b, s: (b * nst + s, PROJ_CKV // A_KVRANK)),
                  pl.BlockSpec((ts, LANES), lambda b, s: (b * nst + s, PROJ_SMALL // LANES)),
                  pl.BlockSpec((1, A_QRANK), lambda b, s: (0, 0)),
                  pl.BlockSpec((1, A_KVRANK), lambda b, s: (0, 0)),
                  pl.BlockSpec((A_QRANK, hw), lambda b, s: (0, 0)),
                  pl.BlockSpec((A_QRANK, hw), lambda b, s: (0, 0)),
                  pl.BlockSpec((A_KVRANK, hw), lambda b, s: (0, 0)),
                  pl.BlockSpec((A_KVRANK, hw), lambda b, s: (0, 0)),
                  pl.BlockSpec((1, hw), lambda b, s: (0, 0)),
                  pl.BlockSpec((ts, LANES), lambda b, s: (s, 0)),
                  pl.BlockSpec((ts, LANES), lambda b, s: (s, 0)),
                  pl.BlockSpec((ts, LANES), lambda b, s: (s, 0)),
                  pl.BlockSpec((ts, LANES), lambda b, s: (s, 0))],
        out_specs=[pl.BlockSpec((ts, hw), lambda b, s: (b * nst + s, 0)),
                   pl.BlockSpec((ts, hw), lambda b, s: (b * nst + s, 0)),
                   pl.BlockSpec((ts, hw), lambda b, s: (b * nst + s, 0))],
        out_shape=[jax.ShapeDtypeStruct((T, hw), BF16),
                   jax.ShapeDtypeStruct((T, hw), BF16),
                   jax.ShapeDtypeStruct((T, hw), BF16)],
        compiler_params=_cparams("parallel", "parallel"),
        name="mla_prep",
    )(proj, proj, proj, qg, kvg, wq, wqs, wk, wv, vone, cq_t, ck_t, s1_t, s2_t)


def _attn_kernel(q_ref, k_ref, v_ref, o_ref, *, tq):
    heads = range(2)
    sls = [slice(hh * LANES, (hh + 1) * LANES) for hh in heads]
    lane = lax.broadcasted_iota(jnp.int32, (tq, LANES), 1)
    below_diag = lax.broadcasted_iota(jnp.int32, (tq, tq), 0) >= lax.broadcasted_iota(jnp.int32, (tq, tq), 1)

    def update(qs, k0, state, causal):
        keys = slice(k0, k0 + tq)
        s = [_dot_nt(qs[hh], k_ref[keys, sls[hh]]) for hh in heads]
        if causal:
            s = [jnp.where(below_diag, s[hh], -jnp.inf) for hh in heads]
        m_new = [jnp.maximum(state[hh][0], jnp.max(s[hh], axis=-1, keepdims=True)) for hh in heads]
        p = [jnp.exp((s[hh] - m_new[hh]).astype(BF16)) for hh in heads]
        pv = [_dot(p[hh], v_ref[keys, sls[hh]]) for hh in heads]
        acc = [jnp.exp(state[hh][0] - m_new[hh]) * state[hh][1] + pv[hh] for hh in heads]
        return tuple((m_new[hh], acc[hh]) for hh in heads)

    for qi in range(q_ref.shape[0] // tq):
        rows = slice(qi * tq, (qi + 1) * tq)
        qs = [q_ref[rows, sl] for sl in sls]
        state = tuple((jnp.full((tq, 1), -jnp.inf, F32), jnp.zeros((tq, LANES), F32)) for _ in heads)
        for kb in range(qi + 1):
            state = update(qs, kb * tq, state, causal=(kb == qi))
        (_, acc0), (_, acc1) = state
        acc = jnp.where(lane < A_VDIM, acc0, acc1)
        den = jnp.where(lane < A_VDIM, pltpu.roll(acc0, A_VDIM, axis=1), pltpu.roll(acc1, A_VDIM, axis=1))
        o_ref[rows, :] = (acc / den).astype(o_ref.dtype)


def _attention(q, k, v, B, S, tq=512):
    T = B * S
    return pl.pallas_call(
        functools.partial(_attn_kernel, tq=tq),
        grid=(B, A_HEADS // 2),
        in_specs=[pl.BlockSpec((S, 2 * LANES), lambda b, p: (b, p)),
                  pl.BlockSpec((S, 2 * LANES), lambda b, p: (b, p)),
                  pl.BlockSpec((S, 2 * LANES), lambda b, p: (b, p))],
        out_specs=pl.BlockSpec((S, LANES), lambda b, p: (b, p)),
        out_shape=jax.ShapeDtypeStruct((T, A_WIDTH), BF16),
        compiler_params=_cparams("parallel", "parallel"),
        name="attention",
    )(q, k, v)


def _outproj_kernel(ym_ref, ya_ref, yp_ref, h_ref, w_ref, g_ref, wr_ref, br_ref,
                    hn_ref, xn_ref, lg_ref):
    mix = _dot(ym_ref[...].astype(BF16), w_ref[0:M_WIDTH, :])
    mix = mix + _dot(ya_ref[...], w_ref[M_WIDTH:M_WIDTH + A_WIDTH, :])
    mix = mix + _dot(yp_ref[...].astype(BF16), w_ref[M_WIDTH + A_WIDTH:, :])
    hn = h_ref[...] + mix
    hn_ref[...] = hn
    xn = hn * lax.rsqrt(jnp.mean(hn * hn, axis=-1, keepdims=True) + EPS) * g_ref[...]
    x_hi = xn.astype(BF16)
    x_lo = (xn - x_hi.astype(F32)).astype(BF16)
    wr = wr_ref[...]
    w_hi = wr.astype(BF16)
    w_lo = (wr - w_hi.astype(F32)).astype(BF16)
    lg_ref[...] = _dot_nt(w_hi, x_hi) + _dot_nt(w_hi, x_lo) + _dot_nt(w_lo, x_hi) + br_ref[...]
    xn_ref[...] = _pack_bf16_pairs(xn)


def _outproj(ym, ya, yp, h2, w, g, wr_t, br, tm=512):
    T = h2.shape[0]
    return pl.pallas_call(
        _outproj_kernel,
        grid=(T // tm,),
        in_specs=[pl.BlockSpec((tm, M_WIDTH), lambda i: (i, 0)),
                  pl.BlockSpec((tm, A_WIDTH), lambda i: (i, 0)),
                  pl.BlockSpec((tm, P_WIDTH), lambda i: (i, 0)),
                  pl.BlockSpec((tm, D_MODEL), lambda i: (i, 0)),
                  pl.BlockSpec((D_MODEL, D_MODEL), lambda i: (0, 0)),
                  pl.BlockSpec((1, D_MODEL), lambda i: (0, 0)),
                  pl.BlockSpec((N_EXPERTS, D_MODEL), lambda i: (0, 0)),
                  pl.BlockSpec((N_EXPERTS, 1), lambda i: (0, 0))],
        out_specs=[pl.BlockSpec((tm, D_MODEL), lambda i: (i, 0)),
                   pl.BlockSpec((tm, D_MODEL // 2), lambda i: (i, 0)),
                   pl.BlockSpec((N_EXPERTS, tm), lambda i: (0, i))],
        out_shape=[jax.ShapeDtypeStruct((T, D_MODEL), F32),
                   jax.ShapeDtypeStruct((T, D_MODEL // 2), jnp.uint32),
                   jax.ShapeDtypeStruct((N_EXPERTS, T), F32)],
        compiler_params=_cparams("parallel"),
        name="outproj",
    )(ym, ya, yp, h2, w, g, wr_t, br)


def _router_kernel(lg_ref, tri_ref, e_ref, w_ref, r_ref, cnt_ref, carry_ref):
    tr = lg_ref.shape[1]

    @pl.when(pl.program_id(0) == 0)
    def _():
        carry_ref[...] = jnp.zeros_like(carry_ref)

    x = lg_ref[...]
    eio = lax.broadcasted_iota(jnp.int32, (N_EXPERTS, tr), 0).astype(F32)
    picked = jnp.zeros((N_EXPERTS, tr), F32)
    vals = []
    idxs = []
    for _ in range(TOP_K):
        mx = jnp.max(x, axis=0, keepdims=True)
        idx = jnp.min(jnp.where(x == mx, eio, float(N_EXPERTS)), axis=0, keepdims=True)
        hit = eio == idx
        vals.append(mx)
        idxs.append(idx)
        picked = picked + hit.astype(F32)
        x = jnp.where(hit, -jnp.inf, x)
    exps = [jnp.exp(vv - vals[0]) for vv in vals]
    tot = exps[0] + exps[1] + exps[2] + exps[3]
    before = _dot(picked.astype(BF16), tri_ref[...]) + carry_ref[:, 0:1]
    for kk in range(TOP_K):
        e_ref[kk:kk + 1, :] = idxs[kk].astype(jnp.int32)
        w_ref[kk:kk + 1, :] = exps[kk] / tot
        rk = jnp.sum(jnp.where(eio == idxs[kk], before, 0.0), axis=0, keepdims=True)
        r_ref[kk:kk + 1, :] = rk.astype(jnp.int32)
    carry_ref[...] = carry_ref[...] + jnp.sum(picked, axis=1, keepdims=True)
    cnt_ref[...] = carry_ref[...]


def _router(logits_t, tri, tr=512):
    T = logits_t.shape[1]
    return pl.pallas_call(
        _router_kernel,
        grid=(T // tr,),
        in_specs=[pl.BlockSpec((N_EXPERTS, tr), lambda i: (0, i)),
                  pl.BlockSpec((tr, tr), lambda i: (0, 0))],
        out_specs=[pl.BlockSpec((TOP_K, tr), lambda i: (0, i)),
                   pl.BlockSpec((TOP_K, tr), lambda i: (0, i)),
                   pl.BlockSpec((TOP_K, tr), lambda i: (0, i)),
                   pl.BlockSpec((N_EXPERTS, LANES), lambda i: (0, 0))],
        out_shape=[jax.ShapeDtypeStruct((TOP_K, T), jnp.int32),
                   jax.ShapeDtypeStruct((TOP_K, T), F32),
                   jax.ShapeDtypeStruct((TOP_K, T), jnp.int32),
                   jax.ShapeDtypeStruct((N_EXPERTS, LANES), F32)],
        scratch_shapes=[pltpu.VMEM((N_EXPERTS, LANES), F32)],
        compiler_params=_cparams("arbitrary"),
        name="router",
    )(logits_t, tri)


def _meta_kernel(cnt_ref, e_ref, r_ref, dest_ref, be_ref, bc_ref, bn_ref, nb_ref, *, nb_pad):
    cnt = cnt_ref[...]
    padded = jnp.floor((cnt + (MOE_ROWS - 1)) * (1.0 / MOE_ROWS)) * MOE_ROWS
    ri = lax.broadcasted_iota(jnp.int32, (N_EXPERTS, N_EXPERTS), 0)
    ci = lax.broadcasted_iota(jnp.int32, (N_EXPERTS, N_EXPERTS), 1)
    pad_end = _dot((ri >= ci).astype(F32), padded, precision=HIGHEST)
    pad_start = pad_end - padded
    e = e_ref[...]
    dest = r_ref[...]
    for ex in range(N_EXPERTS):
        ps = pad_start[ex:ex + 1, 0:1].astype(jnp.int32)
        dest = jnp.where(e == ex, dest + ps, dest)
    dest_ref[...] = dest
    blk0 = (lax.broadcasted_iota(jnp.int32, (N_EXPERTS, nb_pad), 1) * MOE_ROWS).astype(F32)
    be = jnp.sum((pad_end[:, 0:1] <= blk0).astype(F32), axis=0, keepdims=True)
    be = jnp.minimum(be, float(N_EXPERTS - 1))
    eio = lax.broadcasted_iota(jnp.int32, (N_EXPERTS, nb_pad), 0).astype(F32)
    seg_end = jnp.sum(jnp.where(eio == be, pad_start[:, 0:1] + cnt[:, 0:1], 0.0), axis=0, keepdims=True)
    bc = jnp.clip(seg_end - blk0[0:1, :], 0.0, float(MOE_ROWS))
    nxt0 = jnp.sum(jnp.where(eio == be, pad_end[:, 0:1], 0.0), axis=0, keepdims=True)
    bn = jnp.sum((pad_end[:, 0:1] <= nxt0).astype(F32), axis=0, keepdims=True)
    bn = jnp.where(nxt0 < pad_end[N_EXPERTS - 1:N_EXPERTS, 0:1], bn, -1.0)
    be_ref[...] = be.astype(jnp.int32)
    bc_ref[...] = bc.astype(jnp.int32)
    bn_ref[...] = bn.astype(jnp.int32)
    nb_ref[...] = (pad_end[N_EXPERTS - 1:N_EXPERTS, :] * (1.0 / MOE_ROWS)).astype(jnp.int32)


def _meta(counts, eidx, rank, nb_pad):
    T = eidx.shape[1]
    return pl.pallas_call(
        functools.partial(_meta_kernel, nb_pad=nb_pad),
        out_shape=[jax.ShapeDtypeStruct((TOP_K, T), jnp.int32),
                   jax.ShapeDtypeStruct((1, nb_pad), jnp.int32),
                   jax.ShapeDtypeStruct((1, nb_pad), jnp.int32),
                   jax.ShapeDtypeStruct((1, nb_pad), jnp.int32),
                   jax.ShapeDtypeStruct((1, LANES), jnp.int32)],
        compiler_params=pltpu.CompilerParams(vmem_limit_bytes=VMEM_LIMIT),
        name="route_meta",
    )(counts, eidx, rank)


FF_CHUNK = 512


def _gmm_kernel(be_ref, bc_ref, bn_ref, nb_ref, x_ref, wgu_hbm, bgu_ref, wdn_hbm, bdn_ref, y_ref,
                wgu_st, wdn_st, wgu16, wdn16, sem, *, e0):
    i = pl.program_id(0)
    nblk = nb_ref[0]
    bm = MOE_ROWS

    def weight_copies(e):
        return (pltpu.make_async_copy(wgu_hbm.at[e0 + e], wgu_st, sem.at[0]),
                pltpu.make_async_copy(wdn_hbm.at[e0 + e], wdn_st, sem.at[1]))

    @pl.when(i == 0)
    def _():
        for cp in weight_copies(be_ref[0]):
            cp.start()

    @pl.when(i >= nblk)
    def _():
        y_ref[...] = jnp.zeros_like(y_ref)

    @pl.when(i < nblk)
    def _():
        e_changed = jnp.logical_or(i == 0, be_ref[i] != be_ref[jnp.maximum(i - 1, 0)])

        @pl.when(e_changed)
        def _():
            for cp in weight_copies(be_ref[i]):
                cp.wait()
            wgu16[...] = wgu_st[...].astype(BF16)
            wdn16[...] = wdn_st[...].astype(BF16)

            @pl.when(bn_ref[i] >= 0)
            def _():
                for cp in weight_copies(bn_ref[i]):
                    cp.start()

        def expert_rows(nrows):
            valid = lax.broadcasted_iota(jnp.int32, (nrows, 1), 0) < bc_ref[i]
            lo, hi = _unpack_bf16_pairs(jnp.where(valid, x_ref[0:nrows, :], jnp.uint32(0)))
            x16 = jnp.concatenate([lo.astype(BF16), hi.astype(BF16)], axis=1)
            acc = jnp.zeros((nrows, D_MODEL), F32) + bdn_ref[0]
            for c in range(D_FF // FF_CHUNK):
                cs = slice(c * FF_CHUNK, (c + 1) * FF_CHUNK)
                us = slice(D_FF + c * FF_CHUNK, D_FF + (c + 1) * FF_CHUNK)
                gate = _dot(x16, wgu16[:, cs]) + bgu_ref[0, :, cs]
                up = _dot(x16, wgu16[:, us]) + bgu_ref[0, :, us]
                gate = jnp.minimum(gate, SWIGLU_LIMIT)
                up = jnp.clip(up, -SWIGLU_LIMIT, SWIGLU_LIMIT)
                act = (up + 1.0) * gate * _sigmoid(SWIGLU_ALPHA * gate)
                acc = acc + _dot(act.astype(BF16), wdn16[cs, :])
            y_ref[0:nrows, :] = _pack_bf16_pairs(acc)

        @pl.when(bc_ref[i] > bm // 2)
        def _():
            expert_rows(bm)

        @pl.when(bc_ref[i] <= bm // 2)
        def _():
            expert_rows(bm // 2)
            y_ref[bm // 2:, :] = jnp.zeros((bm // 2, D_MODEL // 2), jnp.uint32)


def _gmm(blk_e, blk_cnt, blk_next, nblk, x_rows, wgu, bgu, wdn, bdn, nb, layer):
    bm = MOE_ROWS
    e0 = layer * N_EXPERTS
    dp = D_MODEL // 2

    def expert(i, be, nb_ref):
        return (e0 + be[jnp.minimum(i, jnp.maximum(nb_ref[0] - 1, 0))], 0, 0)

    def rows(i, nb_ref):
        return (jnp.minimum(i, jnp.maximum(nb_ref[0] - 1, 0)), 0)

    grid_spec = pltpu.PrefetchScalarGridSpec(
        num_scalar_prefetch=4,
        grid=(nb,),
        in_specs=[pl.BlockSpec((bm, dp), lambda i, be, bc, bn, nbr: rows(i, nbr)),
                  pl.BlockSpec(memory_space=pl.ANY),
                  pl.BlockSpec((1, 1, 2 * D_FF), lambda i, be, bc, bn, nbr: expert(i, be, nbr)),
                  pl.BlockSpec(memory_space=pl.ANY),
                  pl.BlockSpec((1, 1, D_MODEL), lambda i, be, bc, bn, nbr: expert(i, be, nbr))],
        out_specs=pl.BlockSpec((bm, dp), lambda i, be, bc, bn, nbr: (i, 0)),
        scratch_shapes=[pltpu.VMEM((D_MODEL, 2 * D_FF), F32),
                        pltpu.VMEM((D_FF, D_MODEL), F32),
                        pltpu.VMEM((D_MODEL, 2 * D_FF), BF16),
                        pltpu.VMEM((D_FF, D_MODEL), BF16),
                        pltpu.SemaphoreType.DMA((2,))],
    )
    return pl.pallas_call(
        functools.partial(_gmm_kernel, e0=e0),
        grid_spec=grid_spec,
        out_shape=jax.ShapeDtypeStruct((nb * bm, dp), jnp.uint32),
        compiler_params=_cparams("arbitrary"),
        name="expert_gmm",
    )(blk_e, blk_cnt, blk_next, nblk, x_rows, wgu, bgu, wdn, bdn)


def _final_kernel(h_ref, y0, y1, y2, y3, wk_ref, g_ref, o_ref):
    for rows in _row_slices(h_ref.shape[0]):
        x = _moe_combine(h_ref, (y0, y1, y2, y3), wk_ref, rows)
        o_ref[rows, :] = x * lax.rsqrt(jnp.mean(x * x, axis=-1, keepdims=True) + EPS) * g_ref[...]


def _final(h2, y_tok, w_tok, g, tm=512):
    T = h2.shape[0]
    nt = T // tm
    return pl.pallas_call(
        _final_kernel,
        grid=(nt,),
        in_specs=_combine_specs(tm, nt) + [pl.BlockSpec((1, D_MODEL), lambda i: (0, 0))],
        out_specs=pl.BlockSpec((tm, D_MODEL), lambda i: (i, 0)),
        out_shape=jax.ShapeDtypeStruct((T, D_MODEL), F32),
        compiler_params=_cparams("parallel"),
        name="combine_final_norm",
    )(h2, y_tok, y_tok, y_tok, y_tok, w_tok, g)


SC_CORES = 2
SC_SUBCORES = 16
SC_LANES = 16
SC_WORKERS = SC_CORES * SC_SUBCORES
SC_WINDOW = 64


def _sc_mesh():
    return plsc.VectorSubcoreMesh(core_axis_name="c", subcore_axis_name="s")


def _sc_worker():
    return lax.axis_index("s") * SC_CORES + lax.axis_index("c")


def _sc_dispatch(xn, dest_flat, n_rows):
    T = xn.shape[0]
    tpw = T // SC_WORKERS
    nchunk = tpw // SC_WINDOW
    nvec = SC_WINDOW // SC_LANES

    @functools.partial(
        pl.kernel, out_type=jax.ShapeDtypeStruct((n_rows, xn.shape[1]), xn.dtype), mesh=_sc_mesh(),
        scratch_types=[pltpu.VMEM((TOP_K * tpw,), jnp.int32),
                       pltpu.VMEM((SC_WINDOW, xn.shape[1]), xn.dtype),
                       pltpu.VMEM((SC_WINDOW, xn.shape[1]), xn.dtype),
                       pltpu.SemaphoreType.DMA, pltpu.SemaphoreType.DMA, pltpu.SemaphoreType.DMA],
        name="sc_dispatch")
    def run(x_hbm, d_hbm, o_hbm, idx_v, buf0, buf1, sem0, sem1, sem_out):
        base = _sc_worker() * tpw
        for kk in range(TOP_K):
            pltpu.sync_copy(d_hbm.at[pl.ds(kk * T + base, tpw)], idx_v.at[pl.ds(kk * tpw, tpw)])
        bufs = (buf0, buf1)
        sems = (sem0, sem1)

        def load(c, slot):
            return pltpu.make_async_copy(x_hbm.at[pl.ds(base + c * SC_WINDOW, SC_WINDOW)], bufs[slot], sems[slot])

        load(0, 0).start()

        @pl.loop(0, nchunk, step=2)
        def _(c0):
            for slot in range(2):
                c = c0 + slot
                load(c, slot).wait()

                @pl.when(c + 1 < nchunk)
                def _():
                    load(c + 1, 1 - slot).start()

                copies = []
                for kk in range(TOP_K):
                    for q in range(nvec):
                        off = pl.multiple_of(kk * tpw + c * SC_WINDOW + q * SC_LANES, SC_LANES)
                        rows = idx_v[pl.ds(off, SC_LANES)]
                        cp = pltpu.make_async_copy(bufs[slot].at[pl.ds(q * SC_LANES, SC_LANES)],
                                                   o_hbm.at[rows], sem_out)
                        cp.start()
                        copies.append(cp)
                for cp in copies:
                    cp.wait()

    return run(xn, dest_flat)


def _sc_gather(y_rows, dest_flat):
    n = dest_flat.shape[0]
    rpw = n // SC_WORKERS
    nchunk = rpw // SC_WINDOW
    nvec = SC_WINDOW // SC_LANES

    @functools.partial(
        pl.kernel, out_type=jax.ShapeDtypeStruct((n, y_rows.shape[1]), y_rows.dtype), mesh=_sc_mesh(),
        scratch_types=[pltpu.VMEM((rpw,), jnp.int32),
                       pltpu.VMEM((SC_WINDOW, y_rows.shape[1]), y_rows.dtype),
                       pltpu.VMEM((SC_WINDOW, y_rows.shape[1]), y_rows.dtype),
                       pltpu.SemaphoreType.DMA, pltpu.SemaphoreType.DMA, pltpu.SemaphoreType.DMA],
        name="sc_gather")
    def run(y_hbm, d_hbm, o_hbm, idx_v, buf0, buf1, sem0, sem1, sem_in):
        base = _sc_worker() * rpw
        pltpu.sync_copy(d_hbm.at[pl.ds(base, rpw)], idx_v)
        bufs = (buf0, buf1)
        sems = (sem0, sem1)

        def store(c, slot):
            return pltpu.make_async_copy(bufs[slot], o_hbm.at[pl.ds(base + c * SC_WINDOW, SC_WINDOW)], sems[slot])

        @pl.loop(0, nchunk, step=2)
        def _(c0):
            for slot in range(2):
                c = c0 + slot

                @pl.when(c >= 2)
                def _():
                    store(c - 2, slot).wait()

                copies = []
                for q in range(nvec):
                    off = pl.multiple_of(c * SC_WINDOW + q * SC_LANES, SC_LANES)
                    rows = idx_v[pl.ds(off, SC_LANES)]
                    cp = pltpu.make_async_copy(y_hbm.at[rows], bufs[slot].at[pl.ds(q * SC_LANES, SC_LANES)], sem_in)
                    cp.start()
                    copies.append(cp)
                for cp in copies:
                    cp.wait()
                store(c, slot).start()

        store(nchunk - 2, 0).wait()
        store(nchunk - 1, 1).wait()

    return run(y_rows, dest_flat)


def _prep_w_in(w_in):
    o_g = 4 * M_WIDTH
    o_cq = o_g + 2 * M_HEADS
    o_ckv = o_cq + A_QRANK
    o_kr = o_ckv + A_KVRANK
    o_up = o_kr + A_ROPE
    z = lambda n: jnp.zeros(w_in.shape[:-1] + (n,), w_in.dtype)
    small = jnp.concatenate([z(SMALL_KR), w_in[..., o_kr:o_up], w_in[..., o_g:o_cq],
                             z(LANES - SMALL_GATE - 2 * M_HEADS)], axis=-1)
    return jnp.concatenate([w_in[..., 0:o_g], w_in[..., o_cq:o_ckv], w_in[..., o_up:o_up + P_WIDTH],
                            w_in[..., o_ckv:o_kr], small], axis=-1).astype(BF16)


def _rope_tables(seq):
    inv = ROPE_THETA ** (-jnp.arange(0, A_ROPE, 2, dtype=F32) / A_ROPE)
    ang = jnp.arange(seq, dtype=F32)[:, None] * inv[None, :]
    cos, sin = jnp.cos(ang), jnp.sin(ang)
    half = A_ROPE // 2
    zeros = lambda n: jnp.zeros((seq, n), F32)
    ones = lambda n: jnp.ones((seq, n), F32)
    tail = LANES - A_NOPE - A_ROPE
    cq_t = jnp.concatenate([ones(A_NOPE), cos, cos, zeros(tail)], axis=1)
    ck_t = jnp.concatenate([zeros(A_NOPE), cos, cos, zeros(tail)], axis=1)
    s1_t = jnp.concatenate([zeros(A_NOPE), -sin, zeros(half), zeros(tail)], axis=1)
    s2_t = jnp.concatenate([zeros(A_NOPE), zeros(half), sin, zeros(tail)], axis=1)
    return cq_t, ck_t, s1_t, s2_t


def kernel(x, norm1_g, w_in, conv_w, conv_b, gate_b, mlstm_norm_g, q_norm_g, kv_norm_g, w_uq, w_ukv,
           w_pool, pool_scale, w_out, norm2_g, w_router, b_router, w_gate_up, b_gate_up, w_down, b_down,
           final_norm_g):
    B, S, D = x.shape
    depth = w_in.shape[0]
    T = B * S
    nb = (T * TOP_K) // MOE_ROWS + N_EXPERTS
    nb_pad = -(-nb // LANES) * LANES

    w_in_p = _prep_w_in(w_in)
    wq = w_uq.reshape(depth, A_QRANK, A_HEADS, A_NOPE + A_ROPE)
    wq = jnp.pad(wq, ((0, 0), (0, 0), (0, 0), (0, LANES - A_NOPE - A_ROPE)))
    r0, r1, r2 = A_NOPE, A_NOPE + A_ROPE // 2, A_NOPE + A_ROPE
    wqs = jnp.concatenate([jnp.zeros_like(wq[..., :r0]), wq[..., r1:r2], wq[..., r0:r1],
                           jnp.zeros_like(wq[..., r2:])], axis=-1)
    wq = wq.reshape(depth, A_QRANK, A_HEADS * LANES).astype(BF16)
    wqs = wqs.reshape(depth, A_QRANK, A_HEADS * LANES).astype(BF16)
    wkv = w_ukv.reshape(depth, A_KVRANK, A_HEADS, A_NOPE + A_VDIM)
    wk = jnp.pad(wkv[..., :A_NOPE], ((0, 0), (0, 0), (0, 0), (0, LANES - A_NOPE)))
    wk = wk.reshape(depth, A_KVRANK, A_HEADS * LANES).astype(BF16)
    wv_e = jnp.pad(wkv[:, :, 0::2, A_NOPE:], ((0, 0), (0, 0), (0, 0), (0, LANES - A_VDIM)))
    wv_o = jnp.pad(wkv[:, :, 1::2, A_NOPE:], ((0, 0), (0, 0), (0, 0), (LANES - A_VDIM, 0)))
    wv = jnp.stack([wv_e, wv_o], axis=3).reshape(depth, A_KVRANK, A_HEADS * LANES).astype(BF16)
    half = jnp.arange(A_HEADS * LANES) // A_VDIM
    vone = ((half % 4 == 1) | (half % 4 == 2)).astype(F32)[None, :]
    gsz = P_WIDTH // len(P_WINDOWS)
    w_pool_bd = jnp.zeros((depth, P_WIDTH, P_WIDTH), F32)
    for gi in range(len(P_WINDOWS)):
        w_pool_bd = w_pool_bd.at[:, gi * gsz:(gi + 1) * gsz, gi * gsz:(gi + 1) * gsz].set(w_pool[:, gi])
    w_pool_bd = w_pool_bd.astype(BF16)
    w_out16 = w_out.astype(BF16)
    w_router_t = jnp.swapaxes(w_router, 1, 2)
    gate_b_col = jnp.pad(gate_b, ((0, 0), (SMALL_GATE, LANES - SMALL_GATE - 2 * M_HEADS)))
    cq_t, ck_t, s1_t, s2_t = _rope_tables(S)
    tr = 512
    tri = (jnp.arange(tr)[:, None] < jnp.arange(tr)[None, :]).astype(BF16)

    wgu_all = w_gate_up.reshape(depth * N_EXPERTS, D_MODEL, 2 * D_FF)
    bgu_all = b_gate_up.reshape(depth * N_EXPERTS, 1, 2 * D_FF)
    wdn_all = w_down.reshape(depth * N_EXPERTS, D_FF, D_MODEL)
    bdn_all = b_down.reshape(depth * N_EXPERTS, 1, D_MODEL)

    h = x.reshape(T, D)
    moe = None
    for l in range(depth):
        h, proj = _inproj(h, norm1_g[l][None, :], w_in_p[l], moe)
        y_m = _mlstm(proj, conv_w[l], conv_b[l][None, :], gate_b_col[l][None, :],
                     mlstm_norm_g[l][None, :], B, S)
        q16, k16, v16 = _mla_prep(proj, q_norm_g[l][None, :], kv_norm_g[l][None, :], wq[l], wqs[l], wk[l], wv[l],
                                  vone, cq_t, ck_t, s1_t, s2_t, B, S)
        y_a = _attention(q16, k16, v16, B, S)
        y_p = _pool(proj, w_pool_bd[l], pool_scale[l][None, :], B, S)
        h, xn, logits_t = _outproj(y_m, y_a, y_p, h, w_out16[l], norm2_g[l][None, :],
                                   w_router_t[l], b_router[l][:, None])
        eidx, wts, rank, counts = _router(logits_t, tri, tr)
        dest, blk_e, blk_cnt, blk_next, nblk = _meta(counts, eidx, rank, nb_pad)
        dest_flat = dest.reshape(TOP_K * T)
        x_rows = _sc_dispatch(xn, dest_flat, nb * MOE_ROWS)
        y_rows = _gmm(blk_e[0], blk_cnt[0], blk_next[0], nblk[0], x_rows, wgu_all, bgu_all, wdn_all,
                      bdn_all, nb, l)
        moe = (_sc_gather(y_rows, dest_flat), wts.T)
    return _final(h, moe[0], moe[1], final_norm_g[None, :]).reshape(B, S, D)
```

```python
import functools

import jax
import jax.numpy as jnp
import numpy as np
from jax import lax
from jax.experimental import pallas as pl
from jax.experimental.pallas import tpu as pltpu
from jax.experimental.pallas import tpu_sc as plsc

F32 = jnp.float32
BF16 = jnp.bfloat16
HIGHEST = lax.Precision.HIGHEST

D_MODEL = 1024
M_HEADS = 4
M_HEAD_DIM = 64
M_WIDTH = 256
M_CONV = 4
M_CHUNK = 64
A_HEADS = 8
A_NOPE = 64
A_ROPE = 32
A_VDIM = 64
A_QRANK = 256
A_KVRANK = 128
A_WIDTH = 512
ROPE_THETA = 10000.0
P_WINDOWS = (2, 4, 8, 16)
P_WIDTH = 256
N_EXPERTS = 32
TOP_K = 4
D_FF = 1024
SWIGLU_LIMIT = 7.0
SWIGLU_ALPHA = 1.702
EPS = 1e-6

LANES = 128
SUBLANES = 8

PROJ_QKVO = 0
PROJ_CQ = 1024
PROJ_UP = 1280
PROJ_CKV = 1536
PROJ_SMALL = 1664
PROJ_WIDTH = 1792
SMALL_KR = 64
SMALL_GATE = 96

MOE_ROWS = 1024
MOE_SPLIT = 4
VMEM_LIMIT = 56 * 1024 * 1024


def _cparams(*sem):
    return pltpu.CompilerParams(dimension_semantics=sem, vmem_limit_bytes=VMEM_LIMIT)


def _sigmoid(x):
    return 1.0 / (1.0 + jnp.exp(-x))


def _log_sigmoid(x):
    return jnp.minimum(x, 0.0) - jnp.log(1.0 + jnp.exp(-jnp.abs(x)))


def _dot(a, b, **kw):
    return jnp.dot(a, b, preferred_element_type=F32, **kw)


def _dot_nt(a, b, **kw):
    return lax.dot_general(a, b, (((1,), (1,)), ((), ())), preferred_element_type=F32, **kw)


def _dot_tn(a, b, **kw):
    return lax.dot_general(a, b, (((0,), (0,)), ((), ())), preferred_element_type=F32, **kw)


def _bf16_terms(x, terms=3):
    out = []
    for _ in range(terms):
        piece = x.astype(BF16)
        out.append(piece)
        x = x - piece.astype(F32)
    return out


def _dot_sel(x, sel16, terms=3):
    return sum(_dot(p, sel16) for p in _bf16_terms(x, terms))


def _sel_dot(sel16, x, terms=3):
    return sum(_dot(sel16, p) for p in _bf16_terms(x, terms))


def _pack_bf16_pairs(x):
    n = x.shape[1] // 2
    lo = lax.bitcast_convert_type(x[:, :n].astype(BF16).astype(F32), jnp.uint32)
    hi = lax.bitcast_convert_type(x[:, n:].astype(BF16).astype(F32), jnp.uint32)
    return (lo >> 16) | (hi & jnp.uint32(0xFFFF0000))


def _unpack_bf16_pairs(p):
    lo = lax.bitcast_convert_type(p << 16, F32)
    hi = lax.bitcast_convert_type(p & jnp.uint32(0xFFFF0000), F32)
    return lo, hi


ROW_SUBTILE = 128


def _row_slices(tm):
    return [slice(r, r + ROW_SUBTILE) for r in range(0, tm, ROW_SUBTILE)]


def _moe_combine(h_ref, y_refs, w_ref, rows):
    w = w_ref[rows, :]
    dp = D_MODEL // 2
    acc_lo = h_ref[rows, :dp]
    acc_hi = h_ref[rows, dp:]
    for kk, y_ref in enumerate(y_refs):
        lo, hi = _unpack_bf16_pairs(y_ref[rows, :])
        acc_lo = acc_lo + lo * w[:, kk:kk + 1]
        acc_hi = acc_hi + hi * w[:, kk:kk + 1]
    return jnp.concatenate([acc_lo, acc_hi], axis=1)


def _combine_specs(tm, nt):
    y_specs = [pl.BlockSpec((tm, D_MODEL // 2), functools.partial(lambda i, kk: (kk * nt + i, 0), kk=kk))
               for kk in range(TOP_K)]
    return [pl.BlockSpec((tm, D_MODEL), lambda i: (i, 0))] + y_specs + [pl.BlockSpec((tm, TOP_K), lambda i: (i, 0))]


def _inproj_kernel(*refs, combine):
    if combine:
        h_ref, y0, y1, y2, y3, wk_ref, g_ref, w_ref, hn_ref, o_ref = refs
    else:
        h_ref, g_ref, w_ref, o_ref = refs
    for rows in _row_slices(h_ref.shape[0]):
        if combine:
            x = _moe_combine(h_ref, (y0, y1, y2, y3), wk_ref, rows)
            hn_ref[rows, :] = x
        else:
            x = h_ref[rows, :]
        ms = jnp.mean(x * x, axis=-1, keepdims=True)
        xn = x * lax.rsqrt(ms + EPS) * g_ref[...]
        o_ref[rows, :] = _dot(xn.astype(BF16), w_ref[...])


def _inproj(h2, g, w, moe=None, tm=512):
    T = h2.shape[0]
    nt = T // tm
    w_specs = [pl.BlockSpec((1, D_MODEL), lambda i: (0, 0)),
               pl.BlockSpec((D_MODEL, PROJ_WIDTH), lambda i: (0, 0))]
    proj_spec = pl.BlockSpec((tm, PROJ_WIDTH), lambda i: (i, 0))
    proj_shape = jax.ShapeDtypeStruct((T, PROJ_WIDTH), F32)
    if moe is None:
        return h2, pl.pallas_call(
            functools.partial(_inproj_kernel, combine=False),
            grid=(nt,),
            in_specs=[pl.BlockSpec((tm, D_MODEL), lambda i: (i, 0))] + w_specs,
            out_specs=proj_spec,
            out_shape=proj_shape,
            compiler_params=_cparams("parallel"),
            name="inproj",
        )(h2, g, w)
    y_tok, w_tok = moe
    return pl.pallas_call(
        functools.partial(_inproj_kernel, combine=True),
        grid=(nt,),
        in_specs=_combine_specs(tm, nt) + w_specs,
        out_specs=[pl.BlockSpec((tm, D_MODEL), lambda i: (i, 0)), proj_spec],
        out_shape=[jax.ShapeDtypeStruct((T, D_MODEL), F32), proj_shape],
        compiler_params=_cparams("parallel"),
        name="combine_inproj",
    )(h2, y_tok, y_tok, y_tok, y_tok, w_tok, g, w)


M_SEQS = 2
M_UNROLL = 2


def _mlstm_kernel(proj_ref, small_ref, cw_ref, cb_ref, gbc_ref, ng_ref, o_ref, ct_ref, n_ref, m_ref):
    S = proj_ref.shape[0] // M_SEQS
    L = M_CHUNK
    nc = S // L
    W = M_WIDTH
    ct_ref[...] = jnp.zeros_like(ct_ref)
    n_ref[...] = jnp.zeros_like(n_ref)
    m_ref[...] = jnp.zeros_like(m_ref)

    rh = lax.broadcasted_iota(jnp.int32, (W, W), 0) // M_HEAD_DIM
    chd = lax.broadcasted_iota(jnp.int32, (W, W), 1) // M_HEAD_DIM
    same_head = (rh == chd).astype(F32)
    same_head16 = same_head.astype(BF16)
    tril16 = (lax.broadcasted_iota(jnp.int32, (L, L), 0) >= lax.broadcasted_iota(jnp.int32, (L, L), 1)).astype(BF16)
    row = lax.broadcasted_iota(jnp.int32, (L, W), 0)
    key = lax.broadcasted_iota(jnp.int32, (L, W), 1) % M_HEAD_DIM
    causal = key <= row
    diag = (key == row).astype(F32)
    er = lax.broadcasted_iota(jnp.int32, (LANES, 2 * W), 0)
    ec = lax.broadcasted_iota(jnp.int32, (LANES, 2 * W), 1)
    spread16 = (er == SMALL_GATE + ec // M_HEAD_DIM).astype(BF16)
    cw = cw_ref[...]
    cb = cb_ref[...]
    gbc = gbc_ref[...]
    ng = ng_ref[...]

    halo_row = lax.broadcasted_iota(jnp.int32, (L + SUBLANES, 1), 0) >= SUBLANES

    def prefix_max(x):
        s = 1
        while s < L:
            x = jnp.maximum(x, jnp.where(row >= s, pltpu.roll(x, s, axis=0), -jnp.inf))
            s *= 2
        return x

    def chunk(sq, c):
        r0 = pl.multiple_of(sq * S + c * L, L)
        w0 = pl.multiple_of(sq * S + jnp.maximum(c * L - SUBLANES, 0), SUBLANES)
        win = proj_ref[pl.ds(w0, L + SUBLANES), 0:2 * W]
        first = jnp.where(halo_row, pltpu.roll(win, SUBLANES, axis=0), 0.0)
        win = jnp.where(c == 0, first, win)
        acc = jnp.zeros((L, 2 * W), F32) + cb
        for j in range(M_CONV):
            s = M_CONV - 1 - j
            xs = win if s == 0 else pltpu.roll(win, s, axis=0)
            acc = acc + xs[SUBLANES:, :] * cw[j:j + 1, :]
        qk = acc * _sigmoid(acc)
        q = qk[:, 0:W]
        k = qk[:, W:2 * W] * (M_HEAD_DIM ** -0.5)
        v = proj_ref[pl.ds(r0, L), 2 * W:3 * W]
        og = proj_ref[pl.ds(r0, L), 3 * W:4 * W]
        q16 = q.astype(BF16)
        k16 = k.astype(BF16)
        v16 = v.astype(BF16)
        ct = ct_ref[sq]
        nvec = n_ref[sq]
        m_old = m_ref[sq]
        gates = _dot_sel(small_ref[pl.ds(r0, L), :] + gbc, spread16)
        kb = jnp.concatenate([k16] * M_HEADS, axis=0) * same_head16
        vb = jnp.concatenate([v16] * M_HEADS, axis=0) * same_head16
        qk_all = _dot_nt(q16, kb)
        q_c = _dot(q16, ct.astype(BF16))
        q_n = _dot_sel(q * nvec, same_head16, terms=2)
        yield

        i_pre = gates[:, :W]
        g = _sel_dot(tril16, _log_sigmoid(gates[:, W:]))
        yield
        a = i_pre - g
        a_key = jnp.sum(a * diag, axis=0, keepdims=True)
        a_max = prefix_max(a)
        m_row = g + jnp.maximum(m_old, a_max)
        p = qk_all * jnp.exp(jnp.where(causal, g + a_key, -jnp.inf) - m_row)
        p16 = p.astype(BF16)
        num_intra = _dot(p16, vb)
        rowsum = _dot_sel(p, same_head16, terms=2)
        yield

        w_inter = jnp.exp(g + m_old - m_row)
        num = w_inter * q_c + num_intra
        den = w_inter * q_n + rowsum
        hv = num / jnp.maximum(jnp.abs(den), jnp.exp(-m_row))
        ms = _dot_sel(hv * hv, same_head16, terms=2) * (1.0 / M_HEAD_DIM)
        yield
        y = hv * lax.rsqrt(ms + EPS) * ng * _sigmoid(og)
        o_ref[pl.ds(r0, L), :] = y

        g_end = g[L - 1:L, :]
        m_new = g_end + jnp.maximum(m_old, a_max[L - 1:L, :])
        wa = jnp.exp(g_end + a - m_new)
        decay = jnp.exp(g_end + m_old - m_new)
        upd = _dot_tn(k16, (wa * v).astype(BF16))
        ct_ref[sq] = decay * ct + upd * same_head
        n_ref[sq] = decay * nvec + jnp.sum(wa * k, axis=0, keepdims=True)
        m_ref[sq] = m_new
        yield

    def body(cc, carry):
        for u in range(M_UNROLL):
            for _ in zip(*[chunk(sq, cc * M_UNROLL + u) for sq in range(M_SEQS)]):
                pass
        return carry

    lax.fori_loop(0, nc // M_UNROLL, body, 0)


def _mlstm(proj, cw, cb, gbc, ng, B, S):
    assert M_CHUNK == M_HEAD_DIM
    T = B * S
    rows = M_SEQS * S
    return pl.pallas_call(
        _mlstm_kernel,
        grid=(B // M_SEQS,),
        in_specs=[pl.BlockSpec((rows, 4 * M_WIDTH), lambda b: (b, 0)),
                  pl.BlockSpec((rows, LANES), lambda b: (b, PROJ_SMALL // LANES)),
                  pl.BlockSpec((M_CONV, 2 * M_WIDTH), lambda b: (0, 0)),
                  pl.BlockSpec((1, 2 * M_WIDTH), lambda b: (0, 0)),
                  pl.BlockSpec((1, LANES), lambda b: (0, 0)),
                  pl.BlockSpec((1, M_WIDTH), lambda b: (0, 0))],
        out_specs=pl.BlockSpec((rows, M_WIDTH), lambda b: (b, 0)),
        out_shape=jax.ShapeDtypeStruct((T, M_WIDTH), F32),
        scratch_shapes=[pltpu.VMEM((M_SEQS, M_WIDTH, M_WIDTH), F32),
                        pltpu.VMEM((M_SEQS, 1, M_WIDTH), F32),
                        pltpu.VMEM((M_SEQS, 1, M_WIDTH), F32)],
        compiler_params=_cparams("parallel"),
        name="mlstm",
    )(proj, proj, cw, cb, gbc, ng)


POOL_HALO = 16
POOL_TILE = 256


def _pool_kernel(u_ref, w_ref, sc_ref, o_ref, upad_ref):
    S = u_ref.shape[0]
    upad_ref[0:POOL_HALO, :] = jnp.zeros((POOL_HALO, P_WIDTH), F32)
    upad_ref[POOL_HALO:, :] = u_ref[...]
    grp = lax.broadcasted_iota(jnp.int32, (1, P_WIDTH), 1) // (P_WIDTH // len(P_WINDOWS))
    win_lane = jnp.zeros((1, P_WIDTH), jnp.int32)
    for gi, wn in enumerate(P_WINDOWS):
        win_lane = jnp.where(grp == gi, wn, win_lane)
    w = w_ref[...]
    scale = sc_ref[...]
    rows = POOL_TILE + POOL_HALO

    def body(r, carry):
        r0 = pl.multiple_of(r * POOL_TILE, POOL_TILE)
        a = upad_ref[pl.ds(r0, rows), :]
        sums = []
        cur = a
        span = 1
        for _ in P_WINDOWS:
            cur = cur + pltpu.roll(cur, span, axis=0)
            span *= 2
            sums.append(cur)
        sel = sums[-1]
        for gi in range(len(P_WINDOWS) - 1):
            sel = jnp.where(grp == gi, sums[gi], sel)
        sel = sel[POOL_HALO:, :]
        u = a[POOL_HALO:, :]
        t = r0 + lax.broadcasted_iota(jnp.int32, (POOL_TILE, P_WIDTH), 0)
        cnt = jnp.minimum(t + 1, win_lane).astype(F32)
        pooled = sel / cnt - u
        o_ref[pl.ds(r0, POOL_TILE), :] = _dot(pooled.astype(BF16), w) * scale
        return carry

    lax.fori_loop(0, S // POOL_TILE, body, 0)


def _pool(proj, w_bd, scale, B, S):
    T = B * S
    return pl.pallas_call(
        _pool_kernel,
        grid=(B,),
        in_specs=[pl.BlockSpec((S, P_WIDTH), lambda b: (b, PROJ_UP // P_WIDTH)),
                  pl.BlockSpec((P_WIDTH, P_WIDTH), lambda b: (0, 0)),
                  pl.BlockSpec((1, P_WIDTH), lambda b: (0, 0))],
        out_specs=pl.BlockSpec((S, P_WIDTH), lambda b: (b, 0)),
        out_shape=jax.ShapeDtypeStruct((T, P_WIDTH), F32),
        scratch_shapes=[pltpu.VMEM((S + POOL_HALO, P_WIDTH), F32)],
        compiler_params=_cparams("parallel"),
        name="pool",
    )(proj, w_bd, scale)


def _rope(x, c, s1, s2):
    return x * c + pltpu.roll(x, LANES - A_ROPE // 2, axis=1) * s1 + pltpu.roll(x, A_ROPE // 2, axis=1) * s2


def _mla_prep_kernel(cq_ref, ckv_ref, small_ref, qg_ref, kvg_ref, wq_ref, wqs_ref, wk_ref, wv_ref,
                     vone_ref, cq_t_ref, ck_t_ref, s1_ref, s2_ref, q_ref, k_ref, v_ref):
    def rms(x, g):
        return x * lax.rsqrt(jnp.mean(x * x, axis=-1, keepdims=True) + EPS) * g

    cqn = rms(cq_ref[...], qg_ref[...]).astype(BF16)
    ckvn = rms(ckv_ref[...], kvg_ref[...]).astype(BF16)
    scale = (A_NOPE + A_ROPE) ** -0.5
    qf = _dot(cqn, wq_ref[...]) * scale
    qp = _dot(cqn, wqs_ref[...]) * scale
    kf = _dot(ckvn, wk_ref[...])
    v_ref[...] = (_dot(ckvn, wv_ref[...]) + vone_ref[...]).astype(BF16)
    cqt = cq_t_ref[...]
    s1 = s1_ref[...]
    s2 = s2_ref[...]
    krot = _rope(small_ref[...], ck_t_ref[...], s1, s2)
    sq = s1 + s2
    for h in range(A_HEADS):
        sl = slice(h * LANES, (h + 1) * LANES)
        q_ref[:, sl] = (qf[:, sl] * cqt + qp[:, sl] * sq).astype(BF16)
        k_ref[:, sl] = (kf[:, sl] + krot).astype(BF16)


def _mla_prep(proj, qg, kvg, wq, wqs, wk, wv, vone, cq_t, ck_t, s1_t, s2_t, B, S, ts=1024):
    T = B * S
    nst = S // ts
    hw = A_HEADS * LANES
    return pl.pallas_call(
        _mla_prep_kernel,
        grid=(B, nst),
        in_specs=[pl.BlockSpec((ts, A_QRANK), lambda b, s: (b * nst + s, PROJ_CQ // A_QRANK)),
                  pl.BlockSpec((ts, A_KVRANK), lambda b, s: (b * nst + s, PROJ_CKV // A_KVRANK)),
                  pl.BlockSpec((ts, LANES), lambda b, s: (b * nst + s, PROJ_SMALL // LANES)),
                  pl.BlockSpec((1, A_QRANK), lambda b, s: (0, 0)),
                  pl.BlockSpec((1, A_KVRANK), lambda b, s: (0, 0)),
                  pl.BlockSpec((A_QRANK, hw), lambda b, s: (0, 0)),
                  pl.BlockSpec((A_QRANK, hw), lambda b, s: (0, 0)),
                  pl.BlockSpec((A_KVRANK, hw), lambda b, s: (0, 0)),
                  pl.BlockSpec((A_KVRANK, hw), lambda b, s: (0, 0)),
                  pl.BlockSpec((1, hw), lambda b, s: (0, 0)),
                  pl.BlockSpec((ts, LANES), lambda b, s: (s, 0)),
                  pl.BlockSpec((ts, LANES), lambda b, s: (s, 0)),
                  pl.BlockSpec((ts, LANES), lambda b, s: (s, 0)),
                  pl.BlockSpec((ts, LANES), lambda b, s: (s, 0))],
        out_specs=[pl.BlockSpec((ts, hw), lambda b, s: (b * nst + s, 0)),
                   pl.BlockSpec((ts, hw), lambda b, s: (b * nst + s, 0)),
                   pl.BlockSpec((ts, hw), lambda b, s: (b * nst + s, 0))],
        out_shape=[jax.ShapeDtypeStruct((T, hw), BF16),
                   jax.ShapeDtypeStruct((T, hw), BF16),
                   jax.ShapeDtypeStruct((T, hw), BF16)],
        compiler_params=_cparams("parallel", "parallel"),
        name="mla_prep",
    )(proj, proj, proj, qg, kvg, wq, wqs, wk, wv, vone, cq_t, ck_t, s1_t, s2_t)


def _attn_kernel(q_ref, k_ref, v_ref, o_ref, *, tq):
    heads = range(2)
    sls = [slice(hh * LANES, (hh + 1) * LANES) for hh in heads]
    lane = lax.broadcasted_iota(jnp.int32, (tq, LANES), 1)
    below_diag = lax.broadcasted_iota(jnp.int32, (tq, tq), 0) >= lax.broadcasted_iota(jnp.int32, (tq, tq), 1)

    def update(qs, k0, state, causal):
        keys = slice(k0, k0 + tq)
        s = [_dot_nt(qs[hh], k_ref[keys, sls[hh]]) for hh in heads]
        if causal:
            s = [jnp.where(below_diag, s[hh], -jnp.inf) for hh in heads]
        m_new = [jnp.maximum(state[hh][0], jnp.max(s[hh], axis=-1, keepdims=True)) for hh in heads]
        p = [jnp.exp((s[hh] - m_new[hh]).astype(BF16)) for hh in heads]
        pv = [_dot(p[hh], v_ref[keys, sls[hh]]) for hh in heads]
        acc = [jnp.exp(state[hh][0] - m_new[hh]) * state[hh][1] + pv[hh] for hh in heads]
        return tuple((m_new[hh], acc[hh]) for hh in heads)

    for qi in range(q_ref.shape[0] // tq):
        rows = slice(qi * tq, (qi + 1) * tq)
        qs = [q_ref[rows, sl] for sl in sls]
        state = tuple((jnp.full((tq, 1), -jnp.inf, F32), jnp.zeros((tq, LANES), F32)) for _ in heads)
        for kb in range(qi + 1):
            state = update(qs, kb * tq, state, causal=(kb == qi))
        (_, acc0), (_, acc1) = state
        acc = jnp.where(lane < A_VDIM, acc0, acc1)
        den = jnp.where(lane < A_VDIM, pltpu.roll(acc0, A_VDIM, axis=1), pltpu.roll(acc1, A_VDIM, axis=1))
        o_ref[rows, :] = (acc / den).astype(o_ref.dtype)


def _attention(q, k, v, B, S, tq=512):
    T = B * S
    return pl.pallas_call(
        functools.partial(_attn_kernel, tq=tq),
        grid=(B, A_HEADS // 2),
        in_specs=[pl.BlockSpec((S, 2 * LANES), lambda b, p: (b, p)),
                  pl.BlockSpec((S, 2 * LANES), lambda b, p: (b, p)),
                  pl.BlockSpec((S, 2 * LANES), lambda b, p: (b, p))],
        out_specs=pl.BlockSpec((S, LANES), lambda b, p: (b, p)),
        out_shape=jax.ShapeDtypeStruct((T, A_WIDTH), BF16),
        compiler_params=_cparams("parallel", "parallel"),
        name="attention",
    )(q, k, v)


def _outproj_kernel(ym_ref, ya_ref, yp_ref, h_ref, w_ref, g_ref, wr_ref, br_ref,
                    hn_ref, xn_ref, lg_ref):
    mix = _dot(ym_ref[...].astype(BF16), w_ref[0:M_WIDTH, :])
    mix = mix + _dot(ya_ref[...], w_ref[M_WIDTH:M_WIDTH + A_WIDTH, :])
    mix = mix + _dot(yp_ref[...].astype(BF16), w_ref[M_WIDTH + A_WIDTH:, :])
    hn = h_ref[...] + mix
    hn_ref[...] = hn
    xn = hn * lax.rsqrt(jnp.mean(hn * hn, axis=-1, keepdims=True) + EPS) * g_ref[...]
    x_hi = xn.astype(BF16)
    x_lo = (xn - x_hi.astype(F32)).astype(BF16)
    wr = wr_ref[...]
    w_hi = wr.astype(BF16)
    w_lo = (wr - w_hi.astype(F32)).astype(BF16)
    lg_ref[...] = _dot_nt(w_hi, x_hi) + _dot_nt(w_hi, x_lo) + _dot_nt(w_lo, x_hi) + br_ref[...]
    xn_ref[...] = _pack_bf16_pairs(xn)


def _outproj(ym, ya, yp, h2, w, g, wr_t, br, tm=512):
    T = h2.shape[0]
    return pl.pallas_call(
        _outproj_kernel,
        grid=(T // tm,),
        in_specs=[pl.BlockSpec((tm, M_WIDTH), lambda i: (i, 0)),
                  pl.BlockSpec((tm, A_WIDTH), lambda i: (i, 0)),
                  pl.BlockSpec((tm, P_WIDTH), lambda i: (i, 0)),
                  pl.BlockSpec((tm, D_MODEL), lambda i: (i, 0)),
                  pl.BlockSpec((D_MODEL, D_MODEL), lambda i: (0, 0)),
                  pl.BlockSpec((1, D_MODEL), lambda i: (0, 0)),
                  pl.BlockSpec((N_EXPERTS, D_MODEL), lambda i: (0, 0)),
                  pl.BlockSpec((N_EXPERTS, 1), lambda i: (0, 0))],
        out_specs=[pl.BlockSpec((tm, D_MODEL), lambda i: (i, 0)),
                   pl.BlockSpec((tm, D_MODEL // 2), lambda i: (i, 0)),
                   pl.BlockSpec((N_EXPERTS, tm), lambda i: (0, i))],
        out_shape=[jax.ShapeDtypeStruct((T, D_MODEL), F32),
                   jax.ShapeDtypeStruct((T, D_MODEL // 2), jnp.uint32),
                   jax.ShapeDtypeStruct((N_EXPERTS, T), F32)],
        compiler_params=_cparams("parallel"),
        name="outproj",
    )(ym, ya, yp, h2, w, g, wr_t, br)


def _router_kernel(lg_ref, tri_ref, e_ref, w_ref, r_ref, cnt_ref, carry_ref):
    tr = lg_ref.shape[1]

    @pl.when(pl.program_id(0) == 0)
    def _():
        carry_ref[...] = jnp.zeros_like(carry_ref)

    x = lg_ref[...]
    eio = lax.broadcasted_iota(jnp.int32, (N_EXPERTS, tr), 0).astype(F32)
    picked = jnp.zeros((N_EXPERTS, tr), F32)
    vals = []
    idxs = []
    for _ in range(TOP_K):
        mx = jnp.max(x, axis=0, keepdims=True)
        idx = jnp.min(jnp.where(x == mx, eio, float(N_EXPERTS)), axis=0, keepdims=True)
        hit = eio == idx
        vals.append(mx)
        idxs.append(idx)
        picked = picked + hit.astype(F32)
        x = jnp.where(hit, -jnp.inf, x)
    exps = [jnp.exp(vv - vals[0]) for vv in vals]
    tot = exps[0] + exps[1] + exps[2] + exps[3]
    before = _dot(picked.astype(BF16), tri_ref[...]) + carry_ref[:, 0:1]
    for kk in range(TOP_K):
        e_ref[kk:kk + 1, :] = idxs[kk].astype(jnp.int32)
        w_ref[kk:kk + 1, :] = exps[kk] / tot
        rk = jnp.sum(jnp.where(eio == idxs[kk], before, 0.0), axis=0, keepdims=True)
        r_ref[kk:kk + 1, :] = rk.astype(jnp.int32)
    carry_ref[...] = carry_ref[...] + jnp.sum(picked, axis=1, keepdims=True)
    cnt_ref[...] = carry_ref[...]


def _router(logits_t, tri, tr=512):
    T = logits_t.shape[1]
    return pl.pallas_call(
        _router_kernel,
        grid=(T // tr,),
        in_specs=[pl.BlockSpec((N_EXPERTS, tr), lambda i: (0, i)),
                  pl.BlockSpec((tr, tr), lambda i: (0, 0))],
        out_specs=[pl.BlockSpec((TOP_K, tr), lambda i: (0, i)),
                   pl.BlockSpec((TOP_K, tr), lambda i: (0, i)),
                   pl.BlockSpec((TOP_K, tr), lambda i: (0, i)),
                   pl.BlockSpec((N_EXPERTS, LANES), lambda i: (0, 0))],
        out_shape=[jax.ShapeDtypeStruct((TOP_K, T), jnp.int32),
                   jax.ShapeDtypeStruct((TOP_K, T), F32),
                   jax.ShapeDtypeStruct((TOP_K, T), jnp.int32),
                   jax.ShapeDtypeStruct((N_EXPERTS, LANES), F32)],
        scratch_shapes=[pltpu.VMEM((N_EXPERTS, LANES), F32)],
        compiler_params=_cparams("arbitrary"),
        name="router",
    )(logits_t, tri)


def _meta_kernel(cnt_ref, e_ref, r_ref, dest_ref, be_ref, bc_ref, bn_ref, nb_ref, *, nb_pad):
    cnt = cnt_ref[...]
    padded = jnp.floor((cnt + (MOE_ROWS - 1)) * (1.0 / MOE_ROWS)) * MOE_ROWS
    ri = lax.broadcasted_iota(jnp.int32, (N_EXPERTS, N_EXPERTS), 0)
    ci = lax.broadcasted_iota(jnp.int32, (N_EXPERTS, N_EXPERTS), 1)
    pad_end = _dot((ri >= ci).astype(F32), padded, precision=HIGHEST)
    pad_start = pad_end - padded
    e = e_ref[...]
    dest = r_ref[...]
    for ex in range(N_EXPERTS):
        ps = pad_start[ex:ex + 1, 0:1].astype(jnp.int32)
        dest = jnp.where(e == ex, dest + ps, dest)
    dest_ref[...] = dest
    blk0 = (lax.broadcasted_iota(jnp.int32, (N_EXPERTS, nb_pad), 1) * MOE_ROWS).astype(F32)
    be = jnp.sum((pad_end[:, 0:1] <= blk0).astype(F32), axis=0, keepdims=True)
    be = jnp.minimum(be, float(N_EXPERTS - 1))
    eio = lax.broadcasted_iota(jnp.int32, (N_EXPERTS, nb_pad), 0).astype(F32)
    seg_end = jnp.sum(jnp.where(eio == be, pad_start[:, 0:1] + cnt[:, 0:1], 0.0), axis=0, keepdims=True)
    bc = jnp.clip(seg_end - blk0[0:1, :], 0.0, float(MOE_ROWS))
    nxt0 = jnp.sum(jnp.where(eio == be, pad_end[:, 0:1], 0.0), axis=0, keepdims=True)
    bn = jnp.sum((pad_end[:, 0:1] <= nxt0).astype(F32), axis=0, keepdims=True)
    bn = jnp.where(nxt0 < pad_end[N_EXPERTS - 1:N_EXPERTS, 0:1], bn, -1.0)
    be_ref[...] = be.astype(jnp.int32)
    bc_ref[...] = bc.astype(jnp.int32)
    bn_ref[...] = bn.astype(jnp.int32)
    nb_ref[...] = (pad_end[N_EXPERTS - 1:N_EXPERTS, :] * (1.0 / MOE_ROWS)).astype(jnp.int32)


def _meta(counts, eidx, rank, nb_pad):
    T = eidx.shape[1]
    return pl.pallas_call(
        functools.partial(_meta_kernel, nb_pad=nb_pad),
        out_shape=[jax.ShapeDtypeStruct((TOP_K, T), jnp.int32),
                   jax.ShapeDtypeStruct((1, nb_pad), jnp.int32),
                   jax.ShapeDtypeStruct((1, nb_pad), jnp.int32),
                   jax.ShapeDtypeStruct((1, nb_pad), jnp.int32),
                   jax.ShapeDtypeStruct((1, LANES), jnp.int32)],
        compiler_params=pltpu.CompilerParams(vmem_limit_bytes=VMEM_LIMIT),
        name="route_meta",
    )(counts, eidx, rank)


FF_CHUNK = 512


def _gmm_kernel(be_ref, bc_ref, bn_ref, nb_ref, x_ref, wgu_hbm, bgu_ref, wdn_hbm, bdn_ref, y_ref,
                wgu_st, wdn_st, wgu16, wdn16, sem, *, e0):
    i = pl.program_id(0)
    nblk = nb_ref[0]
    bm = MOE_ROWS

    def weight_copies(e):
        return (pltpu.make_async_copy(wgu_hbm.at[e0 + e], wgu_st, sem.at[0]),
                pltpu.make_async_copy(wdn_hbm.at[e0 + e], wdn_st, sem.at[1]))

    @pl.when(i == 0)
    def _():
        for cp in weight_copies(be_ref[0]):
            cp.start()

    @pl.when(i >= nblk)
    def _():
        y_ref[...] = jnp.zeros_like(y_ref)

    @pl.when(i < nblk)
    def _():
        e_changed = jnp.logical_or(i == 0, be_ref[i] != be_ref[jnp.maximum(i - 1, 0)])

        @pl.when(e_changed)
        def _():
            for cp in weight_copies(be_ref[i]):
                cp.wait()
            wgu16[...] = wgu_st[...].astype(BF16)
            wdn16[...] = wdn_st[...].astype(BF16)

            @pl.when(bn_ref[i] >= 0)
            def _():
                for cp in weight_copies(bn_ref[i]):
                    cp.start()

        def expert_rows(nrows):
            valid = lax.broadcasted_iota(jnp.int32, (nrows, 1), 0) < bc_ref[i]
            lo, hi = _unpack_bf16_pairs(jnp.where(valid, x_ref[0:nrows, :], jnp.uint32(0)))
            x16 = jnp.concatenate([lo.astype(BF16), hi.astype(BF16)], axis=1)
            acc = jnp.zeros((nrows, D_MODEL), F32) + bdn_ref[0]
            for c in range(D_FF // FF_CHUNK):
                cs = slice(c * FF_CHUNK, (c + 1) * FF_CHUNK)
                us = slice(D_FF + c * FF_CHUNK, D_FF + (c + 1) * FF_CHUNK)
                gate = _dot(x16, wgu16[:, cs]) + bgu_ref[0, :, cs]
                up = _dot(x16, wgu16[:, us]) + bgu_ref[0, :, us]
                gate = jnp.minimum(gate, SWIGLU_LIMIT)
                up = jnp.clip(up, -SWIGLU_LIMIT, SWIGLU_LIMIT)
                act = (up + 1.0) * gate * _sigmoid(SWIGLU_ALPHA * gate)
                acc = acc + _dot(act.astype(BF16), wdn16[cs, :])
            y_ref[0:nrows, :] = _pack_bf16_pairs(acc)

        quarter = bm // MOE_SPLIT
        for nq in range(1, MOE_SPLIT + 1):
            lower = (nq - 1) * quarter if nq > 1 else -1
            @pl.when(jnp.logical_and(bc_ref[i] > lower, bc_ref[i] <= nq * quarter))
            def _(nq=nq):
                expert_rows(nq * quarter)
                if nq < MOE_SPLIT:
                    y_ref[nq * quarter:, :] = jnp.zeros((bm - nq * quarter, D_MODEL // 2), jnp.uint32)


def _gmm(blk_e, blk_cnt, blk_next, nblk, x_rows, wgu, bgu, wdn, bdn, nb, layer):
    bm = MOE_ROWS
    e0 = layer * N_EXPERTS
    dp = D_MODEL // 2

    def expert(i, be, nb_ref):
        return (e0 + be[jnp.minimum(i, jnp.maximum(nb_ref[0] - 1, 0))], 0, 0)

    def rows(i, nb_ref):
        return (jnp.minimum(i, jnp.maximum(nb_ref[0] - 1, 0)), 0)

    grid_spec = pltpu.PrefetchScalarGridSpec(
        num_scalar_prefetch=4,
        grid=(nb,),
        in_specs=[pl.BlockSpec((bm, dp), lambda i, be, bc, bn, nbr: rows(i, nbr)),
                  pl.BlockSpec(memory_space=pl.ANY),
                  pl.BlockSpec((1, 1, 2 * D_FF), lambda i, be, bc, bn, nbr: expert(i, be, nbr)),
                  pl.BlockSpec(memory_space=pl.ANY),
                  pl.BlockSpec((1, 1, D_MODEL), lambda i, be, bc, bn, nbr: expert(i, be, nbr))],
        out_specs=pl.BlockSpec((bm, dp), lambda i, be, bc, bn, nbr: (i, 0)),
        scratch_shapes=[pltpu.VMEM((D_MODEL, 2 * D_FF), F32),
                        pltpu.VMEM((D_FF, D_MODEL), F32),
                        pltpu.VMEM((D_MODEL, 2 * D_FF), BF16),
                        pltpu.VMEM((D_FF, D_MODEL), BF16),
                        pltpu.SemaphoreType.DMA((2,))],
    )
    return pl.pallas_call(
        functools.partial(_gmm_kernel, e0=e0),
        grid_spec=grid_spec,
        out_shape=jax.ShapeDtypeStruct((nb * bm, dp), jnp.uint32),
        compiler_params=_cparams("arbitrary"),
        name="expert_gmm",
    )(blk_e, blk_cnt, blk_next, nblk, x_rows, wgu, bgu, wdn, bdn)


def _final_kernel(h_ref, y0, y1, y2, y3, wk_ref, g_ref, o_ref):
    for rows in _row_slices(h_ref.shape[0]):
        x = _moe_combine(h_ref, (y0, y1, y2, y3), wk_ref, rows)
        o_ref[rows, :] = x * lax.rsqrt(jnp.mean(x * x, axis=-1, keepdims=True) + EPS) * g_ref[...]


def _final(h2, y_tok, w_tok, g, tm=512):
    T = h2.shape[0]
    nt = T // tm
    return pl.pallas_call(
        _final_kernel,
        grid=(nt,),
        in_specs=_combine_specs(tm, nt) + [pl.BlockSpec((1, D_MODEL), lambda i: (0, 0))],
        out_specs=pl.BlockSpec((tm, D_MODEL), lambda i: (i, 0)),
        out_shape=jax.ShapeDtypeStruct((T, D_MODEL), F32),
        compiler_params=_cparams("parallel"),
        name="combine_final_norm",
    )(h2, y_tok, y_tok, y_tok, y_tok, w_tok, g)


SC_CORES = 2
SC_SUBCORES = 16
SC_LANES = 16
SC_WORKERS = SC_CORES * SC_SUBCORES
SC_WINDOW = 64


def _sc_mesh():
    return plsc.VectorSubcoreMesh(core_axis_name="c", subcore_axis_name="s")


def _sc_worker():
    return lax.axis_index("s") * SC_CORES + lax.axis_index("c")


def _sc_dispatch(xn, dest_flat, n_rows):
    T = xn.shape[0]
    tpw = T // SC_WORKERS
    nchunk = tpw // SC_WINDOW
    nvec = SC_WINDOW // SC_LANES

    @functools.partial(
        pl.kernel, out_type=jax.ShapeDtypeStruct((n_rows, xn.shape[1]), xn.dtype), mesh=_sc_mesh(),
        scratch_types=[pltpu.VMEM((TOP_K * tpw,), jnp.int32),
                       pltpu.VMEM((SC_WINDOW, xn.shape[1]), xn.dtype),
                       pltpu.VMEM((SC_WINDOW, xn.shape[1]), xn.dtype),
                       pltpu.SemaphoreType.DMA, pltpu.SemaphoreType.DMA, pltpu.SemaphoreType.DMA],
        name="sc_dispatch")
    def run(x_hbm, d_hbm, o_hbm, idx_v, buf0, buf1, sem0, sem1, sem_out):
        base = _sc_worker() * tpw
        for kk in range(TOP_K):
            pltpu.sync_copy(d_hbm.at[pl.ds(kk * T + base, tpw)], idx_v.at[pl.ds(kk * tpw, tpw)])
        bufs = (buf0, buf1)
        sems = (sem0, sem1)

        def load(c, slot):
            return pltpu.make_async_copy(x_hbm.at[pl.ds(base + c * SC_WINDOW, SC_WINDOW)], bufs[slot], sems[slot])

        load(0, 0).start()

        @pl.loop(0, nchunk, step=2)
        def _(c0):
            for slot in range(2):
                c = c0 + slot
                load(c, slot).wait()

                @pl.when(c + 1 < nchunk)
                def _():
                    load(c + 1, 1 - slot).start()

                copies = []
                for kk in range(TOP_K):
                    for q in range(nvec):
                        off = pl.multiple_of(kk * tpw + c * SC_WINDOW + q * SC_LANES, SC_LANES)
                        rows = idx_v[pl.ds(off, SC_LANES)]
                        cp = pltpu.make_async_copy(bufs[slot].at[pl.ds(q * SC_LANES, SC_LANES)],
                                                   o_hbm.at[rows], sem_out)
                        cp.start()
                        copies.append(cp)
                for cp in copies:
                    cp.wait()

    return run(xn, dest_flat)


def _sc_gather(y_rows, dest_flat):
    n = dest_flat.shape[0]
    rpw = n // SC_WORKERS
    nchunk = rpw // SC_WINDOW
    nvec = SC_WINDOW // SC_LANES

    @functools.partial(
        pl.kernel, out_type=jax.ShapeDtypeStruct((n, y_rows.shape[1]), y_rows.dtype), mesh=_sc_mesh(),
        scratch_types=[pltpu.VMEM((rpw,), jnp.int32),
                       pltpu.VMEM((SC_WINDOW, y_rows.shape[1]), y_rows.dtype),
                       pltpu.VMEM((SC_WINDOW, y_rows.shape[1]), y_rows.dtype),
                       pltpu.SemaphoreType.DMA, pltpu.SemaphoreType.DMA, pltpu.SemaphoreType.DMA],
        name="sc_gather")
    def run(y_hbm, d_hbm, o_hbm, idx_v, buf0, buf1, sem0, sem1, sem_in):
        base = _sc_worker() * rpw
        pltpu.sync_copy(d_hbm.at[pl.ds(base, rpw)], idx_v)
        bufs = (buf0, buf1)
        sems = (sem0, sem1)

        def store(c, slot):
            return pltpu.make_async_copy(bufs[slot], o_hbm.at[pl.ds(base + c * SC_WINDOW, SC_WINDOW)], sems[slot])

        @pl.loop(0, nchunk, step=2)
        def _(c0):
            for slot in range(2):
                c = c0 + slot

                @pl.when(c >= 2)
                def _():
                    store(c - 2, slot).wait()

                copies = []
                for q in range(nvec):
                    off = pl.multiple_of(c * SC_WINDOW + q * SC_LANES, SC_LANES)
                    rows = idx_v[pl.ds(off, SC_LANES)]
                    cp = pltpu.make_async_copy(y_hbm.at[rows], bufs[slot].at[pl.ds(q * SC_LANES, SC_LANES)], sem_in)
                    cp.start()
                    copies.append(cp)
                for cp in copies:
                    cp.wait()
                store(c, slot).start()

        store(nchunk - 2, 0).wait()
        store(nchunk - 1, 1).wait()

    return run(y_rows, dest_flat)


def _prep_w_in(w_in):
    o_g = 4 * M_WIDTH
    o_cq = o_g + 2 * M_HEADS
    o_ckv = o_cq + A_QRANK
    o_kr = o_ckv + A_KVRANK
    o_up = o_kr + A_ROPE
    z = lambda n: jnp.zeros(w_in.shape[:-1] + (n,), w_in.dtype)
    small = jnp.concatenate([z(SMALL_KR), w_in[..., o_kr:o_up], w_in[..., o_g:o_cq],
                             z(LANES - SMALL_GATE - 2 * M_HEADS)], axis=-1)
    return jnp.concatenate([w_in[..., 0:o_g], w_in[..., o_cq:o_ckv], w_in[..., o_up:o_up + P_WIDTH],
                            w_in[..., o_ckv:o_kr], small], axis=-1).astype(BF16)


def _rope_tables(seq):
    inv = ROPE_THETA ** (-jnp.arange(0, A_ROPE, 2, dtype=F32) / A_ROPE)
    ang = jnp.arange(seq, dtype=F32)[:, None] * inv[None, :]
    cos, sin = jnp.cos(ang), jnp.sin(ang)
    half = A_ROPE // 2
    zeros = lambda n: jnp.zeros((seq, n), F32)
    ones = lambda n: jnp.ones((seq, n), F32)
    tail = LANES - A_NOPE - A_ROPE
    cq_t = jnp.concatenate([ones(A_NOPE), cos, cos, zeros(tail)], axis=1)
    ck_t = jnp.concatenate([zeros(A_NOPE), cos, cos, zeros(tail)], axis=1)
    s1_t = jnp.concatenate([zeros(A_NOPE), -sin, zeros(half), zeros(tail)], axis=1)
    s2_t = jnp.concatenate([zeros(A_NOPE), zeros(half), sin, zeros(tail)], axis=1)
    return cq_t, ck_t, s1_t, s2_t


def kernel(x, norm1_g, w_in, conv_w, conv_b, gate_b, mlstm_norm_g, q_norm_g, kv_norm_g, w_uq, w_ukv,
           w_pool, pool_scale, w_out, norm2_g, w_router, b_router, w_gate_up, b_gate_up, w_down, b_down,
           final_norm_g):
    B, S, D = x.shape
    depth = w_in.shape[0]
    T = B * S
    nb = (T * TOP_K) // MOE_ROWS + N_EXPERTS
    nb_pad = -(-nb // LANES) * LANES

    w_in_p = _prep_w_in(w_in)
    wq = w_uq.reshape(depth, A_QRANK, A_HEADS, A_NOPE + A_ROPE)
    wq = jnp.pad(wq, ((0, 0), (0, 0), (0, 0), (0, LANES - A_NOPE - A_ROPE)))
    r0, r1, r2 = A_NOPE, A_NOPE + A_ROPE // 2, A_NOPE + A_ROPE
    wqs = jnp.concatenate([jnp.zeros_like(wq[..., :r0]), wq[..., r1:r2], wq[..., r0:r1],
                           jnp.zeros_like(wq[..., r2:])], axis=-1)
    wq = wq.reshape(depth, A_QRANK, A_HEADS * LANES).astype(BF16)
    wqs = wqs.reshape(depth, A_QRANK, A_HEADS * LANES).astype(BF16)
    wkv = w_ukv.reshape(depth, A_KVRANK, A_HEADS, A_NOPE + A_VDIM)
    wk = jnp.pad(wkv[..., :A_NOPE], ((0, 0), (0, 0), (0, 0), (0, LANES - A_NOPE)))
    wk = wk.reshape(depth, A_KVRANK, A_HEADS * LANES).astype(BF16)
    wv_e = jnp.pad(wkv[:, :, 0::2, A_NOPE:], ((0, 0), (0, 0), (0, 0), (0, LANES - A_VDIM)))
    wv_o = jnp.pad(wkv[:, :, 1::2, A_NOPE:], ((0, 0), (0, 0), (0, 0), (LANES - A_VDIM, 0)))
    wv = jnp.stack([wv_e, wv_o], axis=3).reshape(depth, A_KVRANK, A_HEADS * LANES).astype(BF16)
    half = jnp.arange(A_HEADS * LANES) // A_VDIM
    vone = ((half % 4 == 1) | (half % 4 == 2)).astype(F32)[None, :]
    gsz = P_WIDTH // len(P_WINDOWS)
    w_pool_bd = jnp.zeros((depth, P_WIDTH, P_WIDTH), F32)
    for gi in range(len(P_WINDOWS)):
        w_pool_bd = w_pool_bd.at[:, gi * gsz:(gi + 1) * gsz, gi * gsz:(gi + 1) * gsz].set(w_pool[:, gi])
    w_pool_bd = w_pool_bd.astype(BF16)
    w_out16 = w_out.astype(BF16)
    w_router_t = jnp.swapaxes(w_router, 1, 2)
    gate_b_col = jnp.pad(gate_b, ((0, 0), (SMALL_GATE, LANES - SMALL_GATE - 2 * M_HEADS)))
    cq_t, ck_t, s1_t, s2_t = _rope_tables(S)
    tr = 512
    tri = (jnp.arange(tr)[:, None] < jnp.arange(tr)[None, :]).astype(BF16)

    wgu_all = w_gate_up.reshape(depth * N_EXPERTS, D_MODEL, 2 * D_FF)
    bgu_all = b_gate_up.reshape(depth * N_EXPERTS, 1, 2 * D_FF)
    wdn_all = w_down.reshape(depth * N_EXPERTS, D_FF, D_MODEL)
    bdn_all = b_down.reshape(depth * N_EXPERTS, 1, D_MODEL)

    h = x.reshape(T, D)
    moe = None
    for l in range(depth):
        h, proj = _inproj(h, norm1_g[l][None, :], w_in_p[l], moe)
        y_m = _mlstm(proj, conv_w[l], conv_b[l][None, :], gate_b_col[l][None, :],
                     mlstm_norm_g[l][None, :], B, S)
        q16, k16, v16 = _mla_prep(proj, q_norm_g[l][None, :], kv_norm_g[l][None, :], wq[l], wqs[l], wk[l], wv[l],
                                  vone, cq_t, ck_t, s1_t, s2_t, B, S)
        y_a = _attention(q16, k16, v16, B, S)
        y_p = _pool(proj, w_pool_bd[l], pool_scale[l][None, :], B, S)
        h, xn, logits_t = _outproj(y_m, y_a, y_p, h, w_out16[l], norm2_g[l][None, :],
                                   w_router_t[l], b_router[l][:, None])
        eidx, wts, rank, counts = _router(logits_t, tri, tr)
        dest, blk_e, blk_cnt, blk_next, nblk = _meta(counts, eidx, rank, nb_pad)
        dest_flat = dest.reshape(TOP_K * T)
        x_rows = _sc_dispatch(xn, dest_flat, nb * MOE_ROWS)
        y_rows = _gmm(blk_e[0], blk_cnt[0], blk_next[0], nblk[0], x_rows, wgu_all, bgu_all, wdn_all,
                      bdn_all, nb, l)
        moe = (_sc_gather(y_rows, dest_flat), wts.T)
    return _final(h, moe[0], moe[1], final_norm_g[None, :]).reshape(B, S, D)
```

```python
import functools

import jax
import jax.numpy as jnp
import numpy as np
from jax import lax
from jax.experimental import pallas as pl
from jax.experimental.pallas import tpu as pltpu
from jax.experimental.pallas import tpu_sc as plsc

F32 = jnp.float32
BF16 = jnp.bfloat16
HIGHEST = lax.Precision.HIGHEST

D_MODEL = 1024
M_HEADS = 4
M_HEAD_DIM = 64
M_WIDTH = 256
M_CONV = 4
M_CHUNK = 64
A_HEADS = 8
A_NOPE = 64
A_ROPE = 32
A_VDIM = 64
A_QRANK = 256
A_KVRANK = 128
A_WIDTH = 512
ROPE_THETA = 10000.0
P_WINDOWS = (2, 4, 8, 16)
P_WIDTH = 256
N_EXPERTS = 32
TOP_K = 4
D_FF = 1024
SWIGLU_LIMIT = 7.0
SWIGLU_ALPHA = 1.702
EPS = 1e-6

LANES = 128
SUBLANES = 8

PROJ_QKVO = 0
PROJ_CQ = 1024
PROJ_UP = 1280
PROJ_CKV = 1536
PROJ_SMALL = 1664
PROJ_WIDTH = 1792
SMALL_KR = 64
SMALL_GATE = 96

MOE_ROWS = 1024
MOE_SPLIT = 4
VMEM_LIMIT = 56 * 1024 * 1024


def _cparams(*sem):
    return pltpu.CompilerParams(dimension_semantics=sem, vmem_limit_bytes=VMEM_LIMIT)


def _sigmoid(x):
    return 1.0 / (1.0 + jnp.exp(-x))


def _log_sigmoid(x):
    return jnp.minimum(x, 0.0) - jnp.log(1.0 + jnp.exp(-jnp.abs(x)))


def _dot(a, b, **kw):
    return jnp.dot(a, b, preferred_element_type=F32, **kw)


def _dot_nt(a, b, **kw):
    return lax.dot_general(a, b, (((1,), (1,)), ((), ())), preferred_element_type=F32, **kw)


def _dot_tn(a, b, **kw):
    return lax.dot_general(a, b, (((0,), (0,)), ((), ())), preferred_element_type=F32, **kw)


def _bf16_terms(x, terms=3):
    out = []
    for _ in range(terms):
        piece = x.astype(BF16)
        out.append(piece)
        x = x - piece.astype(F32)
    return out


def _dot_sel(x, sel16, terms=3):
    return sum(_dot(p, sel16) for p in _bf16_terms(x, terms))


def _sel_dot(sel16, x, terms=3):
    return sum(_dot(sel16, p) for p in _bf16_terms(x, terms))


def _pack_bf16_pairs(x):
    n = x.shape[1] // 2
    lo = lax.bitcast_convert_type(x[:, :n].astype(BF16).astype(F32), jnp.uint32)
    hi = lax.bitcast_convert_type(x[:, n:].astype(BF16).astype(F32), jnp.uint32)
    return (lo >> 16) | (hi & jnp.uint32(0xFFFF0000))


def _unpack_bf16_pairs(p):
    lo = lax.bitcast_convert_type(p << 16, F32)
    hi = lax.bitcast_convert_type(p & jnp.uint32(0xFFFF0000), F32)
    return lo, hi


ROW_SUBTILE = 128


def _row_slices(tm):
    return [slice(r, r + ROW_SUBTILE) for r in range(0, tm, ROW_SUBTILE)]


def _moe_combine(h_ref, y_refs, w_ref, rows):
    w = w_ref[rows, :]
    dp = D_MODEL // 2
    acc_lo = h_ref[rows, :dp]
    acc_hi = h_ref[rows, dp:]
    for kk, y_ref in enumerate(y_refs):
        lo, hi = _unpack_bf16_pairs(y_ref[rows, :])
        acc_lo = acc_lo + lo * w[:, kk:kk + 1]
        acc_hi = acc_hi + hi * w[:, kk:kk + 1]
    return jnp.concatenate([acc_lo, acc_hi], axis=1)


def _combine_specs(tm, nt):
    y_specs = [pl.BlockSpec((tm, D_MODEL // 2), functools.partial(lambda i, kk: (kk * nt + i, 0), kk=kk))
               for kk in range(TOP_K)]
    return [pl.BlockSpec((tm, D_MODEL), lambda i: (i, 0))] + y_specs + [pl.BlockSpec((tm, TOP_K), lambda i: (i, 0))]


def _inproj_kernel(*refs, combine):
    if combine:
        h_ref, y0, y1, y2, y3, wk_ref, g_ref, w_ref, hn_ref, o_ref = refs
    else:
        h_ref, g_ref, w_ref, o_ref = refs
    for rows in _row_slices(h_ref.shape[0]):
        if combine:
            x = _moe_combine(h_ref, (y0, y1, y2, y3), wk_ref, rows)
            hn_ref[rows, :] = x
        else:
            x = h_ref[rows, :]
        ms = jnp.mean(x * x, axis=-1, keepdims=True)
        xn = x * lax.rsqrt(ms + EPS) * g_ref[...]
        o_ref[rows, :] = _dot(xn.astype(BF16), w_ref[...])


def _inproj(h2, g, w, moe=None, tm=512):
    T = h2.shape[0]
    nt = T // tm
    w_specs = [pl.BlockSpec((1, D_MODEL), lambda i: (0, 0)),
               pl.BlockSpec((D_MODEL, PROJ_WIDTH), lambda i: (0, 0))]
    proj_spec = pl.BlockSpec((tm, PROJ_WIDTH), lambda i: (i, 0))
    proj_shape = jax.ShapeDtypeStruct((T, PROJ_WIDTH), F32)
    if moe is None:
        return h2, pl.pallas_call(
            functools.partial(_inproj_kernel, combine=False),
            grid=(nt,),
            in_specs=[pl.BlockSpec((tm, D_MODEL), lambda i: (i, 0))] + w_specs,
            out_specs=proj_spec,
            out_shape=proj_shape,
            compiler_params=_cparams("parallel"),
            name="inproj",
        )(h2, g, w)
    y_tok, w_tok = moe
    return pl.pallas_call(
        functools.partial(_inproj_kernel, combine=True),
        grid=(nt,),
        in_specs=_combine_specs(tm, nt) + w_specs,
        out_specs=[pl.BlockSpec((tm, D_MODEL), lambda i: (i, 0)), proj_spec],
        out_shape=[jax.ShapeDtypeStruct((T, D_MODEL), F32), proj_shape],
        compiler_params=_cparams("parallel"),
        name="combine_inproj",
    )(h2, y_tok, y_tok, y_tok, y_tok, w_tok, g, w)


M_SEQS = 2
M_UNROLL = 2


def _mlstm_kernel(proj_ref, small_ref, cw_ref, cb_ref, gbc_ref, ng_ref, o_ref, ct_ref, n_ref, m_ref):
    S = proj_ref.shape[0] // M_SEQS
    L = M_CHUNK
    nc = S // L
    W = M_WIDTH
    ct_ref[...] = jnp.zeros_like(ct_ref)
    n_ref[...] = jnp.zeros_like(n_ref)
    m_ref[...] = jnp.zeros_like(m_ref)

    rh = lax.broadcasted_iota(jnp.int32, (W, W), 0) // M_HEAD_DIM
    chd = lax.broadcasted_iota(jnp.int32, (W, W), 1) // M_HEAD_DIM
    same_head = (rh == chd).astype(F32)
    same_head16 = same_head.astype(BF16)
    tril16 = (lax.broadcasted_iota(jnp.int32, (L, L), 0) >= lax.broadcasted_iota(jnp.int32, (L, L), 1)).astype(BF16)
    row = lax.broadcasted_iota(jnp.int32, (L, W), 0)
    key = lax.broadcasted_iota(jnp.int32, (L, W), 1) % M_HEAD_DIM
    causal = key <= row
    diag = (key == row).astype(F32)
    er = lax.broadcasted_iota(jnp.int32, (LANES, 2 * W), 0)
    ec = lax.broadcasted_iota(jnp.int32, (LANES, 2 * W), 1)
    spread16 = (er == SMALL_GATE + ec // M_HEAD_DIM).astype(BF16)
    forget_lane = lax.broadcasted_iota(jnp.int32, (1, LANES), 1) >= SMALL_GATE + M_HEADS
    cw = cw_ref[...]
    cb = cb_ref[...]
    gbc = gbc_ref[...]
    ng = ng_ref[...]

    def prefix_max(x):
        s = 1
        while s < L:
            x = jnp.maximum(x, jnp.where(row >= s, pltpu.roll(x, s, axis=0), -jnp.inf))
            s *= 2
        return x

    def chunk(sq, c):
        r0 = pl.multiple_of(sq * S + c * L, L)
        h0 = pl.multiple_of(sq * S + jnp.maximum(c * L - SUBLANES, 0), SUBLANES)
        halo = proj_ref[pl.ds(h0, SUBLANES), 0:2 * W] * (c > 0).astype(F32)
        win = jnp.concatenate([halo, proj_ref[pl.ds(r0, L), 0:2 * W]], axis=0)
        acc = jnp.zeros((L, 2 * W), F32) + cb
        for j in range(M_CONV):
            s = M_CONV - 1 - j
            xs = win if s == 0 else pltpu.roll(win, s, axis=0)
            acc = acc + xs[SUBLANES:, :] * cw[j:j + 1, :]
        qk = acc * _sigmoid(acc)
        q = qk[:, 0:W]
        k = qk[:, W:2 * W] * (M_HEAD_DIM ** -0.5)
        v = proj_ref[pl.ds(r0, L), 2 * W:3 * W]
        og = proj_ref[pl.ds(r0, L), 3 * W:4 * W]
        q16 = q.astype(BF16)
        k16 = k.astype(BF16)
        v16 = v.astype(BF16)
        ct = ct_ref[sq]
        nvec = n_ref[sq]
        m_old = m_ref[sq]
        pre = small_ref[pl.ds(r0, L), :] + gbc
        gates = _dot_sel(jnp.where(forget_lane, _log_sigmoid(pre), pre), spread16)
        kb = jnp.concatenate([k16] * M_HEADS, axis=0) * same_head16
        vb = jnp.concatenate([v16] * M_HEADS, axis=0) * same_head16
        qk_all = _dot_nt(q16, kb)
        q_c = _dot(q16, ct.astype(BF16))
        q_n = _dot_sel(q * nvec, same_head16, terms=2)
        yield

        i_pre = gates[:, :W]
        g = _sel_dot(tril16, gates[:, W:])
        yield
        a = i_pre - g
        a_key = jnp.sum(a * diag, axis=0, keepdims=True)
        a_max = prefix_max(a)
        m_row = g + jnp.maximum(m_old, a_max)
        p = qk_all * jnp.exp(jnp.where(causal, g + a_key, -jnp.inf) - m_row)
        p16 = p.astype(BF16)
        num_intra = _dot(p16, vb)
        rowsum = _dot_sel(p, same_head16, terms=2)
        yield

        w_inter = jnp.exp(g + m_old - m_row)
        num = w_inter * q_c + num_intra
        den = w_inter * q_n + rowsum
        hv = num / jnp.maximum(jnp.abs(den), jnp.exp(-m_row))
        ms = _dot_sel(hv * hv, same_head16, terms=2) * (1.0 / M_HEAD_DIM)
        yield
        y = hv * lax.rsqrt(ms + EPS) * ng * _sigmoid(og)
        o_ref[pl.ds(r0, L), :] = y

        g_end = g[L - 1:L, :]
        m_new = g_end + jnp.maximum(m_old, a_max[L - 1:L, :])
        wa = jnp.exp(g_end + a - m_new)
        decay = jnp.exp(g_end + m_old - m_new)
        upd = _dot_tn(k16, (wa * v).astype(BF16))
        ct_ref[sq] = decay * ct + upd * same_head
        n_ref[sq] = decay * nvec + jnp.sum(wa * k, axis=0, keepdims=True)
        m_ref[sq] = m_new
        yield

    def body(cc, carry):
        for u in range(M_UNROLL):
            for _ in zip(*[chunk(sq, cc * M_UNROLL + u) for sq in range(M_SEQS)]):
                pass
        return carry

    lax.fori_loop(0, nc // M_UNROLL, body, 0)


def _mlstm(proj, cw, cb, gbc, ng, B, S):
    assert M_CHUNK == M_HEAD_DIM
    T = B * S
    rows = M_SEQS * S
    return pl.pallas_call(
        _mlstm_kernel,
        grid=(B // M_SEQS,),
        in_specs=[pl.BlockSpec((rows, 4 * M_WIDTH), lambda b: (b, 0)),
                  pl.BlockSpec((rows, LANES), lambda b: (b, PROJ_SMALL // LANES)),
                  pl.BlockSpec((M_CONV, 2 * M_WIDTH), lambda b: (0, 0)),
                  pl.BlockSpec((1, 2 * M_WIDTH), lambda b: (0, 0)),
                  pl.BlockSpec((1, LANES), lambda b: (0, 0)),
                  pl.BlockSpec((1, M_WIDTH), lambda b: (0, 0))],
        out_specs=pl.BlockSpec((rows, M_WIDTH), lambda b: (b, 0)),
        out_shape=jax.ShapeDtypeStruct((T, M_WIDTH), F32),
        scratch_shapes=[pltpu.VMEM((M_SEQS, M_WIDTH, M_WIDTH), F32),
                        pltpu.VMEM((M_SEQS, 1, M_WIDTH), F32),
                        pltpu.VMEM((M_SEQS, 1, M_WIDTH), F32)],
        compiler_params=_cparams("parallel"),
        name="mlstm",
    )(proj, proj, cw, cb, gbc, ng)


POOL_HALO = 16
POOL_TILE = 256


def _pool_kernel(u_ref, w_ref, sc_ref, o_ref, upad_ref):
    S = u_ref.shape[0]
    upad_ref[0:POOL_HALO, :] = jnp.zeros((POOL_HALO, P_WIDTH), F32)
    upad_ref[POOL_HALO:, :] = u_ref[...]
    grp = lax.broadcasted_iota(jnp.int32, (1, P_WIDTH), 1) // (P_WIDTH // len(P_WINDOWS))
    win_lane = jnp.zeros((1, P_WIDTH), jnp.int32)
    for gi, wn in enumerate(P_WINDOWS):
        win_lane = jnp.where(grp == gi, wn, win_lane)
    w = w_ref[...]
    scale = sc_ref[...]
    rows = POOL_TILE + POOL_HALO

    def body(r, carry):
        r0 = pl.multiple_of(r * POOL_TILE, POOL_TILE)
        a = upad_ref[pl.ds(r0, rows), :]
        sums = []
        cur = a
        span = 1
        for _ in P_WINDOWS:
            cur = cur + pltpu.roll(cur, span, axis=0)
            span *= 2
            sums.append(cur)
        sel = sums[-1]
        for gi in range(len(P_WINDOWS) - 1):
            sel = jnp.where(grp == gi, sums[gi], sel)
        sel = sel[POOL_HALO:, :]
        u = a[POOL_HALO:, :]
        t = r0 + lax.broadcasted_iota(jnp.int32, (POOL_TILE, P_WIDTH), 0)
        cnt = jnp.minimum(t + 1, win_lane).astype(F32)
        pooled = sel / cnt - u
        o_ref[pl.ds(r0, POOL_TILE), :] = _dot(pooled.astype(BF16), w) * scale
        return carry

    lax.fori_loop(0, S // POOL_TILE, body, 0)


def _pool(proj, w_bd, scale, B, S):
    T = B * S
    return pl.pallas_call(
        _pool_kernel,
        grid=(B,),
        in_specs=[pl.BlockSpec((S, P_WIDTH), lambda b: (b, PROJ_UP // P_WIDTH)),
                  pl.BlockSpec((P_WIDTH, P_WIDTH), lambda b: (0, 0)),
                  pl.BlockSpec((1, P_WIDTH), lambda b: (0, 0))],
        out_specs=pl.BlockSpec((S, P_WIDTH), lambda b: (b, 0)),
        out_shape=jax.ShapeDtypeStruct((T, P_WIDTH), F32),
        scratch_shapes=[pltpu.VMEM((S + POOL_HALO, P_WIDTH), F32)],
        compiler_params=_cparams("parallel"),
        name="pool",
    )(proj, w_bd, scale)


def _rope(x, c, s1, s2):
    return x * c + pltpu.roll(x, LANES - A_ROPE // 2, axis=1) * s1 + pltpu.roll(x, A_ROPE // 2, axis=1) * s2


def _mla_prep_kernel(cq_ref, ckv_ref, small_ref, qg_ref, kvg_ref, wq_ref, wqs_ref, wk_ref, wv_ref,
                     vone_ref, cq_t_ref, ck_t_ref, s1_ref, s2_ref, q_ref, k_ref, v_ref):
    def rms(x, g):
        return x * lax.rsqrt(jnp.mean(x * x, axis=-1, keepdims=True) + EPS) * g

    cqn = rms(cq_ref[...], qg_ref[...]).astype(BF16)
    ckvn = rms(ckv_ref[...], kvg_ref[...]).astype(BF16)
    scale = (A_NOPE + A_ROPE) ** -0.5
    qf = _dot(cqn, wq_ref[...]) * scale
    qp = _dot(cqn, wqs_ref[...]) * scale
    kf = _dot(ckvn, wk_ref[...])
    v_ref[...] = (_dot(ckvn, wv_ref[...]) + vone_ref[...]).astype(BF16)
    cqt = cq_t_ref[...]
    s1 = s1_ref[...]
    s2 = s2_ref[...]
    krot = _rope(small_ref[...], ck_t_ref[...], s1, s2)
    sq = s1 + s2
    for h in range(A_HEADS):
        sl = slice(h * LANES, (h + 1) * LANES)
        q_ref[:, sl] = (qf[:, sl] * cqt + qp[:, sl] * sq).astype(BF16)
        k_ref[:, sl] = (kf[:, sl] + krot).astype(BF16)


def _mla_prep(proj, qg, kvg, wq, wqs, wk, wv, vone, cq_t, ck_t, s1_t, s2_t, B, S, ts=1024):
    T = B * S
    nst = S // ts
    hw = A_HEADS * LANES
    return pl.pallas_call(
        _mla_prep_kernel,
        grid=(B, nst),
        in_specs=[pl.BlockSpec((ts, A_QRANK), lambda b, s: (b * nst + s, PROJ_CQ // A_QRANK)),
                  pl.BlockSpec((ts, A_KVRANK), lambda b, s: (b * nst + s, PROJ_CKV // A_KVRANK)),
                  pl.BlockSpec((ts, LANES), lambda b, s: (b * nst + s, PROJ_SMALL // LANES)),
                  pl.BlockSpec((1, A_QRANK), lambda b, s: (0, 0)),
                  pl.BlockSpec((1, A_KVRANK), lambda b, s: (0, 0)),
                  pl.BlockSpec((A_QRANK, hw), lambda b, s: (0, 0)),
                  pl.BlockSpec((A_QRANK, hw), lambda b, s: (0, 0)),
                  pl.BlockSpec((A_KVRANK, hw), lambda b, s: (0, 0)),
                  pl.BlockSpec((A_KVRANK, hw), lambda b, s: (0, 0)),
                  pl.BlockSpec((1, hw), lambda b, s: (0, 0)),
                  pl.BlockSpec((ts, LANES), lambda b, s: (s, 0)),
                  pl.BlockSpec((ts, LANES), lambda b, s: (s, 0)),
                  pl.BlockSpec((ts, LANES), lambda b, s: (s, 0)),
                  pl.BlockSpec((ts, LANES), lambda b, s: (s, 0))],
        out_specs=[pl.BlockSpec((ts, hw), lambda b, s: (b * nst + s, 0)),
                   pl.BlockSpec((ts, hw), lambda b, s: (b * nst + s, 0)),
                   pl.BlockSpec((ts, hw), lambda b, s: (b * nst + s, 0))],
        out_shape=[jax.ShapeDtypeStruct((T, hw), BF16),
                   jax.ShapeDtypeStruct((T, hw), BF16),
                   jax.ShapeDtypeStruct((T, hw), BF16)],
        compiler_params=_cparams("parallel", "parallel"),
        name="mla_prep",
    )(proj, proj, proj, qg, kvg, wq, wqs, wk, wv, vone, cq_t, ck_t, s1_t, s2_t)


def _attn_kernel(q_ref, k_ref, v_ref, o_ref, *, tq):
    heads = range(2)
    sls = [slice(hh * LANES, (hh + 1) * LANES) for hh in heads]
    lane = lax.broadcasted_iota(jnp.int32, (tq, LANES), 1)
    below_diag = lax.broadcasted_iota(jnp.int32, (tq, tq), 0) >= lax.broadcasted_iota(jnp.int32, (tq, tq), 1)

    def update(qs, k0, state, causal):
        keys = slice(k0, k0 + tq)
        s = [_dot_nt(qs[hh], k_ref[keys, sls[hh]]) for hh in heads]
        if causal:
            s = [jnp.where(below_diag, s[hh], -jnp.inf) for hh in heads]
        m_new = [jnp.maximum(state[hh][0], jnp.max(s[hh], axis=-1, keepdims=True)) for hh in heads]
        p = [jnp.exp((s[hh] - m_new[hh]).astype(BF16)) for hh in heads]
        pv = [_dot(p[hh], v_ref[keys, sls[hh]]) for hh in heads]
        acc = [jnp.exp(state[hh][0] - m_new[hh]) * state[hh][1] + pv[hh] for hh in heads]
        return tuple((m_new[hh], acc[hh]) for hh in heads)

    for qi in range(q_ref.shape[0] // tq):
        rows = slice(qi * tq, (qi + 1) * tq)
        qs = [q_ref[rows, sl] for sl in sls]
        state = tuple((jnp.full((tq, 1), -jnp.inf, F32), jnp.zeros((tq, LANES), F32)) for _ in heads)
        for kb in range(qi + 1):
            state = update(qs, kb * tq, state, causal=(kb == qi))
        (_, acc0), (_, acc1) = state
        acc = jnp.where(lane < A_VDIM, acc0, acc1)
        den = jnp.where(lane < A_VDIM, pltpu.roll(acc0, A_VDIM, axis=1), pltpu.roll(acc1, A_VDIM, axis=1))
        o_ref[rows, :] = (acc / den).astype(o_ref.dtype)


def _attention(q, k, v, B, S, tq=512):
    T = B * S
    return pl.pallas_call(
        functools.partial(_attn_kernel, tq=tq),
        grid=(B, A_HEADS // 2),
        in_specs=[pl.BlockSpec((S, 2 * LANES), lambda b, p: (b, p)),
                  pl.BlockSpec((S, 2 * LANES), lambda b, p: (b, p)),
                  pl.BlockSpec((S, 2 * LANES), lambda b, p: (b, p))],
        out_specs=pl.BlockSpec((S, LANES), lambda b, p: (b, p)),
        out_shape=jax.ShapeDtypeStruct((T, A_WIDTH), BF16),
        compiler_params=_cparams("parallel", "parallel"),
        name="attention",
    )(q, k, v)


def _outproj_kernel(ym_ref, ya_ref, yp_ref, h_ref, w_ref, g_ref, wr_ref, br_ref,
                    hn_ref, xn_ref, lg_ref):
    mix = _dot(ym_ref[...].astype(BF16), w_ref[0:M_WIDTH, :])
    mix = mix + _dot(ya_ref[...], w_ref[M_WIDTH:M_WIDTH + A_WIDTH, :])
    mix = mix + _dot(yp_ref[...].astype(BF16), w_ref[M_WIDTH + A_WIDTH:, :])
    hn = h_ref[...] + mix
    hn_ref[...] = hn
    xn = hn * lax.rsqrt(jnp.mean(hn * hn, axis=-1, keepdims=True) + EPS) * g_ref[...]
    x_hi = xn.astype(BF16)
    x_lo = (xn - x_hi.astype(F32)).astype(BF16)
    wr = wr_ref[...]
    w_hi = wr.astype(BF16)
    w_lo = (wr - w_hi.astype(F32)).astype(BF16)
    lg_ref[...] = _dot_nt(w_hi, x_hi) + _dot_nt(w_hi, x_lo) + _dot_nt(w_lo, x_hi) + br_ref[...]
    xn_ref[...] = _pack_bf16_pairs(xn)


def _outproj(ym, ya, yp, h2, w, g, wr_t, br, tm=1024):
    T = h2.shape[0]
    return pl.pallas_call(
        _outproj_kernel,
        grid=(T // tm,),
        in_specs=[pl.BlockSpec((tm, M_WIDTH), lambda i: (i, 0)),
                  pl.BlockSpec((tm, A_WIDTH), lambda i: (i, 0)),
                  pl.BlockSpec((tm, P_WIDTH), lambda i: (i, 0)),
                  pl.BlockSpec((tm, D_MODEL), lambda i: (i, 0)),
                  pl.BlockSpec((D_MODEL, D_MODEL), lambda i: (0, 0)),
                  pl.BlockSpec((1, D_MODEL), lambda i: (0, 0)),
                  pl.BlockSpec((N_EXPERTS, D_MODEL), lambda i: (0, 0)),
                  pl.BlockSpec((N_EXPERTS, 1), lambda i: (0, 0))],
        out_specs=[pl.BlockSpec((tm, D_MODEL), lambda i: (i, 0)),
                   pl.BlockSpec((tm, D_MODEL // 2), lambda i: (i, 0)),
                   pl.BlockSpec((N_EXPERTS, tm), lambda i: (0, i))],
        out_shape=[jax.ShapeDtypeStruct((T, D_MODEL), F32),
                   jax.ShapeDtypeStruct((T, D_MODEL // 2), jnp.uint32),
                   jax.ShapeDtypeStruct((N_EXPERTS, T), F32)],
        compiler_params=_cparams("parallel"),
        name="outproj",
    )(ym, ya, yp, h2, w, g, wr_t, br)


def _router_kernel(lg_ref, tri_ref, e_ref, w_ref, r_ref, cnt_ref, carry_ref):
    tr = lg_ref.shape[1]

    @pl.when(pl.program_id(0) == 0)
    def _():
        carry_ref[...] = jnp.zeros_like(carry_ref)

    x = lg_ref[...]
    eio = lax.broadcasted_iota(jnp.int32, (N_EXPERTS, tr), 0).astype(F32)
    picked = jnp.zeros((N_EXPERTS, tr), F32)
    vals = []
    idxs = []
    for _ in range(TOP_K):
        mx = jnp.max(x, axis=0, keepdims=True)
        idx = jnp.min(jnp.where(x == mx, eio, float(N_EXPERTS)), axis=0, keepdims=True)
        hit = eio == idx
        vals.append(mx)
        idxs.append(idx)
        picked = picked + hit.astype(F32)
        x = jnp.where(hit, -jnp.inf, x)
    exps = [jnp.exp(vv - vals[0]) for vv in vals]
    tot = exps[0] + exps[1] + exps[2] + exps[3]
    before = _dot(picked.astype(BF16), tri_ref[...]) + carry_ref[:, 0:1]
    for kk in range(TOP_K):
        e_ref[kk:kk + 1, :] = idxs[kk].astype(jnp.int32)
        w_ref[kk:kk + 1, :] = exps[kk] / tot
        rk = jnp.sum(jnp.where(eio == idxs[kk], before, 0.0), axis=0, keepdims=True)
        r_ref[kk:kk + 1, :] = rk.astype(jnp.int32)
    carry_ref[...] = carry_ref[...] + jnp.sum(picked, axis=1, keepdims=True)
    cnt_ref[...] = carry_ref[...]


def _router(logits_t, tri, tr=512):
    T = logits_t.shape[1]
    return pl.pallas_call(
        _router_kernel,
        grid=(T // tr,),
        in_specs=[pl.BlockSpec((N_EXPERTS, tr), lambda i: (0, i)),
                  pl.BlockSpec((tr, tr), lambda i: (0, 0))],
        out_specs=[pl.BlockSpec((TOP_K, tr), lambda i: (0, i)),
                   pl.BlockSpec((TOP_K, tr), lambda i: (0, i)),
                   pl.BlockSpec((TOP_K, tr), lambda i: (0, i)),
                   pl.BlockSpec((N_EXPERTS, LANES), lambda i: (0, 0))],
        out_shape=[jax.ShapeDtypeStruct((TOP_K, T), jnp.int32),
                   jax.ShapeDtypeStruct((TOP_K, T), F32),
                   jax.ShapeDtypeStruct((TOP_K, T), jnp.int32),
                   jax.ShapeDtypeStruct((N_EXPERTS, LANES), F32)],
        scratch_shapes=[pltpu.VMEM((N_EXPERTS, LANES), F32)],
        compiler_params=_cparams("arbitrary"),
        name="router",
    )(logits_t, tri)


def _meta_kernel(cnt_ref, e_ref, r_ref, dest_ref, be_ref, bc_ref, bn_ref, nb_ref, *, nb_pad):
    cnt = cnt_ref[...]
    padded = jnp.floor((cnt + (MOE_ROWS - 1)) * (1.0 / MOE_ROWS)) * MOE_ROWS
    ri = lax.broadcasted_iota(jnp.int32, (N_EXPERTS, N_EXPERTS), 0)
    ci = lax.broadcasted_iota(jnp.int32, (N_EXPERTS, N_EXPERTS), 1)
    pad_end = _dot((ri >= ci).astype(F32), padded, precision=HIGHEST)
    pad_start = pad_end - padded
    e = e_ref[...]
    dest = r_ref[...]
    for ex in range(N_EXPERTS):
        ps = pad_start[ex:ex + 1, 0:1].astype(jnp.int32)
        dest = jnp.where(e == ex, dest + ps, dest)
    dest_ref[...] = dest
    blk0 = (lax.broadcasted_iota(jnp.int32, (N_EXPERTS, nb_pad), 1) * MOE_ROWS).astype(F32)
    be = jnp.sum((pad_end[:, 0:1] <= blk0).astype(F32), axis=0, keepdims=True)
    be = jnp.minimum(be, float(N_EXPERTS - 1))
    eio = lax.broadcasted_iota(jnp.int32, (N_EXPERTS, nb_pad), 0).astype(F32)
    seg_end = jnp.sum(jnp.where(eio == be, pad_start[:, 0:1] + cnt[:, 0:1], 0.0), axis=0, keepdims=True)
    bc = jnp.clip(seg_end - blk0[0:1, :], 0.0, float(MOE_ROWS))
    nxt0 = jnp.sum(jnp.where(eio == be, pad_end[:, 0:1], 0.0), axis=0, keepdims=True)
    bn = jnp.sum((pad_end[:, 0:1] <= nxt0).astype(F32), axis=0, keepdims=True)
    bn = jnp.where(nxt0 < pad_end[N_EXPERTS - 1:N_EXPERTS, 0:1], bn, -1.0)
    be_ref[...] = be.astype(jnp.int32)
    bc_ref[...] = bc.astype(jnp.int32)
    bn_ref[...] = bn.astype(jnp.int32)
    nb_ref[...] = (pad_end[N_EXPERTS - 1:N_EXPERTS, :] * (1.0 / MOE_ROWS)).astype(jnp.int32)


def _meta(counts, eidx, rank, nb_pad):
    T = eidx.shape[1]
    return pl.pallas_call(
        functools.partial(_meta_kernel, nb_pad=nb_pad),
        out_shape=[jax.ShapeDtypeStruct((TOP_K, T), jnp.int32),
                   jax.ShapeDtypeStruct((1, nb_pad), jnp.int32),
                   jax.ShapeDtypeStruct((1, nb_pad), jnp.int32),
                   jax.ShapeDtypeStruct((1, nb_pad), jnp.int32),
                   jax.ShapeDtypeStruct((1, LANES), jnp.int32)],
        compiler_params=pltpu.CompilerParams(vmem_limit_bytes=VMEM_LIMIT),
        name="route_meta",
    )(counts, eidx, rank)


FF_CHUNK = 512


def _gmm_kernel(be_ref, bc_ref, bn_ref, nb_ref, x_ref, wgu_hbm, bgu_ref, wdn_hbm, bdn_ref, y_ref,
                wgu_st, wdn_st, wgu16, wdn16, sem, *, e0):
    i = pl.program_id(0)
    nblk = nb_ref[0]
    bm = MOE_ROWS

    def weight_copies(e):
        return (pltpu.make_async_copy(wgu_hbm.at[e0 + e], wgu_st, sem.at[0]),
                pltpu.make_async_copy(wdn_hbm.at[e0 + e], wdn_st, sem.at[1]))

    @pl.when(i == 0)
    def _():
        for cp in weight_copies(be_ref[0]):
            cp.start()

    @pl.when(i >= nblk)
    def _():
        y_ref[...] = jnp.zeros_like(y_ref)

    @pl.when(i < nblk)
    def _():
        e_changed = jnp.logical_or(i == 0, be_ref[i] != be_ref[jnp.maximum(i - 1, 0)])

        @pl.when(e_changed)
        def _():
            for cp in weight_copies(be_ref[i]):
                cp.wait()
            wgu16[...] = wgu_st[...].astype(BF16)
            wdn16[...] = wdn_st[...].astype(BF16)

            @pl.when(bn_ref[i] >= 0)
            def _():
                for cp in weight_copies(bn_ref[i]):
                    cp.start()

        def expert_rows(nrows):
            valid = lax.broadcasted_iota(jnp.int32, (nrows, 1), 0) < bc_ref[i]
            lo, hi = _unpack_bf16_pairs(jnp.where(valid, x_ref[0:nrows, :], jnp.uint32(0)))
            x16 = jnp.concatenate([lo.astype(BF16), hi.astype(BF16)], axis=1)
            acc = jnp.zeros((nrows, D_MODEL), F32) + bdn_ref[0]
            for c in range(D_FF // FF_CHUNK):
                cs = slice(c * FF_CHUNK, (c + 1) * FF_CHUNK)
                us = slice(D_FF + c * FF_CHUNK, D_FF + (c + 1) * FF_CHUNK)
                gate = _dot(x16, wgu16[:, cs]) + bgu_ref[0, :, cs]
                up = _dot(x16, wgu16[:, us]) + bgu_ref[0, :, us]
                gate = jnp.minimum(gate, SWIGLU_LIMIT)
                up = jnp.clip(up, -SWIGLU_LIMIT, SWIGLU_LIMIT)
                act = (up + 1.0) * gate * _sigmoid(SWIGLU_ALPHA * gate)
                acc = acc + _dot(act.astype(BF16), wdn16[cs, :])
            y_ref[0:nrows, :] = _pack_bf16_pairs(acc)

        quarter = bm // MOE_SPLIT
        for nq in range(1, MOE_SPLIT + 1):
            lower = (nq - 1) * quarter if nq > 1 else -1
            @pl.when(jnp.logical_and(bc_ref[i] > lower, bc_ref[i] <= nq * quarter))
            def _(nq=nq):
                expert_rows(nq * quarter)
                if nq < MOE_SPLIT:
                    y_ref[nq * quarter:, :] = jnp.zeros((bm - nq * quarter, D_MODEL // 2), jnp.uint32)


def _gmm(blk_e, blk_cnt, blk_next, nblk, x_rows, wgu, bgu, wdn, bdn, nb, layer):
    bm = MOE_ROWS
    e0 = layer * N_EXPERTS
    dp = D_MODEL // 2

    def expert(i, be, nb_ref):
        return (e0 + be[jnp.minimum(i, jnp.maximum(nb_ref[0] - 1, 0))], 0, 0)

    def rows(i, nb_ref):
        return (jnp.minimum(i, jnp.maximum(nb_ref[0] - 1, 0)), 0)

    grid_spec = pltpu.PrefetchScalarGridSpec(
        num_scalar_prefetch=4,
        grid=(nb,),
        in_specs=[pl.BlockSpec((bm, dp), lambda i, be, bc, bn, nbr: rows(i, nbr)),
                  pl.BlockSpec(memory_space=pl.ANY),
                  pl.BlockSpec((1, 1, 2 * D_FF), lambda i, be, bc, bn, nbr: expert(i, be, nbr)),
                  pl.BlockSpec(memory_space=pl.ANY),
                  pl.BlockSpec((1, 1, D_MODEL), lambda i, be, bc, bn, nbr: expert(i, be, nbr))],
        out_specs=pl.BlockSpec((bm, dp), lambda i, be, bc, bn, nbr: (i, 0)),
        scratch_shapes=[pltpu.VMEM((D_MODEL, 2 * D_FF), F32),
                        pltpu.VMEM((D_FF, D_MODEL), F32),
                        pltpu.VMEM((D_MODEL, 2 * D_FF), BF16),
                        pltpu.VMEM((D_FF, D_MODEL), BF16),
                        pltpu.SemaphoreType.DMA((2,))],
    )
    return pl.pallas_call(
        functools.partial(_gmm_kernel, e0=e0),
        grid_spec=grid_spec,
        out_shape=jax.ShapeDtypeStruct((nb * bm, dp), jnp.uint32),
        compiler_params=_cparams("arbitrary"),
        name="expert_gmm",
    )(blk_e, blk_cnt, blk_next, nblk, x_rows, wgu, bgu, wdn, bdn)


def _final_kernel(h_ref, y0, y1, y2, y3, wk_ref, g_ref, o_ref):
    for rows in _row_slices(h_ref.shape[0]):
        x = _moe_combine(h_ref, (y0, y1, y2, y3), wk_ref, rows)
        o_ref[rows, :] = x * lax.rsqrt(jnp.mean(x * x, axis=-1, keepdims=True) + EPS) * g_ref[...]


def _final(h2, y_tok, w_tok, g, tm=512):
    T = h2.shape[0]
    nt = T // tm
    return pl.pallas_call(
        _final_kernel,
        grid=(nt,),
        in_specs=_combine_specs(tm, nt) + [pl.BlockSpec((1, D_MODEL), lambda i: (0, 0))],
        out_specs=pl.BlockSpec((tm, D_MODEL), lambda i: (i, 0)),
        out_shape=jax.ShapeDtypeStruct((T, D_MODEL), F32),
        compiler_params=_cparams("parallel"),
        name="combine_final_norm",
    )(h2, y_tok, y_tok, y_tok, y_tok, w_tok, g)


SC_CORES = 2
SC_SUBCORES = 16
SC_LANES = 16
SC_WORKERS = SC_CORES * SC_SUBCORES
SC_WINDOW = 64


def _sc_mesh():
    return plsc.VectorSubcoreMesh(core_axis_name="c", subcore_axis_name="s")


def _sc_worker():
    return lax.axis_index("s") * SC_CORES + lax.axis_index("c")


def _sc_dispatch(xn, dest_flat, n_rows):
    T = xn.shape[0]
    tpw = T // SC_WORKERS
    nchunk = tpw // SC_WINDOW
    nvec = SC_WINDOW // SC_LANES

    @functools.partial(
        pl.kernel, out_type=jax.ShapeDtypeStruct((n_rows, xn.shape[1]), xn.dtype), mesh=_sc_mesh(),
        scratch_types=[pltpu.VMEM((TOP_K * tpw,), jnp.int32),
                       pltpu.VMEM((SC_WINDOW, xn.shape[1]), xn.dtype),
                       pltpu.VMEM((SC_WINDOW, xn.shape[1]), xn.dtype),
                       pltpu.SemaphoreType.DMA, pltpu.SemaphoreType.DMA, pltpu.SemaphoreType.DMA],
        name="sc_dispatch")
    def run(x_hbm, d_hbm, o_hbm, idx_v, buf0, buf1, sem0, sem1, sem_out):
        base = _sc_worker() * tpw
        for kk in range(TOP_K):
            pltpu.sync_copy(d_hbm.at[pl.ds(kk * T + base, tpw)], idx_v.at[pl.ds(kk * tpw, tpw)])
        bufs = (buf0, buf1)
        sems = (sem0, sem1)

        def load(c, slot):
            return pltpu.make_async_copy(x_hbm.at[pl.ds(base + c * SC_WINDOW, SC_WINDOW)], bufs[slot], sems[slot])

        load(0, 0).start()

        @pl.loop(0, nchunk, step=2)
        def _(c0):
            for slot in range(2):
                c = c0 + slot
                load(c, slot).wait()

                @pl.when(c + 1 < nchunk)
                def _():
                    load(c + 1, 1 - slot).start()

                copies = []
                for kk in range(TOP_K):
                    for q in range(nvec):
                        off = pl.multiple_of(kk * tpw + c * SC_WINDOW + q * SC_LANES, SC_LANES)
                        rows = idx_v[pl.ds(off, SC_LANES)]
                        cp = pltpu.make_async_copy(bufs[slot].at[pl.ds(q * SC_LANES, SC_LANES)],
                                                   o_hbm.at[rows], sem_out)
                        cp.start()
                        copies.append(cp)
                for cp in copies:
                    cp.wait()

    return run(xn, dest_flat)


def _sc_gather(y_rows, dest_flat):
    n = dest_flat.shape[0]
    rpw = n // SC_WORKERS
    nchunk = rpw // SC_WINDOW
    nvec = SC_WINDOW // SC_LANES

    @functools.partial(
        pl.kernel, out_type=jax.ShapeDtypeStruct((n, y_rows.shape[1]), y_rows.dtype), mesh=_sc_mesh(),
        scratch_types=[pltpu.VMEM((rpw,), jnp.int32),
                       pltpu.VMEM((SC_WINDOW, y_rows.shape[1]), y_rows.dtype),
                       pltpu.VMEM((SC_WINDOW, y_rows.shape[1]), y_rows.dtype),
                       pltpu.SemaphoreType.DMA, pltpu.SemaphoreType.DMA, pltpu.SemaphoreType.DMA],
        name="sc_gather")
    def run(y_hbm, d_hbm, o_hbm, idx_v, buf0, buf1, sem0, sem1, sem_in):
        base = _sc_worker() * rpw
        pltpu.sync_copy(d_hbm.at[pl.ds(base, rpw)], idx_v)
        bufs = (buf0, buf1)
        sems = (sem0, sem1)

        def store(c, slot):
            return pltpu.make_async_copy(bufs[slot], o_hbm.at[pl.ds(base + c * SC_WINDOW, SC_WINDOW)], sems[slot])

        @pl.loop(0, nchunk, step=2)
        def _(c0):
            for slot in range(2):
                c = c0 + slot

                @pl.when(c >= 2)
                def _():
                    store(c - 2, slot).wait()

                copies = []
                for q in range(nvec):
                    off = pl.multiple_of(c * SC_WINDOW + q * SC_LANES, SC_LANES)
                    rows = idx_v[pl.ds(off, SC_LANES)]
                    cp = pltpu.make_async_copy(y_hbm.at[rows], bufs[slot].at[pl.ds(q * SC_LANES, SC_LANES)], sem_in)
                    cp.start()
                    copies.append(cp)
                for cp in copies:
                    cp.wait()
                store(c, slot).start()

        store(nchunk - 2, 0).wait()
        store(nchunk - 1, 1).wait()

    return run(y_rows, dest_flat)


def _prep_w_in(w_in):
    o_g = 4 * M_WIDTH
    o_cq = o_g + 2 * M_HEADS
    o_ckv = o_cq + A_QRANK
    o_kr = o_ckv + A_KVRANK
    o_up = o_kr + A_ROPE
    z = lambda n: jnp.zeros(w_in.shape[:-1] + (n,), w_in.dtype)
    small = jnp.concatenate([z(SMALL_KR), w_in[..., o_kr:o_up], w_in[..., o_g:o_cq],
                             z(LANES - SMALL_GATE - 2 * M_HEADS)], axis=-1)
    return jnp.concatenate([w_in[..., 0:o_g], w_in[..., o_cq:o_ckv], w_in[..., o_up:o_up + P_WIDTH],
                            w_in[..., o_ckv:o_kr], small], axis=-1).astype(BF16)


def _rope_tables(seq):
    inv = ROPE_THETA ** (-jnp.arange(0, A_ROPE, 2, dtype=F32) / A_ROPE)
    ang = jnp.arange(seq, dtype=F32)[:, None] * inv[None, :]
    cos, sin = jnp.cos(ang), jnp.sin(ang)
    half = A_ROPE // 2
    zeros = lambda n: jnp.zeros((seq, n), F32)
    ones = lambda n: jnp.ones((seq, n), F32)
    tail = LANES - A_NOPE - A_ROPE
    cq_t = jnp.concatenate([ones(A_NOPE), cos, cos, zeros(tail)], axis=1)
    ck_t = jnp.concatenate([zeros(A_NOPE), cos, cos, zeros(tail)], axis=1)
    s1_t = jnp.concatenate([zeros(A_NOPE), -sin, zeros(half), zeros(tail)], axis=1)
    s2_t = jnp.concatenate([zeros(A_NOPE), zeros(half), sin, zeros(tail)], axis=1)
    return cq_t, ck_t, s1_t, s2_t


def kernel(x, norm1_g, w_in, conv_w, conv_b, gate_b, mlstm_norm_g, q_norm_g, kv_norm_g, w_uq, w_ukv,
           w_pool, pool_scale, w_out, norm2_g, w_router, b_router, w_gate_up, b_gate_up, w_down, b_down,
           final_norm_g):
    B, S, D = x.shape
    depth = w_in.shape[0]
    T = B * S
    nb = (T * TOP_K) // MOE_ROWS + N_EXPERTS
    nb_pad = -(-nb // LANES) * LANES

    w_in_p = _prep_w_in(w_in)
    wq = w_uq.reshape(depth, A_QRANK, A_HEADS, A_NOPE + A_ROPE)
    wq = jnp.pad(wq, ((0, 0), (0, 0), (0, 0), (0, LANES - A_NOPE - A_ROPE)))
    r0, r1, r2 = A_NOPE, A_NOPE + A_ROPE // 2, A_NOPE + A_ROPE
    wqs = jnp.concatenate([jnp.zeros_like(wq[..., :r0]), wq[..., r1:r2], wq[..., r0:r1],
                           jnp.zeros_like(wq[..., r2:])], axis=-1)
    wq = wq.reshape(depth, A_QRANK, A_HEADS * LANES).astype(BF16)
    wqs = wqs.reshape(depth, A_QRANK, A_HEADS * LANES).astype(BF16)
    wkv = w_ukv.reshape(depth, A_KVRANK, A_HEADS, A_NOPE + A_VDIM)
    wk = jnp.pad(wkv[..., :A_NOPE], ((0, 0), (0, 0), (0, 0), (0, LANES - A_NOPE)))
    wk = wk.reshape(depth, A_KVRANK, A_HEADS * LANES).astype(BF16)
    wv_e = jnp.pad(wkv[:, :, 0::2, A_NOPE:], ((0, 0), (0, 0), (0, 0), (0, LANES - A_VDIM)))
    wv_o = jnp.pad(wkv[:, :, 1::2, A_NOPE:], ((0, 0), (0, 0), (0, 0), (LANES - A_VDIM, 0)))
    wv = jnp.stack([wv_e, wv_o], axis=3).reshape(depth, A_KVRANK, A_HEADS * LANES).astype(BF16)
    half = jnp.arange(A_HEADS * LANES) // A_VDIM
    vone = ((half % 4 == 1) | (half % 4 == 2)).astype(F32)[None, :]
    gsz = P_WIDTH // len(P_WINDOWS)
    w_pool_bd = jnp.zeros((depth, P_WIDTH, P_WIDTH), F32)
    for gi in range(len(P_WINDOWS)):
        w_pool_bd = w_pool_bd.at[:, gi * gsz:(gi + 1) * gsz, gi * gsz:(gi + 1) * gsz].set(w_pool[:, gi])
    w_pool_bd = w_pool_bd.astype(BF16)
    w_out16 = w_out.astype(BF16)
    w_router_t = jnp.swapaxes(w_router, 1, 2)
    gate_b_col = jnp.pad(gate_b, ((0, 0), (SMALL_GATE, LANES - SMALL_GATE - 2 * M_HEADS)))
    cq_t, ck_t, s1_t, s2_t = _rope_tables(S)
    tr = 1024
    tri = (jnp.arange(tr)[:, None] < jnp.arange(tr)[None, :]).astype(BF16)

    wgu_all = w_gate_up.reshape(depth * N_EXPERTS, D_MODEL, 2 * D_FF)
    bgu_all = b_gate_up.reshape(depth * N_EXPERTS, 1, 2 * D_FF)
    wdn_all = w_down.reshape(depth * N_EXPERTS, D_FF, D_MODEL)
    bdn_all = b_down.reshape(depth * N_EXPERTS, 1, D_MODEL)

    h = x.reshape(T, D)
    moe = None
    for l in range(depth):
        h, proj = _inproj(h, norm1_g[l][None, :], w_in_p[l], moe)
        y_m = _mlstm(proj, conv_w[l], conv_b[l][None, :], gate_b_col[l][None, :],
                     mlstm_norm_g[l][None, :], B, S)
        q16, k16, v16 = _mla_prep(proj, q_norm_g[l][None, :], kv_norm_g[l][None, :], wq[l], wqs[l], wk[l], wv[l],
                                  vone, cq_t, ck_t, s1_t, s2_t, B, S)
        y_a = _attention(q16, k16, v16, B, S)
        y_p = _pool(proj, w_pool_bd[l], pool_scale[l][None, :], B, S)
        h, xn, logits_t = _outproj(y_m, y_a, y_p, h, w_out16[l], norm2_g[l][None, :],
                                   w_router_t[l], b_router[l][:, None])
        eidx, wts, rank, counts = _router(logits_t, tri, tr)
        dest, blk_e, blk_cnt, blk_next, nblk = _meta(counts, eidx, rank, nb_pad)
        dest_flat = dest.reshape(TOP_K * T)
        x_rows = _sc_dispatch(xn, dest_flat, nb * MOE_ROWS)
        y_rows = _gmm(blk_e[0], blk_cnt[0], blk_next[0], nblk[0], x_rows, wgu_all, bgu_all, wdn_all,
                      bdn_all, nb, l)
        moe = (_sc_gather(y_rows, dest_flat), wts.T)
    return _final(h, moe[0], moe[1], final_norm_g[None, :]).reshape(B, S, D)
```

```python
import functools

import jax
import jax.numpy as jnp
import numpy as np
from jax import lax
from jax.experimental import pallas as pl
from jax.experimental.pallas import tpu as pltpu
from jax.experimental.pallas import tpu_sc as plsc

F32 = jnp.float32
BF16 = jnp.bfloat16
HIGHEST = lax.Precision.HIGHEST

D_MODEL = 1024
M_HEADS = 4
M_HEAD_DIM = 64
M_WIDTH = 256
M_CONV = 4
M_CHUNK = 64
A_HEADS = 8
A_NOPE = 64
A_ROPE = 32
A_VDIM = 64
A_QRANK = 256
A_KVRANK = 128
A_WIDTH = 512
ROPE_THETA = 10000.0
P_WINDOWS = (2, 4, 8, 16)
P_WIDTH = 256
N_EXPERTS = 32
TOP_K = 4
D_FF = 1024
SWIGLU_LIMIT = 7.0
SWIGLU_ALPHA = 1.702
EPS = 1e-6

LANES = 128
SUBLANES = 8
BF16_ROWS = 16

PROJ_QKVO = 0
PROJ_CQ = 1024
PROJ_UP = 1280
PROJ_CKV = 1536
PROJ_SMALL = 1664
PROJ_WIDTH = 1792
SMALL_KR = 64
SMALL_GATE = 96

MOE_ROWS = 1024
MOE_SPLIT = 4
VMEM_LIMIT = 56 * 1024 * 1024


def _cparams(*sem):
    return pltpu.CompilerParams(dimension_semantics=sem, vmem_limit_bytes=VMEM_LIMIT)


def _sigmoid(x):
    return 1.0 / (1.0 + jnp.exp(-x))


def _log_sigmoid(x):
    return jnp.minimum(x, 0.0) - jnp.log(1.0 + jnp.exp(-jnp.abs(x)))


def _dot(a, b, **kw):
    return jnp.dot(a, b, preferred_element_type=F32, **kw)


def _dot_nt(a, b, **kw):
    return lax.dot_general(a, b, (((1,), (1,)), ((), ())), preferred_element_type=F32, **kw)


def _dot_tn(a, b, **kw):
    return lax.dot_general(a, b, (((0,), (0,)), ((), ())), preferred_element_type=F32, **kw)


def _bf16_terms(x, terms=3):
    out = []
    for _ in range(terms):
        piece = x.astype(BF16)
        out.append(piece)
        x = x - piece.astype(F32)
    return out


def _dot_sel(x, sel16, terms=3):
    return sum(_dot(p, sel16) for p in _bf16_terms(x, terms))


def _sel_dot(sel16, x, terms=3):
    return sum(_dot(sel16, p) for p in _bf16_terms(x, terms))


def _pack_bf16_pairs(x):
    n = x.shape[1] // 2
    lo = lax.bitcast_convert_type(x[:, :n].astype(BF16).astype(F32), jnp.uint32)
    hi = lax.bitcast_convert_type(x[:, n:].astype(BF16).astype(F32), jnp.uint32)
    return (lo >> 16) | (hi & jnp.uint32(0xFFFF0000))


def _unpack_bf16_pairs(p):
    lo = lax.bitcast_convert_type(p << 16, F32)
    hi = lax.bitcast_convert_type(p & jnp.uint32(0xFFFF0000), F32)
    return lo, hi


ROW_SUBTILE = 128


def _row_slices(tm):
    return [slice(r, r + ROW_SUBTILE) for r in range(0, tm, ROW_SUBTILE)]


def _moe_combine(h_ref, y_refs, w_ref, rows):
    w = w_ref[rows, :]
    dp = D_MODEL // 2
    acc_lo = h_ref[rows, :dp]
    acc_hi = h_ref[rows, dp:]
    for kk, y_ref in enumerate(y_refs):
        lo, hi = _unpack_bf16_pairs(y_ref[rows, :])
        acc_lo = acc_lo + lo * w[:, kk:kk + 1]
        acc_hi = acc_hi + hi * w[:, kk:kk + 1]
    return jnp.concatenate([acc_lo, acc_hi], axis=1)


def _combine_specs(tm, nt):
    y_specs = [pl.BlockSpec((tm, D_MODEL // 2), functools.partial(lambda i, kk: (kk * nt + i, 0), kk=kk))
               for kk in range(TOP_K)]
    return [pl.BlockSpec((tm, D_MODEL), lambda i: (i, 0))] + y_specs + [pl.BlockSpec((tm, TOP_K), lambda i: (i, 0))]


def _inproj_kernel(*refs, combine):
    if combine:
        h_ref, y0, y1, y2, y3, wk_ref, g_ref, w_ref, hn_ref, o_ref, sm_ref = refs
    else:
        h_ref, g_ref, w_ref, o_ref, sm_ref = refs
    for rows in _row_slices(h_ref.shape[0]):
        if combine:
            x = _moe_combine(h_ref, (y0, y1, y2, y3), wk_ref, rows)
            hn_ref[rows, :] = x
        else:
            x = h_ref[rows, :]
        ms = jnp.mean(x * x, axis=-1, keepdims=True)
        xn = x * lax.rsqrt(ms + EPS) * g_ref[...]
        res = _dot(xn.astype(BF16), w_ref[...])
        o_ref[rows, :] = res[:, :PROJ_SMALL].astype(BF16)
        sm_ref[rows, :] = res[:, PROJ_SMALL:]


def _inproj(h2, g, w, moe=None, tm=512):
    T = h2.shape[0]
    nt = T // tm
    w_specs = [pl.BlockSpec((1, D_MODEL), lambda i: (0, 0)),
               pl.BlockSpec((D_MODEL, PROJ_WIDTH), lambda i: (0, 0))]
    proj_specs = [pl.BlockSpec((tm, PROJ_SMALL), lambda i: (i, 0)),
                  pl.BlockSpec((tm, PROJ_WIDTH - PROJ_SMALL), lambda i: (i, 0))]
    proj_shapes = [jax.ShapeDtypeStruct((T, PROJ_SMALL), BF16),
                   jax.ShapeDtypeStruct((T, PROJ_WIDTH - PROJ_SMALL), F32)]
    if moe is None:
        proj, small = pl.pallas_call(
            functools.partial(_inproj_kernel, combine=False),
            grid=(nt,),
            in_specs=[pl.BlockSpec((tm, D_MODEL), lambda i: (i, 0))] + w_specs,
            out_specs=proj_specs,
            out_shape=proj_shapes,
            compiler_params=_cparams("parallel"),
            name="inproj",
        )(h2, g, w)
        return h2, proj, small
    y_tok, w_tok = moe
    return pl.pallas_call(
        functools.partial(_inproj_kernel, combine=True),
        grid=(nt,),
        in_specs=_combine_specs(tm, nt) + w_specs,
        out_specs=[pl.BlockSpec((tm, D_MODEL), lambda i: (i, 0))] + proj_specs,
        out_shape=[jax.ShapeDtypeStruct((T, D_MODEL), F32)] + proj_shapes,
        compiler_params=_cparams("parallel"),
        name="combine_inproj",
    )(h2, y_tok, y_tok, y_tok, y_tok, w_tok, g, w)


M_SEQS = 2
M_UNROLL = 2


def _mlstm_kernel(proj_ref, small_ref, cw_ref, cb_ref, gbc_ref, ng_ref, o_ref, ct_ref, n_ref, m_ref):
    S = proj_ref.shape[0] // M_SEQS
    L = M_CHUNK
    nc = S // L
    W = M_WIDTH
    ct_ref[...] = jnp.zeros_like(ct_ref)
    n_ref[...] = jnp.zeros_like(n_ref)
    m_ref[...] = jnp.zeros_like(m_ref)

    rh = lax.broadcasted_iota(jnp.int32, (W, W), 0) // M_HEAD_DIM
    chd = lax.broadcasted_iota(jnp.int32, (W, W), 1) // M_HEAD_DIM
    same_head = (rh == chd).astype(F32)
    same_head16 = same_head.astype(BF16)
    tril16 = (lax.broadcasted_iota(jnp.int32, (L, L), 0) >= lax.broadcasted_iota(jnp.int32, (L, L), 1)).astype(BF16)
    row = lax.broadcasted_iota(jnp.int32, (L, W), 0)
    key = lax.broadcasted_iota(jnp.int32, (L, W), 1) % M_HEAD_DIM
    causal = key <= row
    diag = (key == row).astype(F32)
    er = lax.broadcasted_iota(jnp.int32, (LANES, 2 * W), 0)
    ec = lax.broadcasted_iota(jnp.int32, (LANES, 2 * W), 1)
    spread16 = (er == SMALL_GATE + ec // M_HEAD_DIM).astype(BF16)
    forget_lane = lax.broadcasted_iota(jnp.int32, (1, LANES), 1) >= SMALL_GATE + M_HEADS
    cw = cw_ref[...]
    cb = cb_ref[...]
    gbc = gbc_ref[...]
    ng = ng_ref[...]

    def prefix_max(x):
        s = 1
        while s < L:
            x = jnp.maximum(x, jnp.where(row >= s, pltpu.roll(x, s, axis=0), -jnp.inf))
            s *= 2
        return x

    def chunk(sq, c):
        r0 = pl.multiple_of(sq * S + c * L, L)
        h0 = pl.multiple_of(sq * S + jnp.maximum(c * L - BF16_ROWS, 0), BF16_ROWS)
        halo = proj_ref[pl.ds(h0, BF16_ROWS), 0:2 * W].astype(F32)[BF16_ROWS - SUBLANES:, :]
        halo = halo * jnp.where(c > 0, 1.0, 0.0)
        win = jnp.concatenate([halo, proj_ref[pl.ds(r0, L), 0:2 * W].astype(F32)], axis=0)
        acc = jnp.zeros((L, 2 * W), F32) + cb
        for j in range(M_CONV):
            s = M_CONV - 1 - j
            xs = win if s == 0 else pltpu.roll(win, s, axis=0)
            acc = acc + xs[SUBLANES:, :] * cw[j:j + 1, :]
        qk = acc * _sigmoid(acc)
        q = qk[:, 0:W]
        k = qk[:, W:2 * W] * (M_HEAD_DIM ** -0.5)
        v16 = proj_ref[pl.ds(r0, L), 2 * W:3 * W]
        v = v16.astype(F32)
        og = proj_ref[pl.ds(r0, L), 3 * W:4 * W].astype(F32)
        q16 = q.astype(BF16)
        k16 = k.astype(BF16)
        ct = ct_ref[sq]
        nvec = n_ref[sq]
        m_old = m_ref[sq]
        pre = small_ref[pl.ds(r0, L), :] + gbc
        gates = _dot_sel(jnp.where(forget_lane, _log_sigmoid(pre), pre), spread16)
        kb = jnp.concatenate([k16] * M_HEADS, axis=0) * same_head16
        vb = jnp.concatenate([v16] * M_HEADS, axis=0) * same_head16
        qk_all = _dot_nt(q16, kb)
        q_c = _dot(q16, ct.astype(BF16))
        q_n = _dot_sel(q * nvec, same_head16, terms=2)
        yield

        i_pre = gates[:, :W]
        g = _sel_dot(tril16, gates[:, W:])
        yield
        a = i_pre - g
        a_key = jnp.sum(a * diag, axis=0, keepdims=True)
        a_max = prefix_max(a)
        m_row = g + jnp.maximum(m_old, a_max)
        p = qk_all * jnp.exp(jnp.where(causal, g + a_key, -jnp.inf) - m_row)
        p16 = p.astype(BF16)
        num_intra = _dot(p16, vb)
        rowsum = _dot_sel(p, same_head16, terms=2)
        yield

        w_inter = jnp.exp(g + m_old - m_row)
        num = w_inter * q_c + num_intra
        den = w_inter * q_n + rowsum
        hv = num / jnp.maximum(jnp.abs(den), jnp.exp(-m_row))
        ms = _dot_sel(hv * hv, same_head16, terms=2) * (1.0 / M_HEAD_DIM)
        yield
        y = hv * lax.rsqrt(ms + EPS) * ng * _sigmoid(og)
        o_ref[pl.ds(r0, L), :] = y

        g_end = g[L - 1:L, :]
        m_new = g_end + jnp.maximum(m_old, a_max[L - 1:L, :])
        wa = jnp.exp(g_end + a - m_new)
        decay = jnp.exp(g_end + m_old - m_new)
        upd = _dot_tn(k16, (wa * v).astype(BF16))
        ct_ref[sq] = decay * ct + upd * same_head
        n_ref[sq] = decay * nvec + jnp.sum(wa * k, axis=0, keepdims=True)
        m_ref[sq] = m_new
        yield

    def body(cc, carry):
        for u in range(M_UNROLL):
            for _ in zip(*[chunk(sq, cc * M_UNROLL + u) for sq in range(M_SEQS)]):
                pass
        return carry

    lax.fori_loop(0, nc // M_UNROLL, body, 0)


def _mlstm(proj, small, cw, cb, gbc, ng, B, S):
    assert M_CHUNK == M_HEAD_DIM
    T = B * S
    rows = M_SEQS * S
    return pl.pallas_call(
        _mlstm_kernel,
        grid=(B // M_SEQS,),
        in_specs=[pl.BlockSpec((rows, 4 * M_WIDTH), lambda b: (b, 0)),
                  pl.BlockSpec((rows, LANES), lambda b: (b, 0)),
                  pl.BlockSpec((M_CONV, 2 * M_WIDTH), lambda b: (0, 0)),
                  pl.BlockSpec((1, 2 * M_WIDTH), lambda b: (0, 0)),
                  pl.BlockSpec((1, LANES), lambda b: (0, 0)),
                  pl.BlockSpec((1, M_WIDTH), lambda b: (0, 0))],
        out_specs=pl.BlockSpec((rows, M_WIDTH), lambda b: (b, 0)),
        out_shape=jax.ShapeDtypeStruct((T, M_WIDTH), F32),
        scratch_shapes=[pltpu.VMEM((M_SEQS, M_WIDTH, M_WIDTH), F32),
                        pltpu.VMEM((M_SEQS, 1, M_WIDTH), F32),
                        pltpu.VMEM((M_SEQS, 1, M_WIDTH), F32)],
        compiler_params=_cparams("parallel"),
        name="mlstm",
    )(proj, small, cw, cb, gbc, ng)


POOL_HALO = 16
POOL_TILE = 256


def _pool_kernel(u_ref, w_ref, sc_ref, o_ref, upad_ref):
    S = u_ref.shape[0]
    upad_ref[0:POOL_HALO, :] = jnp.zeros((POOL_HALO, P_WIDTH), F32)
    upad_ref[POOL_HALO:, :] = u_ref[...].astype(F32)
    grp = lax.broadcasted_iota(jnp.int32, (1, P_WIDTH), 1) // (P_WIDTH // len(P_WINDOWS))
    win_lane = jnp.zeros((1, P_WIDTH), jnp.int32)
    for gi, wn in enumerate(P_WINDOWS):
        win_lane = jnp.where(grp == gi, wn, win_lane)
    w = w_ref[...]
    scale = sc_ref[...]
    rows = POOL_TILE + POOL_HALO

    def body(r, carry):
        r0 = pl.multiple_of(r * POOL_TILE, POOL_TILE)
        a = upad_ref[pl.ds(r0, rows), :]
        sums = []
        cur = a
        span = 1
        for _ in P_WINDOWS:
            cur = cur + pltpu.roll(cur, span, axis=0)
            span *= 2
            sums.append(cur)
        sel = sums[-1]
        for gi in range(len(P_WINDOWS) - 1):
            sel = jnp.where(grp == gi, sums[gi], sel)
        sel = sel[POOL_HALO:, :]
        u = a[POOL_HALO:, :]
        t = r0 + lax.broadcasted_iota(jnp.int32, (POOL_TILE, P_WIDTH), 0)
        cnt = jnp.minimum(t + 1, win_lane).astype(F32)
        pooled = sel / cnt - u
        o_ref[pl.ds(r0, POOL_TILE), :] = _dot(pooled.astype(BF16), w) * scale
        return carry

    lax.fori_loop(0, S // POOL_TILE, body, 0)


def _pool(proj, w_bd, scale, B, S):
    T = B * S
    return pl.pallas_call(
        _pool_kernel,
        grid=(B,),
        in_specs=[pl.BlockSpec((S, P_WIDTH), lambda b: (b, PROJ_UP // P_WIDTH)),
                  pl.BlockSpec((P_WIDTH, P_WIDTH), lambda b: (0, 0)),
                  pl.BlockSpec((1, P_WIDTH), lambda b: (0, 0))],
        out_specs=pl.BlockSpec((S, P_WIDTH), lambda b: (b, 0)),
        out_shape=jax.ShapeDtypeStruct((T, P_WIDTH), F32),
        scratch_shapes=[pltpu.VMEM((S + POOL_HALO, P_WIDTH), F32)],
        compiler_params=_cparams("parallel"),
        name="pool",
    )(proj, w_bd, scale)


def _rope(x, c, s1, s2):
    return x * c + pltpu.roll(x, LANES - A_ROPE // 2, axis=1) * s1 + pltpu.roll(x, A_ROPE // 2, axis=1) * s2


def _mla_prep_kernel(cq_ref, ckv_ref, small_ref, qg_ref, kvg_ref, wq_ref, wqs_ref, wk_ref, wv_ref,
                     vone_ref, cq_t_ref, ck_t_ref, s1_ref, s2_ref, q_ref, k_ref, v_ref):
    def rms(x, g):
        return x * lax.rsqrt(jnp.mean(x * x, axis=-1, keepdims=True) + EPS) * g

    cqn = rms(cq_ref[...].astype(F32), qg_ref[...]).astype(BF16)
    ckvn = rms(ckv_ref[...].astype(F32), kvg_ref[...]).astype(BF16)
    scale = (A_NOPE + A_ROPE) ** -0.5
    qf = _dot(cqn, wq_ref[...]) * scale
    qp = _dot(cqn, wqs_ref[...]) * scale
    kf = _dot(ckvn, wk_ref[...])
    v_ref[...] = (_dot(ckvn, wv_ref[...]) + vone_ref[...]).astype(BF16)
    cqt = cq_t_ref[...]
    s1 = s1_ref[...]
    s2 = s2_ref[...]
    krot = _rope(small_ref[...], ck_t_ref[...], s1, s2)
    sq = s1 + s2
    for h in range(A_HEADS):
        sl = slice(h * LANES, (h + 1) * LANES)
        q_ref[:, sl] = (qf[:, sl] * cqt + qp[:, sl] * sq).astype(BF16)
        k_ref[:, sl] = (kf[:, sl] + krot).astype(BF16)


def _mla_prep(proj, small, qg, kvg, wq, wqs, wk, wv, vone, cq_t, ck_t, s1_t, s2_t, B, S, ts=1024):
    T = B * S
    nst = S // ts
    hw = A_HEADS * LANES
    return pl.pallas_call(
        _mla_prep_kernel,
        grid=(B, nst),
        in_specs=[pl.BlockSpec((ts, A_QRANK), lambda b, s: (b * nst + s, PROJ_CQ // A_QRANK)),
                  pl.BlockSpec((ts, A_KVRANK), lambda b, s: (b * nst + s, PROJ_CKV // A_KVRANK)),
                  pl.BlockSpec((ts, LANES), lambda b, s: (b * nst + s, 0)),
                  pl.BlockSpec((1, A_QRANK), lambda b, s: (0, 0)),
                  pl.BlockSpec((1, A_KVRANK), lambda b, s: (0, 0)),
                  pl.BlockSpec((A_QRANK, hw), lambda b, s: (0, 0)),
                  pl.BlockSpec((A_QRANK, hw), lambda b, s: (0, 0)),
                  pl.BlockSpec((A_KVRANK, hw), lambda b, s: (0, 0)),
                  pl.BlockSpec((A_KVRANK, hw), lambda b, s: (0, 0)),
                  pl.BlockSpec((1, hw), lambda b, s: (0, 0)),
                  pl.BlockSpec((ts, LANES), lambda b, s: (s, 0)),
                  pl.BlockSpec((ts, LANES), lambda b, s: (s, 0)),
                  pl.BlockSpec((ts, LANES), lambda b, s: (s, 0)),
                  pl.BlockSpec((ts, LANES), lambda b, s: (s, 0))],
        out_specs=[pl.BlockSpec((ts, hw), lambda b, s: (b * nst + s, 0)),
                   pl.BlockSpec((ts, hw), lambda b, s: (b * nst + s, 0)),
                   pl.BlockSpec((ts, hw), lambda b, s: (b * nst + s, 0))],
        out_shape=[jax.ShapeDtypeStruct((T, hw), BF16),
                   jax.ShapeDtypeStruct((T, hw), BF16),
                   jax.ShapeDtypeStruct((T, hw), BF16)],
        compiler_params=_cparams("parallel", "parallel"),
        name="mla_prep",
    )(proj, proj, small, qg, kvg, wq, wqs, wk, wv, vone, cq_t, ck_t, s1_t, s2_t)


def _attn_kernel(q_ref, k_ref, v_ref, o_ref, *, tq):
    heads = range(2)
    sls = [slice(hh * LANES, (hh + 1) * LANES) for hh in heads]
    lane = lax.broadcasted_iota(jnp.int32, (tq, LANES), 1)
    below_diag = lax.broadcasted_iota(jnp.int32, (tq, tq), 0) >= lax.broadcasted_iota(jnp.int32, (tq, tq), 1)

    def update(qs, k0, state, causal):
        keys = slice(k0, k0 + tq)
        s = [_dot_nt(qs[hh], k_ref[keys, sls[hh]]) for hh in heads]
        if causal:
            s = [jnp.where(below_diag, s[hh], -jnp.inf) for hh in heads]
        m_new = [jnp.maximum(state[hh][0], jnp.max(s[hh], axis=-1, keepdims=True)) for hh in heads]
        p = [jnp.exp((s[hh] - m_new[hh]).astype(BF16)) for hh in heads]
        pv = [_dot(p[hh], v_ref[keys, sls[hh]]) for hh in heads]
        acc = [jnp.exp(state[hh][0] - m_new[hh]) * state[hh][1] + pv[hh] for hh in heads]
        return tuple((m_new[hh], acc[hh]) for hh in heads)

    for qi in range(q_ref.shape[0] // tq):
        rows = slice(qi * tq, (qi + 1) * tq)
        qs = [q_ref[rows, sl] for sl in sls]
        state = tuple((jnp.full((tq, 1), -jnp.inf, F32), jnp.zeros((tq, LANES), F32)) for _ in heads)
        for kb in range(qi + 1):
            state = update(qs, kb * tq, state, causal=(kb == qi))
        (_, acc0), (_, acc1) = state
        acc = jnp.where(lane < A_VDIM, acc0, acc1)
        den = jnp.where(lane < A_VDIM, pltpu.roll(acc0, A_VDIM, axis=1), pltpu.roll(acc1, A_VDIM, axis=1))
        o_ref[rows, :] = (acc / den).astype(o_ref.dtype)


def _attention(q, k, v, B, S, tq=512):
    T = B * S
    return pl.pallas_call(
        functools.partial(_attn_kernel, tq=tq),
        grid=(B, A_HEADS // 2),
        in_specs=[pl.BlockSpec((S, 2 * LANES), lambda b, p: (b, p)),
                  pl.BlockSpec((S, 2 * LANES), lambda b, p: (b, p)),
                  pl.BlockSpec((S, 2 * LANES), lambda b, p: (b, p))],
        out_specs=pl.BlockSpec((S, LANES), lambda b, p: (b, p)),
        out_shape=jax.ShapeDtypeStruct((T, A_WIDTH), BF16),
        compiler_params=_cparams("parallel", "parallel"),
        name="attention",
    )(q, k, v)


def _outproj_kernel(ym_ref, ya_ref, yp_ref, h_ref, w_ref, g_ref, wr_ref, br_ref,
                    hn_ref, xn_ref, lg_ref):
    mix = _dot(ym_ref[...].astype(BF16), w_ref[0:M_WIDTH, :])
    mix = mix + _dot(ya_ref[...], w_ref[M_WIDTH:M_WIDTH + A_WIDTH, :])
    mix = mix + _dot(yp_ref[...].astype(BF16), w_ref[M_WIDTH + A_WIDTH:, :])
    hn = h_ref[...] + mix
    hn_ref[...] = hn
    xn = hn * lax.rsqrt(jnp.mean(hn * hn, axis=-1, keepdims=True) + EPS) * g_ref[...]
    x_hi = xn.astype(BF16)
    x_lo = (xn - x_hi.astype(F32)).astype(BF16)
    wr = wr_ref[...]
    w_hi = wr.astype(BF16)
    w_lo = (wr - w_hi.astype(F32)).astype(BF16)
    lg_ref[...] = _dot_nt(w_hi, x_hi) + _dot_nt(w_hi, x_lo) + _dot_nt(w_lo, x_hi) + br_ref[...]
    xn_ref[...] = _pack_bf16_pairs(xn)


def _outproj(ym, ya, yp, h2, w, g, wr_t, br, tm=1024):
    T = h2.shape[0]
    return pl.pallas_call(
        _outproj_kernel,
        grid=(T // tm,),
        in_specs=[pl.BlockSpec((tm, M_WIDTH), lambda i: (i, 0)),
                  pl.BlockSpec((tm, A_WIDTH), lambda i: (i, 0)),
                  pl.BlockSpec((tm, P_WIDTH), lambda i: (i, 0)),
                  pl.BlockSpec((tm, D_MODEL), lambda i: (i, 0)),
                  pl.BlockSpec((D_MODEL, D_MODEL), lambda i: (0, 0)),
                  pl.BlockSpec((1, D_MODEL), lambda i: (0, 0)),
                  pl.BlockSpec((N_EXPERTS, D_MODEL), lambda i: (0, 0)),
                  pl.BlockSpec((N_EXPERTS, 1), lambda i: (0, 0))],
        out_specs=[pl.BlockSpec((tm, D_MODEL), lambda i: (i, 0)),
                   pl.BlockSpec((tm, D_MODEL // 2), lambda i: (i, 0)),
                   pl.BlockSpec((N_EXPERTS, tm), lambda i: (0, i))],
        out_shape=[jax.ShapeDtypeStruct((T, D_MODEL), F32),
                   jax.ShapeDtypeStruct((T, D_MODEL // 2), jnp.uint32),
                   jax.ShapeDtypeStruct((N_EXPERTS, T), F32)],
        compiler_params=_cparams("parallel"),
        name="outproj",
    )(ym, ya, yp, h2, w, g, wr_t, br)


def _router_kernel(lg_ref, tri_ref, e_ref, w_ref, r_ref, cnt_ref, carry_ref):
    tr = lg_ref.shape[1]

    @pl.when(pl.program_id(0) == 0)
    def _():
        carry_ref[...] = jnp.zeros_like(carry_ref)

    x = lg_ref[...]
    eio = lax.broadcasted_iota(jnp.int32, (N_EXPERTS, tr), 0).astype(F32)
    picked = jnp.zeros((N_EXPERTS, tr), F32)
    vals = []
    idxs = []
    for _ in range(TOP_K):
        mx = jnp.max(x, axis=0, keepdims=True)
        idx = jnp.min(jnp.where(x == mx, eio, float(N_EXPERTS)), axis=0, keepdims=True)
        hit = eio == idx
        vals.append(mx)
        idxs.append(idx)
        picked = picked + hit.astype(F32)
        x = jnp.where(hit, -jnp.inf, x)
    exps = [jnp.exp(vv - vals[0]) for vv in vals]
    tot = exps[0] + exps[1] + exps[2] + exps[3]
    before = _dot(picked.astype(BF16), tri_ref[...]) + carry_ref[:, 0:1]
    for kk in range(TOP_K):
        e_ref[kk:kk + 1, :] = idxs[kk].astype(jnp.int32)
        w_ref[kk:kk + 1, :] = exps[kk] / tot
        rk = jnp.sum(jnp.where(eio == idxs[kk], before, 0.0), axis=0, keepdims=True)
        r_ref[kk:kk + 1, :] = rk.astype(jnp.int32)
    carry_ref[...] = carry_ref[...] + jnp.sum(picked, axis=1, keepdims=True)
    cnt_ref[...] = carry_ref[...]


def _router(logits_t, tri, tr=512):
    T = logits_t.shape[1]
    return pl.pallas_call(
        _router_kernel,
        grid=(T // tr,),
        in_specs=[pl.BlockSpec((N_EXPERTS, tr), lambda i: (0, i)),
                  pl.BlockSpec((tr, tr), lambda i: (0, 0))],
        out_specs=[pl.BlockSpec((TOP_K, tr), lambda i: (0, i)),
                   pl.BlockSpec((TOP_K, tr), lambda i: (0, i)),
                   pl.BlockSpec((TOP_K, tr), lambda i: (0, i)),
                   pl.BlockSpec((N_EXPERTS, LANES), lambda i: (0, 0))],
        out_shape=[jax.ShapeDtypeStruct((TOP_K, T), jnp.int32),
                   jax.ShapeDtypeStruct((TOP_K, T), F32),
                   jax.ShapeDtypeStruct((TOP_K, T), jnp.int32),
                   jax.ShapeDtypeStruct((N_EXPERTS, LANES), F32)],
        scratch_shapes=[pltpu.VMEM((N_EXPERTS, LANES), F32)],
        compiler_params=_cparams("arbitrary"),
        name="router",
    )(logits_t, tri)


def _meta_kernel(cnt_ref, e_ref, r_ref, dest_ref, be_ref, bc_ref, bn_ref, nb_ref, *, nb_pad):
    cnt = cnt_ref[...]
    padded = jnp.floor((cnt + (MOE_ROWS - 1)) * (1.0 / MOE_ROWS)) * MOE_ROWS
    ri = lax.broadcasted_iota(jnp.int32, (N_EXPERTS, N_EXPERTS), 0)
    ci = lax.broadcasted_iota(jnp.int32, (N_EXPERTS, N_EXPERTS), 1)
    pad_end = _dot((ri >= ci).astype(F32), padded, precision=HIGHEST)
    pad_start = pad_end - padded
    e = e_ref[...]
    dest = r_ref[...]
    for ex in range(N_EXPERTS):
        ps = pad_start[ex:ex + 1, 0:1].astype(jnp.int32)
        dest = jnp.where(e == ex, dest + ps, dest)
    dest_ref[...] = dest
    blk0 = (lax.broadcasted_iota(jnp.int32, (N_EXPERTS, nb_pad), 1) * MOE_ROWS).astype(F32)
    be = jnp.sum((pad_end[:, 0:1] <= blk0).astype(F32), axis=0, keepdims=True)
    be = jnp.minimum(be, float(N_EXPERTS - 1))
    eio = lax.broadcasted_iota(jnp.int32, (N_EXPERTS, nb_pad), 0).astype(F32)
    seg_end = jnp.sum(jnp.where(eio == be, pad_start[:, 0:1] + cnt[:, 0:1], 0.0), axis=0, keepdims=True)
    bc = jnp.clip(seg_end - blk0[0:1, :], 0.0, float(MOE_ROWS))
    nxt0 = jnp.sum(jnp.where(eio == be, pad_end[:, 0:1], 0.0), axis=0, keepdims=True)
    bn = jnp.sum((pad_end[:, 0:1] <= nxt0).astype(F32), axis=0, keepdims=True)
    bn = jnp.where(nxt0 < pad_end[N_EXPERTS - 1:N_EXPERTS, 0:1], bn, -1.0)
    be_ref[...] = be.astype(jnp.int32)
    bc_ref[...] = bc.astype(jnp.int32)
    bn_ref[...] = bn.astype(jnp.int32)
    nb_ref[...] = (pad_end[N_EXPERTS - 1:N_EXPERTS, :] * (1.0 / MOE_ROWS)).astype(jnp.int32)


def _meta(counts, eidx, rank, nb_pad):
    T = eidx.shape[1]
    return pl.pallas_call(
        functools.partial(_meta_kernel, nb_pad=nb_pad),
        out_shape=[jax.ShapeDtypeStruct((TOP_K, T), jnp.int32),
                   jax.ShapeDtypeStruct((1, nb_pad), jnp.int32),
                   jax.ShapeDtypeStruct((1, nb_pad), jnp.int32),
                   jax.ShapeDtypeStruct((1, nb_pad), jnp.int32),
                   jax.ShapeDtypeStruct((1, LANES), jnp.int32)],
        compiler_params=pltpu.CompilerParams(vmem_limit_bytes=VMEM_LIMIT),
        name="route_meta",
    )(counts, eidx, rank)


FF_CHUNK = 512


def _gmm_kernel(be_ref, bc_ref, bn_ref, nb_ref, x_ref, wgu_hbm, bgu_ref, wdn_hbm, bdn_ref, y_ref,
                wgu_st, wdn_st, wgu16, wdn16, sem, *, e0):
    i = pl.program_id(0)
    nblk = nb_ref[0]
    bm = MOE_ROWS

    def weight_copies(e):
        return (pltpu.make_async_copy(wgu_hbm.at[e0 + e], wgu_st, sem.at[0]),
                pltpu.make_async_copy(wdn_hbm.at[e0 + e], wdn_st, sem.at[1]))

    @pl.when(i == 0)
    def _():
        for cp in weight_copies(be_ref[0]):
            cp.start()

    @pl.when(i >= nblk)
    def _():
        y_ref[...] = jnp.zeros_like(y_ref)

    @pl.when(i < nblk)
    def _():
        e_changed = jnp.logical_or(i == 0, be_ref[i] != be_ref[jnp.maximum(i - 1, 0)])

        @pl.when(e_changed)
        def _():
            for cp in weight_copies(be_ref[i]):
                cp.wait()
            wgu16[...] = wgu_st[...].astype(BF16)
            wdn16[...] = wdn_st[...].astype(BF16)

            @pl.when(bn_ref[i] >= 0)
            def _():
                for cp in weight_copies(bn_ref[i]):
                    cp.start()

        def expert_rows(nrows):
            valid = lax.broadcasted_iota(jnp.int32, (nrows, 1), 0) < bc_ref[i]
            lo, hi = _unpack_bf16_pairs(jnp.where(valid, x_ref[0:nrows, :], jnp.uint32(0)))
            x16 = jnp.concatenate([lo.astype(BF16), hi.astype(BF16)], axis=1)
            acc = jnp.zeros((nrows, D_MODEL), F32) + bdn_ref[0]
            for c in range(D_FF // FF_CHUNK):
                cs = slice(c * FF_CHUNK, (c + 1) * FF_CHUNK)
                us = slice(D_FF + c * FF_CHUNK, D_FF + (c + 1) * FF_CHUNK)
                gate = _dot(x16, wgu16[:, cs]) + bgu_ref[0, :, cs]
                up = _dot(x16, wgu16[:, us]) + bgu_ref[0, :, us]
                gate = jnp.minimum(gate, SWIGLU_LIMIT)
                up = jnp.clip(up, -SWIGLU_LIMIT, SWIGLU_LIMIT)
                act = (up + 1.0) * gate * _sigmoid(SWIGLU_ALPHA * gate)
                acc = acc + _dot(act.astype(BF16), wdn16[cs, :])
            y_ref[0:nrows, :] = _pack_bf16_pairs(acc)

        quarter = bm // MOE_SPLIT
        for nq in range(1, MOE_SPLIT + 1):
            lower = (nq - 1) * quarter if nq > 1 else -1
            @pl.when(jnp.logical_and(bc_ref[i] > lower, bc_ref[i] <= nq * quarter))
            def _(nq=nq):
                expert_rows(nq * quarter)
                if nq < MOE_SPLIT:
                    y_ref[nq * quarter:, :] = jnp.zeros((bm - nq * quarter, D_MODEL // 2), jnp.uint32)


def _gmm(blk_e, blk_cnt, blk_next, nblk, x_rows, wgu, bgu, wdn, bdn, nb, layer):
    bm = MOE_ROWS
    e0 = layer * N_EXPERTS
    dp = D_MODEL // 2

    def expert(i, be, nb_ref):
        return (e0 + be[jnp.minimum(i, jnp.maximum(nb_ref[0] - 1, 0))], 0, 0)

    def rows(i, nb_ref):
        return (jnp.minimum(i, jnp.maximum(nb_ref[0] - 1, 0)), 0)

    grid_spec = pltpu.PrefetchScalarGridSpec(
        num_scalar_prefetch=4,
        grid=(nb,),
        in_specs=[pl.BlockSpec((bm, dp), lambda i, be, bc, bn, nbr: rows(i, nbr)),
                  pl.BlockSpec(memory_space=pl.ANY),
                  pl.BlockSpec((1, 1, 2 * D_FF), lambda i, be, bc, bn, nbr: expert(i, be, nbr)),
                  pl.BlockSpec(memory_space=pl.ANY),
                  pl.BlockSpec((1, 1, D_MODEL), lambda i, be, bc, bn, nbr: expert(i, be, nbr))],
        out_specs=pl.BlockSpec((bm, dp), lambda i, be, bc, bn, nbr: (i, 0)),
        scratch_shapes=[pltpu.VMEM((D_MODEL, 2 * D_FF), F32),
                        pltpu.VMEM((D_FF, D_MODEL), F32),
                        pltpu.VMEM((D_MODEL, 2 * D_FF), BF16),
                        pltpu.VMEM((D_FF, D_MODEL), BF16),
                        pltpu.SemaphoreType.DMA((2,))],
    )
    return pl.pallas_call(
        functools.partial(_gmm_kernel, e0=e0),
        grid_spec=grid_spec,
        out_shape=jax.ShapeDtypeStruct((nb * bm, dp), jnp.uint32),
        compiler_params=_cparams("arbitrary"),
        name="expert_gmm",
    )(blk_e, blk_cnt, blk_next, nblk, x_rows, wgu, bgu, wdn, bdn)


def _final_kernel(h_ref, y0, y1, y2, y3, wk_ref, g_ref, o_ref):
    for rows in _row_slices(h_ref.shape[0]):
        x = _moe_combine(h_ref, (y0, y1, y2, y3), wk_ref, rows)
        o_ref[rows, :] = x * lax.rsqrt(jnp.mean(x * x, axis=-1, keepdims=True) + EPS) * g_ref[...]


def _final(h2, y_tok, w_tok, g, tm=512):
    T = h2.shape[0]
    nt = T // tm
    return pl.pallas_call(
        _final_kernel,
        grid=(nt,),
        in_specs=_combine_specs(tm, nt) + [pl.BlockSpec((1, D_MODEL), lambda i: (0, 0))],
        out_specs=pl.BlockSpec((tm, D_MODEL), lambda i: (i, 0)),
        out_shape=jax.ShapeDtypeStruct((T, D_MODEL), F32),
        compiler_params=_cparams("parallel"),
        name="combine_final_norm",
    )(h2, y_tok, y_tok, y_tok, y_tok, w_tok, g)


SC_CORES = 2
SC_SUBCORES = 16
SC_LANES = 16
SC_WORKERS = SC_CORES * SC_SUBCORES
SC_WINDOW = 64


def _sc_mesh():
    return plsc.VectorSubcoreMesh(core_axis_name="c", subcore_axis_name="s")


def _sc_worker():
    return lax.axis_index("s") * SC_CORES + lax.axis_index("c")


def _sc_dispatch(xn, dest_flat, n_rows):
    T = xn.shape[0]
    tpw = T // SC_WORKERS
    nchunk = tpw // SC_WINDOW
    nvec = SC_WINDOW // SC_LANES

    @functools.partial(
        pl.kernel, out_type=jax.ShapeDtypeStruct((n_rows, xn.shape[1]), xn.dtype), mesh=_sc_mesh(),
        scratch_types=[pltpu.VMEM((TOP_K * tpw,), jnp.int32),
                       pltpu.VMEM((SC_WINDOW, xn.shape[1]), xn.dtype),
                       pltpu.VMEM((SC_WINDOW, xn.shape[1]), xn.dtype),
                       pltpu.SemaphoreType.DMA, pltpu.SemaphoreType.DMA, pltpu.SemaphoreType.DMA],
        name="sc_dispatch")
    def run(x_hbm, d_hbm, o_hbm, idx_v, buf0, buf1, sem0, sem1, sem_out):
        base = _sc_worker() * tpw
        for kk in range(TOP_K):
            pltpu.sync_copy(d_hbm.at[pl.ds(kk * T + base, tpw)], idx_v.at[pl.ds(kk * tpw, tpw)])
        bufs = (buf0, buf1)
        sems = (sem0, sem1)

        def load(c, slot):
            return pltpu.make_async_copy(x_hbm.at[pl.ds(base + c * SC_WINDOW, SC_WINDOW)], bufs[slot], sems[slot])

        load(0, 0).start()

        @pl.loop(0, nchunk, step=2)
        def _(c0):
            for slot in range(2):
                c = c0 + slot
                load(c, slot).wait()

                @pl.when(c + 1 < nchunk)
                def _():
                    load(c + 1, 1 - slot).start()

                copies = []
                for kk in range(TOP_K):
                    for q in range(nvec):
                        off = pl.multiple_of(kk * tpw + c * SC_WINDOW + q * SC_LANES, SC_LANES)
                        rows = idx_v[pl.ds(off, SC_LANES)]
                        cp = pltpu.make_async_copy(bufs[slot].at[pl.ds(q * SC_LANES, SC_LANES)],
                                                   o_hbm.at[rows], sem_out)
                        cp.start()
                        copies.append(cp)
                for cp in copies:
                    cp.wait()

    return run(xn, dest_flat)


def _sc_gather(y_rows, dest_flat):
    n = dest_flat.shape[0]
    rpw = n // SC_WORKERS
    nchunk = rpw // SC_WINDOW
    nvec = SC_WINDOW // SC_LANES

    @functools.partial(
        pl.kernel, out_type=jax.ShapeDtypeStruct((n, y_rows.shape[1]), y_rows.dtype), mesh=_sc_mesh(),
        scratch_types=[pltpu.VMEM((rpw,), jnp.int32),
                       pltpu.VMEM((SC_WINDOW, y_rows.shape[1]), y_rows.dtype),
                       pltpu.VMEM((SC_WINDOW, y_rows.shape[1]), y_rows.dtype),
                       pltpu.SemaphoreType.DMA, pltpu.SemaphoreType.DMA, pltpu.SemaphoreType.DMA],
        name="sc_gather")
    def run(y_hbm, d_hbm, o_hbm, idx_v, buf0, buf1, sem0, sem1, sem_in):
        base = _sc_worker() * rpw
        pltpu.sync_copy(d_hbm.at[pl.ds(base, rpw)], idx_v)
        bufs = (buf0, buf1)
        sems = (sem0, sem1)

        def store(c, slot):
            return pltpu.make_async_copy(bufs[slot], o_hbm.at[pl.ds(base + c * SC_WINDOW, SC_WINDOW)], sems[slot])

        @pl.loop(0, nchunk, step=2)
        def _(c0):
            for slot in range(2):
                c = c0 + slot

                @pl.when(c >= 2)
                def _():
                    store(c - 2, slot).wait()

                copies = []
                for q in range(nvec):
                    off = pl.multiple_of(c * SC_WINDOW + q * SC_LANES, SC_LANES)
                    rows = idx_v[pl.ds(off, SC_LANES)]
                    cp = pltpu.make_async_copy(y_hbm.at[rows], bufs[slot].at[pl.ds(q * SC_LANES, SC_LANES)], sem_in)
                    cp.start()
                    copies.append(cp)
                for cp in copies:
                    cp.wait()
                store(c, slot).start()

        store(nchunk - 2, 0).wait()
        store(nchunk - 1, 1).wait()

    return run(y_rows, dest_flat)


def _prep_w_in(w_in):
    o_g = 4 * M_WIDTH
    o_cq = o_g + 2 * M_HEADS
    o_ckv = o_cq + A_QRANK
    o_kr = o_ckv + A_KVRANK
    o_up = o_kr + A_ROPE
    z = lambda n: jnp.zeros(w_in.shape[:-1] + (n,), w_in.dtype)
    small = jnp.concatenate([z(SMALL_KR), w_in[..., o_kr:o_up], w_in[..., o_g:o_cq],
                             z(LANES - SMALL_GATE - 2 * M_HEADS)], axis=-1)
    return jnp.concatenate([w_in[..., 0:o_g], w_in[..., o_cq:o_ckv], w_in[..., o_up:o_up + P_WIDTH],
                            w_in[..., o_ckv:o_kr], small], axis=-1).astype(BF16)


def _rope_tables(seq):
    inv = ROPE_THETA ** (-jnp.arange(0, A_ROPE, 2, dtype=F32) / A_ROPE)
    ang = jnp.arange(seq, dtype=F32)[:, None] * inv[None, :]
    cos, sin = jnp.cos(ang), jnp.sin(ang)
    half = A_ROPE // 2
    zeros = lambda n: jnp.zeros((seq, n), F32)
    ones = lambda n: jnp.ones((seq, n), F32)
    tail = LANES - A_NOPE - A_ROPE
    cq_t = jnp.concatenate([ones(A_NOPE), cos, cos, zeros(tail)], axis=1)
    ck_t = jnp.concatenate([zeros(A_NOPE), cos, cos, zeros(tail)], axis=1)
    s1_t = jnp.concatenate([zeros(A_NOPE), -sin, zeros(half), zeros(tail)], axis=1)
    s2_t = jnp.concatenate([zeros(A_NOPE), zeros(half), sin, zeros(tail)], axis=1)
    return cq_t, ck_t, s1_t, s2_t


def kernel(x, norm1_g, w_in, conv_w, conv_b, gate_b, mlstm_norm_g, q_norm_g, kv_norm_g, w_uq, w_ukv,
           w_pool, pool_scale, w_out, norm2_g, w_router, b_router, w_gate_up, b_gate_up, w_down, b_down,
           final_norm_g):
    B, S, D = x.shape
    depth = w_in.shape[0]
    T = B * S
    nb = (T * TOP_K) // MOE_ROWS + N_EXPERTS
    nb_pad = -(-nb // LANES) * LANES

    w_in_p = _prep_w_in(w_in)
    wq = w_uq.reshape(depth, A_QRANK, A_HEADS, A_NOPE + A_ROPE)
    wq = jnp.pad(wq, ((0, 0), (0, 0), (0, 0), (0, LANES - A_NOPE - A_ROPE)))
    r0, r1, r2 = A_NOPE, A_NOPE + A_ROPE // 2, A_NOPE + A_ROPE
    wqs = jnp.concatenate([jnp.zeros_like(wq[..., :r0]), wq[..., r1:r2], wq[..., r0:r1],
                           jnp.zeros_like(wq[..., r2:])], axis=-1)
    wq = wq.reshape(depth, A_QRANK, A_HEADS * LANES).astype(BF16)
    wqs = wqs.reshape(depth, A_QRANK, A_HEADS * LANES).astype(BF16)
    wkv = w_ukv.reshape(depth, A_KVRANK, A_HEADS, A_NOPE + A_VDIM)
    wk = jnp.pad(wkv[..., :A_NOPE], ((0, 0), (0, 0), (0, 0), (0, LANES - A_NOPE)))
    wk = wk.reshape(depth, A_KVRANK, A_HEADS * LANES).astype(BF16)
    wv_e = jnp.pad(wkv[:, :, 0::2, A_NOPE:], ((0, 0), (0, 0), (0, 0), (0, LANES - A_VDIM)))
    wv_o = jnp.pad(wkv[:, :, 1::2, A_NOPE:], ((0, 0), (0, 0), (0, 0), (LANES - A_VDIM, 0)))
    wv = jnp.stack([wv_e, wv_o], axis=3).reshape(depth, A_KVRANK, A_HEADS * LANES).astype(BF16)
    half = jnp.arange(A_HEADS * LANES) // A_VDIM
    vone = ((half % 4 == 1) | (half % 4 == 2)).astype(F32)[None, :]
    gsz = P_WIDTH // len(P_WINDOWS)
    w_pool_bd = jnp.zeros((depth, P_WIDTH, P_WIDTH), F32)
    for gi in range(len(P_WINDOWS)):
        w_pool_bd = w_pool_bd.at[:, gi * gsz:(gi + 1) * gsz, gi * gsz:(gi + 1) * gsz].set(w_pool[:, gi])
    w_pool_bd = w_pool_bd.astype(BF16)
    w_out16 = w_out.astype(BF16)
    w_router_t = jnp.swapaxes(w_router, 1, 2)
    gate_b_col = jnp.pad(gate_b, ((0, 0), (SMALL_GATE, LANES - SMALL_GATE - 2 * M_HEADS)))
    cq_t, ck_t, s1_t, s2_t = _rope_tables(S)
    tr = 1024
    tri = (jnp.arange(tr)[:, None] < jnp.arange(tr)[None, :]).astype(BF16)

    wgu_all = w_gate_up.reshape(depth * N_EXPERTS, D_MODEL, 2 * D_FF)
    bgu_all = b_gate_up.reshape(depth * N_EXPERTS, 1, 2 * D_FF)
    wdn_all = w_down.reshape(depth * N_EXPERTS, D_FF, D_MODEL)
    bdn_all = b_down.reshape(depth * N_EXPERTS, 1, D_MODEL)

    h = x.reshape(T, D)
    moe = None
    for l in range(depth):
        h, proj, small = _inproj(h, norm1_g[l][None, :], w_in_p[l], moe)
        y_m = _mlstm(proj, small, conv_w[l], conv_b[l][None, :], gate_b_col[l][None, :],
                     mlstm_norm_g[l][None, :], B, S)
        q16, k16, v16 = _mla_prep(proj, small, q_norm_g[l][None, :], kv_norm_g[l][None, :], wq[l], wqs[l],
                                  wk[l], wv[l], vone, cq_t, ck_t, s1_t, s2_t, B, S)
        y_a = _attention(q16, k16, v16, B, S)
        y_p = _pool(proj, w_pool_bd[l], pool_scale[l][None, :], B, S)
        h, xn, logits_t = _outproj(y_m, y_a, y_p, h, w_out16[l], norm2_g[l][None, :],
                                   w_router_t[l], b_router[l][:, None])
        eidx, wts, rank, counts = _router(logits_t, tri, tr)
        dest, blk_e, blk_cnt, blk_next, nblk = _meta(counts, eidx, rank, nb_pad)
        dest_flat = dest.reshape(TOP_K * T)
        x_rows = _sc_dispatch(xn, dest_flat, nb * MOE_ROWS)
        y_rows = _gmm(blk_e[0], blk_cnt[0], blk_next[0], nblk[0], x_rows, wgu_all, bgu_all, wdn_all,
                      bdn_all, nb, l)
        moe = (_sc_gather(y_rows, dest_flat), wts.T)
    return _final(h, moe[0], moe[1], final_norm_g[None, :]).reshape(B, S, D)
```

```python
import functools

import jax
import jax.numpy as jnp
import numpy as np
from jax import lax
from jax.experimental import pallas as pl
from jax.experimental.pallas import tpu as pltpu
from jax.experimental.pallas import tpu_sc as plsc

F32 = jnp.float32
BF16 = jnp.bfloat16
HIGHEST = lax.Precision.HIGHEST

D_MODEL = 1024
M_HEADS = 4
M_HEAD_DIM = 64
M_WIDTH = 256
M_CONV = 4
M_CHUNK = 64
A_HEADS = 8
A_NOPE = 64
A_ROPE = 32
A_VDIM = 64
A_QRANK = 256
A_KVRANK = 128
A_WIDTH = 512
ROPE_THETA = 10000.0
P_WINDOWS = (2, 4, 8, 16)
P_WIDTH = 256
N_EXPERTS = 32
TOP_K = 4
D_FF = 1024
SWIGLU_LIMIT = 7.0
SWIGLU_ALPHA = 1.702
EPS = 1e-6

LANES = 128
SUBLANES = 8
BF16_ROWS = 16

PROJ_QKVO = 0
PROJ_CQ = 1024
PROJ_UP = 1280
PROJ_CKV = 1536
PROJ_SMALL = 1664
PROJ_WIDTH = 1792
SMALL_KR = 64
SMALL_GATE = 96

MOE_ROWS = 1024
MOE_SPLIT = 4
VMEM_LIMIT = 56 * 1024 * 1024


def _cparams(*sem):
    return pltpu.CompilerParams(dimension_semantics=sem, vmem_limit_bytes=VMEM_LIMIT)


def _sigmoid(x):
    return 1.0 / (1.0 + jnp.exp(-x))


def _log_sigmoid(x):
    return jnp.minimum(x, 0.0) - jnp.log(1.0 + jnp.exp(-jnp.abs(x)))


def _dot(a, b, **kw):
    return jnp.dot(a, b, preferred_element_type=F32, **kw)


def _dot_nt(a, b, **kw):
    return lax.dot_general(a, b, (((1,), (1,)), ((), ())), preferred_element_type=F32, **kw)


def _dot_tn(a, b, **kw):
    return lax.dot_general(a, b, (((0,), (0,)), ((), ())), preferred_element_type=F32, **kw)


def _bf16_terms(x, terms=3):
    out = []
    for _ in range(terms):
        piece = x.astype(BF16)
        out.append(piece)
        x = x - piece.astype(F32)
    return out


def _dot_sel(x, sel16, terms=3):
    return sum(_dot(p, sel16) for p in _bf16_terms(x, terms))


def _sel_dot(sel16, x, terms=3):
    return sum(_dot(sel16, p) for p in _bf16_terms(x, terms))


def _pack_bf16_pairs(x):
    n = x.shape[1] // 2
    lo = lax.bitcast_convert_type(x[:, :n].astype(BF16).astype(F32), jnp.uint32)
    hi = lax.bitcast_convert_type(x[:, n:].astype(BF16).astype(F32), jnp.uint32)
    return (lo >> 16) | (hi & jnp.uint32(0xFFFF0000))


def _unpack_bf16_pairs(p):
    lo = lax.bitcast_convert_type(p << 16, F32)
    hi = lax.bitcast_convert_type(p & jnp.uint32(0xFFFF0000), F32)
    return lo, hi


ROW_SUBTILE = 128


def _row_slices(tm):
    return [slice(r, r + ROW_SUBTILE) for r in range(0, tm, ROW_SUBTILE)]


def _moe_combine(h_ref, y_refs, w_ref, rows):
    w = w_ref[rows, :]
    dp = D_MODEL // 2
    acc_lo = h_ref[rows, :dp]
    acc_hi = h_ref[rows, dp:]
    for kk, y_ref in enumerate(y_refs):
        lo, hi = _unpack_bf16_pairs(y_ref[rows, :])
        acc_lo = acc_lo + lo * w[:, kk:kk + 1]
        acc_hi = acc_hi + hi * w[:, kk:kk + 1]
    return jnp.concatenate([acc_lo, acc_hi], axis=1)


def _combine_specs(tm, nt):
    y_specs = [pl.BlockSpec((tm, D_MODEL // 2), functools.partial(lambda i, kk: (kk * nt + i, 0), kk=kk))
               for kk in range(TOP_K)]
    return [pl.BlockSpec((tm, D_MODEL), lambda i: (i, 0))] + y_specs + [pl.BlockSpec((tm, TOP_K), lambda i: (i, 0))]


def _inproj_kernel(*refs, combine):
    if combine:
        h_ref, y0, y1, y2, y3, wk_ref, g_ref, w_ref, hn_ref, o_ref, sm_ref = refs
    else:
        h_ref, g_ref, w_ref, o_ref, sm_ref = refs
    for rows in _row_slices(h_ref.shape[0]):
        if combine:
            x = _moe_combine(h_ref, (y0, y1, y2, y3), wk_ref, rows)
            hn_ref[rows, :] = x
        else:
            x = h_ref[rows, :]
        ms = jnp.mean(x * x, axis=-1, keepdims=True)
        xn = x * lax.rsqrt(ms + EPS) * g_ref[...]
        res = _dot(xn.astype(BF16), w_ref[...])
        o_ref[rows, :] = res[:, :PROJ_SMALL].astype(BF16)
        sm_ref[rows, :] = res[:, PROJ_SMALL:]


def _inproj(h2, g, w, moe=None, tm=512):
    T = h2.shape[0]
    nt = T // tm
    w_specs = [pl.BlockSpec((1, D_MODEL), lambda i: (0, 0)),
               pl.BlockSpec((D_MODEL, PROJ_WIDTH), lambda i: (0, 0))]
    proj_specs = [pl.BlockSpec((tm, PROJ_SMALL), lambda i: (i, 0)),
                  pl.BlockSpec((tm, PROJ_WIDTH - PROJ_SMALL), lambda i: (i, 0))]
    proj_shapes = [jax.ShapeDtypeStruct((T, PROJ_SMALL), BF16),
                   jax.ShapeDtypeStruct((T, PROJ_WIDTH - PROJ_SMALL), F32)]
    if moe is None:
        proj, small = pl.pallas_call(
            functools.partial(_inproj_kernel, combine=False),
            grid=(nt,),
            in_specs=[pl.BlockSpec((tm, D_MODEL), lambda i: (i, 0))] + w_specs,
            out_specs=proj_specs,
            out_shape=proj_shapes,
            compiler_params=_cparams("parallel"),
            name="inproj",
        )(h2, g, w)
        return h2, proj, small
    y_tok, w_tok = moe
    return pl.pallas_call(
        functools.partial(_inproj_kernel, combine=True),
        grid=(nt,),
        in_specs=_combine_specs(tm, nt) + w_specs,
        out_specs=[pl.BlockSpec((tm, D_MODEL), lambda i: (i, 0))] + proj_specs,
        out_shape=[jax.ShapeDtypeStruct((T, D_MODEL), F32)] + proj_shapes,
        compiler_params=_cparams("parallel"),
        name="combine_inproj",
    )(h2, y_tok, y_tok, y_tok, y_tok, w_tok, g, w)


M_SEQS = 2
M_UNROLL = 2


def _mlstm_kernel(proj_ref, small_ref, cw_ref, cb_ref, gbc_ref, ng_ref, o_ref, ct_ref, n_ref, m_ref):
    S = proj_ref.shape[0] // M_SEQS
    L = M_CHUNK
    nc = S // L
    W = M_WIDTH
    ct_ref[...] = jnp.zeros_like(ct_ref)
    n_ref[...] = jnp.zeros_like(n_ref)
    m_ref[...] = jnp.zeros_like(m_ref)

    rh = lax.broadcasted_iota(jnp.int32, (W, W), 0) // M_HEAD_DIM
    chd = lax.broadcasted_iota(jnp.int32, (W, W), 1) // M_HEAD_DIM
    same_head = (rh == chd).astype(F32)
    same_head16 = same_head.astype(BF16)
    tril16 = (lax.broadcasted_iota(jnp.int32, (L, L), 0) >= lax.broadcasted_iota(jnp.int32, (L, L), 1)).astype(BF16)
    row = lax.broadcasted_iota(jnp.int32, (L, W), 0)
    key = lax.broadcasted_iota(jnp.int32, (L, W), 1) % M_HEAD_DIM
    causal = key <= row
    diag = (key == row).astype(F32)
    er = lax.broadcasted_iota(jnp.int32, (LANES, 2 * W), 0)
    ec = lax.broadcasted_iota(jnp.int32, (LANES, 2 * W), 1)
    spread16 = (er == SMALL_GATE + ec // M_HEAD_DIM).astype(BF16)
    forget_lane = lax.broadcasted_iota(jnp.int32, (1, LANES), 1) >= SMALL_GATE + M_HEADS
    cw = cw_ref[...]
    cb = cb_ref[...]
    gbc = gbc_ref[...]
    ng = ng_ref[...]

    def prefix_max(x):
        s = 1
        while s < L:
            x = jnp.maximum(x, jnp.where(row >= s, pltpu.roll(x, s, axis=0), -jnp.inf))
            s *= 2
        return x

    def chunk(sq, c):
        r0 = pl.multiple_of(sq * S + c * L, L)
        h0 = pl.multiple_of(sq * S + jnp.maximum(c * L - BF16_ROWS, 0), BF16_ROWS)
        halo = proj_ref[pl.ds(h0, BF16_ROWS), 0:2 * W].astype(F32)[BF16_ROWS - SUBLANES:, :]
        halo = halo * jnp.where(c > 0, 1.0, 0.0)
        win = jnp.concatenate([halo, proj_ref[pl.ds(r0, L), 0:2 * W].astype(F32)], axis=0)
        acc = jnp.zeros((L, 2 * W), F32) + cb
        for j in range(M_CONV):
            s = M_CONV - 1 - j
            xs = win if s == 0 else pltpu.roll(win, s, axis=0)
            acc = acc + xs[SUBLANES:, :] * cw[j:j + 1, :]
        qk = acc * _sigmoid(acc)
        q = qk[:, 0:W]
        k = qk[:, W:2 * W] * (M_HEAD_DIM ** -0.5)
        v16 = proj_ref[pl.ds(r0, L), 2 * W:3 * W]
        v = v16.astype(F32)
        og = proj_ref[pl.ds(r0, L), 3 * W:4 * W].astype(F32)
        q16 = q.astype(BF16)
        k16 = k.astype(BF16)
        ct = ct_ref[sq]
        nvec = n_ref[sq]
        m_old = m_ref[sq]
        pre = small_ref[pl.ds(r0, L), :] + gbc
        gates = _dot_sel(jnp.where(forget_lane, _log_sigmoid(pre), pre), spread16)
        kb = jnp.concatenate([k16] * M_HEADS, axis=0) * same_head16
        vb = jnp.concatenate([v16] * M_HEADS, axis=0) * same_head16
        qk_all = _dot_nt(q16, kb)
        q_c = _dot(q16, ct.astype(BF16))
        q_n = _dot_sel(q * nvec, same_head16, terms=2)
        yield

        i_pre = gates[:, :W]
        g = _sel_dot(tril16, gates[:, W:])
        yield
        a = i_pre - g
        a_key = jnp.sum(a * diag, axis=0, keepdims=True)
        a_max = prefix_max(a)
        m_row = g + jnp.maximum(m_old, a_max)
        p = qk_all * jnp.exp(jnp.where(causal, g + a_key, -jnp.inf) - m_row)
        p16 = p.astype(BF16)
        num_intra = _dot(p16, vb)
        rowsum = _dot_sel(p, same_head16, terms=2)
        yield

        w_inter = jnp.exp(g + m_old - m_row)
        num = w_inter * q_c + num_intra
        den = w_inter * q_n + rowsum
        hv = num / jnp.maximum(jnp.abs(den), jnp.exp(-m_row))
        ms = _dot_sel(hv * hv, same_head16, terms=2) * (1.0 / M_HEAD_DIM)
        yield
        y = hv * lax.rsqrt(ms + EPS) * ng * _sigmoid(og)
        o_ref[pl.ds(r0, L), :] = y.astype(o_ref.dtype)

        g_end = g[L - 1:L, :]
        m_new = g_end + jnp.maximum(m_old, a_max[L - 1:L, :])
        wa = jnp.exp(g_end + a - m_new)
        decay = jnp.exp(g_end + m_old - m_new)
        upd = _dot_tn(k16, (wa * v).astype(BF16))
        ct_ref[sq] = decay * ct + upd * same_head
        n_ref[sq] = decay * nvec + jnp.sum(wa * k, axis=0, keepdims=True)
        m_ref[sq] = m_new
        yield

    def body(cc, carry):
        for u in range(M_UNROLL):
            for _ in zip(*[chunk(sq, cc * M_UNROLL + u) for sq in range(M_SEQS)]):
                pass
        return carry

    lax.fori_loop(0, nc // M_UNROLL, body, 0)


def _mlstm(proj, small, cw, cb, gbc, ng, B, S):
    assert M_CHUNK == M_HEAD_DIM
    T = B * S
    rows = M_SEQS * S
    return pl.pallas_call(
        _mlstm_kernel,
        grid=(B // M_SEQS,),
        in_specs=[pl.BlockSpec((rows, 4 * M_WIDTH), lambda b: (b, 0)),
                  pl.BlockSpec((rows, LANES), lambda b: (b, 0)),
                  pl.BlockSpec((M_CONV, 2 * M_WIDTH), lambda b: (0, 0)),
                  pl.BlockSpec((1, 2 * M_WIDTH), lambda b: (0, 0)),
                  pl.BlockSpec((1, LANES), lambda b: (0, 0)),
                  pl.BlockSpec((1, M_WIDTH), lambda b: (0, 0))],
        out_specs=pl.BlockSpec((rows, M_WIDTH), lambda b: (b, 0)),
        out_shape=jax.ShapeDtypeStruct((T, M_WIDTH), BF16),
        scratch_shapes=[pltpu.VMEM((M_SEQS, M_WIDTH, M_WIDTH), F32),
                        pltpu.VMEM((M_SEQS, 1, M_WIDTH), F32),
                        pltpu.VMEM((M_SEQS, 1, M_WIDTH), F32)],
        compiler_params=_cparams("parallel"),
        name="mlstm",
    )(proj, small, cw, cb, gbc, ng)


POOL_HALO = 16
POOL_TILE = 256


def _pool_kernel(u_ref, w_ref, sc_ref, o_ref, upad_ref):
    S = u_ref.shape[0]
    upad_ref[0:POOL_HALO, :] = jnp.zeros((POOL_HALO, P_WIDTH), F32)
    upad_ref[POOL_HALO:, :] = u_ref[...].astype(F32)
    grp = lax.broadcasted_iota(jnp.int32, (1, P_WIDTH), 1) // (P_WIDTH // len(P_WINDOWS))
    win_lane = jnp.zeros((1, P_WIDTH), jnp.int32)
    for gi, wn in enumerate(P_WINDOWS):
        win_lane = jnp.where(grp == gi, wn, win_lane)
    w = w_ref[...]
    scale = sc_ref[...]
    rows = POOL_TILE + POOL_HALO

    for r0 in range(0, S, POOL_TILE):
        a = upad_ref[r0:r0 + rows, :]
        sums = []
        cur = a
        span = 1
        for _ in P_WINDOWS:
            cur = cur + pltpu.roll(cur, span, axis=0)
            span *= 2
            sums.append(cur)
        sel = sums[-1]
        for gi in range(len(P_WINDOWS) - 1):
            sel = jnp.where(grp == gi, sums[gi], sel)
        sel = sel[POOL_HALO:, :]
        u = a[POOL_HALO:, :]
        t = r0 + lax.broadcasted_iota(jnp.int32, (POOL_TILE, P_WIDTH), 0)
        cnt = jnp.minimum(t + 1, win_lane).astype(F32)
        pooled = sel / cnt - u
        o_ref[r0:r0 + POOL_TILE, :] = (_dot(pooled.astype(BF16), w) * scale).astype(o_ref.dtype)


def _pool(proj, w_bd, scale, B, S):
    T = B * S
    return pl.pallas_call(
        _pool_kernel,
        grid=(B,),
        in_specs=[pl.BlockSpec((S, P_WIDTH), lambda b: (b, PROJ_UP // P_WIDTH)),
                  pl.BlockSpec((P_WIDTH, P_WIDTH), lambda b: (0, 0)),
                  pl.BlockSpec((1, P_WIDTH), lambda b: (0, 0))],
        out_specs=pl.BlockSpec((S, P_WIDTH), lambda b: (b, 0)),
        out_shape=jax.ShapeDtypeStruct((T, P_WIDTH), BF16),
        scratch_shapes=[pltpu.VMEM((S + POOL_HALO, P_WIDTH), F32)],
        compiler_params=_cparams("parallel"),
        name="pool",
    )(proj, w_bd, scale)


def _rope(x, c, s1, s2):
    return x * c + pltpu.roll(x, LANES - A_ROPE // 2, axis=1) * s1 + pltpu.roll(x, A_ROPE // 2, axis=1) * s2


def _mla_prep_kernel(cq_ref, ckv_ref, small_ref, qg_ref, kvg_ref, wq_ref, wqs_ref, wk_ref, wv_ref,
                     vone_ref, cq_t_ref, ck_t_ref, s1_ref, s2_ref, q_ref, k_ref, v_ref):
    def rms(x, g):
        return x * lax.rsqrt(jnp.mean(x * x, axis=-1, keepdims=True) + EPS) * g

    cqn = rms(cq_ref[...].astype(F32), qg_ref[...]).astype(BF16)
    ckvn = rms(ckv_ref[...].astype(F32), kvg_ref[...]).astype(BF16)
    scale = (A_NOPE + A_ROPE) ** -0.5
    qf = _dot(cqn, wq_ref[...]) * scale
    qp = _dot(cqn, wqs_ref[...]) * scale
    kf = _dot(ckvn, wk_ref[...])
    v_ref[...] = (_dot(ckvn, wv_ref[...]) + vone_ref[...]).astype(BF16)
    cqt = cq_t_ref[...]
    s1 = s1_ref[...]
    s2 = s2_ref[...]
    krot = _rope(small_ref[...], ck_t_ref[...], s1, s2)
    sq = s1 + s2
    for h in range(A_HEADS):
        sl = slice(h * LANES, (h + 1) * LANES)
        q_ref[:, sl] = (qf[:, sl] * cqt + qp[:, sl] * sq).astype(BF16)
        k_ref[:, sl] = (kf[:, sl] + krot).astype(BF16)


def _mla_prep(proj, small, qg, kvg, wq, wqs, wk, wv, vone, cq_t, ck_t, s1_t, s2_t, B, S, ts=1024):
    T = B * S
    nst = S // ts
    hw = A_HEADS * LANES
    return pl.pallas_call(
        _mla_prep_kernel,
        grid=(B, nst),
        in_specs=[pl.BlockSpec((ts, A_QRANK), lambda b, s: (b * nst + s, PROJ_CQ // A_QRANK)),
                  pl.BlockSpec((ts, A_KVRANK), lambda b, s: (b * nst + s, PROJ_CKV // A_KVRANK)),
                  pl.BlockSpec((ts, LANES), lambda b, s: (b * nst + s, 0)),
                  pl.BlockSpec((1, A_QRANK), lambda b, s: (0, 0)),
                  pl.BlockSpec((1, A_KVRANK), lambda b, s: (0, 0)),
                  pl.BlockSpec((A_QRANK, hw), lambda b, s: (0, 0)),
                  pl.BlockSpec((A_QRANK, hw), lambda b, s: (0, 0)),
                  pl.BlockSpec((A_KVRANK, hw), lambda b, s: (0, 0)),
                  pl.BlockSpec((A_KVRANK, hw), lambda b, s: (0, 0)),
                  pl.BlockSpec((1, hw), lambda b, s: (0, 0)),
                  pl.BlockSpec((ts, LANES), lambda b, s: (s, 0)),
                  pl.BlockSpec((ts, LANES), lambda b, s: (s, 0)),
                  pl.BlockSpec((ts, LANES), lambda b, s: (s, 0)),
                  pl.BlockSpec((ts, LANES), lambda b, s: (s, 0))],
        out_specs=[pl.BlockSpec((ts, hw), lambda b, s: (b * nst + s, 0)),
                   pl.BlockSpec((ts, hw), lambda b, s: (b * nst + s, 0)),
                   pl.BlockSpec((ts, hw), lambda b, s: (b * nst + s, 0))],
        out_shape=[jax.ShapeDtypeStruct((T, hw), BF16),
                   jax.ShapeDtypeStruct((T, hw), BF16),
                   jax.ShapeDtypeStruct((T, hw), BF16)],
        compiler_params=_cparams("parallel", "parallel"),
        name="mla_prep",
    )(proj, proj, small, qg, kvg, wq, wqs, wk, wv, vone, cq_t, ck_t, s1_t, s2_t)


def _attn_kernel(q_ref, k_ref, v_ref, o_ref, *, tq):
    heads = range(2)
    sls = [slice(hh * LANES, (hh + 1) * LANES) for hh in heads]
    lane = lax.broadcasted_iota(jnp.int32, (tq, LANES), 1)
    below_diag = lax.broadcasted_iota(jnp.int32, (tq, tq), 0) >= lax.broadcasted_iota(jnp.int32, (tq, tq), 1)

    def update(qs, k0, state, causal):
        keys = slice(k0, k0 + tq)
        s = [_dot_nt(qs[hh], k_ref[keys, sls[hh]]) for hh in heads]
        if causal:
            s = [jnp.where(below_diag, s[hh], -jnp.inf) for hh in heads]
        m_new = [jnp.maximum(state[hh][0], jnp.max(s[hh], axis=-1, keepdims=True)) for hh in heads]
        p = [jnp.exp((s[hh] - m_new[hh]).astype(BF16)) for hh in heads]
        pv = [_dot(p[hh], v_ref[keys, sls[hh]]) for hh in heads]
        acc = [jnp.exp(state[hh][0] - m_new[hh]) * state[hh][1] + pv[hh] for hh in heads]
        return tuple((m_new[hh], acc[hh]) for hh in heads)

    for qi in range(q_ref.shape[0] // tq):
        rows = slice(qi * tq, (qi + 1) * tq)
        qs = [q_ref[rows, sl] for sl in sls]
        state = tuple((jnp.full((tq, 1), -jnp.inf, F32), jnp.zeros((tq, LANES), F32)) for _ in heads)
        for kb in range(qi + 1):
            state = update(qs, kb * tq, state, causal=(kb == qi))
        (_, acc0), (_, acc1) = state
        acc = jnp.where(lane < A_VDIM, acc0, acc1)
        den = jnp.where(lane < A_VDIM, pltpu.roll(acc0, A_VDIM, axis=1), pltpu.roll(acc1, A_VDIM, axis=1))
        o_ref[rows, :] = (acc / den).astype(o_ref.dtype)


def _attention(q, k, v, B, S, tq=512):
    T = B * S
    return pl.pallas_call(
        functools.partial(_attn_kernel, tq=tq),
        grid=(B, A_HEADS // 2),
        in_specs=[pl.BlockSpec((S, 2 * LANES), lambda b, p: (b, p)),
                  pl.BlockSpec((S, 2 * LANES), lambda b, p: (b, p)),
                  pl.BlockSpec((S, 2 * LANES), lambda b, p: (b, p))],
        out_specs=pl.BlockSpec((S, LANES), lambda b, p: (b, p)),
        out_shape=jax.ShapeDtypeStruct((T, A_WIDTH), BF16),
        compiler_params=_cparams("parallel", "parallel"),
        name="attention",
    )(q, k, v)


def _outproj_kernel(ym_ref, ya_ref, yp_ref, h_ref, w_ref, g_ref, wr_ref, br_ref,
                    hn_ref, xn_ref, lg_ref):
    mix = _dot(ym_ref[...], w_ref[0:M_WIDTH, :])
    mix = mix + _dot(ya_ref[...], w_ref[M_WIDTH:M_WIDTH + A_WIDTH, :])
    mix = mix + _dot(yp_ref[...], w_ref[M_WIDTH + A_WIDTH:, :])
    hn = h_ref[...] + mix
    hn_ref[...] = hn
    xn = hn * lax.rsqrt(jnp.mean(hn * hn, axis=-1, keepdims=True) + EPS) * g_ref[...]
    x_hi = xn.astype(BF16)
    x_lo = (xn - x_hi.astype(F32)).astype(BF16)
    wr = wr_ref[...]
    w_hi = wr.astype(BF16)
    w_lo = (wr - w_hi.astype(F32)).astype(BF16)
    lg_ref[...] = _dot_nt(w_hi, x_hi) + _dot_nt(w_hi, x_lo) + _dot_nt(w_lo, x_hi) + br_ref[...]
    xn_ref[...] = _pack_bf16_pairs(xn)


def _outproj(ym, ya, yp, h2, w, g, wr_t, br, tm=1024):
    T = h2.shape[0]
    return pl.pallas_call(
        _outproj_kernel,
        grid=(T // tm,),
        in_specs=[pl.BlockSpec((tm, M_WIDTH), lambda i: (i, 0)),
                  pl.BlockSpec((tm, A_WIDTH), lambda i: (i, 0)),
                  pl.BlockSpec((tm, P_WIDTH), lambda i: (i, 0)),
                  pl.BlockSpec((tm, D_MODEL), lambda i: (i, 0)),
                  pl.BlockSpec((D_MODEL, D_MODEL), lambda i: (0, 0)),
                  pl.BlockSpec((1, D_MODEL), lambda i: (0, 0)),
                  pl.BlockSpec((N_EXPERTS, D_MODEL), lambda i: (0, 0)),
                  pl.BlockSpec((N_EXPERTS, 1), lambda i: (0, 0))],
        out_specs=[pl.BlockSpec((tm, D_MODEL), lambda i: (i, 0)),
                   pl.BlockSpec((tm, D_MODEL // 2), lambda i: (i, 0)),
                   pl.BlockSpec((N_EXPERTS, tm), lambda i: (0, i))],
        out_shape=[jax.ShapeDtypeStruct((T, D_MODEL), F32),
                   jax.ShapeDtypeStruct((T, D_MODEL // 2), jnp.uint32),
                   jax.ShapeDtypeStruct((N_EXPERTS, T), F32)],
        compiler_params=_cparams("parallel"),
        name="outproj",
    )(ym, ya, yp, h2, w, g, wr_t, br)


def _router_kernel(lg_ref, tri_ref, e_ref, w_ref, r_ref, cnt_ref, carry_ref):
    tr = lg_ref.shape[1]

    @pl.when(pl.program_id(0) == 0)
    def _():
        carry_ref[...] = jnp.zeros_like(carry_ref)

    x = lg_ref[...]
    eio = lax.broadcasted_iota(jnp.int32, (N_EXPERTS, tr), 0).astype(F32)
    picked = jnp.zeros((N_EXPERTS, tr), F32)
    vals = []
    idxs = []
    for _ in range(TOP_K):
        mx = jnp.max(x, axis=0, keepdims=True)
        idx = jnp.min(jnp.where(x == mx, eio, float(N_EXPERTS)), axis=0, keepdims=True)
        hit = eio == idx
        vals.append(mx)
        idxs.append(idx)
        picked = picked + hit.astype(F32)
        x = jnp.where(hit, -jnp.inf, x)
    exps = [jnp.exp(vv - vals[0]) for vv in vals]
    tot = exps[0] + exps[1] + exps[2] + exps[3]
    before = _dot(picked.astype(BF16), tri_ref[...]) + carry_ref[:, 0:1]
    for kk in range(TOP_K):
        e_ref[kk:kk + 1, :] = idxs[kk].astype(jnp.int32)
        w_ref[kk:kk + 1, :] = exps[kk] / tot
        rk = jnp.sum(jnp.where(eio == idxs[kk], before, 0.0), axis=0, keepdims=True)
        r_ref[kk:kk + 1, :] = rk.astype(jnp.int32)
    carry_ref[...] = carry_ref[...] + jnp.sum(picked, axis=1, keepdims=True)
    cnt_ref[...] = carry_ref[...]


def _router(logits_t, tri, tr=512):
    T = logits_t.shape[1]
    return pl.pallas_call(
        _router_kernel,
        grid=(T // tr,),
        in_specs=[pl.BlockSpec((N_EXPERTS, tr), lambda i: (0, i)),
                  pl.BlockSpec((tr, tr), lambda i: (0, 0))],
        out_specs=[pl.BlockSpec((TOP_K, tr), lambda i: (0, i)),
                   pl.BlockSpec((TOP_K, tr), lambda i: (0, i)),
                   pl.BlockSpec((TOP_K, tr), lambda i: (0, i)),
                   pl.BlockSpec((N_EXPERTS, LANES), lambda i: (0, 0))],
        out_shape=[jax.ShapeDtypeStruct((TOP_K, T), jnp.int32),
                   jax.ShapeDtypeStruct((TOP_K, T), F32),
                   jax.ShapeDtypeStruct((TOP_K, T), jnp.int32),
                   jax.ShapeDtypeStruct((N_EXPERTS, LANES), F32)],
        scratch_shapes=[pltpu.VMEM((N_EXPERTS, LANES), F32)],
        compiler_params=_cparams("arbitrary"),
        name="router",
    )(logits_t, tri)


def _meta_kernel(cnt_ref, e_ref, r_ref, dest_ref, be_ref, bc_ref, bn_ref, nb_ref, *, nb_pad):
    cnt = cnt_ref[...]
    padded = jnp.floor((cnt + (MOE_ROWS - 1)) * (1.0 / MOE_ROWS)) * MOE_ROWS
    ri = lax.broadcasted_iota(jnp.int32, (N_EXPERTS, N_EXPERTS), 0)
    ci = lax.broadcasted_iota(jnp.int32, (N_EXPERTS, N_EXPERTS), 1)
    pad_end = _dot((ri >= ci).astype(F32), padded, precision=HIGHEST)
    pad_start = pad_end - padded
    e = e_ref[...]
    dest = r_ref[...]
    for ex in range(N_EXPERTS):
        ps = pad_start[ex:ex + 1, 0:1].astype(jnp.int32)
        dest = jnp.where(e == ex, dest + ps, dest)
    dest_ref[...] = dest
    blk0 = (lax.broadcasted_iota(jnp.int32, (N_EXPERTS, nb_pad), 1) * MOE_ROWS).astype(F32)
    be = jnp.sum((pad_end[:, 0:1] <= blk0).astype(F32), axis=0, keepdims=True)
    be = jnp.minimum(be, float(N_EXPERTS - 1))
    eio = lax.broadcasted_iota(jnp.int32, (N_EXPERTS, nb_pad), 0).astype(F32)
    seg_end = jnp.sum(jnp.where(eio == be, pad_start[:, 0:1] + cnt[:, 0:1], 0.0), axis=0, keepdims=True)
    bc = jnp.clip(seg_end - blk0[0:1, :], 0.0, float(MOE_ROWS))
    nxt0 = jnp.sum(jnp.where(eio == be, pad_end[:, 0:1], 0.0), axis=0, keepdims=True)
    bn = jnp.sum((pad_end[:, 0:1] <= nxt0).astype(F32), axis=0, keepdims=True)
    bn = jnp.where(nxt0 < pad_end[N_EXPERTS - 1:N_EXPERTS, 0:1], bn, -1.0)
    be_ref[...] = be.astype(jnp.int32)
    bc_ref[...] = bc.astype(jnp.int32)
    bn_ref[...] = bn.astype(jnp.int32)
    nb_ref[...] = (pad_end[N_EXPERTS - 1:N_EXPERTS, :] * (1.0 / MOE_ROWS)).astype(jnp.int32)


def _meta(counts, eidx, rank, nb_pad):
    T = eidx.shape[1]
    return pl.pallas_call(
        functools.partial(_meta_kernel, nb_pad=nb_pad),
        out_shape=[jax.ShapeDtypeStruct((TOP_K, T), jnp.int32),
                   jax.ShapeDtypeStruct((1, nb_pad), jnp.int32),
                   jax.ShapeDtypeStruct((1, nb_pad), jnp.int32),
                   jax.ShapeDtypeStruct((1, nb_pad), jnp.int32),
                   jax.ShapeDtypeStruct((1, LANES), jnp.int32)],
        compiler_params=pltpu.CompilerParams(vmem_limit_bytes=VMEM_LIMIT),
        name="route_meta",
    )(counts, eidx, rank)


FF_CHUNK = 512


def _gmm_kernel(be_ref, bc_ref, bn_ref, nb_ref, x_ref, wgu_hbm, bgu_ref, wdn_hbm, bdn_ref, y_ref,
                wgu_st, wdn_st, wgu16, wdn16, sem, *, e0):
    i = pl.program_id(0)
    nblk = nb_ref[0]
    bm = MOE_ROWS

    def weight_copies(e):
        return (pltpu.make_async_copy(wgu_hbm.at[e0 + e], wgu_st, sem.at[0]),
                pltpu.make_async_copy(wdn_hbm.at[e0 + e], wdn_st, sem.at[1]))

    @pl.when(i == 0)
    def _():
        for cp in weight_copies(be_ref[0]):
            cp.start()

    @pl.when(i >= nblk)
    def _():
        y_ref[...] = jnp.zeros_like(y_ref)

    @pl.when(i < nblk)
    def _():
        e_changed = jnp.logical_or(i == 0, be_ref[i] != be_ref[jnp.maximum(i - 1, 0)])

        @pl.when(e_changed)
        def _():
            for cp in weight_copies(be_ref[i]):
                cp.wait()
            wgu16[...] = wgu_st[...].astype(BF16)
            wdn16[...] = wdn_st[...].astype(BF16)

            @pl.when(bn_ref[i] >= 0)
            def _():
                for cp in weight_copies(bn_ref[i]):
                    cp.start()

        def expert_rows(nrows):
            valid = lax.broadcasted_iota(jnp.int32, (nrows, 1), 0) < bc_ref[i]
            lo, hi = _unpack_bf16_pairs(jnp.where(valid, x_ref[0:nrows, :], jnp.uint32(0)))
            x16 = jnp.concatenate([lo.astype(BF16), hi.astype(BF16)], axis=1)
            acc = jnp.zeros((nrows, D_MODEL), F32) + bdn_ref[0]
            for c in range(D_FF // FF_CHUNK):
                cs = slice(c * FF_CHUNK, (c + 1) * FF_CHUNK)
                us = slice(D_FF + c * FF_CHUNK, D_FF + (c + 1) * FF_CHUNK)
                gate = _dot(x16, wgu16[:, cs]) + bgu_ref[0, :, cs]
                up = _dot(x16, wgu16[:, us]) + bgu_ref[0, :, us]
                gate = jnp.minimum(gate, SWIGLU_LIMIT)
                up = jnp.clip(up, -SWIGLU_LIMIT, SWIGLU_LIMIT)
                act = (up + 1.0) * gate * _sigmoid(SWIGLU_ALPHA * gate)
                acc = acc + _dot(act.astype(BF16), wdn16[cs, :])
            y_ref[0:nrows, :] = _pack_bf16_pairs(acc)

        quarter = bm // MOE_SPLIT
        for nq in range(1, MOE_SPLIT + 1):
            lower = (nq - 1) * quarter if nq > 1 else -1
            @pl.when(jnp.logical_and(bc_ref[i] > lower, bc_ref[i] <= nq * quarter))
            def _(nq=nq):
                expert_rows(nq * quarter)
                if nq < MOE_SPLIT:
                    y_ref[nq * quarter:, :] = jnp.zeros((bm - nq * quarter, D_MODEL // 2), jnp.uint32)


def _gmm(blk_e, blk_cnt, blk_next, nblk, x_rows, wgu, bgu, wdn, bdn, nb, layer):
    bm = MOE_ROWS
    e0 = layer * N_EXPERTS
    dp = D_MODEL // 2

    def expert(i, be, nb_ref):
        return (e0 + be[jnp.minimum(i, jnp.maximum(nb_ref[0] - 1, 0))], 0, 0)

    def rows(i, nb_ref):
        return (jnp.minimum(i, jnp.maximum(nb_ref[0] - 1, 0)), 0)

    grid_spec = pltpu.PrefetchScalarGridSpec(
        num_scalar_prefetch=4,
        grid=(nb,),
        in_specs=[pl.BlockSpec((bm, dp), lambda i, be, bc, bn, nbr: rows(i, nbr)),
                  pl.BlockSpec(memory_space=pl.ANY),
                  pl.BlockSpec((1, 1, 2 * D_FF), lambda i, be, bc, bn, nbr: expert(i, be, nbr)),
                  pl.BlockSpec(memory_space=pl.ANY),
                  pl.BlockSpec((1, 1, D_MODEL), lambda i, be, bc, bn, nbr: expert(i, be, nbr))],
        out_specs=pl.BlockSpec((bm, dp), lambda i, be, bc, bn, nbr: (i, 0)),
        scratch_shapes=[pltpu.VMEM((D_MODEL, 2 * D_FF), F32),
                        pltpu.VMEM((D_FF, D_MODEL), F32),
                        pltpu.VMEM((D_MODEL, 2 * D_FF), BF16),
                        pltpu.VMEM((D_FF, D_MODEL), BF16),
                        pltpu.SemaphoreType.DMA((2,))],
    )
    return pl.pallas_call(
        functools.partial(_gmm_kernel, e0=e0),
        grid_spec=grid_spec,
        out_shape=jax.ShapeDtypeStruct((nb * bm, dp), jnp.uint32),
        compiler_params=_cparams("arbitrary"),
        name="expert_gmm",
    )(blk_e, blk_cnt, blk_next, nblk, x_rows, wgu, bgu, wdn, bdn)


def _final_kernel(h_ref, y0, y1, y2, y3, wk_ref, g_ref, o_ref):
    for rows in _row_slices(h_ref.shape[0]):
        x = _moe_combine(h_ref, (y0, y1, y2, y3), wk_ref, rows)
        o_ref[rows, :] = x * lax.rsqrt(jnp.mean(x * x, axis=-1, keepdims=True) + EPS) * g_ref[...]


def _final(h2, y_tok, w_tok, g, tm=512):
    T = h2.shape[0]
    nt = T // tm
    return pl.pallas_call(
        _final_kernel,
        grid=(nt,),
        in_specs=_combine_specs(tm, nt) + [pl.BlockSpec((1, D_MODEL), lambda i: (0, 0))],
        out_specs=pl.BlockSpec((tm, D_MODEL), lambda i: (i, 0)),
        out_shape=jax.ShapeDtypeStruct((T, D_MODEL), F32),
        compiler_params=_cparams("parallel"),
        name="combine_final_norm",
    )(h2, y_tok, y_tok, y_tok, y_tok, w_tok, g)


SC_CORES = 2
SC_SUBCORES = 16
SC_LANES = 16
SC_WORKERS = SC_CORES * SC_SUBCORES
SC_WINDOW = 64


def _sc_mesh():
    return plsc.VectorSubcoreMesh(core_axis_name="c", subcore_axis_name="s")


def _sc_worker():
    return lax.axis_index("s") * SC_CORES + lax.axis_index("c")


def _sc_dispatch(xn, dest_flat, n_rows):
    T = xn.shape[0]
    tpw = T // SC_WORKERS
    nchunk = tpw // SC_WINDOW
    nvec = SC_WINDOW // SC_LANES

    @functools.partial(
        pl.kernel, out_type=jax.ShapeDtypeStruct((n_rows, xn.shape[1]), xn.dtype), mesh=_sc_mesh(),
        scratch_types=[pltpu.VMEM((TOP_K * tpw,), jnp.int32),
                       pltpu.VMEM((SC_WINDOW, xn.shape[1]), xn.dtype),
                       pltpu.VMEM((SC_WINDOW, xn.shape[1]), xn.dtype),
                       pltpu.SemaphoreType.DMA, pltpu.SemaphoreType.DMA, pltpu.SemaphoreType.DMA],
        name="sc_dispatch")
    def run(x_hbm, d_hbm, o_hbm, idx_v, buf0, buf1, sem0, sem1, sem_out):
        base = _sc_worker() * tpw
        for kk in range(TOP_K):
            pltpu.sync_copy(d_hbm.at[pl.ds(kk * T + base, tpw)], idx_v.at[pl.ds(kk * tpw, tpw)])
        bufs = (buf0, buf1)
        sems = (sem0, sem1)

        def load(c, slot):
            return pltpu.make_async_copy(x_hbm.at[pl.ds(base + c * SC_WINDOW, SC_WINDOW)], bufs[slot], sems[slot])

        load(0, 0).start()

        @pl.loop(0, nchunk, step=2)
        def _(c0):
            for slot in range(2):
                c = c0 + slot
                load(c, slot).wait()

                @pl.when(c + 1 < nchunk)
                def _():
                    load(c + 1, 1 - slot).start()

                copies = []
                for kk in range(TOP_K):
                    for q in range(nvec):
                        off = pl.multiple_of(kk * tpw + c * SC_WINDOW + q * SC_LANES, SC_LANES)
                        rows = idx_v[pl.ds(off, SC_LANES)]
                        cp = pltpu.make_async_copy(bufs[slot].at[pl.ds(q * SC_LANES, SC_LANES)],
                                                   o_hbm.at[rows], sem_out)
                        cp.start()
                        copies.append(cp)
                for cp in copies:
                    cp.wait()

    return run(xn, dest_flat)


def _sc_gather(y_rows, dest_flat):
    n = dest_flat.shape[0]
    rpw = n // SC_WORKERS
    nchunk = rpw // SC_WINDOW
    nvec = SC_WINDOW // SC_LANES

    @functools.partial(
        pl.kernel, out_type=jax.ShapeDtypeStruct((n, y_rows.shape[1]), y_rows.dtype), mesh=_sc_mesh(),
        scratch_types=[pltpu.VMEM((rpw,), jnp.int32),
                       pltpu.VMEM((SC_WINDOW, y_rows.shape[1]), y_rows.dtype),
                       pltpu.VMEM((SC_WINDOW, y_rows.shape[1]), y_rows.dtype),
                       pltpu.SemaphoreType.DMA, pltpu.SemaphoreType.DMA, pltpu.SemaphoreType.DMA],
        name="sc_gather")
    def run(y_hbm, d_hbm, o_hbm, idx_v, buf0, buf1, sem0, sem1, sem_in):
        base = _sc_worker() * rpw
        pltpu.sync_copy(d_hbm.at[pl.ds(base, rpw)], idx_v)
        bufs = (buf0, buf1)
        sems = (sem0, sem1)

        def store(c, slot):
            return pltpu.make_async_copy(bufs[slot], o_hbm.at[pl.ds(base + c * SC_WINDOW, SC_WINDOW)], sems[slot])

        @pl.loop(0, nchunk, step=2)
        def _(c0):
            for slot in range(2):
                c = c0 + slot

                @pl.when(c >= 2)
                def _():
                    store(c - 2, slot).wait()

                copies = []
                for q in range(nvec):
                    off = pl.multiple_of(c * SC_WINDOW + q * SC_LANES, SC_LANES)
                    rows = idx_v[pl.ds(off, SC_LANES)]
                    cp = pltpu.make_async_copy(y_hbm.at[rows], bufs[slot].at[pl.ds(q * SC_LANES, SC_LANES)], sem_in)
                    cp.start()
                    copies.append(cp)
                for cp in copies:
                    cp.wait()
                store(c, slot).start()

        store(nchunk - 2, 0).wait()
        store(nchunk - 1, 1).wait()

    return run(y_rows, dest_flat)


def _prep_w_in(w_in):
    o_g = 4 * M_WIDTH
    o_cq = o_g + 2 * M_HEADS
    o_ckv = o_cq + A_QRANK
    o_kr = o_ckv + A_KVRANK
    o_up = o_kr + A_ROPE
    z = lambda n: jnp.zeros(w_in.shape[:-1] + (n,), w_in.dtype)
    small = jnp.concatenate([z(SMALL_KR), w_in[..., o_kr:o_up], w_in[..., o_g:o_cq],
                             z(LANES - SMALL_GATE - 2 * M_HEADS)], axis=-1)
    return jnp.concatenate([w_in[..., 0:o_g], w_in[..., o_cq:o_ckv], w_in[..., o_up:o_up + P_WIDTH],
                            w_in[..., o_ckv:o_kr], small], axis=-1).astype(BF16)


def _rope_tables(seq):
    inv = ROPE_THETA ** (-jnp.arange(0, A_ROPE, 2, dtype=F32) / A_ROPE)
    ang = jnp.arange(seq, dtype=F32)[:, None] * inv[None, :]
    cos, sin = jnp.cos(ang), jnp.sin(ang)
    half = A_ROPE // 2
    zeros = lambda n: jnp.zeros((seq, n), F32)
    ones = lambda n: jnp.ones((seq, n), F32)
    tail = LANES - A_NOPE - A_ROPE
    cq_t = jnp.concatenate([ones(A_NOPE), cos, cos, zeros(tail)], axis=1)
    ck_t = jnp.concatenate([zeros(A_NOPE), cos, cos, zeros(tail)], axis=1)
    s1_t = jnp.concatenate([zeros(A_NOPE), -sin, zeros(half), zeros(tail)], axis=1)
    s2_t = jnp.concatenate([zeros(A_NOPE), zeros(half), sin, zeros(tail)], axis=1)
    return cq_t, ck_t, s1_t, s2_t


def kernel(x, norm1_g, w_in, conv_w, conv_b, gate_b, mlstm_norm_g, q_norm_g, kv_norm_g, w_uq, w_ukv,
           w_pool, pool_scale, w_out, norm2_g, w_router, b_router, w_gate_up, b_gate_up, w_down, b_down,
           final_norm_g):
    B, S, D = x.shape
    depth = w_in.shape[0]
    T = B * S
    nb = (T * TOP_K) // MOE_ROWS + N_EXPERTS
    nb_pad = -(-nb // LANES) * LANES

    w_in_p = _prep_w_in(w_in)
    wq = w_uq.reshape(depth, A_QRANK, A_HEADS, A_NOPE + A_ROPE)
    wq = jnp.pad(wq, ((0, 0), (0, 0), (0, 0), (0, LANES - A_NOPE - A_ROPE)))
    r0, r1, r2 = A_NOPE, A_NOPE + A_ROPE // 2, A_NOPE + A_ROPE
    wqs = jnp.concatenate([jnp.zeros_like(wq[..., :r0]), wq[..., r1:r2], wq[..., r0:r1],
                           jnp.zeros_like(wq[..., r2:])], axis=-1)
    wq = wq.reshape(depth, A_QRANK, A_HEADS * LANES).astype(BF16)
    wqs = wqs.reshape(depth, A_QRANK, A_HEADS * LANES).astype(BF16)
    wkv = w_ukv.reshape(depth, A_KVRANK, A_HEADS, A_NOPE + A_VDIM)
    wk = jnp.pad(wkv[..., :A_NOPE], ((0, 0), (0, 0), (0, 0), (0, LANES - A_NOPE)))
    wk = wk.reshape(depth, A_KVRANK, A_HEADS * LANES).astype(BF16)
    wv_e = jnp.pad(wkv[:, :, 0::2, A_NOPE:], ((0, 0), (0, 0), (0, 0), (0, LANES - A_VDIM)))
    wv_o = jnp.pad(wkv[:, :, 1::2, A_NOPE:], ((0, 0), (0, 0), (0, 0), (LANES - A_VDIM, 0)))
    wv = jnp.stack([wv_e, wv_o], axis=3).reshape(depth, A_KVRANK, A_HEADS * LANES).astype(BF16)
    half = jnp.arange(A_HEADS * LANES) // A_VDIM
    vone = ((half % 4 == 1) | (half % 4 == 2)).astype(F32)[None, :]
    gsz = P_WIDTH // len(P_WINDOWS)
    w_pool_bd = jnp.zeros((depth, P_WIDTH, P_WIDTH), F32)
    for gi in range(len(P_WINDOWS)):
        w_pool_bd = w_pool_bd.at[:, gi * gsz:(gi + 1) * gsz, gi * gsz:(gi + 1) * gsz].set(w_pool[:, gi])
    w_pool_bd = w_pool_bd.astype(BF16)
    w_out16 = w_out.astype(BF16)
    w_router_t = jnp.swapaxes(w_router, 1, 2)
    gate_b_col = jnp.pad(gate_b, ((0, 0), (SMALL_GATE, LANES - SMALL_GATE - 2 * M_HEADS)))
    cq_t, ck_t, s1_t, s2_t = _rope_tables(S)
    tr = 1024
    tri = (jnp.arange(tr)[:, None] < jnp.arange(tr)[None, :]).astype(BF16)

    wgu_all = w_gate_up.reshape(depth * N_EXPERTS, D_MODEL, 2 * D_FF)
    bgu_all = b_gate_up.reshape(depth * N_EXPERTS, 1, 2 * D_FF)
    wdn_all = w_down.reshape(depth * N_EXPERTS, D_FF, D_MODEL)
    bdn_all = b_down.reshape(depth * N_EXPERTS, 1, D_MODEL)

    h = x.reshape(T, D)
    moe = None
    for l in range(depth):
        h, proj, small = _inproj(h, norm1_g[l][None, :], w_in_p[l], moe)
        y_m = _mlstm(proj, small, conv_w[l], conv_b[l][None, :], gate_b_col[l][None, :],
                     mlstm_norm_g[l][None, :], B, S)
        q16, k16, v16 = _mla_prep(proj, small, q_norm_g[l][None, :], kv_norm_g[l][None, :], wq[l], wqs[l],
                                  wk[l], wv[l], vone, cq_t, ck_t, s1_t, s2_t, B, S)
        y_a = _attention(q16, k16, v16, B, S)
        y_p = _pool(proj, w_pool_bd[l], pool_scale[l][None, :], B, S)
        h, xn, logits_t = _outproj(y_m, y_a, y_p, h, w_out16[l], norm2_g[l][None, :],
                                   w_router_t[l], b_router[l][:, None])
        eidx, wts, rank, counts = _router(logits_t, tri, tr)
        dest, blk_e, blk_cnt, blk_next, nblk = _meta(counts, eidx, rank, nb_pad)
        dest_flat = dest.reshape(TOP_K * T)
        x_rows = _sc_dispatch(xn, dest_flat, nb * MOE_ROWS)
        y_rows = _gmm(blk_e[0], blk_cnt[0], blk_next[0], nblk[0], x_rows, wgu_all, bgu_all, wdn_all,
                      bdn_all, nb, l)
        moe = (_sc_gather(y_rows, dest_flat), wts.T)
    return _final(h, moe[0], moe[1], final_norm_g[None, :]).reshape(B, S, D)
```

```python
import functools

import jax
import jax.numpy as jnp
import numpy as np
from jax import lax
from jax.experimental import pallas as pl
from jax.experimental.pallas import tpu as pltpu
from jax.experimental.pallas import tpu_sc as plsc

F32 = jnp.float32
BF16 = jnp.bfloat16
HIGHEST = lax.Precision.HIGHEST

D_MODEL = 1024
M_HEADS = 4
M_HEAD_DIM = 64
M_WIDTH = 256
M_CONV = 4
M_CHUNK = 64
A_HEADS = 8
A_NOPE = 64
A_ROPE = 32
A_VDIM = 64
A_QRANK = 256
A_KVRANK = 128
A_WIDTH = 512
ROPE_THETA = 10000.0
P_WINDOWS = (2, 4, 8, 16)
P_WIDTH = 256
N_EXPERTS = 32
TOP_K = 4
D_FF = 1024
SWIGLU_LIMIT = 7.0
SWIGLU_ALPHA = 1.702
EPS = 1e-6

LANES = 128
SUBLANES = 8
BF16_ROWS = 16

PROJ_QKVO = 0
PROJ_CQ = 1024
PROJ_UP = 1280
PROJ_CKV = 1536
PROJ_SMALL = 1664
PROJ_WIDTH = 1792
SMALL_KR = 64
SMALL_GATE = 96

MOE_ROWS = 1024
MOE_SPLIT = 4
VMEM_LIMIT = 56 * 1024 * 1024


def _cparams(*sem):
    return pltpu.CompilerParams(dimension_semantics=sem, vmem_limit_bytes=VMEM_LIMIT)


def _sigmoid(x):
    return 1.0 / (1.0 + jnp.exp(-x))


def _log_sigmoid(x):
    return jnp.minimum(x, 0.0) - jnp.log(1.0 + jnp.exp(-jnp.abs(x)))


def _dot(a, b, **kw):
    return jnp.dot(a, b, preferred_element_type=F32, **kw)


def _dot_nt(a, b, **kw):
    return lax.dot_general(a, b, (((1,), (1,)), ((), ())), preferred_element_type=F32, **kw)


def _dot_tn(a, b, **kw):
    return lax.dot_general(a, b, (((0,), (0,)), ((), ())), preferred_element_type=F32, **kw)


def _bf16_terms(x, terms=3):
    out = []
    for _ in range(terms):
        piece = x.astype(BF16)
        out.append(piece)
        x = x - piece.astype(F32)
    return out


def _dot_sel(x, sel16, terms=3):
    return sum(_dot(p, sel16) for p in _bf16_terms(x, terms))


def _sel_dot(sel16, x, terms=3):
    return sum(_dot(sel16, p) for p in _bf16_terms(x, terms))


def _pack_bf16_pairs(x):
    n = x.shape[1] // 2
    lo = lax.bitcast_convert_type(x[:, :n].astype(BF16).astype(F32), jnp.uint32)
    hi = lax.bitcast_convert_type(x[:, n:].astype(BF16).astype(F32), jnp.uint32)
    return (lo >> 16) | (hi & jnp.uint32(0xFFFF0000))


def _unpack_bf16_pairs(p):
    lo = lax.bitcast_convert_type(p << 16, F32)
    hi = lax.bitcast_convert_type(p & jnp.uint32(0xFFFF0000), F32)
    return lo, hi


ROW_SUBTILE = 128


def _row_slices(tm):
    return [slice(r, r + ROW_SUBTILE) for r in range(0, tm, ROW_SUBTILE)]


def _moe_combine(h_ref, y_refs, w_ref, rows):
    w = w_ref[rows, :]
    dp = D_MODEL // 2
    acc_lo = h_ref[rows, :dp]
    acc_hi = h_ref[rows, dp:]
    for kk, y_ref in enumerate(y_refs):
        lo, hi = _unpack_bf16_pairs(y_ref[rows, :])
        acc_lo = acc_lo + lo * w[:, kk:kk + 1]
        acc_hi = acc_hi + hi * w[:, kk:kk + 1]
    return jnp.concatenate([acc_lo, acc_hi], axis=1)


def _combine_specs(tm, nt):
    y_specs = [pl.BlockSpec((tm, D_MODEL // 2), functools.partial(lambda i, kk: (kk * nt + i, 0), kk=kk))
               for kk in range(TOP_K)]
    return [pl.BlockSpec((tm, D_MODEL), lambda i: (i, 0))] + y_specs + [pl.BlockSpec((tm, TOP_K), lambda i: (i, 0))]


def _inproj_kernel(*refs, combine):
    if combine:
        h_ref, y0, y1, y2, y3, wk_ref, g_ref, w_ref, hn_ref, o_ref, sm_ref = refs
    else:
        h_ref, g_ref, w_ref, o_ref, sm_ref = refs
    for rows in _row_slices(h_ref.shape[0]):
        if combine:
            x = _moe_combine(h_ref, (y0, y1, y2, y3), wk_ref, rows)
            hn_ref[rows, :] = x
        else:
            x = h_ref[rows, :]
        ms = jnp.mean(x * x, axis=-1, keepdims=True)
        xn = x * lax.rsqrt(ms + EPS) * g_ref[...]
        res = _dot(xn.astype(BF16), w_ref[...])
        o_ref[rows, :] = res[:, :PROJ_SMALL].astype(BF16)
        sm_ref[rows, :] = res[:, PROJ_SMALL:]


def _inproj(h2, g, w, moe=None, tm=512):
    T = h2.shape[0]
    nt = T // tm
    w_specs = [pl.BlockSpec((1, D_MODEL), lambda i: (0, 0)),
               pl.BlockSpec((D_MODEL, PROJ_WIDTH), lambda i: (0, 0))]
    proj_specs = [pl.BlockSpec((tm, PROJ_SMALL), lambda i: (i, 0)),
                  pl.BlockSpec((tm, PROJ_WIDTH - PROJ_SMALL), lambda i: (i, 0))]
    proj_shapes = [jax.ShapeDtypeStruct((T, PROJ_SMALL), BF16),
                   jax.ShapeDtypeStruct((T, PROJ_WIDTH - PROJ_SMALL), F32)]
    if moe is None:
        proj, small = pl.pallas_call(
            functools.partial(_inproj_kernel, combine=False),
            grid=(nt,),
            in_specs=[pl.BlockSpec((tm, D_MODEL), lambda i: (i, 0))] + w_specs,
            out_specs=proj_specs,
            out_shape=proj_shapes,
            compiler_params=_cparams("parallel"),
            name="inproj",
        )(h2, g, w)
        return h2, proj, small
    y_tok, w_tok = moe
    return pl.pallas_call(
        functools.partial(_inproj_kernel, combine=True),
        grid=(nt,),
        in_specs=_combine_specs(tm, nt) + w_specs,
        out_specs=[pl.BlockSpec((tm, D_MODEL), lambda i: (i, 0))] + proj_specs,
        out_shape=[jax.ShapeDtypeStruct((T, D_MODEL), F32)] + proj_shapes,
        compiler_params=_cparams("parallel"),
        name="combine_inproj",
    )(h2, y_tok, y_tok, y_tok, y_tok, w_tok, g, w)


M_SEQS = 2
M_UNROLL = 2


def _mlstm_kernel(proj_ref, small_ref, cw_ref, cb_ref, gbc_ref, ng_ref, o_ref, ct_ref, n_ref, m_ref):
    S = proj_ref.shape[0] // M_SEQS
    L = M_CHUNK
    nc = S // L
    W = M_WIDTH
    ct_ref[...] = jnp.zeros_like(ct_ref)
    n_ref[...] = jnp.zeros_like(n_ref)
    m_ref[...] = jnp.zeros_like(m_ref)

    rh = lax.broadcasted_iota(jnp.int32, (W, W), 0) // M_HEAD_DIM
    chd = lax.broadcasted_iota(jnp.int32, (W, W), 1) // M_HEAD_DIM
    same_head = (rh == chd).astype(F32)
    same_head16 = same_head.astype(BF16)
    tril16 = (lax.broadcasted_iota(jnp.int32, (L, L), 0) >= lax.broadcasted_iota(jnp.int32, (L, L), 1)).astype(BF16)
    row = lax.broadcasted_iota(jnp.int32, (L, W), 0)
    key = lax.broadcasted_iota(jnp.int32, (L, W), 1) % M_HEAD_DIM
    causal = key <= row
    diag = (key == row).astype(F32)
    er = lax.broadcasted_iota(jnp.int32, (LANES, 2 * W), 0)
    ec = lax.broadcasted_iota(jnp.int32, (LANES, 2 * W), 1)
    spread16 = (er == SMALL_GATE + ec // M_HEAD_DIM).astype(BF16)
    forget_lane = lax.broadcasted_iota(jnp.int32, (1, LANES), 1) >= SMALL_GATE + M_HEADS
    cw = cw_ref[...]
    cb = cb_ref[...]
    gbc = gbc_ref[...]
    ng = ng_ref[...]

    def prefix_max(x):
        s = 1
        while s < L:
            x = jnp.maximum(x, jnp.where(row >= s, pltpu.roll(x, s, axis=0), -jnp.inf))
            s *= 2
        return x

    def chunk(sq, c):
        r0 = pl.multiple_of(sq * S + c * L, L)
        h0 = pl.multiple_of(sq * S + jnp.maximum(c * L - BF16_ROWS, 0), BF16_ROWS)
        halo = proj_ref[pl.ds(h0, BF16_ROWS), 0:2 * W].astype(F32)[BF16_ROWS - SUBLANES:, :]
        halo = halo * jnp.where(c > 0, 1.0, 0.0)
        win = jnp.concatenate([halo, proj_ref[pl.ds(r0, L), 0:2 * W].astype(F32)], axis=0)
        acc = jnp.zeros((L, 2 * W), F32) + cb
        for j in range(M_CONV):
            s = M_CONV - 1 - j
            xs = win if s == 0 else pltpu.roll(win, s, axis=0)
            acc = acc + xs[SUBLANES:, :] * cw[j:j + 1, :]
        qk = acc * _sigmoid(acc)
        q = qk[:, 0:W]
        k = qk[:, W:2 * W] * (M_HEAD_DIM ** -0.5)
        v16 = proj_ref[pl.ds(r0, L), 2 * W:3 * W]
        v = v16.astype(F32)
        og = proj_ref[pl.ds(r0, L), 3 * W:4 * W].astype(F32)
        q16 = q.astype(BF16)
        k16 = k.astype(BF16)
        ct = ct_ref[sq]
        nvec = n_ref[sq]
        m_old = m_ref[sq]
        pre = small_ref[pl.ds(r0, L), :] + gbc
        gates = _dot_sel(jnp.where(forget_lane, _log_sigmoid(pre), pre), spread16)
        kb = jnp.concatenate([k16] * M_HEADS, axis=0) * same_head16
        vb = jnp.concatenate([v16] * M_HEADS, axis=0) * same_head16
        qk_all = _dot_nt(q16, kb)
        q_c = _dot(q16, ct.astype(BF16))
        q_n = _dot_sel(q * nvec, same_head16, terms=2)
        yield

        i_pre = gates[:, :W]
        g = _sel_dot(tril16, gates[:, W:])
        yield
        a = i_pre - g
        a_key = jnp.sum(a * diag, axis=0, keepdims=True)
        a_max = prefix_max(a)
        m_row = g + jnp.maximum(m_old, a_max)
        p = qk_all * jnp.exp(jnp.where(causal, g + a_key, -jnp.inf) - m_row)
        p16 = p.astype(BF16)
        num_intra = _dot(p16, vb)
        rowsum = _dot_sel(p, same_head16, terms=2)
        yield

        w_inter = jnp.exp(g + m_old - m_row)
        num = w_inter * q_c + num_intra
        den = w_inter * q_n + rowsum
        hv = num / jnp.maximum(jnp.abs(den), jnp.exp(-m_row))
        ms = _dot_sel(hv * hv, same_head16, terms=2) * (1.0 / M_HEAD_DIM)
        yield
        y = hv * lax.rsqrt(ms + EPS) * ng * _sigmoid(og)
        o_ref[pl.ds(r0, L), :] = y.astype(o_ref.dtype)

        g_end = g[L - 1:L, :]
        m_new = g_end + jnp.maximum(m_old, a_max[L - 1:L, :])
        wa = jnp.exp(g_end + a - m_new)
        decay = jnp.exp(g_end + m_old - m_new)
        upd = _dot_tn(k16, (wa * v).astype(BF16))
        ct_ref[sq] = decay * ct + upd * same_head
        n_ref[sq] = decay * nvec + jnp.sum(wa * k, axis=0, keepdims=True)
        m_ref[sq] = m_new
        yield

    def body(cc, carry):
        for u in range(M_UNROLL):
            for _ in zip(*[chunk(sq, cc * M_UNROLL + u) for sq in range(M_SEQS)]):
                pass
        return carry

    lax.fori_loop(0, nc // M_UNROLL, body, 0)


def _mlstm(proj, small, cw, cb, gbc, ng, B, S):
    assert M_CHUNK == M_HEAD_DIM
    T = B * S
    rows = M_SEQS * S
    return pl.pallas_call(
        _mlstm_kernel,
        grid=(B // M_SEQS,),
        in_specs=[pl.BlockSpec((rows, 4 * M_WIDTH), lambda b: (b, 0)),
                  pl.BlockSpec((rows, LANES), lambda b: (b, 0)),
                  pl.BlockSpec((M_CONV, 2 * M_WIDTH), lambda b: (0, 0)),
                  pl.BlockSpec((1, 2 * M_WIDTH), lambda b: (0, 0)),
                  pl.BlockSpec((1, LANES), lambda b: (0, 0)),
                  pl.BlockSpec((1, M_WIDTH), lambda b: (0, 0))],
        out_specs=pl.BlockSpec((rows, M_WIDTH), lambda b: (b, 0)),
        out_shape=jax.ShapeDtypeStruct((T, M_WIDTH), BF16),
        scratch_shapes=[pltpu.VMEM((M_SEQS, M_WIDTH, M_WIDTH), F32),
                        pltpu.VMEM((M_SEQS, 1, M_WIDTH), F32),
                        pltpu.VMEM((M_SEQS, 1, M_WIDTH), F32)],
        compiler_params=_cparams("parallel"),
        name="mlstm",
    )(proj, small, cw, cb, gbc, ng)


POOL_HALO = 16
POOL_TILE = 256


def _pool_kernel(u_ref, w_ref, sc_ref, o_ref, upad_ref):
    S = u_ref.shape[0]
    upad_ref[0:POOL_HALO, :] = jnp.zeros((POOL_HALO, P_WIDTH), F32)
    upad_ref[POOL_HALO:, :] = u_ref[...].astype(F32)
    grp = lax.broadcasted_iota(jnp.int32, (1, P_WIDTH), 1) // (P_WIDTH // len(P_WINDOWS))
    win_lane = jnp.zeros((1, P_WIDTH), jnp.int32)
    for gi, wn in enumerate(P_WINDOWS):
        win_lane = jnp.where(grp == gi, wn, win_lane)
    w = w_ref[...]
    scale = sc_ref[...]
    rows = POOL_TILE + POOL_HALO

    for r0 in range(0, S, POOL_TILE):
        a = upad_ref[r0:r0 + rows, :]
        sums = []
        cur = a
        span = 1
        for _ in P_WINDOWS:
            cur = cur + pltpu.roll(cur, span, axis=0)
            span *= 2
            sums.append(cur)
        sel = sums[-1]
        for gi in range(len(P_WINDOWS) - 1):
            sel = jnp.where(grp == gi, sums[gi], sel)
        sel = sel[POOL_HALO:, :]
        u = a[POOL_HALO:, :]
        t = r0 + lax.broadcasted_iota(jnp.int32, (POOL_TILE, P_WIDTH), 0)
        cnt = jnp.minimum(t + 1, win_lane).astype(F32)
        pooled = sel / cnt - u
        o_ref[r0:r0 + POOL_TILE, :] = (_dot(pooled.astype(BF16), w) * scale).astype(o_ref.dtype)


def _pool(proj, w_bd, scale, B, S):
    T = B * S
    return pl.pallas_call(
        _pool_kernel,
        grid=(B,),
        in_specs=[pl.BlockSpec((S, P_WIDTH), lambda b: (b, PROJ_UP // P_WIDTH)),
                  pl.BlockSpec((P_WIDTH, P_WIDTH), lambda b: (0, 0)),
                  pl.BlockSpec((1, P_WIDTH), lambda b: (0, 0))],
        out_specs=pl.BlockSpec((S, P_WIDTH), lambda b: (b, 0)),
        out_shape=jax.ShapeDtypeStruct((T, P_WIDTH), BF16),
        scratch_shapes=[pltpu.VMEM((S + POOL_HALO, P_WIDTH), F32)],
        compiler_params=_cparams("parallel"),
        name="pool",
    )(proj, w_bd, scale)


def _rope(x, c, s1, s2):
    return x * c + pltpu.roll(x, LANES - A_ROPE // 2, axis=1) * s1 + pltpu.roll(x, A_ROPE // 2, axis=1) * s2


def _mla_prep_kernel(cq_ref, ckv_ref, small_ref, qg_ref, kvg_ref, wq_ref, wqs_ref, wk_ref, wv_ref,
                     vone_ref, cq_t_ref, ck_t_ref, s1_ref, s2_ref, q_ref, k_ref, v_ref):
    def rms(x, g):
        return x * lax.rsqrt(jnp.mean(x * x, axis=-1, keepdims=True) + EPS) * g

    cqn = rms(cq_ref[...].astype(F32), qg_ref[...]).astype(BF16)
    ckvn = rms(ckv_ref[...].astype(F32), kvg_ref[...]).astype(BF16)
    scale = (A_NOPE + A_ROPE) ** -0.5
    qf = _dot(cqn, wq_ref[...]) * scale
    qp = _dot(cqn, wqs_ref[...]) * scale
    kf = _dot(ckvn, wk_ref[...])
    v_ref[...] = (_dot(ckvn, wv_ref[...]) + vone_ref[...]).astype(BF16)
    cqt = cq_t_ref[...]
    s1 = s1_ref[...]
    s2 = s2_ref[...]
    krot = _rope(small_ref[...], ck_t_ref[...], s1, s2)
    sq = s1 + s2
    for h in range(A_HEADS):
        sl = slice(h * LANES, (h + 1) * LANES)
        q_ref[:, sl] = (qf[:, sl] * cqt + qp[:, sl] * sq).astype(BF16)
        k_ref[:, sl] = (kf[:, sl] + krot).astype(BF16)


def _mla_prep(proj, small, qg, kvg, wq, wqs, wk, wv, vone, cq_t, ck_t, s1_t, s2_t, B, S, ts=1024):
    T = B * S
    nst = S // ts
    hw = A_HEADS * LANES
    return pl.pallas_call(
        _mla_prep_kernel,
        grid=(B, nst),
        in_specs=[pl.BlockSpec((ts, A_QRANK), lambda b, s: (b * nst + s, PROJ_CQ // A_QRANK)),
                  pl.BlockSpec((ts, A_KVRANK), lambda b, s: (b * nst + s, PROJ_CKV // A_KVRANK)),
                  pl.BlockSpec((ts, LANES), lambda b, s: (b * nst + s, 0)),
                  pl.BlockSpec((1, A_QRANK), lambda b, s: (0, 0)),
                  pl.BlockSpec((1, A_KVRANK), lambda b, s: (0, 0)),
                  pl.BlockSpec((A_QRANK, hw), lambda b, s: (0, 0)),
                  pl.BlockSpec((A_QRANK, hw), lambda b, s: (0, 0)),
                  pl.BlockSpec((A_KVRANK, hw), lambda b, s: (0, 0)),
                  pl.BlockSpec((A_KVRANK, hw), lambda b, s: (0, 0)),
                  pl.BlockSpec((1, hw), lambda b, s: (0, 0)),
                  pl.BlockSpec((ts, LANES), lambda b, s: (s, 0)),
                  pl.BlockSpec((ts, LANES), lambda b, s: (s, 0)),
                  pl.BlockSpec((ts, LANES), lambda b, s: (s, 0)),
                  pl.BlockSpec((ts, LANES), lambda b, s: (s, 0))],
        out_specs=[pl.BlockSpec((ts, hw), lambda b, s: (b * nst + s, 0)),
                   pl.BlockSpec((ts, hw), lambda b, s: (b * nst + s, 0)),
                   pl.BlockSpec((ts, hw), lambda b, s: (b * nst + s, 0))],
        out_shape=[jax.ShapeDtypeStruct((T, hw), BF16),
                   jax.ShapeDtypeStruct((T, hw), BF16),
                   jax.ShapeDtypeStruct((T, hw), BF16)],
        compiler_params=_cparams("parallel", "parallel"),
        name="mla_prep",
    )(proj, proj, small, qg, kvg, wq, wqs, wk, wv, vone, cq_t, ck_t, s1_t, s2_t)


def _attn_kernel(q_ref, k_ref, v_ref, o_ref, *, tq):
    heads = range(2)
    sls = [slice(hh * LANES, (hh + 1) * LANES) for hh in heads]
    lane = lax.broadcasted_iota(jnp.int32, (tq, LANES), 1)
    below_diag = lax.broadcasted_iota(jnp.int32, (tq, tq), 0) >= lax.broadcasted_iota(jnp.int32, (tq, tq), 1)

    def update(qs, k0, state, causal):
        keys = slice(k0, k0 + tq)
        s = [_dot_nt(qs[hh], k_ref[keys, sls[hh]]) for hh in heads]
        if causal:
            s = [jnp.where(below_diag, s[hh], -jnp.inf) for hh in heads]
        m_new = [jnp.maximum(state[hh][0], jnp.max(s[hh], axis=-1, keepdims=True)) for hh in heads]
        p = [jnp.exp((s[hh] - m_new[hh]).astype(BF16)) for hh in heads]
        pv = [_dot(p[hh], v_ref[keys, sls[hh]]) for hh in heads]
        acc = [jnp.exp(state[hh][0] - m_new[hh]) * state[hh][1] + pv[hh] for hh in heads]
        return tuple((m_new[hh], acc[hh]) for hh in heads)

    for qi in range(q_ref.shape[0] // tq):
        rows = slice(qi * tq, (qi + 1) * tq)
        qs = [q_ref[rows, sl] for sl in sls]
        state = tuple((jnp.full((tq, 1), -jnp.inf, F32), jnp.zeros((tq, LANES), F32)) for _ in heads)
        for kb in range(qi + 1):
            state = update(qs, kb * tq, state, causal=(kb == qi))
        (_, acc0), (_, acc1) = state
        acc = jnp.where(lane < A_VDIM, acc0, acc1)
        den = jnp.where(lane < A_VDIM, pltpu.roll(acc0, A_VDIM, axis=1), pltpu.roll(acc1, A_VDIM, axis=1))
        o_ref[rows, :] = (acc / den).astype(o_ref.dtype)


def _attention(q, k, v, B, S, tq=512):
    T = B * S
    return pl.pallas_call(
        functools.partial(_attn_kernel, tq=tq),
        grid=(B, A_HEADS // 2),
        in_specs=[pl.BlockSpec((S, 2 * LANES), lambda b, p: (b, p)),
                  pl.BlockSpec((S, 2 * LANES), lambda b, p: (b, p)),
                  pl.BlockSpec((S, 2 * LANES), lambda b, p: (b, p))],
        out_specs=pl.BlockSpec((S, LANES), lambda b, p: (b, p)),
        out_shape=jax.ShapeDtypeStruct((T, A_WIDTH), BF16),
        compiler_params=_cparams("parallel", "parallel"),
        name="attention",
    )(q, k, v)


def _outproj_kernel(ym_ref, ya_ref, yp_ref, h_ref, w_ref, g_ref, wr_ref, br_ref,
                    hn_ref, xn_ref, lg_ref):
    mix = _dot(ym_ref[...], w_ref[0:M_WIDTH, :])
    mix = mix + _dot(ya_ref[...], w_ref[M_WIDTH:M_WIDTH + A_WIDTH, :])
    mix = mix + _dot(yp_ref[...], w_ref[M_WIDTH + A_WIDTH:, :])
    hn = h_ref[...] + mix
    hn_ref[...] = hn
    xn = hn * lax.rsqrt(jnp.mean(hn * hn, axis=-1, keepdims=True) + EPS) * g_ref[...]
    x_hi = xn.astype(BF16)
    x_lo = (xn - x_hi.astype(F32)).astype(BF16)
    wr = wr_ref[...]
    w_hi = wr.astype(BF16)
    w_lo = (wr - w_hi.astype(F32)).astype(BF16)
    lg_ref[...] = _dot_nt(w_hi, x_hi) + _dot_nt(w_hi, x_lo) + _dot_nt(w_lo, x_hi) + br_ref[...]
    xn_ref[...] = _pack_bf16_pairs(xn)


def _outproj(ym, ya, yp, h2, w, g, wr_t, br, tm=1024):
    T = h2.shape[0]
    return pl.pallas_call(
        _outproj_kernel,
        grid=(T // tm,),
        in_specs=[pl.BlockSpec((tm, M_WIDTH), lambda i: (i, 0)),
                  pl.BlockSpec((tm, A_WIDTH), lambda i: (i, 0)),
                  pl.BlockSpec((tm, P_WIDTH), lambda i: (i, 0)),
                  pl.BlockSpec((tm, D_MODEL), lambda i: (i, 0)),
                  pl.BlockSpec((D_MODEL, D_MODEL), lambda i: (0, 0)),
                  pl.BlockSpec((1, D_MODEL), lambda i: (0, 0)),
                  pl.BlockSpec((N_EXPERTS, D_MODEL), lambda i: (0, 0)),
                  pl.BlockSpec((N_EXPERTS, 1), lambda i: (0, 0))],
        out_specs=[pl.BlockSpec((tm, D_MODEL), lambda i: (i, 0)),
                   pl.BlockSpec((tm, D_MODEL // 2), lambda i: (i, 0)),
                   pl.BlockSpec((N_EXPERTS, tm), lambda i: (0, i))],
        out_shape=[jax.ShapeDtypeStruct((T, D_MODEL), F32),
                   jax.ShapeDtypeStruct((T, D_MODEL // 2), jnp.uint32),
                   jax.ShapeDtypeStruct((N_EXPERTS, T), F32)],
        compiler_params=_cparams("parallel"),
        name="outproj",
    )(ym, ya, yp, h2, w, g, wr_t, br)


def _router_kernel(lg_ref, tri_ref, e_ref, w_ref, r_ref, cnt_ref, carry_ref):
    tr = lg_ref.shape[1]

    @pl.when(pl.program_id(0) == 0)
    def _():
        carry_ref[...] = jnp.zeros_like(carry_ref)

    x = lg_ref[...]
    eio = lax.broadcasted_iota(jnp.int32, (N_EXPERTS, tr), 0).astype(F32)
    picked = jnp.zeros((N_EXPERTS, tr), F32)
    vals = []
    idxs = []
    for _ in range(TOP_K):
        mx = jnp.max(x, axis=0, keepdims=True)
        idx = jnp.min(jnp.where(x == mx, eio, float(N_EXPERTS)), axis=0, keepdims=True)
        hit = eio == idx
        vals.append(mx)
        idxs.append(idx)
        picked = picked + hit.astype(F32)
        x = jnp.where(hit, -jnp.inf, x)
    exps = [jnp.exp(vv - vals[0]) for vv in vals]
    tot = exps[0] + exps[1] + exps[2] + exps[3]
    before = _dot(picked.astype(BF16), tri_ref[...]) + carry_ref[:, 0:1]
    for kk in range(TOP_K):
        e_ref[kk:kk + 1, :] = idxs[kk].astype(jnp.int32)
        w_ref[kk:kk + 1, :] = exps[kk] / tot
        rk = jnp.sum(jnp.where(eio == idxs[kk], before, 0.0), axis=0, keepdims=True)
        r_ref[kk:kk + 1, :] = rk.astype(jnp.int32)
    carry_ref[...] = carry_ref[...] + jnp.sum(picked, axis=1, keepdims=True)
    cnt_ref[...] = carry_ref[...]


def _router(logits_t, tri, tr=512):
    T = logits_t.shape[1]
    return pl.pallas_call(
        _router_kernel,
        grid=(T // tr,),
        in_specs=[pl.BlockSpec((N_EXPERTS, tr), lambda i: (0, i)),
                  pl.BlockSpec((tr, tr), lambda i: (0, 0))],
        out_specs=[pl.BlockSpec((TOP_K, tr), lambda i: (0, i)),
                   pl.BlockSpec((TOP_K, tr), lambda i: (0, i)),
                   pl.BlockSpec((TOP_K, tr), lambda i: (0, i)),
                   pl.BlockSpec((N_EXPERTS, LANES), lambda i: (0, 0))],
        out_shape=[jax.ShapeDtypeStruct((TOP_K, T), jnp.int32),
                   jax.ShapeDtypeStruct((TOP_K, T), F32),
                   jax.ShapeDtypeStruct((TOP_K, T), jnp.int32),
                   jax.ShapeDtypeStruct((N_EXPERTS, LANES), F32)],
        scratch_shapes=[pltpu.VMEM((N_EXPERTS, LANES), F32)],
        compiler_params=_cparams("arbitrary"),
        name="router",
    )(logits_t, tri)


def _meta_kernel(cnt_ref, e_ref, r_ref, dest_ref, be_ref, bc_ref, bn_ref, nb_ref, *, nb_pad):
    cnt = cnt_ref[...]
    padded = jnp.floor((cnt + (MOE_ROWS - 1)) * (1.0 / MOE_ROWS)) * MOE_ROWS
    ri = lax.broadcasted_iota(jnp.int32, (N_EXPERTS, N_EXPERTS), 0)
    ci = lax.broadcasted_iota(jnp.int32, (N_EXPERTS, N_EXPERTS), 1)
    pad_end = _dot((ri >= ci).astype(F32), padded, precision=HIGHEST)
    pad_start = pad_end - padded
    e = e_ref[...]
    dest = r_ref[...]
    for ex in range(N_EXPERTS):
        ps = pad_start[ex:ex + 1, 0:1].astype(jnp.int32)
        dest = jnp.where(e == ex, dest + ps, dest)
    dest_ref[...] = dest
    blk0 = (lax.broadcasted_iota(jnp.int32, (N_EXPERTS, nb_pad), 1) * MOE_ROWS).astype(F32)
    be = jnp.sum((pad_end[:, 0:1] <= blk0).astype(F32), axis=0, keepdims=True)
    be = jnp.minimum(be, float(N_EXPERTS - 1))
    eio = lax.broadcasted_iota(jnp.int32, (N_EXPERTS, nb_pad), 0).astype(F32)
    seg_end = jnp.sum(jnp.where(eio == be, pad_start[:, 0:1] + cnt[:, 0:1], 0.0), axis=0, keepdims=True)
    bc = jnp.clip(seg_end - blk0[0:1, :], 0.0, float(MOE_ROWS))
    nxt0 = jnp.sum(jnp.where(eio == be, pad_end[:, 0:1], 0.0), axis=0, keepdims=True)
    bn = jnp.sum((pad_end[:, 0:1] <= nxt0).astype(F32), axis=0, keepdims=True)
    bn = jnp.where(nxt0 < pad_end[N_EXPERTS - 1:N_EXPERTS, 0:1], bn, -1.0)
    be_ref[...] = be.astype(jnp.int32)
    bc_ref[...] = bc.astype(jnp.int32)
    bn_ref[...] = bn.astype(jnp.int32)
    nb_ref[...] = (pad_end[N_EXPERTS - 1:N_EXPERTS, :] * (1.0 / MOE_ROWS)).astype(jnp.int32)


def _meta(counts, eidx, rank, nb_pad):
    T = eidx.shape[1]
    return pl.pallas_call(
        functools.partial(_meta_kernel, nb_pad=nb_pad),
        out_shape=[jax.ShapeDtypeStruct((TOP_K, T), jnp.int32),
                   jax.ShapeDtypeStruct((1, nb_pad), jnp.int32),
                   jax.ShapeDtypeStruct((1, nb_pad), jnp.int32),
                   jax.ShapeDtypeStruct((1, nb_pad), jnp.int32),
                   jax.ShapeDtypeStruct((1, LANES), jnp.int32)],
        compiler_params=pltpu.CompilerParams(vmem_limit_bytes=VMEM_LIMIT),
        name="route_meta",
    )(counts, eidx, rank)


FF_CHUNK = 512


def _gmm_kernel(be_ref, bc_ref, bn_ref, nb_ref, x_ref, wgu_hbm, bgu_ref, wdn_hbm, bdn_ref, y_ref,
                wgu_st, wdn_st, wgu16, wdn16, sem, *, e0):
    i = pl.program_id(0)
    nblk = nb_ref[0]
    bm = MOE_ROWS

    def weight_copies(e):
        return (pltpu.make_async_copy(wgu_hbm.at[e0 + e], wgu_st, sem.at[0]),
                pltpu.make_async_copy(wdn_hbm.at[e0 + e], wdn_st, sem.at[1]))

    @pl.when(i == 0)
    def _():
        for cp in weight_copies(be_ref[0]):
            cp.start()

    @pl.when(i >= nblk)
    def _():
        y_ref[...] = jnp.zeros_like(y_ref)

    @pl.when(i < nblk)
    def _():
        e_changed = jnp.logical_or(i == 0, be_ref[i] != be_ref[jnp.maximum(i - 1, 0)])

        quarter = bm // MOE_SPLIT
        full = bc_ref[i] > (MOE_SPLIT - 1) * quarter
        fused = jnp.logical_and(e_changed, full)

        @pl.when(e_changed)
        def _():
            for cp in weight_copies(be_ref[i]):
                cp.wait()

        @pl.when(jnp.logical_and(e_changed, jnp.logical_not(full)))
        def _():
            wgu16[...] = wgu_st[...].astype(BF16)
            wdn16[...] = wdn_st[...].astype(BF16)

        def expert_rows(nrows, convert=False):
            valid = lax.broadcasted_iota(jnp.int32, (nrows, 1), 0) < bc_ref[i]
            lo, hi = _unpack_bf16_pairs(jnp.where(valid, x_ref[0:nrows, :], jnp.uint32(0)))
            x16 = jnp.concatenate([lo.astype(BF16), hi.astype(BF16)], axis=1)
            acc = jnp.zeros((nrows, D_MODEL), F32) + bdn_ref[0]
            for c in range(D_FF // FF_CHUNK):
                cs = slice(c * FF_CHUNK, (c + 1) * FF_CHUNK)
                us = slice(D_FF + c * FF_CHUNK, D_FF + (c + 1) * FF_CHUNK)
                if convert:
                    wgu16[:, cs] = wgu_st[:, cs].astype(BF16)
                    wgu16[:, us] = wgu_st[:, us].astype(BF16)
                    wdn16[cs, :] = wdn_st[cs, :].astype(BF16)
                gate = _dot(x16, wgu16[:, cs]) + bgu_ref[0, :, cs]
                up = _dot(x16, wgu16[:, us]) + bgu_ref[0, :, us]
                gate = jnp.minimum(gate, SWIGLU_LIMIT)
                up = jnp.clip(up, -SWIGLU_LIMIT, SWIGLU_LIMIT)
                act = (up + 1.0) * gate * _sigmoid(SWIGLU_ALPHA * gate)
                acc = acc + _dot(act.astype(BF16), wdn16[cs, :])
            y_ref[0:nrows, :] = _pack_bf16_pairs(acc)

        @pl.when(fused)
        def _():
            expert_rows(bm, convert=True)

        for nq in range(1, MOE_SPLIT + 1):
            lower = (nq - 1) * quarter if nq > 1 else -1
            in_range = jnp.logical_and(bc_ref[i] > lower, bc_ref[i] <= nq * quarter)
            @pl.when(jnp.logical_and(in_range, jnp.logical_not(fused)))
            def _(nq=nq):
                expert_rows(nq * quarter)
                if nq < MOE_SPLIT:
                    y_ref[nq * quarter:, :] = jnp.zeros((bm - nq * quarter, D_MODEL // 2), jnp.uint32)

        @pl.when(jnp.logical_and(e_changed, bn_ref[i] >= 0))
        def _():
            for cp in weight_copies(bn_ref[i]):
                cp.start()


def _gmm(blk_e, blk_cnt, blk_next, nblk, x_rows, wgu, bgu, wdn, bdn, nb, layer):
    bm = MOE_ROWS
    e0 = layer * N_EXPERTS
    dp = D_MODEL // 2

    def expert(i, be, nb_ref):
        return (e0 + be[jnp.minimum(i, jnp.maximum(nb_ref[0] - 1, 0))], 0, 0)

    def rows(i, nb_ref):
        return (jnp.minimum(i, jnp.maximum(nb_ref[0] - 1, 0)), 0)

    grid_spec = pltpu.PrefetchScalarGridSpec(
        num_scalar_prefetch=4,
        grid=(nb,),
        in_specs=[pl.BlockSpec((bm, dp), lambda i, be, bc, bn, nbr: rows(i, nbr)),
                  pl.BlockSpec(memory_space=pl.ANY),
                  pl.BlockSpec((1, 1, 2 * D_FF), lambda i, be, bc, bn, nbr: expert(i, be, nbr)),
                  pl.BlockSpec(memory_space=pl.ANY),
                  pl.BlockSpec((1, 1, D_MODEL), lambda i, be, bc, bn, nbr: expert(i, be, nbr))],
        out_specs=pl.BlockSpec((bm, dp), lambda i, be, bc, bn, nbr: (i, 0)),
        scratch_shapes=[pltpu.VMEM((D_MODEL, 2 * D_FF), F32),
                        pltpu.VMEM((D_FF, D_MODEL), F32),
                        pltpu.VMEM((D_MODEL, 2 * D_FF), BF16),
                        pltpu.VMEM((D_FF, D_MODEL), BF16),
                        pltpu.SemaphoreType.DMA((2,))],
    )
    return pl.pallas_call(
        functools.partial(_gmm_kernel, e0=e0),
        grid_spec=grid_spec,
        out_shape=jax.ShapeDtypeStruct((nb * bm, dp), jnp.uint32),
        compiler_params=_cparams("arbitrary"),
        name="expert_gmm",
    )(blk_e, blk_cnt, blk_next, nblk, x_rows, wgu, bgu, wdn, bdn)


def _final_kernel(h_ref, y0, y1, y2, y3, wk_ref, g_ref, o_ref):
    for rows in _row_slices(h_ref.shape[0]):
        x = _moe_combine(h_ref, (y0, y1, y2, y3), wk_ref, rows)
        o_ref[rows, :] = x * lax.rsqrt(jnp.mean(x * x, axis=-1, keepdims=True) + EPS) * g_ref[...]


def _final(h2, y_tok, w_tok, g, tm=512):
    T = h2.shape[0]
    nt = T // tm
    return pl.pallas_call(
        _final_kernel,
        grid=(nt,),
        in_specs=_combine_specs(tm, nt) + [pl.BlockSpec((1, D_MODEL), lambda i: (0, 0))],
        out_specs=pl.BlockSpec((tm, D_MODEL), lambda i: (i, 0)),
        out_shape=jax.ShapeDtypeStruct((T, D_MODEL), F32),
        compiler_params=_cparams("parallel"),
        name="combine_final_norm",
    )(h2, y_tok, y_tok, y_tok, y_tok, w_tok, g)


SC_CORES = 2
SC_SUBCORES = 16
SC_LANES = 16
SC_WORKERS = SC_CORES * SC_SUBCORES
SC_WINDOW = 64


def _sc_mesh():
    return plsc.VectorSubcoreMesh(core_axis_name="c", subcore_axis_name="s")


def _sc_worker():
    return lax.axis_index("s") * SC_CORES + lax.axis_index("c")


def _sc_dispatch(xn, dest_flat, n_rows):
    T = xn.shape[0]
    tpw = T // SC_WORKERS
    nchunk = tpw // SC_WINDOW
    nvec = SC_WINDOW // SC_LANES

    @functools.partial(
        pl.kernel, out_type=jax.ShapeDtypeStruct((n_rows, xn.shape[1]), xn.dtype), mesh=_sc_mesh(),
        scratch_types=[pltpu.VMEM((TOP_K * tpw,), jnp.int32),
                       pltpu.VMEM((SC_WINDOW, xn.shape[1]), xn.dtype),
                       pltpu.VMEM((SC_WINDOW, xn.shape[1]), xn.dtype),
                       pltpu.SemaphoreType.DMA, pltpu.SemaphoreType.DMA, pltpu.SemaphoreType.DMA],
        name="sc_dispatch")
    def run(x_hbm, d_hbm, o_hbm, idx_v, buf0, buf1, sem0, sem1, sem_out):
        base = _sc_worker() * tpw
        for kk in range(TOP_K):
            pltpu.sync_copy(d_hbm.at[pl.ds(kk * T + base, tpw)], idx_v.at[pl.ds(kk * tpw, tpw)])
        bufs = (buf0, buf1)
        sems = (sem0, sem1)

        def load(c, slot):
            return pltpu.make_async_copy(x_hbm.at[pl.ds(base + c * SC_WINDOW, SC_WINDOW)], bufs[slot], sems[slot])

        load(0, 0).start()

        @pl.loop(0, nchunk, step=2)
        def _(c0):
            for slot in range(2):
                c = c0 + slot
                load(c, slot).wait()

                @pl.when(c + 1 < nchunk)
                def _():
                    load(c + 1, 1 - slot).start()

                copies = []
                for kk in range(TOP_K):
                    for q in range(nvec):
                        off = pl.multiple_of(kk * tpw + c * SC_WINDOW + q * SC_LANES, SC_LANES)
                        rows = idx_v[pl.ds(off, SC_LANES)]
                        cp = pltpu.make_async_copy(bufs[slot].at[pl.ds(q * SC_LANES, SC_LANES)],
                                                   o_hbm.at[rows], sem_out)
                        cp.start()
                        copies.append(cp)
                for cp in copies:
                    cp.wait()

    return run(xn, dest_flat)


def _sc_gather(y_rows, dest_flat):
    n = dest_flat.shape[0]
    rpw = n // SC_WORKERS
    nchunk = rpw // SC_WINDOW
    nvec = SC_WINDOW // SC_LANES

    @functools.partial(
        pl.kernel, out_type=jax.ShapeDtypeStruct((n, y_rows.shape[1]), y_rows.dtype), mesh=_sc_mesh(),
        scratch_types=[pltpu.VMEM((rpw,), jnp.int32),
                       pltpu.VMEM((SC_WINDOW, y_rows.shape[1]), y_rows.dtype),
                       pltpu.VMEM((SC_WINDOW, y_rows.shape[1]), y_rows.dtype),
                       pltpu.SemaphoreType.DMA, pltpu.SemaphoreType.DMA, pltpu.SemaphoreType.DMA],
        name="sc_gather")
    def run(y_hbm, d_hbm, o_hbm, idx_v, buf0, buf1, sem0, sem1, sem_in):
        base = _sc_worker() * rpw
        pltpu.sync_copy(d_hbm.at[pl.ds(base, rpw)], idx_v)
        bufs = (buf0, buf1)
        sems = (sem0, sem1)

        def store(c, slot):
            return pltpu.make_async_copy(bufs[slot], o_hbm.at[pl.ds(base + c * SC_WINDOW, SC_WINDOW)], sems[slot])

        @pl.loop(0, nchunk, step=2)
        def _(c0):
            for slot in range(2):
                c = c0 + slot

                @pl.when(c >= 2)
                def _():
                    store(c - 2, slot).wait()

                copies = []
                for q in range(nvec):
                    off = pl.multiple_of(c * SC_WINDOW + q * SC_LANES, SC_LANES)
                    rows = idx_v[pl.ds(off, SC_LANES)]
                    cp = pltpu.make_async_copy(y_hbm.at[rows], bufs[slot].at[pl.ds(q * SC_LANES, SC_LANES)], sem_in)
                    cp.start()
                    copies.append(cp)
                for cp in copies:
                    cp.wait()
                store(c, slot).start()

        store(nchunk - 2, 0).wait()
        store(nchunk - 1, 1).wait()

    return run(y_rows, dest_flat)


def _prep_w_in(w_in):
    o_g = 4 * M_WIDTH
    o_cq = o_g + 2 * M_HEADS
    o_ckv = o_cq + A_QRANK
    o_kr = o_ckv + A_KVRANK
    o_up = o_kr + A_ROPE
    z = lambda n: jnp.zeros(w_in.shape[:-1] + (n,), w_in.dtype)
    small = jnp.concatenate([z(SMALL_KR), w_in[..., o_kr:o_up], w_in[..., o_g:o_cq],
                             z(LANES - SMALL_GATE - 2 * M_HEADS)], axis=-1)
    return jnp.concatenate([w_in[..., 0:o_g], w_in[..., o_cq:o_ckv], w_in[..., o_up:o_up + P_WIDTH],
                            w_in[..., o_ckv:o_kr], small], axis=-1).astype(BF16)


def _rope_tables(seq):
    inv = ROPE_THETA ** (-jnp.arange(0, A_ROPE, 2, dtype=F32) / A_ROPE)
    ang = jnp.arange(seq, dtype=F32)[:, None] * inv[None, :]
    cos, sin = jnp.cos(ang), jnp.sin(ang)
    half = A_ROPE // 2
    zeros = lambda n: jnp.zeros((seq, n), F32)
    ones = lambda n: jnp.ones((seq, n), F32)
    tail = LANES - A_NOPE - A_ROPE
    cq_t = jnp.concatenate([ones(A_NOPE), cos, cos, zeros(tail)], axis=1)
    ck_t = jnp.concatenate([zeros(A_NOPE), cos, cos, zeros(tail)], axis=1)
    s1_t = jnp.concatenate([zeros(A_NOPE), -sin, zeros(half), zeros(tail)], axis=1)
    s2_t = jnp.concatenate([zeros(A_NOPE), zeros(half), sin, zeros(tail)], axis=1)
    return cq_t, ck_t, s1_t, s2_t


def kernel(x, norm1_g, w_in, conv_w, conv_b, gate_b, mlstm_norm_g, q_norm_g, kv_norm_g, w_uq, w_ukv,
           w_pool, pool_scale, w_out, norm2_g, w_router, b_router, w_gate_up, b_gate_up, w_down, b_down,
           final_norm_g):
    B, S, D = x.shape
    depth = w_in.shape[0]
    T = B * S
    nb = (T * TOP_K) // MOE_ROWS + N_EXPERTS
    nb_pad = -(-nb // LANES) * LANES

    w_in_p = _prep_w_in(w_in)
    wq = w_uq.reshape(depth, A_QRANK, A_HEADS, A_NOPE + A_ROPE)
    wq = jnp.pad(wq, ((0, 0), (0, 0), (0, 0), (0, LANES - A_NOPE - A_ROPE)))
    r0, r1, r2 = A_NOPE, A_NOPE + A_ROPE // 2, A_NOPE + A_ROPE
    wqs = jnp.concatenate([jnp.zeros_like(wq[..., :r0]), wq[..., r1:r2], wq[..., r0:r1],
                           jnp.zeros_like(wq[..., r2:])], axis=-1)
    wq = wq.reshape(depth, A_QRANK, A_HEADS * LANES).astype(BF16)
    wqs = wqs.reshape(depth, A_QRANK, A_HEADS * LANES).astype(BF16)
    wkv = w_ukv.reshape(depth, A_KVRANK, A_HEADS, A_NOPE + A_VDIM)
    wk = jnp.pad(wkv[..., :A_NOPE], ((0, 0), (0, 0), (0, 0), (0, LANES - A_NOPE)))
    wk = wk.reshape(depth, A_KVRANK, A_HEADS * LANES).astype(BF16)
    wv_e = jnp.pad(wkv[:, :, 0::2, A_NOPE:], ((0, 0), (0, 0), (0, 0), (0, LANES - A_VDIM)))
    wv_o = jnp.pad(wkv[:, :, 1::2, A_NOPE:], ((0, 0), (0, 0), (0, 0), (LANES - A_VDIM, 0)))
    wv = jnp.stack([wv_e, wv_o], axis=3).reshape(depth, A_KVRANK, A_HEADS * LANES).astype(BF16)
    half = jnp.arange(A_HEADS * LANES) // A_VDIM
    vone = ((half % 4 == 1) | (half % 4 == 2)).astype(F32)[None, :]
    gsz = P_WIDTH // len(P_WINDOWS)
    w_pool_bd = jnp.zeros((depth, P_WIDTH, P_WIDTH), F32)
    for gi in range(len(P_WINDOWS)):
        w_pool_bd = w_pool_bd.at[:, gi * gsz:(gi + 1) * gsz, gi * gsz:(gi + 1) * gsz].set(w_pool[:, gi])
    w_pool_bd = w_pool_bd.astype(BF16)
    w_out16 = w_out.astype(BF16)
    w_router_t = jnp.swapaxes(w_router, 1, 2)
    gate_b_col = jnp.pad(gate_b, ((0, 0), (SMALL_GATE, LANES - SMALL_GATE - 2 * M_HEADS)))
    cq_t, ck_t, s1_t, s2_t = _rope_tables(S)
    tr = 1024
    tri = (jnp.arange(tr)[:, None] < jnp.arange(tr)[None, :]).astype(BF16)

    wgu_all = w_gate_up.reshape(depth * N_EXPERTS, D_MODEL, 2 * D_FF)
    bgu_all = b_gate_up.reshape(depth * N_EXPERTS, 1, 2 * D_FF)
    wdn_all = w_down.reshape(depth * N_EXPERTS, D_FF, D_MODEL)
    bdn_all = b_down.reshape(depth * N_EXPERTS, 1, D_MODEL)

    h = x.reshape(T, D)
    moe = None
    for l in range(depth):
        h, proj, small = _inproj(h, norm1_g[l][None, :], w_in_p[l], moe)
        y_m = _mlstm(proj, small, conv_w[l], conv_b[l][None, :], gate_b_col[l][None, :],
                     mlstm_norm_g[l][None, :], B, S)
        q16, k16, v16 = _mla_prep(proj, small, q_norm_g[l][None, :], kv_norm_g[l][None, :], wq[l], wqs[l],
                                  wk[l], wv[l], vone, cq_t, ck_t, s1_t, s2_t, B, S)
        y_a = _attention(q16, k16, v16, B, S)
        y_p = _pool(proj, w_pool_bd[l], pool_scale[l][None, :], B, S)
        h, xn, logits_t = _outproj(y_m, y_a, y_p, h, w_out16[l], norm2_g[l][None, :],
                                   w_router_t[l], b_router[l][:, None])
        eidx, wts, rank, counts = _router(logits_t, tri, tr)
        dest, blk_e, blk_cnt, blk_next, nblk = _meta(counts, eidx, rank, nb_pad)
        dest_flat = dest.reshape(TOP_K * T)
        x_rows = _sc_dispatch(xn, dest_flat, nb * MOE_ROWS)
        y_rows = _gmm(blk_e[0], blk_cnt[0], blk_next[0], nblk[0], x_rows, wgu_all, bgu_all, wdn_all,
                      bdn_all, nb, l)
        moe = (_sc_gather(y_rows, dest_flat), wts.T)
    return _final(h, moe[0], moe[1], final_norm_g[None, :]).reshape(B, S, D)
```

```python
import functools

import jax
import jax.numpy as jnp
import numpy as np
from jax import lax
from jax.experimental import pallas as pl
from jax.experimental.pallas import tpu as pltpu
from jax.experimental.pallas import tpu_sc as plsc

F32 = jnp.float32
BF16 = jnp.bfloat16
HIGHEST = lax.Precision.HIGHEST

D_MODEL = 1024
M_HEADS = 4
M_HEAD_DIM = 64
M_WIDTH = 256
M_CONV = 4
M_CHUNK = 64
A_HEADS = 8
A_NOPE = 64
A_ROPE = 32
A_VDIM = 64
A_QRANK = 256
A_KVRANK = 128
A_WIDTH = 512
ROPE_THETA = 10000.0
P_WINDOWS = (2, 4, 8, 16)
P_WIDTH = 256
N_EXPERTS = 32
TOP_K = 4
D_FF = 1024
SWIGLU_LIMIT = 7.0
SWIGLU_ALPHA = 1.702
EPS = 1e-6

LANES = 128
SUBLANES = 8
BF16_ROWS = 16

PROJ_QKVO = 0
PROJ_CQ = 1024
PROJ_UP = 1280
PROJ_CKV = 1536
PROJ_SMALL = 1664
PROJ_WIDTH = 1792
SMALL_KR = 64
SMALL_GATE = 96

MOE_ROWS = 1024
MOE_SPLIT = 4
VMEM_LIMIT = 56 * 1024 * 1024


def _cparams(*sem):
    return pltpu.CompilerParams(dimension_semantics=sem, vmem_limit_bytes=VMEM_LIMIT)


def _sigmoid(x):
    return 1.0 / (1.0 + jnp.exp(-x))


def _log_sigmoid(x):
    return jnp.minimum(x, 0.0) - jnp.log(1.0 + jnp.exp(-jnp.abs(x)))


def _dot(a, b, **kw):
    return jnp.dot(a, b, preferred_element_type=F32, **kw)


def _dot_nt(a, b, **kw):
    return lax.dot_general(a, b, (((1,), (1,)), ((), ())), preferred_element_type=F32, **kw)


def _dot_tn(a, b, **kw):
    return lax.dot_general(a, b, (((0,), (0,)), ((), ())), preferred_element_type=F32, **kw)


def _bf16_terms(x, terms=3):
    out = []
    for _ in range(terms):
        piece = x.astype(BF16)
        out.append(piece)
        x = x - piece.astype(F32)
    return out


def _dot_sel(x, sel16, terms=3):
    return sum(_dot(p, sel16) for p in _bf16_terms(x, terms))


def _sel_dot(sel16, x, terms=3):
    return sum(_dot(sel16, p) for p in _bf16_terms(x, terms))


def _pack_bf16_pairs(x):
    n = x.shape[1] // 2
    lo = lax.bitcast_convert_type(x[:, :n].astype(BF16).astype(F32), jnp.uint32)
    hi = lax.bitcast_convert_type(x[:, n:].astype(BF16).astype(F32), jnp.uint32)
    return (lo >> 16) | (hi & jnp.uint32(0xFFFF0000))


def _unpack_bf16_pairs(p):
    lo = lax.bitcast_convert_type(p << 16, F32)
    hi = lax.bitcast_convert_type(p & jnp.uint32(0xFFFF0000), F32)
    return lo, hi


ROW_SUBTILE = 128


def _row_slices(tm):
    return [slice(r, r + ROW_SUBTILE) for r in range(0, tm, ROW_SUBTILE)]


def _moe_combine(h_ref, y_refs, w_ref, rows):
    w = w_ref[rows, :]
    dp = D_MODEL // 2
    acc_lo = h_ref[rows, :dp]
    acc_hi = h_ref[rows, dp:]
    for kk, y_ref in enumerate(y_refs):
        lo, hi = _unpack_bf16_pairs(y_ref[rows, :])
        acc_lo = acc_lo + lo * w[:, kk:kk + 1]
        acc_hi = acc_hi + hi * w[:, kk:kk + 1]
    return jnp.concatenate([acc_lo, acc_hi], axis=1)


def _combine_specs(tm, nt):
    y_specs = [pl.BlockSpec((tm, D_MODEL // 2), functools.partial(lambda i, kk: (kk * nt + i, 0), kk=kk))
               for kk in range(TOP_K)]
    return [pl.BlockSpec((tm, D_MODEL), lambda i: (i, 0))] + y_specs + [pl.BlockSpec((tm, TOP_K), lambda i: (i, 0))]


def _inproj_kernel(*refs, combine):
    if combine:
        h_ref, y0, y1, y2, y3, wk_ref, g_ref, w_ref, hn_ref, o_ref, sm_ref = refs
    else:
        h_ref, g_ref, w_ref, o_ref, sm_ref = refs
    for rows in _row_slices(h_ref.shape[0]):
        if combine:
            x = _moe_combine(h_ref, (y0, y1, y2, y3), wk_ref, rows)
            hn_ref[rows, :] = x
        else:
            x = h_ref[rows, :]
        ms = jnp.mean(x * x, axis=-1, keepdims=True)
        xn = x * lax.rsqrt(ms + EPS) * g_ref[...]
        res = _dot(xn.astype(BF16), w_ref[...])
        o_ref[rows, :] = res[:, :PROJ_SMALL].astype(BF16)
        sm_ref[rows, :] = res[:, PROJ_SMALL:]


def _inproj(h2, g, w, moe=None, tm=512):
    T = h2.shape[0]
    nt = T // tm
    w_specs = [pl.BlockSpec((1, D_MODEL), lambda i: (0, 0)),
               pl.BlockSpec((D_MODEL, PROJ_WIDTH), lambda i: (0, 0))]
    proj_specs = [pl.BlockSpec((tm, PROJ_SMALL), lambda i: (i, 0)),
                  pl.BlockSpec((tm, PROJ_WIDTH - PROJ_SMALL), lambda i: (i, 0))]
    proj_shapes = [jax.ShapeDtypeStruct((T, PROJ_SMALL), BF16),
                   jax.ShapeDtypeStruct((T, PROJ_WIDTH - PROJ_SMALL), F32)]
    if moe is None:
        proj, small = pl.pallas_call(
            functools.partial(_inproj_kernel, combine=False),
            grid=(nt,),
            in_specs=[pl.BlockSpec((tm, D_MODEL), lambda i: (i, 0))] + w_specs,
            out_specs=proj_specs,
            out_shape=proj_shapes,
            compiler_params=_cparams("parallel"),
            name="inproj",
        )(h2, g, w)
        return h2, proj, small
    y_tok, w_tok = moe
    return pl.pallas_call(
        functools.partial(_inproj_kernel, combine=True),
        grid=(nt,),
        in_specs=_combine_specs(tm, nt) + w_specs,
        out_specs=[pl.BlockSpec((tm, D_MODEL), lambda i: (i, 0))] + proj_specs,
        out_shape=[jax.ShapeDtypeStruct((T, D_MODEL), F32)] + proj_shapes,
        compiler_params=_cparams("parallel"),
        name="combine_inproj",
    )(h2, y_tok, y_tok, y_tok, y_tok, w_tok, g, w)


M_SEQS = 2
M_UNROLL = 2


def _mlstm_kernel(proj_ref, small_ref, cw_ref, cb_ref, gbc_ref, ng_ref, o_ref, ct_ref, n_ref, m_ref):
    S = proj_ref.shape[0] // M_SEQS
    L = M_CHUNK
    nc = S // L
    W = M_WIDTH
    ct_ref[...] = jnp.zeros_like(ct_ref)
    n_ref[...] = jnp.zeros_like(n_ref)
    m_ref[...] = jnp.zeros_like(m_ref)

    rh = lax.broadcasted_iota(jnp.int32, (W, W), 0) // M_HEAD_DIM
    chd = lax.broadcasted_iota(jnp.int32, (W, W), 1) // M_HEAD_DIM
    same_head = (rh == chd).astype(F32)
    same_head16 = same_head.astype(BF16)
    tril16 = (lax.broadcasted_iota(jnp.int32, (L, L), 0) >= lax.broadcasted_iota(jnp.int32, (L, L), 1)).astype(BF16)
    row = lax.broadcasted_iota(jnp.int32, (L, W), 0)
    key = lax.broadcasted_iota(jnp.int32, (L, W), 1) % M_HEAD_DIM
    causal = key <= row
    diag = (key == row).astype(F32)
    er = lax.broadcasted_iota(jnp.int32, (LANES, 2 * W), 0)
    ec = lax.broadcasted_iota(jnp.int32, (LANES, 2 * W), 1)
    spread16 = (er == SMALL_GATE + ec // M_HEAD_DIM).astype(BF16)
    forget_lane = lax.broadcasted_iota(jnp.int32, (1, LANES), 1) >= SMALL_GATE + M_HEADS
    cw = cw_ref[...]
    cb = cb_ref[...]
    gbc = gbc_ref[...]
    ng = ng_ref[...]

    def prefix_max(x):
        s = 1
        while s < L:
            x = jnp.maximum(x, jnp.where(row >= s, pltpu.roll(x, s, axis=0), -jnp.inf))
            s *= 2
        return x

    def chunk(sq, c):
        r0 = pl.multiple_of(sq * S + c * L, L)
        h0 = pl.multiple_of(sq * S + jnp.maximum(c * L - BF16_ROWS, 0), BF16_ROWS)
        halo = proj_ref[pl.ds(h0, BF16_ROWS), 0:2 * W].astype(F32)[BF16_ROWS - SUBLANES:, :]
        halo = halo * jnp.where(c > 0, 1.0, 0.0)
        win = jnp.concatenate([halo, proj_ref[pl.ds(r0, L), 0:2 * W].astype(F32)], axis=0)
        acc = jnp.zeros((L, 2 * W), F32) + cb
        for j in range(M_CONV):
            s = M_CONV - 1 - j
            xs = win if s == 0 else pltpu.roll(win, s, axis=0)
            acc = acc + xs[SUBLANES:, :] * cw[j:j + 1, :]
        qk = acc * _sigmoid(acc)
        q = qk[:, 0:W]
        k = qk[:, W:2 * W] * (M_HEAD_DIM ** -0.5)
        v16 = proj_ref[pl.ds(r0, L), 2 * W:3 * W]
        v = v16.astype(F32)
        og = proj_ref[pl.ds(r0, L), 3 * W:4 * W].astype(F32)
        q16 = q.astype(BF16)
        k16 = k.astype(BF16)
        ct = ct_ref[sq]
        nvec = n_ref[sq]
        m_old = m_ref[sq]
        pre = small_ref[pl.ds(r0, L), :] + gbc
        gates = _dot_sel(jnp.where(forget_lane, _log_sigmoid(pre), pre), spread16)
        kb = jnp.concatenate([k16] * M_HEADS, axis=0) * same_head16
        vb = jnp.concatenate([v16] * M_HEADS, axis=0) * same_head16
        qk_all = _dot_nt(q16, kb)
        q_c = _dot(q16, ct.astype(BF16))
        q_n = _dot_sel(q * nvec, same_head16, terms=2)
        yield

        i_pre = gates[:, :W]
        g = _sel_dot(tril16, gates[:, W:])
        yield
        a = i_pre - g
        a_key = jnp.sum(a * diag, axis=0, keepdims=True)
        a_max = prefix_max(a)
        m_row = g + jnp.maximum(m_old, a_max)
        p = qk_all * jnp.exp(jnp.where(causal, g + a_key, -jnp.inf) - m_row)
        p16 = p.astype(BF16)
        num_intra = _dot(p16, vb)
        rowsum = _dot_sel(p, same_head16, terms=2)
        yield

        w_inter = jnp.exp(g + m_old - m_row)
        num = w_inter * q_c + num_intra
        den = w_inter * q_n + rowsum
        hv = num / jnp.maximum(jnp.abs(den), jnp.exp(-m_row))
        ms = _dot_sel(hv * hv, same_head16, terms=2) * (1.0 / M_HEAD_DIM)
        yield
        y = hv * lax.rsqrt(ms + EPS) * ng * _sigmoid(og)
        o_ref[pl.ds(r0, L), :] = y.astype(o_ref.dtype)

        g_end = g[L - 1:L, :]
        m_new = g_end + jnp.maximum(m_old, a_max[L - 1:L, :])
        wa = jnp.exp(g_end + a - m_new)
        decay = jnp.exp(g_end + m_old - m_new)
        upd = _dot_tn(k16, (wa * v).astype(BF16))
        ct_ref[sq] = decay * ct + upd * same_head
        n_ref[sq] = decay * nvec + jnp.sum(wa * k, axis=0, keepdims=True)
        m_ref[sq] = m_new
        yield

    def body(cc, carry):
        for u in range(M_UNROLL):
            for _ in zip(*[chunk(sq, cc * M_UNROLL + u) for sq in range(M_SEQS)]):
                pass
        return carry

    lax.fori_loop(0, nc // M_UNROLL, body, 0)


def _mlstm(proj, small, cw, cb, gbc, ng, B, S):
    assert M_CHUNK == M_HEAD_DIM
    T = B * S
    rows = M_SEQS * S
    return pl.pallas_call(
        _mlstm_kernel,
        grid=(B // M_SEQS,),
        in_specs=[pl.BlockSpec((rows, 4 * M_WIDTH), lambda b: (b, 0)),
                  pl.BlockSpec((rows, LANES), lambda b: (b, 0)),
                  pl.BlockSpec((M_CONV, 2 * M_WIDTH), lambda b: (0, 0)),
                  pl.BlockSpec((1, 2 * M_WIDTH), lambda b: (0, 0)),
                  pl.BlockSpec((1, LANES), lambda b: (0, 0)),
                  pl.BlockSpec((1, M_WIDTH), lambda b: (0, 0))],
        out_specs=pl.BlockSpec((rows, M_WIDTH), lambda b: (b, 0)),
        out_shape=jax.ShapeDtypeStruct((T, M_WIDTH), BF16),
        scratch_shapes=[pltpu.VMEM((M_SEQS, M_WIDTH, M_WIDTH), F32),
                        pltpu.VMEM((M_SEQS, 1, M_WIDTH), F32),
                        pltpu.VMEM((M_SEQS, 1, M_WIDTH), F32)],
        compiler_params=_cparams("parallel"),
        name="mlstm",
    )(proj, small, cw, cb, gbc, ng)


POOL_HALO = 16
POOL_TILE = 256


def _pool_kernel(u_ref, w_ref, sc_ref, o_ref, upad_ref):
    S = u_ref.shape[0]
    upad_ref[0:POOL_HALO, :] = jnp.zeros((POOL_HALO, P_WIDTH), F32)
    upad_ref[POOL_HALO:, :] = u_ref[...].astype(F32)
    grp = lax.broadcasted_iota(jnp.int32, (1, P_WIDTH), 1) // (P_WIDTH // len(P_WINDOWS))
    win_lane = jnp.zeros((1, P_WIDTH), jnp.int32)
    for gi, wn in enumerate(P_WINDOWS):
        win_lane = jnp.where(grp == gi, wn, win_lane)
    w = w_ref[...]
    scale = sc_ref[...]
    rows = POOL_TILE + POOL_HALO

    for r0 in range(0, S, POOL_TILE):
        a = upad_ref[r0:r0 + rows, :]
        sums = []
        cur = a
        span = 1
        for _ in P_WINDOWS:
            cur = cur + pltpu.roll(cur, span, axis=0)
            span *= 2
            sums.append(cur)
        sel = sums[-1]
        for gi in range(len(P_WINDOWS) - 1):
            sel = jnp.where(grp == gi, sums[gi], sel)
        sel = sel[POOL_HALO:, :]
        u = a[POOL_HALO:, :]
        t = r0 + lax.broadcasted_iota(jnp.int32, (POOL_TILE, P_WIDTH), 0)
        cnt = jnp.minimum(t + 1, win_lane).astype(F32)
        pooled = sel / cnt - u
        o_ref[r0:r0 + POOL_TILE, :] = (_dot(pooled.astype(BF16), w) * scale).astype(o_ref.dtype)


def _pool(proj, w_bd, scale, B, S):
    T = B * S
    return pl.pallas_call(
        _pool_kernel,
        grid=(B,),
        in_specs=[pl.BlockSpec((S, P_WIDTH), lambda b: (b, PROJ_UP // P_WIDTH)),
                  pl.BlockSpec((P_WIDTH, P_WIDTH), lambda b: (0, 0)),
                  pl.BlockSpec((1, P_WIDTH), lambda b: (0, 0))],
        out_specs=pl.BlockSpec((S, P_WIDTH), lambda b: (b, 0)),
        out_shape=jax.ShapeDtypeStruct((T, P_WIDTH), BF16),
        scratch_shapes=[pltpu.VMEM((S + POOL_HALO, P_WIDTH), F32)],
        compiler_params=_cparams("parallel"),
        name="pool",
    )(proj, w_bd, scale)


def _rope(x, c, s1, s2):
    return x * c + pltpu.roll(x, LANES - A_ROPE // 2, axis=1) * s1 + pltpu.roll(x, A_ROPE // 2, axis=1) * s2


def _mla_prep_kernel(cq_ref, ckv_ref, small_ref, qg_ref, kvg_ref, wq_ref, wqs_ref, wk_ref, wv_ref,
                     vone_ref, cq_t_ref, ck_t_ref, s1_ref, s2_ref, q_ref, k_ref, v_ref):
    def rms(x, g):
        return x * lax.rsqrt(jnp.mean(x * x, axis=-1, keepdims=True) + EPS) * g

    cqn = rms(cq_ref[...].astype(F32), qg_ref[...]).astype(BF16)
    ckvn = rms(ckv_ref[...].astype(F32), kvg_ref[...]).astype(BF16)
    scale = (A_NOPE + A_ROPE) ** -0.5
    qf = _dot(cqn, wq_ref[...]) * scale
    qp = _dot(cqn, wqs_ref[...]) * scale
    kf = _dot(ckvn, wk_ref[...])
    v_ref[...] = (_dot(ckvn, wv_ref[...]) + vone_ref[...]).astype(BF16)
    cqt = cq_t_ref[...]
    s1 = s1_ref[...]
    s2 = s2_ref[...]
    krot = _rope(small_ref[...], ck_t_ref[...], s1, s2)
    sq = s1 + s2
    for h in range(A_HEADS):
        sl = slice(h * LANES, (h + 1) * LANES)
        q_ref[:, sl] = (qf[:, sl] * cqt + qp[:, sl] * sq).astype(BF16)
        k_ref[:, sl] = (kf[:, sl] + krot).astype(BF16)


def _mla_prep(proj, small, qg, kvg, wq, wqs, wk, wv, vone, cq_t, ck_t, s1_t, s2_t, B, S, ts=1024):
    T = B * S
    nst = S // ts
    hw = A_HEADS * LANES
    return pl.pallas_call(
        _mla_prep_kernel,
        grid=(B, nst),
        in_specs=[pl.BlockSpec((ts, A_QRANK), lambda b, s: (b * nst + s, PROJ_CQ // A_QRANK)),
                  pl.BlockSpec((ts, A_KVRANK), lambda b, s: (b * nst + s, PROJ_CKV // A_KVRANK)),
                  pl.BlockSpec((ts, LANES), lambda b, s: (b * nst + s, 0)),
                  pl.BlockSpec((1, A_QRANK), lambda b, s: (0, 0)),
                  pl.BlockSpec((1, A_KVRANK), lambda b, s: (0, 0)),
                  pl.BlockSpec((A_QRANK, hw), lambda b, s: (0, 0)),
                  pl.BlockSpec((A_QRANK, hw), lambda b, s: (0, 0)),
                  pl.BlockSpec((A_KVRANK, hw), lambda b, s: (0, 0)),
                  pl.BlockSpec((A_KVRANK, hw), lambda b, s: (0, 0)),
                  pl.BlockSpec((1, hw), lambda b, s: (0, 0)),
                  pl.BlockSpec((ts, LANES), lambda b, s: (s, 0)),
                  pl.BlockSpec((ts, LANES), lambda b, s: (s, 0)),
                  pl.BlockSpec((ts, LANES), lambda b, s: (s, 0)),
                  pl.BlockSpec((ts, LANES), lambda b, s: (s, 0))],
        out_specs=[pl.BlockSpec((ts, hw), lambda b, s: (b * nst + s, 0)),
                   pl.BlockSpec((ts, hw), lambda b, s: (b * nst + s, 0)),
                   pl.BlockSpec((ts, hw), lambda b, s: (b * nst + s, 0))],
        out_shape=[jax.ShapeDtypeStruct((T, hw), BF16),
                   jax.ShapeDtypeStruct((T, hw), BF16),
                   jax.ShapeDtypeStruct((T, hw), BF16)],
        compiler_params=_cparams("parallel", "parallel"),
        name="mla_prep",
    )(proj, proj, small, qg, kvg, wq, wqs, wk, wv, vone, cq_t, ck_t, s1_t, s2_t)


def _attn_kernel(q_ref, k_ref, v_ref, o_ref, *, tq):
    heads = range(2)
    sls = [slice(hh * LANES, (hh + 1) * LANES) for hh in heads]
    lane = lax.broadcasted_iota(jnp.int32, (tq, LANES), 1)
    below_diag = lax.broadcasted_iota(jnp.int32, (tq, tq), 0) >= lax.broadcasted_iota(jnp.int32, (tq, tq), 1)

    def update(qs, k0, state, causal):
        keys = slice(k0, k0 + tq)
        s = [_dot_nt(qs[hh], k_ref[keys, sls[hh]]) for hh in heads]
        if causal:
            s = [jnp.where(below_diag, s[hh], -jnp.inf) for hh in heads]
        m_new = [jnp.maximum(state[hh][0], jnp.max(s[hh], axis=-1, keepdims=True)) for hh in heads]
        p = [jnp.exp((s[hh] - m_new[hh]).astype(BF16)) for hh in heads]
        pv = [_dot(p[hh], v_ref[keys, sls[hh]]) for hh in heads]
        acc = [jnp.exp(state[hh][0] - m_new[hh]) * state[hh][1] + pv[hh] for hh in heads]
        return tuple((m_new[hh], acc[hh]) for hh in heads)

    for qi in range(q_ref.shape[0] // tq):
        rows = slice(qi * tq, (qi + 1) * tq)
        qs = [q_ref[rows, sl] for sl in sls]
        state = tuple((jnp.full((tq, 1), -jnp.inf, F32), jnp.zeros((tq, LANES), F32)) for _ in heads)
        for kb in range(qi + 1):
            state = update(qs, kb * tq, state, causal=(kb == qi))
        (_, acc0), (_, acc1) = state
        acc = jnp.where(lane < A_VDIM, acc0, acc1)
        den = jnp.where(lane < A_VDIM, pltpu.roll(acc0, A_VDIM, axis=1), pltpu.roll(acc1, A_VDIM, axis=1))
        o_ref[rows, :] = (acc / den).astype(o_ref.dtype)


def _attention(q, k, v, B, S, tq=512):
    T = B * S
    return pl.pallas_call(
        functools.partial(_attn_kernel, tq=tq),
        grid=(B, A_HEADS // 2),
        in_specs=[pl.BlockSpec((S, 2 * LANES), lambda b, p: (b, p)),
                  pl.BlockSpec((S, 2 * LANES), lambda b, p: (b, p)),
                  pl.BlockSpec((S, 2 * LANES), lambda b, p: (b, p))],
        out_specs=pl.BlockSpec((S, LANES), lambda b, p: (b, p)),
        out_shape=jax.ShapeDtypeStruct((T, A_WIDTH), BF16),
        compiler_params=_cparams("parallel", "parallel"),
        name="attention",
    )(q, k, v)


def _outproj_kernel(ym_ref, ya_ref, yp_ref, h_ref, w_ref, g_ref, wr_ref, br_ref,
                    hn_ref, xn_ref, lg_ref):
    mix = _dot(ym_ref[...], w_ref[0:M_WIDTH, :])
    mix = mix + _dot(ya_ref[...], w_ref[M_WIDTH:M_WIDTH + A_WIDTH, :])
    mix = mix + _dot(yp_ref[...], w_ref[M_WIDTH + A_WIDTH:, :])
    hn = h_ref[...] + mix
    hn_ref[...] = hn
    xn = hn * lax.rsqrt(jnp.mean(hn * hn, axis=-1, keepdims=True) + EPS) * g_ref[...]
    x_hi = xn.astype(BF16)
    x_lo = (xn - x_hi.astype(F32)).astype(BF16)
    wr = wr_ref[...]
    w_hi = wr.astype(BF16)
    w_lo = (wr - w_hi.astype(F32)).astype(BF16)
    lg_ref[...] = _dot_nt(w_hi, x_hi) + _dot_nt(w_hi, x_lo) + _dot_nt(w_lo, x_hi) + br_ref[...]
    xn_ref[...] = _pack_bf16_pairs(xn)


def _outproj(ym, ya, yp, h2, w, g, wr_t, br, tm=1024):
    T = h2.shape[0]
    return pl.pallas_call(
        _outproj_kernel,
        grid=(T // tm,),
        in_specs=[pl.BlockSpec((tm, M_WIDTH), lambda i: (i, 0)),
                  pl.BlockSpec((tm, A_WIDTH), lambda i: (i, 0)),
                  pl.BlockSpec((tm, P_WIDTH), lambda i: (i, 0)),
                  pl.BlockSpec((tm, D_MODEL), lambda i: (i, 0)),
                  pl.BlockSpec((D_MODEL, D_MODEL), lambda i: (0, 0)),
                  pl.BlockSpec((1, D_MODEL), lambda i: (0, 0)),
                  pl.BlockSpec((N_EXPERTS, D_MODEL), lambda i: (0, 0)),
                  pl.BlockSpec((N_EXPERTS, 1), lambda i: (0, 0))],
        out_specs=[pl.BlockSpec((tm, D_MODEL), lambda i: (i, 0)),
                   pl.BlockSpec((tm, D_MODEL // 2), lambda i: (i, 0)),
                   pl.BlockSpec((N_EXPERTS, tm), lambda i: (0, i))],
        out_shape=[jax.ShapeDtypeStruct((T, D_MODEL), F32),
                   jax.ShapeDtypeStruct((T, D_MODEL // 2), jnp.uint32),
                   jax.ShapeDtypeStruct((N_EXPERTS, T), F32)],
        compiler_params=_cparams("parallel"),
        name="outproj",
    )(ym, ya, yp, h2, w, g, wr_t, br)


def _router_kernel(lg_ref, tri_ref, e_ref, w_ref, r_ref, cnt_ref, carry_ref):
    tr = lg_ref.shape[1]

    @pl.when(pl.program_id(0) == 0)
    def _():
        carry_ref[...] = jnp.zeros_like(carry_ref)

    x = lg_ref[...]
    eio = lax.broadcasted_iota(jnp.int32, (N_EXPERTS, tr), 0).astype(F32)
    picked = jnp.zeros((N_EXPERTS, tr), F32)
    vals = []
    idxs = []
    for _ in range(TOP_K):
        mx = jnp.max(x, axis=0, keepdims=True)
        idx = jnp.min(jnp.where(x == mx, eio, float(N_EXPERTS)), axis=0, keepdims=True)
        hit = eio == idx
        vals.append(mx)
        idxs.append(idx)
        picked = picked + hit.astype(F32)
        x = jnp.where(hit, -jnp.inf, x)
    exps = [jnp.exp(vv - vals[0]) for vv in vals]
    tot = exps[0] + exps[1] + exps[2] + exps[3]
    before = _dot(picked.astype(BF16), tri_ref[...]) + carry_ref[:, 0:1]
    for kk in range(TOP_K):
        e_ref[kk:kk + 1, :] = idxs[kk].astype(jnp.int32)
        w_ref[kk:kk + 1, :] = exps[kk] / tot
        rk = jnp.sum(jnp.where(eio == idxs[kk], before, 0.0), axis=0, keepdims=True)
        r_ref[kk:kk + 1, :] = rk.astype(jnp.int32)
    carry_ref[...] = carry_ref[...] + jnp.sum(picked, axis=1, keepdims=True)
    cnt_ref[...] = carry_ref[...]


def _router(logits_t, tri, tr=512):
    T = logits_t.shape[1]
    return pl.pallas_call(
        _router_kernel,
        grid=(T // tr,),
        in_specs=[pl.BlockSpec((N_EXPERTS, tr), lambda i: (0, i)),
                  pl.BlockSpec((tr, tr), lambda i: (0, 0))],
        out_specs=[pl.BlockSpec((TOP_K, tr), lambda i: (0, i)),
                   pl.BlockSpec((TOP_K, tr), lambda i: (0, i)),
                   pl.BlockSpec((TOP_K, tr), lambda i: (0, i)),
                   pl.BlockSpec((N_EXPERTS, LANES), lambda i: (0, 0))],
        out_shape=[jax.ShapeDtypeStruct((TOP_K, T), jnp.int32),
                   jax.ShapeDtypeStruct((TOP_K, T), F32),
                   jax.ShapeDtypeStruct((TOP_K, T), jnp.int32),
                   jax.ShapeDtypeStruct((N_EXPERTS, LANES), F32)],
        scratch_shapes=[pltpu.VMEM((N_EXPERTS, LANES), F32)],
        compiler_params=_cparams("arbitrary"),
        name="router",
    )(logits_t, tri)


def _meta_kernel(cnt_ref, e_ref, r_ref, dest_ref, be_ref, bc_ref, bn_ref, nb_ref, *, nb_pad):
    cnt = cnt_ref[...]
    padded = jnp.floor((cnt + (MOE_ROWS - 1)) * (1.0 / MOE_ROWS)) * MOE_ROWS
    ri = lax.broadcasted_iota(jnp.int32, (N_EXPERTS, N_EXPERTS), 0)
    ci = lax.broadcasted_iota(jnp.int32, (N_EXPERTS, N_EXPERTS), 1)
    pad_end = _dot((ri >= ci).astype(F32), padded, precision=HIGHEST)
    pad_start = pad_end - padded
    e = e_ref[...]
    dest = r_ref[...]
    for ex in range(N_EXPERTS):
        ps = pad_start[ex:ex + 1, 0:1].astype(jnp.int32)
        dest = jnp.where(e == ex, dest + ps, dest)
    dest_ref[...] = dest
    blk0 = (lax.broadcasted_iota(jnp.int32, (N_EXPERTS, nb_pad), 1) * MOE_ROWS).astype(F32)
    be = jnp.sum((pad_end[:, 0:1] <= blk0).astype(F32), axis=0, keepdims=True)
    be = jnp.minimum(be, float(N_EXPERTS - 1))
    eio = lax.broadcasted_iota(jnp.int32, (N_EXPERTS, nb_pad), 0).astype(F32)
    seg_end = jnp.sum(jnp.where(eio == be, pad_start[:, 0:1] + cnt[:, 0:1], 0.0), axis=0, keepdims=True)
    bc = jnp.clip(seg_end - blk0[0:1, :], 0.0, float(MOE_ROWS))
    nxt0 = jnp.sum(jnp.where(eio == be, pad_end[:, 0:1], 0.0), axis=0, keepdims=True)
    bn = jnp.sum((pad_end[:, 0:1] <= nxt0).astype(F32), axis=0, keepdims=True)
    bn = jnp.where(nxt0 < pad_end[N_EXPERTS - 1:N_EXPERTS, 0:1], bn, -1.0)
    be_ref[...] = be.astype(jnp.int32)
    bc_ref[...] = bc.astype(jnp.int32)
    bn_ref[...] = bn.astype(jnp.int32)
    nb_ref[...] = (pad_end[N_EXPERTS - 1:N_EXPERTS, :] * (1.0 / MOE_ROWS)).astype(jnp.int32)


def _meta(counts, eidx, rank, nb_pad):
    T = eidx.shape[1]
    return pl.pallas_call(
        functools.partial(_meta_kernel, nb_pad=nb_pad),
        out_shape=[jax.ShapeDtypeStruct((TOP_K, T), jnp.int32),
                   jax.ShapeDtypeStruct((1, nb_pad), jnp.int32),
                   jax.ShapeDtypeStruct((1, nb_pad), jnp.int32),
                   jax.ShapeDtypeStruct((1, nb_pad), jnp.int32),
                   jax.ShapeDtypeStruct((1, LANES), jnp.int32)],
        compiler_params=pltpu.CompilerParams(vmem_limit_bytes=VMEM_LIMIT),
        name="route_meta",
    )(counts, eidx, rank)


FF_CHUNK = 1024


def _gmm_kernel(be_ref, bc_ref, bn_ref, nb_ref, x_ref, wgu_hbm, bgu_ref, wdn_hbm, bdn_ref, y_ref,
                wgu_st, wdn_st, wgu16, wdn16, sem, *, e0):
    i = pl.program_id(0)
    nblk = nb_ref[0]
    bm = MOE_ROWS

    def weight_copies(e):
        return (pltpu.make_async_copy(wgu_hbm.at[e0 + e], wgu_st, sem.at[0]),
                pltpu.make_async_copy(wdn_hbm.at[e0 + e], wdn_st, sem.at[1]))

    @pl.when(i == 0)
    def _():
        for cp in weight_copies(be_ref[0]):
            cp.start()

    @pl.when(i >= nblk)
    def _():
        y_ref[...] = jnp.zeros_like(y_ref)

    @pl.when(i < nblk)
    def _():
        e_changed = jnp.logical_or(i == 0, be_ref[i] != be_ref[jnp.maximum(i - 1, 0)])

        quarter = bm // MOE_SPLIT
        full = bc_ref[i] > (MOE_SPLIT - 1) * quarter
        fused = jnp.logical_and(e_changed, full)

        @pl.when(e_changed)
        def _():
            for cp in weight_copies(be_ref[i]):
                cp.wait()

        @pl.when(jnp.logical_and(e_changed, jnp.logical_not(full)))
        def _():
            wgu16[...] = wgu_st[...].astype(BF16)
            wdn16[...] = wdn_st[...].astype(BF16)

        def expert_rows(nrows, convert=False):
            valid = lax.broadcasted_iota(jnp.int32, (nrows, 1), 0) < bc_ref[i]
            lo, hi = _unpack_bf16_pairs(jnp.where(valid, x_ref[0:nrows, :], jnp.uint32(0)))
            x16 = jnp.concatenate([lo.astype(BF16), hi.astype(BF16)], axis=1)
            acc = jnp.zeros((nrows, D_MODEL), F32) + bdn_ref[0]
            for c in range(D_FF // FF_CHUNK):
                cs = slice(c * FF_CHUNK, (c + 1) * FF_CHUNK)
                us = slice(D_FF + c * FF_CHUNK, D_FF + (c + 1) * FF_CHUNK)
                if convert:
                    wgu16[:, cs] = wgu_st[:, cs].astype(BF16)
                    wgu16[:, us] = wgu_st[:, us].astype(BF16)
                    wdn16[cs, :] = wdn_st[cs, :].astype(BF16)
                gate = _dot(x16, wgu16[:, cs]) + bgu_ref[0, :, cs]
                up = _dot(x16, wgu16[:, us]) + bgu_ref[0, :, us]
                gate = jnp.minimum(gate, SWIGLU_LIMIT)
                up = jnp.clip(up, -SWIGLU_LIMIT, SWIGLU_LIMIT)
                act = (up + 1.0) * gate * _sigmoid(SWIGLU_ALPHA * gate)
                acc = acc + _dot(act.astype(BF16), wdn16[cs, :])
            y_ref[0:nrows, :] = _pack_bf16_pairs(acc)

        @pl.when(fused)
        def _():
            expert_rows(bm, convert=True)

        for nq in range(1, MOE_SPLIT + 1):
            lower = (nq - 1) * quarter if nq > 1 else -1
            in_range = jnp.logical_and(bc_ref[i] > lower, bc_ref[i] <= nq * quarter)
            @pl.when(jnp.logical_and(in_range, jnp.logical_not(fused)))
            def _(nq=nq):
                expert_rows(nq * quarter)
                if nq < MOE_SPLIT:
                    y_ref[nq * quarter:, :] = jnp.zeros((bm - nq * quarter, D_MODEL // 2), jnp.uint32)

        @pl.when(jnp.logical_and(e_changed, bn_ref[i] >= 0))
        def _():
            for cp in weight_copies(bn_ref[i]):
                cp.start()


def _gmm(blk_e, blk_cnt, blk_next, nblk, x_rows, wgu, bgu, wdn, bdn, nb, layer):
    bm = MOE_ROWS
    e0 = layer * N_EXPERTS
    dp = D_MODEL // 2

    def expert(i, be, nb_ref):
        return (e0 + be[jnp.minimum(i, jnp.maximum(nb_ref[0] - 1, 0))], 0, 0)

    def rows(i, nb_ref):
        return (jnp.minimum(i, jnp.maximum(nb_ref[0] - 1, 0)), 0)

    grid_spec = pltpu.PrefetchScalarGridSpec(
        num_scalar_prefetch=4,
        grid=(nb,),
        in_specs=[pl.BlockSpec((bm, dp), lambda i, be, bc, bn, nbr: rows(i, nbr)),
                  pl.BlockSpec(memory_space=pl.ANY),
                  pl.BlockSpec((1, 1, 2 * D_FF), lambda i, be, bc, bn, nbr: expert(i, be, nbr)),
                  pl.BlockSpec(memory_space=pl.ANY),
                  pl.BlockSpec((1, 1, D_MODEL), lambda i, be, bc, bn, nbr: expert(i, be, nbr))],
        out_specs=pl.BlockSpec((bm, dp), lambda i, be, bc, bn, nbr: (i, 0)),
        scratch_shapes=[pltpu.VMEM((D_MODEL, 2 * D_FF), F32),
                        pltpu.VMEM((D_FF, D_MODEL), F32),
                        pltpu.VMEM((D_MODEL, 2 * D_FF), BF16),
                        pltpu.VMEM((D_FF, D_MODEL), BF16),
                        pltpu.SemaphoreType.DMA((2,))],
    )
    return pl.pallas_call(
        functools.partial(_gmm_kernel, e0=e0),
        grid_spec=grid_spec,
        out_shape=jax.ShapeDtypeStruct((nb * bm, dp), jnp.uint32),
        compiler_params=_cparams("arbitrary"),
        name="expert_gmm",
    )(blk_e, blk_cnt, blk_next, nblk, x_rows, wgu, bgu, wdn, bdn)


def _final_kernel(h_ref, y0, y1, y2, y3, wk_ref, g_ref, o_ref):
    for rows in _row_slices(h_ref.shape[0]):
        x = _moe_combine(h_ref, (y0, y1, y2, y3), wk_ref, rows)
        o_ref[rows, :] = x * lax.rsqrt(jnp.mean(x * x, axis=-1, keepdims=True) + EPS) * g_ref[...]


def _final(h2, y_tok, w_tok, g, tm=512):
    T = h2.shape[0]
    nt = T // tm
    return pl.pallas_call(
        _final_kernel,
        grid=(nt,),
        in_specs=_combine_specs(tm, nt) + [pl.BlockSpec((1, D_MODEL), lambda i: (0, 0))],
        out_specs=pl.BlockSpec((tm, D_MODEL), lambda i: (i, 0)),
        out_shape=jax.ShapeDtypeStruct((T, D_MODEL), F32),
        compiler_params=_cparams("parallel"),
        name="combine_final_norm",
    )(h2, y_tok, y_tok, y_tok, y_tok, w_tok, g)


SC_CORES = 2
SC_SUBCORES = 16
SC_LANES = 16
SC_WORKERS = SC_CORES * SC_SUBCORES
SC_WINDOW = 64


def _sc_mesh():
    return plsc.VectorSubcoreMesh(core_axis_name="c", subcore_axis_name="s")


def _sc_worker():
    return lax.axis_index("s") * SC_CORES + lax.axis_index("c")


def _sc_dispatch(xn, dest_flat, n_rows):
    T = xn.shape[0]
    tpw = T // SC_WORKERS
    nchunk = tpw // SC_WINDOW
    nvec = SC_WINDOW // SC_LANES

    @functools.partial(
        pl.kernel, out_type=jax.ShapeDtypeStruct((n_rows, xn.shape[1]), xn.dtype), mesh=_sc_mesh(),
        scratch_types=[pltpu.VMEM((TOP_K * tpw,), jnp.int32),
                       pltpu.VMEM((SC_WINDOW, xn.shape[1]), xn.dtype),
                       pltpu.VMEM((SC_WINDOW, xn.shape[1]), xn.dtype),
                       pltpu.SemaphoreType.DMA, pltpu.SemaphoreType.DMA, pltpu.SemaphoreType.DMA],
        name="sc_dispatch")
    def run(x_hbm, d_hbm, o_hbm, idx_v, buf0, buf1, sem0, sem1, sem_out):
        base = _sc_worker() * tpw
        for kk in range(TOP_K):
            pltpu.sync_copy(d_hbm.at[pl.ds(kk * T + base, tpw)], idx_v.at[pl.ds(kk * tpw, tpw)])
        bufs = (buf0, buf1)
        sems = (sem0, sem1)

        def load(c, slot):
            return pltpu.make_async_copy(x_hbm.at[pl.ds(base + c * SC_WINDOW, SC_WINDOW)], bufs[slot], sems[slot])

        load(0, 0).start()

        @pl.loop(0, nchunk, step=2)
        def _(c0):
            for slot in range(2):
                c = c0 + slot
                load(c, slot).wait()

                @pl.when(c + 1 < nchunk)
                def _():
                    load(c + 1, 1 - slot).start()

                copies = []
                for kk in range(TOP_K):
                    for q in range(nvec):
                        off = pl.multiple_of(kk * tpw + c * SC_WINDOW + q * SC_LANES, SC_LANES)
                        rows = idx_v[pl.ds(off, SC_LANES)]
                        cp = pltpu.make_async_copy(bufs[slot].at[pl.ds(q * SC_LANES, SC_LANES)],
                                                   o_hbm.at[rows], sem_out)
                        cp.start()
                        copies.append(cp)
                for cp in copies:
                    cp.wait()

    return run(xn, dest_flat)


def _sc_gather(y_rows, dest_flat):
    n = dest_flat.shape[0]
    rpw = n // SC_WORKERS
    nchunk = rpw // SC_WINDOW
    nvec = SC_WINDOW // SC_LANES

    @functools.partial(
        pl.kernel, out_type=jax.ShapeDtypeStruct((n, y_rows.shape[1]), y_rows.dtype), mesh=_sc_mesh(),
        scratch_types=[pltpu.VMEM((rpw,), jnp.int32),
                       pltpu.VMEM((SC_WINDOW, y_rows.shape[1]), y_rows.dtype),
                       pltpu.VMEM((SC_WINDOW, y_rows.shape[1]), y_rows.dtype),
                       pltpu.SemaphoreType.DMA, pltpu.SemaphoreType.DMA, pltpu.SemaphoreType.DMA],
        name="sc_gather")
    def run(y_hbm, d_hbm, o_hbm, idx_v, buf0, buf1, sem0, sem1, sem_in):
        base = _sc_worker() * rpw
        pltpu.sync_copy(d_hbm.at[pl.ds(base, rpw)], idx_v)
        bufs = (buf0, buf1)
        sems = (sem0, sem1)

        def store(c, slot):
            return pltpu.make_async_copy(bufs[slot], o_hbm.at[pl.ds(base + c * SC_WINDOW, SC_WINDOW)], sems[slot])

        @pl.loop(0, nchunk, step=2)
        def _(c0):
            for slot in range(2):
                c = c0 + slot

                @pl.when(c >= 2)
                def _():
                    store(c - 2, slot).wait()

                copies = []
                for q in range(nvec):
                    off = pl.multiple_of(c * SC_WINDOW + q * SC_LANES, SC_LANES)
                    rows = idx_v[pl.ds(off, SC_LANES)]
                    cp = pltpu.make_async_copy(y_hbm.at[rows], bufs[slot].at[pl.ds(q * SC_LANES, SC_LANES)], sem_in)
                    cp.start()
                    copies.append(cp)
                for cp in copies:
                    cp.wait()
                store(c, slot).start()

        store(nchunk - 2, 0).wait()
        store(nchunk - 1, 1).wait()

    return run(y_rows, dest_flat)


def _prep_w_in(w_in):
    o_g = 4 * M_WIDTH
    o_cq = o_g + 2 * M_HEADS
    o_ckv = o_cq + A_QRANK
    o_kr = o_ckv + A_KVRANK
    o_up = o_kr + A_ROPE
    z = lambda n: jnp.zeros(w_in.shape[:-1] + (n,), w_in.dtype)
    small = jnp.concatenate([z(SMALL_KR), w_in[..., o_kr:o_up], w_in[..., o_g:o_cq],
                             z(LANES - SMALL_GATE - 2 * M_HEADS)], axis=-1)
    return jnp.concatenate([w_in[..., 0:o_g], w_in[..., o_cq:o_ckv], w_in[..., o_up:o_up + P_WIDTH],
                            w_in[..., o_ckv:o_kr], small], axis=-1).astype(BF16)


def _rope_tables(seq):
    inv = ROPE_THETA ** (-jnp.arange(0, A_ROPE, 2, dtype=F32) / A_ROPE)
    ang = jnp.arange(seq, dtype=F32)[:, None] * inv[None, :]
    cos, sin = jnp.cos(ang), jnp.sin(ang)
    half = A_ROPE // 2
    zeros = lambda n: jnp.zeros((seq, n), F32)
    ones = lambda n: jnp.ones((seq, n), F32)
    tail = LANES - A_NOPE - A_ROPE
    cq_t = jnp.concatenate([ones(A_NOPE), cos, cos, zeros(tail)], axis=1)
    ck_t = jnp.concatenate([zeros(A_NOPE), cos, cos, zeros(tail)], axis=1)
    s1_t = jnp.concatenate([zeros(A_NOPE), -sin, zeros(half), zeros(tail)], axis=1)
    s2_t = jnp.concatenate([zeros(A_NOPE), zeros(half), sin, zeros(tail)], axis=1)
    return cq_t, ck_t, s1_t, s2_t


def kernel(x, norm1_g, w_in, conv_w, conv_b, gate_b, mlstm_norm_g, q_norm_g, kv_norm_g, w_uq, w_ukv,
           w_pool, pool_scale, w_out, norm2_g, w_router, b_router, w_gate_up, b_gate_up, w_down, b_down,
           final_norm_g):
    B, S, D = x.shape
    depth = w_in.shape[0]
    T = B * S
    nb = (T * TOP_K) // MOE_ROWS + N_EXPERTS
    nb_pad = -(-nb // LANES) * LANES

    w_in_p = _prep_w_in(w_in)
    wq = w_uq.reshape(depth, A_QRANK, A_HEADS, A_NOPE + A_ROPE)
    wq = jnp.pad(wq, ((0, 0), (0, 0), (0, 0), (0, LANES - A_NOPE - A_ROPE)))
    r0, r1, r2 = A_NOPE, A_NOPE + A_ROPE // 2, A_NOPE + A_ROPE
    wqs = jnp.concatenate([jnp.zeros_like(wq[..., :r0]), wq[..., r1:r2], wq[..., r0:r1],
                           jnp.zeros_like(wq[..., r2:])], axis=-1)
    wq = wq.reshape(depth, A_QRANK, A_HEADS * LANES).astype(BF16)
    wqs = wqs.reshape(depth, A_QRANK, A_HEADS * LANES).astype(BF16)
    wkv = w_ukv.reshape(depth, A_KVRANK, A_HEADS, A_NOPE + A_VDIM)
    wk = jnp.pad(wkv[..., :A_NOPE], ((0, 0), (0, 0), (0, 0), (0, LANES - A_NOPE)))
    wk = wk.reshape(depth, A_KVRANK, A_HEADS * LANES).astype(BF16)
    wv_e = jnp.pad(wkv[:, :, 0::2, A_NOPE:], ((0, 0), (0, 0), (0, 0), (0, LANES - A_VDIM)))
    wv_o = jnp.pad(wkv[:, :, 1::2, A_NOPE:], ((0, 0), (0, 0), (0, 0), (LANES - A_VDIM, 0)))
    wv = jnp.stack([wv_e, wv_o], axis=3).reshape(depth, A_KVRANK, A_HEADS * LANES).astype(BF16)
    half = jnp.arange(A_HEADS * LANES) // A_VDIM
    vone = ((half % 4 == 1) | (half % 4 == 2)).astype(F32)[None, :]
    gsz = P_WIDTH // len(P_WINDOWS)
    w_pool_bd = jnp.zeros((depth, P_WIDTH, P_WIDTH), F32)
    for gi in range(len(P_WINDOWS)):
        w_pool_bd = w_pool_bd.at[:, gi * gsz:(gi + 1) * gsz, gi * gsz:(gi + 1) * gsz].set(w_pool[:, gi])
    w_pool_bd = w_pool_bd.astype(BF16)
    w_out16 = w_out.astype(BF16)
    w_router_t = jnp.swapaxes(w_router, 1, 2)
    gate_b_col = jnp.pad(gate_b, ((0, 0), (SMALL_GATE, LANES - SMALL_GATE - 2 * M_HEADS)))
    cq_t, ck_t, s1_t, s2_t = _rope_tables(S)
    tr = 1024
    tri = (jnp.arange(tr)[:, None] < jnp.arange(tr)[None, :]).astype(BF16)

    wgu_all = w_gate_up.reshape(depth * N_EXPERTS, D_MODEL, 2 * D_FF)
    bgu_all = b_gate_up.reshape(depth * N_EXPERTS, 1, 2 * D_FF)
    wdn_all = w_down.reshape(depth * N_EXPERTS, D_FF, D_MODEL)
    bdn_all = b_down.reshape(depth * N_EXPERTS, 1, D_MODEL)

    h = x.reshape(T, D)
    moe = None
    for l in range(depth):
        h, proj, small = _inproj(h, norm1_g[l][None, :], w_in_p[l], moe)
        y_m = _mlstm(proj, small, conv_w[l], conv_b[l][None, :], gate_b_col[l][None, :],
                     mlstm_norm_g[l][None, :], B, S)
        q16, k16, v16 = _mla_prep(proj, small, q_norm_g[l][None, :], kv_norm_g[l][None, :], wq[l], wqs[l],
                                  wk[l], wv[l], vone, cq_t, ck_t, s1_t, s2_t, B, S)
        y_a = _attention(q16, k16, v16, B, S)
        y_p = _pool(proj, w_pool_bd[l], pool_scale[l][None, :], B, S)
        h, xn, logits_t = _outproj(y_m, y_a, y_p, h, w_out16[l], norm2_g[l][None, :],
                                   w_router_t[l], b_router[l][:, None])
        eidx, wts, rank, counts = _router(logits_t, tri, tr)
        dest, blk_e, blk_cnt, blk_next, nblk = _meta(counts, eidx, rank, nb_pad)
        dest_flat = dest.reshape(TOP_K * T)
        x_rows = _sc_dispatch(xn, dest_flat, nb * MOE_ROWS)
        y_rows = _gmm(blk_e[0], blk_cnt[0], blk_next[0], nblk[0], x_rows, wgu_all, bgu_all, wdn_all,
                      bdn_all, nb, l)
        moe = (_sc_gather(y_rows, dest_flat), wts.T)
    return _final(h, moe[0], moe[1], final_norm_g[None, :]).reshape(B, S, D)
```
